```python
import jax, jax.numpy as jnp
from jax import lax
import numpy as np

D_MODEL = 2048
BATCH = 8
SEQ = 2048
DEPTH = 1

D_MIX = D_MODEL
NSA_WIDTH = D_MIX // 2
NSA_HEAD_DIM = 64
NSA_HEADS = NSA_WIDTH // NSA_HEAD_DIM
NSA_KV_HEADS = 4
NSA_GROUP = NSA_HEADS // NSA_KV_HEADS
NSA_KV_WIDTH = NSA_KV_HEADS * NSA_HEAD_DIM
CMP_BLOCK = 32
CMP_STRIDE = 16
CMP_HIDDEN = 4 * NSA_HEAD_DIM
SEL_BLOCK = 64
SEL_TOP = 8
SEL_BONUS = 1.0e4
WINDOW = 512
Q_BLOCK = 128
HGRN_WIDTH = D_MIX - NSA_WIDTH
HGRN_HEAD_DIM = 128
HGRN_HEADS = HGRN_WIDTH // HGRN_HEAD_DIM
HGRN_CHUNK = 64
EPS = 1e-6
NEG_INF = -1e30
IN_SPLITS = [NSA_WIDTH] + [NSA_KV_WIDTH] * 6 + [3 * NSA_HEADS, NSA_WIDTH] + [HGRN_WIDTH] * 4
D_IN = sum(IN_SPLITS)

kernel_name = "hymba_nsa_hgrn2_layer"


def rmsnorm(x, g):
    xf = x.astype(jnp.float32)
    y = xf * lax.rsqrt(jnp.mean(xf * xf, axis=-1, keepdims=True) + EPS)
    return (y * g.astype(jnp.float32)).astype(x.dtype)


def head_rmsnorm(x, g, n_heads):
    B, S, W = x.shape
    xh = x.astype(jnp.float32).reshape(B, S, n_heads, W // n_heads)
    y = xh * lax.rsqrt(jnp.mean(xh * xh, axis=-1, keepdims=True) + EPS)
    return (y.reshape(B, S, W) * g.astype(jnp.float32)).astype(x.dtype)


def alibi_slopes(n):
    return jnp.asarray(2.0 ** (-8.0 * np.arange(1, n + 1) / n), dtype=jnp.float32)


def masked_softmax(logits, mask):
    logits = jnp.where(mask, logits, NEG_INF)
    return jax.nn.softmax(logits, axis=-1) * mask


def compress_blocks(kv, pe, w1, w2):
    B, S, G, dh = kv.shape
    n_cmp = (S - CMP_BLOCK) // CMP_STRIDE + 1
    idx = CMP_STRIDE * np.arange(n_cmp)[:, None] + np.arange(CMP_BLOCK)[None, :]
    blk = kv[:, idx] + pe[:, None, :]
    blk = blk.transpose(0, 1, 3, 2, 4).reshape(B, n_cmp, G, CMP_BLOCK * dh)
    return jax.nn.gelu(blk @ w1) @ w2


def nsa_mixer(q, k_cmp, v_cmp, k_slc, v_slc, k_win, v_win, gate,
              pe_k, w1_k, w2_k, pe_v, w1_v, w2_v):
    B, S = q.shape[:2]
    G, R, dh = NSA_KV_HEADS, NSA_GROUP, NSA_HEAD_DIM
    q = q.reshape(B, S, G, R, dh) * (dh ** -0.5)
    kv4 = lambda a: a.reshape(B, S, G, dh)
    k_cmp, v_cmp, k_slc, v_slc, k_win, v_win = map(kv4, (k_cmp, v_cmp, k_slc, v_slc, k_win, v_win))
    slopes = alibi_slopes(NSA_HEADS).reshape(G, R)[:, :, None, None]
    t = jnp.arange(S)

    kc = compress_blocks(k_cmp, pe_k, w1_k, w2_k)
    vc = compress_blocks(v_cmp, pe_v, w1_v, w2_v)
    n_cmp = kc.shape[1]
    c_end = CMP_STRIDE * jnp.arange(n_cmp) + CMP_BLOCK - 1
    dist_c = t[:, None] - c_end[None, :]
    logits = jnp.einsum('btgrd,bngd->bgrtn', q, kc).astype(jnp.float32) - slopes * dist_c
    p_cmp = masked_softmax(logits, dist_c >= 0)
    o_cmp = jnp.einsum('bgrtn,bngd->btgrd', p_cmp.astype(vc.dtype), vc)

    n_sel = S // SEL_BLOCK
    top = min(SEL_TOP, n_sel)
    cs = CMP_STRIDE * np.arange(n_cmp)[:, None]
    ss = SEL_BLOCK * np.arange(n_sel)[None, :]
    overlap = np.clip(np.minimum(cs + CMP_BLOCK, ss + SEL_BLOCK) - np.maximum(cs, ss), 0, None)
    m_map = jnp.asarray(overlap / CMP_BLOCK, dtype=jnp.float32)
    p_slc = jnp.einsum('bgtn,nj->bgtj', p_cmp.sum(axis=2), m_map)
    j = jnp.arange(n_sel)[None, :]
    cur = (t // SEL_BLOCK)[:, None]
    forced = ((j == 0) | (j == cur) | (j == cur - 1)).astype(jnp.float32)
    future = j * SEL_BLOCK > t[:, None]
    score = jnp.where(future, -1.0, p_slc + SEL_BONUS * forced)
    _, sel_idx = lax.top_k(score, top)

    k_blocks = k_slc.reshape(B, n_sel, SEL_BLOCK, G, dh).transpose(0, 3, 1, 2, 4)
    v_blocks = v_slc.reshape(B, n_sel, SEL_BLOCK, G, dh).transpose(0, 3, 1, 2, 4)
    nq = S // Q_BLOCK
    q_chunks = q.reshape(B, nq, Q_BLOCK, G, R, dh).transpose(1, 0, 3, 4, 2, 5)
    idx_chunks = sel_idx.reshape(B, G, nq, Q_BLOCK, top).transpose(2, 0, 1, 3, 4)
    gather = jax.vmap(jax.vmap(lambda blocks, ids: blocks[ids]))
    n_tok = top * SEL_BLOCK

    def sel_block(args):
        qc, ic, c = args
        kg = gather(k_blocks, ic).reshape(B, G, Q_BLOCK, n_tok, dh)
        vg = gather(v_blocks, ic).reshape(B, G, Q_BLOCK, n_tok, dh)
        pos = (ic[..., None] * SEL_BLOCK + jnp.arange(SEL_BLOCK)).reshape(B, G, Q_BLOCK, n_tok)
        t_c = c * Q_BLOCK + jnp.arange(Q_BLOCK)
        dist = (t_c[:, None] - pos)[:, :, None]
        lg = jnp.einsum('bgrqd,bgqkd->bgrqk', qc, kg).astype(jnp.float32) - slopes * dist
        p = masked_softmax(lg, dist >= 0)
        return jnp.einsum('bgrqk,bgqkd->bgrqd', p.astype(vg.dtype), vg)

    o_slc = lax.map(sel_block, (q_chunks, idx_chunks, jnp.arange(nq)))
    o_slc = o_slc.transpose(1, 0, 4, 2, 3, 5).reshape(B, S, G, R, dh)

    span = WINDOW + Q_BLOCK
    k_pad = jnp.pad(k_win, ((0, 0), (WINDOW, 0), (0, 0), (0, 0)))
    v_pad = jnp.pad(v_win, ((0, 0), (WINDOW, 0), (0, 0), (0, 0)))

    def win_block(args):
        qc, c = args
        start = c * Q_BLOCK
        kw = lax.dynamic_slice_in_dim(k_pad, start, span, axis=1)
        vw = lax.dynamic_slice_in_dim(v_pad, start, span, axis=1)
        pos = start - WINDOW + jnp.arange(span)
        t_c = start + jnp.arange(Q_BLOCK)
        dist = t_c[:, None] - pos[None, :]
        mask = (pos[None, :] >= 0) & (dist >= 0) & (dist < WINDOW)
        lg = jnp.einsum('bgrqd,bkgd->bgrqk', qc, kw).astype(jnp.float32) - slopes * dist
        p = masked_softmax(lg, mask)
        return jnp.einsum('bgrqk,bkgd->bgrqd', p.astype(vw.dtype), vw)

    o_win = lax.map(win_block, (q_chunks, jnp.arange(nq)))
    o_win = o_win.transpose(1, 0, 4, 2, 3, 5).reshape(B, S, G, R, dh)

    g = jax.nn.sigmoid(gate.reshape(B, S, G, R, 3))
    o = g[..., 0:1] * o_cmp + g[..., 1:2] * o_slc + g[..., 2:3] * o_win
    return o.reshape(B, S, NSA_WIDTH)


def hgrn2_mixer(q, f_pre, v, lb):
    out_dtype = q.dtype
    B, S, W = q.shape
    H, dh, C = HGRN_HEADS, HGRN_HEAD_DIM, HGRN_CHUNK
    nc = S // C
    qf, vf = q.astype(jnp.float32), v.astype(jnp.float32)
    f = lb + (1.0 - lb) * jax.nn.sigmoid(f_pre.astype(jnp.float32))
    k = 1.0 - f
    log_f = jnp.log(f)
    chunk = lambda a: a.reshape(B, nc, C, H, dh).transpose(1, 0, 3, 2, 4)
    causal = jnp.tril(jnp.ones((C, C), dtype=bool))[:, :, None]

    def step(state, xs):
        qc, kc, vc, lfc = xs
        b = jnp.cumsum(lfc, axis=2)
        o_inter = jnp.einsum('bhtd,bhde->bhte', qc * jnp.exp(b), state)
        decay = jnp.exp(jnp.where(causal, b[:, :, :, None, :] - b[:, :, None, :, :], -jnp.inf))
        attn = jnp.einsum('bhtd,bhsd,bhtsd->bhts', qc, kc, decay)
        o = o_inter + jnp.einsum('bhts,bhse->bhte', attn, vc)
        b_last = b[:, :, -1:, :]
        state = jnp.exp(b_last[:, :, 0, :])[..., None] * state + \
            jnp.einsum('bhsd,bhse->bhde', kc * jnp.exp(b_last - b), vc)
        return state, o

    state0 = jnp.zeros((B, H, dh, dh), dtype=jnp.float32)
    _, o = lax.scan(step, state0, (chunk(qf), chunk(k), chunk(vf), chunk(log_f)))
    return o.transpose(1, 0, 3, 2, 4).reshape(B, S, W).astype(out_dtype)


def setup_inputs(seed: int = 0) -> dict:
    key = jax.random.key(seed)
    ks = jax.random.split(key, 16)
    nrm = lambda k, shape, s: jax.random.normal(k, shape, dtype=jnp.float32) * s
    gain = lambda k, shape: 1.0 + nrm(k, shape, 0.02)
    return {
        "x": nrm(ks[0], (BATCH, SEQ, D_MODEL), 1.0),
        "norm_in": gain(ks[1], (DEPTH, D_MODEL)),
        "w_in": nrm(ks[2], (DEPTH, D_MODEL, D_IN), D_MODEL ** -0.5),
        "cmp_pe_k": nrm(ks[3], (DEPTH, CMP_BLOCK, NSA_HEAD_DIM), 0.1),
        "cmp_w1_k": nrm(ks[4], (DEPTH, CMP_BLOCK * NSA_HEAD_DIM, CMP_HIDDEN), (CMP_BLOCK * NSA_HEAD_DIM) ** -0.5),
        "cmp_w2_k": nrm(ks[5], (DEPTH, CMP_HIDDEN, NSA_HEAD_DIM), CMP_HIDDEN ** -0.5),
        "cmp_pe_v": nrm(ks[6], (DEPTH, CMP_BLOCK, NSA_HEAD_DIM), 0.1),
        "cmp_w1_v": nrm(ks[7], (DEPTH, CMP_BLOCK * NSA_HEAD_DIM, CMP_HIDDEN), (CMP_BLOCK * NSA_HEAD_DIM) ** -0.5),
        "cmp_w2_v": nrm(ks[8], (DEPTH, CMP_HIDDEN, NSA_HEAD_DIM), CMP_HIDDEN ** -0.5),
        "lower_bounds": nrm(ks[9], (DEPTH + 1, HGRN_WIDTH), 0.1),
        "nsa_out_norm": gain(ks[10], (DEPTH, NSA_WIDTH)),
        "hgrn_out_norm": gain(ks[11], (DEPTH, HGRN_WIDTH)),
        "w_out": nrm(ks[12], (DEPTH, D_MIX, D_MODEL), D_MIX ** -0.5),
        "final_norm": gain(ks[13], (D_MODEL,)),
    }


def reference(x, norm_in, w_in, cmp_pe_k, cmp_w1_k, cmp_w2_k, cmp_pe_v, cmp_w1_v, cmp_w2_v,
              lower_bounds, nsa_out_norm, hgrn_out_norm, w_out, final_norm):
    split_at = np.cumsum(IN_SPLITS)[:-1].tolist()
    lbs = jnp.cumsum(jax.nn.softmax(lower_bounds.astype(jnp.float32), axis=0), axis=0)
    for l in range(DEPTH):
        h = rmsnorm(x, norm_in[l])
        proj = h @ w_in[l]
        (q_a, k_cmp, v_cmp, k_slc, v_slc, k_win, v_win, gate_a, z_a,
         q_h, f_h, i_h, z_h) = jnp.split(proj, split_at, axis=-1)
        o_a = nsa_mixer(q_a, k_cmp, v_cmp, k_slc, v_slc, k_win, v_win, gate_a,
                        cmp_pe_k[l], cmp_w1_k[l], cmp_w2_k[l], cmp_pe_v[l], cmp_w1_v[l], cmp_w2_v[l])
        o_h = hgrn2_mixer(q_h, f_h, i_h, lbs[l])
        o_a = head_rmsnorm(o_a, nsa_out_norm[l], NSA_HEADS) * jax.nn.silu(z_a)
        o_h = head_rmsnorm(o_h, hgrn_out_norm[l], HGRN_HEADS) * jax.nn.silu(z_h)
        mixed = jnp.concatenate([o_a, o_h], axis=-1)
        x = x + mixed @ w_out[l]
    return rmsnorm(x, final_norm)
```

```python
import functools

import numpy as np
import jax
import jax.numpy as jnp
from jax import lax
from jax.experimental import pallas as pl
from jax.experimental.pallas import tpu as pltpu

F32 = jnp.float32
BF16 = jnp.bfloat16

EPS = 1e-6
NEG_INF = -1e30

NSA_HEAD_DIM = 64
NSA_KV_HEADS = 4
CMP_BLOCK = 32
CMP_STRIDE = 16
SEL_BLOCK = 64
SEL_TOP = 8
SEL_BONUS = 1.0e4
WINDOW = 512
Q_BLOCK = 128
HGRN_HEAD_DIM = 128
HGRN_CHUNK = 64
HGRN_SUB = 16

LANES = 128
VMEM_LIMIT = 56 * 1024 * 1024


def _nt_dot(a, b):
    return lax.dot_general(a, b, (((1,), (1,)), ((), ())), preferred_element_type=F32)


def _tn_dot(a, b):
    return lax.dot_general(a, b, (((0,), (0,)), ((), ())), preferred_element_type=F32)


def _split3(x):
    hi = x.astype(BF16)
    r1 = x - hi.astype(F32)
    mid = r1.astype(BF16)
    lo = (r1 - mid.astype(F32)).astype(BF16)
    return hi, mid, lo


def _proj_kernel(x_ref, g_ref, w_ref, wg_ref, o_ref, og_ref, h_ref, *, row_chunk):
    j = pl.program_id(1)

    @pl.when(j == 0)
    def _():
        n_chunks = x_ref.shape[0] // row_chunk

        def body(c, carry):
            rows = pl.ds(pl.multiple_of(c * row_chunk, row_chunk), row_chunk)
            x = x_ref[rows, :]
            ms = jnp.mean(x * x, axis=-1, keepdims=True)
            h_ref[rows, :] = (x * lax.rsqrt(ms + EPS) * g_ref[...]).astype(BF16)
            return carry

        lax.fori_loop(0, n_chunks, body, 0)
        og_ref[...] = jnp.dot(h_ref[...], wg_ref[...], preferred_element_type=F32)

    o_ref[...] = jnp.dot(h_ref[...], w_ref[...], preferred_element_type=F32)


def _proj(x2, g, w, wg, *, tm, tn):
    M, D = x2.shape
    N = w.shape[1]
    NG = wg.shape[1]
    return pl.pallas_call(
        functools.partial(_proj_kernel, row_chunk=128),
        grid=(M // tm, N // tn),
        in_specs=[
            pl.BlockSpec((tm, D), lambda i, j: (i, 0)),
            pl.BlockSpec((1, D), lambda i, j: (0, 0)),
            pl.BlockSpec((D, tn), lambda i, j: (0, j)),
            pl.BlockSpec((D, NG), lambda i, j: (0, 0)),
        ],
        out_specs=[
            pl.BlockSpec((tm, tn), lambda i, j: (i, j)),
            pl.BlockSpec((tm, NG), lambda i, j: (i, 0)),
        ],
        out_shape=[
            jax.ShapeDtypeStruct((M, N), F32),
            jax.ShapeDtypeStruct((M, NG), F32),
        ],
        scratch_shapes=[pltpu.VMEM((tm, D), BF16)],
        compiler_params=pltpu.CompilerParams(
            dimension_semantics=("arbitrary", "arbitrary"),
            vmem_limit_bytes=VMEM_LIMIT),
        name="proj",
    )(x2, g, w, wg)


def _compress_kernel(xk_ref, xv_ref, pek_ref, w1k_ref, w2k_ref, pev_ref, w1v_ref, w2v_ref,
                     kc_ref, vc_ref, *, n_half):
    half_k = w1k_ref.shape[0] // 2

    def one(x_ref, pe_ref, w1_ref, w2_ref, o_ref):
        x = x_ref[...]
        rows = x.shape[0]
        first = jnp.dot((x + pe_ref[0:1, :]).astype(BF16), w1_ref[0:half_k, :].astype(BF16),
                        preferred_element_type=F32)
        second = jnp.dot((x + pe_ref[1:2, :]).astype(BF16), w1_ref[half_k:, :].astype(BF16),
                         preferred_element_type=F32)
        hidden = first + pltpu.roll(second, rows - 1, 0)
        out = jnp.dot(jax.nn.gelu(hidden).astype(BF16), w2_ref[...].astype(BF16),
                      preferred_element_type=F32)
        m = lax.broadcasted_iota(jnp.int32, out.shape, 0) % n_half
        o_ref[...] = jnp.where(m < n_half - 1, out, 0.0)

    one(xk_ref, pek_ref, w1k_ref, w2k_ref, kc_ref)
    one(xv_ref, pev_ref, w1v_ref, w2v_ref, vc_ref)


def _compress(xk, xv, pek, w1k, w2k, pev, w1v, w2v, *, batch, n_half):
    rows = xk.shape[0] // batch
    width = xk.shape[1]
    dh = w2k.shape[1]
    full = lambda a: pl.BlockSpec(a.shape, lambda b: (0,) * a.ndim)
    return pl.pallas_call(
        functools.partial(_compress_kernel, n_half=n_half),
        grid=(batch,),
        in_specs=[
            pl.BlockSpec((rows, width), lambda b: (b, 0)),
            pl.BlockSpec((rows, width), lambda b: (b, 0)),
            full(pek), full(w1k), full(w2k), full(pev), full(w1v), full(w2v),
        ],
        out_specs=[
            pl.BlockSpec((rows, dh), lambda b: (b, 0)),
            pl.BlockSpec((rows, dh), lambda b: (b, 0)),
        ],
        out_shape=[
            jax.ShapeDtypeStruct((batch * rows, dh), F32),
            jax.ShapeDtypeStruct((batch * rows, dh), F32),
        ],
        compiler_params=pltpu.CompilerParams(
            dimension_semantics=("arbitrary",),
            vmem_limit_bytes=VMEM_LIMIT),
        name="compress",
    )(xk, xv, pek, w1k, w2k, pev, w1v, w2v)


def _masked_softmax(lg, mask):
    l = jnp.where(mask, lg, NEG_INF)
    m = jnp.max(l, axis=-1, keepdims=True)
    e = jnp.exp(l - m)
    s = jnp.sum(e, axis=-1, keepdims=True)
    return jnp.where(mask, e / s, 0.0)


def _nsa_kernel(q_ref, z_ref, gate_ref, kslc_ref, vslc_ref, kwin_ref, vwin_ref, kc_ref, vc_ref,
                mmap_ref, expand_ref, norm_ref, o_ref, *, slopes, n_cmp):
    qi = pl.program_id(1)
    tq = q_ref.shape[0]
    seq = kslc_ref.shape[0]
    dh = NSA_HEAD_DIM
    n_groups = NSA_KV_HEADS
    rep = len(slopes) // n_groups
    n_half = kc_ref.shape[0] // n_groups
    span = WINDOW + tq
    t0 = qi * tq

    t_col = t0 + lax.broadcasted_iota(jnp.int32, (tq, 1), 0)

    n_idx = lax.broadcasted_iota(jnp.int32, (tq, n_half), 1)
    dist_c = t_col - (CMP_STRIDE * n_idx + (CMP_BLOCK - 1))
    mask_c = (dist_c >= 0) & (n_idx < n_cmp)
    dist_c_f = dist_c.astype(F32)

    pos_s = lax.broadcasted_iota(jnp.int32, (tq, seq), 1)
    dist_s = t_col - pos_s
    causal_s = dist_s >= 0
    dist_s_f = dist_s.astype(F32)

    w0 = pl.multiple_of(jnp.maximum(t0 - WINDOW, 0), tq)
    pos_w = w0 + lax.broadcasted_iota(jnp.int32, (tq, span), 1)
    dist_w = t_col - pos_w
    mask_w = (dist_w >= 0) & (dist_w < WINDOW)
    dist_w_f = dist_w.astype(F32)

    n_sel = seq // SEL_BLOCK
    j_idx = lax.broadcasted_iota(jnp.int32, (tq, LANES), 1)
    cur = t_col // SEL_BLOCK
    forced = (j_idx == 0) | (j_idx == cur) | (j_idx == cur - 1)
    future = j_idx > cur
    valid_j = j_idx < n_sel

    for g in range(n_groups):
        qg = q_ref[:, g * rep * dh:(g + 1) * rep * dh]
        q_heads = [qg[:, r * dh:(r + 1) * dh].astype(BF16) for r in range(rep)]
        kc = kc_ref[g * n_half:(g + 1) * n_half, :].astype(BF16)
        vc = vc_ref[g * n_half:(g + 1) * n_half, :].astype(BF16)

        o_cmp = []
        p_sum = jnp.zeros((tq, n_half), F32)
        for r in range(rep):
            lg = _nt_dot(q_heads[r], kc) - slopes[g * rep + r] * dist_c_f
            p = _masked_softmax(lg, mask_c)
            p_sum = p_sum + p
            o_cmp.append(jnp.dot(p.astype(BF16), vc, preferred_element_type=F32))

        mm = mmap_ref[...]
        hi, mid, lo = _split3(p_sum)
        p_slc = (jnp.dot(hi, mm, preferred_element_type=F32)
                 + jnp.dot(mid, mm, preferred_element_type=F32)
                 + jnp.dot(lo, mm, preferred_element_type=F32))
        score = jnp.where(future, -1.0, p_slc + jnp.where(forced, SEL_BONUS, 0.0))
        score = jnp.where(valid_j, score, -2.0)
        rank = jnp.zeros((tq, LANES), jnp.int32)
        for i in range(n_sel):
            col = jnp.broadcast_to(score[:, i:i + 1], (tq, LANES))
            beats = (col > score) | ((col == score) & (j_idx > i))
            rank = rank + beats.astype(jnp.int32)
        selected = jnp.where((rank < SEL_TOP) & valid_j, 1.0, 0.0).astype(BF16)
        sel_full = jnp.dot(selected, expand_ref[...], preferred_element_type=F32)
        mask_s = (sel_full > 0.5) & causal_s

        ks = kslc_ref[:, g * dh:(g + 1) * dh].astype(BF16)
        vs = vslc_ref[:, g * dh:(g + 1) * dh].astype(BF16)
        kw = kwin_ref[pl.ds(w0, span), g * dh:(g + 1) * dh].astype(BF16)
        vw = vwin_ref[pl.ds(w0, span), g * dh:(g + 1) * dh].astype(BF16)

        for r in range(rep):
            h = g * rep + r
            lg = _nt_dot(q_heads[r], ks) - slopes[h] * dist_s_f
            p = _masked_softmax(lg, mask_s)
            o_slc = jnp.dot(p.astype(BF16), vs, preferred_element_type=F32)
            lg = _nt_dot(q_heads[r], kw) - slopes[h] * dist_w_f
            p = _masked_softmax(lg, mask_w)
            o_win = jnp.dot(p.astype(BF16), vw, preferred_element_type=F32)
            gates = jax.nn.sigmoid(gate_ref[:, 3 * h:3 * h + 3])
            o = gates[:, 0:1] * o_cmp[r] + gates[:, 1:2] * o_slc + gates[:, 2:3] * o_win
            y = o * lax.rsqrt(jnp.mean(o * o, axis=-1, keepdims=True) + EPS)
            y = y * norm_ref[:, h * dh:(h + 1) * dh]
            z = z_ref[:, h * dh:(h + 1) * dh]
            o_ref[:, h * dh:(h + 1) * dh] = (y * (z * jax.nn.sigmoid(z))).astype(o_ref.dtype)


def _nsa(proj, gate, kc, vc, mmap, expand, norm, *, batch, seq, cols, slopes, n_cmp):
    tq = Q_BLOCK
    nq = seq // tq
    width = len(slopes) * NSA_HEAD_DIM
    kvw = NSA_KV_HEADS * NSA_HEAD_DIM
    rows_c = kc.shape[0] // batch
    const = lambda a: pl.BlockSpec(a.shape, lambda b, i: (0,) * a.ndim)
    kv_spec = lambda off: pl.BlockSpec((seq, kvw), lambda b, i, o=off // kvw: (b, o))
    return pl.pallas_call(
        functools.partial(_nsa_kernel, slopes=slopes, n_cmp=n_cmp),
        grid=(batch, nq),
        in_specs=[
            pl.BlockSpec((tq, width), lambda b, i, o=cols["q_a"] // width: (b * nq + i, o)),
            pl.BlockSpec((tq, width), lambda b, i, o=cols["z_a"] // width: (b * nq + i, o)),
            pl.BlockSpec((tq, gate.shape[1]), lambda b, i: (b * nq + i, 0)),
            kv_spec(cols["k_slc"]), kv_spec(cols["v_slc"]),
            kv_spec(cols["k_win"]), kv_spec(cols["v_win"]),
            pl.BlockSpec((rows_c, NSA_HEAD_DIM), lambda b, i: (b, 0)),
            pl.BlockSpec((rows_c, NSA_HEAD_DIM), lambda b, i: (b, 0)),
            const(mmap), const(expand), const(norm),
        ],
        out_specs=pl.BlockSpec((tq, width), lambda b, i: (b * nq + i, 0)),
        out_shape=jax.ShapeDtypeStruct((batch * seq, width), BF16),
        compiler_params=pltpu.CompilerParams(
            dimension_semantics=("arbitrary", "arbitrary"),
            vmem_limit_bytes=VMEM_LIMIT),
        name="nsa",
    )(proj, proj, gate, proj, proj, proj, proj, kc, vc, mmap, expand, norm)


def _hgrn_kernel(q_ref, f_ref, v_ref, z_ref, lb_ref, norm_ref, cum_ref, o_ref, st_ref):
    C, SUB = HGRN_CHUNK, HGRN_SUB
    n_sub = C // SUB
    seq = q_ref.shape[0]

    lbr = lb_ref[...]
    e = jnp.exp(lbr - jnp.max(lbr, axis=0, keepdims=True))
    lb = e[0:1, :] / jnp.sum(e, axis=0, keepdims=True)
    gain = norm_ref[...]
    cum = cum_ref[...]

    st_ref[...] = jnp.zeros_like(st_ref)

    def chunk(c, carry):
        rows = pl.ds(pl.multiple_of(c * C, C), C)
        q = q_ref[rows, :]
        v = v_ref[rows, :]
        f = lb + (1.0 - lb) * jax.nn.sigmoid(f_ref[rows, :])
        k = 1.0 - f
        lf = jnp.log(f)
        hi, mid, lo = _split3(lf)
        sums = (jnp.dot(cum, hi, preferred_element_type=F32)
                + jnp.dot(cum, mid, preferred_element_type=F32)
                + jnp.dot(cum, lo, preferred_element_type=F32))
        b = sums[0:C]
        a = sums[C:2 * C]
        to_sub_end = sums[2 * C:3 * C]
        to_end = sums[3 * C:4 * C]

        st = st_ref[...]
        v16 = v.astype(BF16)
        o_inter = _nt_dot((q * jnp.exp(b)).astype(BF16), st.astype(BF16))
        q1 = q * jnp.exp(a)
        k1 = k * jnp.exp(-a)
        k2 = k * jnp.exp(to_sub_end)
        outs = []
        for i in range(n_sub):
            parts = []
            for jb in range(i):
                kj = k2[jb * SUB:(jb + 1) * SUB]
                if jb < i - 1:
                    kj = kj * jnp.exp(b[i * SUB - 1:i * SUB] - b[(jb + 1) * SUB - 1:(jb + 1) * SUB])
                parts.append(kj)
            parts.append(k1[i * SUB:(i + 1) * SUB])
            rhs = jnp.concatenate(parts, axis=0) if len(parts) > 1 else parts[0]
            att = _nt_dot(q1[i * SUB:(i + 1) * SUB].astype(BF16), rhs.astype(BF16))
            width = (i + 1) * SUB
            keep = (lax.broadcasted_iota(jnp.int32, (SUB, width), 1)
                    <= lax.broadcasted_iota(jnp.int32, (SUB, width), 0) + i * SUB)
            att = jnp.where(keep, att, 0.0)
            outs.append(jnp.dot(att.astype(BF16), v16[0:(i + 1) * SUB], preferred_element_type=F32))
        o = o_inter + jnp.concatenate(outs, axis=0)

        k3 = (k * jnp.exp(to_end)).astype(BF16)
        st_ref[...] = st * jnp.exp(b[C - 1:C]) + _tn_dot(v16, k3)

        y = o * lax.rsqrt(jnp.mean(o * o, axis=-1, keepdims=True) + EPS) * gain
        z = z_ref[rows, :]
        o_ref[rows, :] = (y * (z * jax.nn.sigmoid(z))).astype(o_ref.dtype)
        return carry

    lax.fori_loop(0, seq // C, chunk, 0)


def _hgrn_patterns():
    C, SUB = HGRN_CHUNK, HGRN_SUB
    t = np.arange(C)[:, None]
    s = np.arange(C)[None, :]
    same = (t // SUB) == (s // SUB)
    incl = s <= t
    sub_incl = same & (s <= t)
    sub_after = same & (s > t)
    after = s > t
    return np.concatenate([incl, sub_incl, sub_after, after], axis=0).astype(np.float32)


def _hgrn(proj, lower_bounds, norm, *, batch, seq, cols):
    dk = HGRN_HEAD_DIM
    heads = norm.shape[1] // dk
    cum = jnp.asarray(_hgrn_patterns(), dtype=BF16)
    col = lambda name: (lambda b, h, o=cols[name] // dk: (b, o + h))
    return pl.pallas_call(
        _hgrn_kernel,
        grid=(batch, heads),
        in_specs=[
            pl.BlockSpec((seq, dk), col("q_h")),
            pl.BlockSpec((seq, dk), col("f_h")),
            pl.BlockSpec((seq, dk), col("i_h")),
            pl.BlockSpec((seq, dk), col("z_h")),
            pl.BlockSpec((lower_bounds.shape[0], dk), lambda b, h: (0, h)),
            pl.BlockSpec((1, dk), lambda b, h: (0, h)),
            pl.BlockSpec(cum.shape, lambda b, h: (0, 0)),
        ],
        out_specs=pl.BlockSpec((seq, dk), lambda b, h: (b, h)),
        out_shape=jax.ShapeDtypeStruct((batch * seq, heads * dk), BF16),
        scratch_shapes=[pltpu.VMEM((dk, dk), F32)],
        compiler_params=pltpu.CompilerParams(
            dimension_semantics=("arbitrary", "arbitrary"),
            vmem_limit_bytes=VMEM_LIMIT),
        name="hgrn",
    )(proj, proj, proj, proj, lower_bounds, norm, cum)


def _outproj_kernel(x_ref, oa_ref, oh_ref, wa_ref, wh_ref, g_ref, o_ref):
    y = x_ref[...] + jnp.dot(oa_ref[...], wa_ref[...], preferred_element_type=F32)
    y = y + jnp.dot(oh_ref[...], wh_ref[...], preferred_element_type=F32)
    ms = jnp.mean(y * y, axis=-1, keepdims=True)
    o_ref[...] = y * lax.rsqrt(ms + EPS) * g_ref[...]


def _outproj(x2, o_a, o_h, wa, wh, g, *, tm):
    M, D = x2.shape
    const = lambda a: pl.BlockSpec(a.shape, lambda i: (0,) * a.ndim)
    return pl.pallas_call(
        _outproj_kernel,
        grid=(M // tm,),
        in_specs=[
            pl.BlockSpec((tm, D), lambda i: (i, 0)),
            pl.BlockSpec((tm, o_a.shape[1]), lambda i: (i, 0)),
            pl.BlockSpec((tm, o_h.shape[1]), lambda i: (i, 0)),
            const(wa), const(wh), const(g),
        ],
        out_specs=pl.BlockSpec((tm, D), lambda i: (i, 0)),
        out_shape=jax.ShapeDtypeStruct((M, D), F32),
        compiler_params=pltpu.CompilerParams(
            dimension_semantics=("arbitrary",),
            vmem_limit_bytes=VMEM_LIMIT),
        name="outproj",
    )(x2, o_a, o_h, wa, wh, g)


def kernel(x, norm_in, w_in, cmp_pe_k, cmp_w1_k, cmp_w2_k, cmp_pe_v, cmp_w1_v, cmp_w2_v,
           lower_bounds, nsa_out_norm, hgrn_out_norm, w_out, final_norm):
    B, S, D = x.shape
    assert norm_in.shape[0] == 1, "single-layer problem"
    nsa_w = nsa_out_norm.shape[1]
    hgrn_w = hgrn_out_norm.shape[1]
    dh = NSA_HEAD_DIM
    heads = nsa_w // dh
    kvw = NSA_KV_HEADS * dh
    n_gate = 3 * heads
    n_cmp = (S - CMP_BLOCK) // CMP_STRIDE + 1
    n_half = S // CMP_STRIDE
    n_sel = S // SEL_BLOCK
    assert n_half <= LANES and S % Q_BLOCK == 0 and S >= WINDOW + Q_BLOCK

    names = ["q_a", "k_cmp", "v_cmp", "k_slc", "v_slc", "k_win", "v_win", "gate", "z_a",
             "q_h", "f_h", "i_h", "z_h"]
    widths = [nsa_w] + [kvw] * 6 + [n_gate, nsa_w] + [hgrn_w] * 4
    starts = dict(zip(names, np.cumsum([0] + widths[:-1]).tolist()))
    wd = dict(zip(names, widths))
    order = ["q_a", "z_a", "k_cmp", "v_cmp", "k_slc", "v_slc", "k_win", "v_win",
             "q_h", "f_h", "i_h", "z_h"]
    cols, off = {}, 0
    for n in order:
        cols[n] = off
        off += wd[n]
    n_main = off

    w = w_in[0]
    piece = lambda n: w[:, starts[n]:starts[n] + wd[n]]
    w_main = jnp.concatenate(
        [piece(n) * (dh ** -0.5) if n == "q_a" else piece(n) for n in order], axis=1).astype(BF16)
    w_gate = jnp.pad(piece("gate"), ((0, 0), (0, LANES - n_gate))).astype(BF16)

    x2 = x.reshape(B * S, D)
    proj, gate = _proj(x2, norm_in, w_main, w_gate, tm=min(1024, B * S), tn=1536)

    def half_blocks(name):
        a = proj[:, cols[name]:cols[name] + kvw].reshape(B, n_half, CMP_STRIDE, NSA_KV_HEADS, dh)
        return a.transpose(0, 3, 1, 2, 4).reshape(B * NSA_KV_HEADS * n_half, CMP_STRIDE * dh)

    pe2 = lambda pe: pe[0].reshape(2, CMP_STRIDE * dh)
    kc, vc = _compress(half_blocks("k_cmp"), half_blocks("v_cmp"),
                       pe2(cmp_pe_k), cmp_w1_k[0], cmp_w2_k[0],
                       pe2(cmp_pe_v), cmp_w1_v[0], cmp_w2_v[0], batch=B, n_half=n_half)

    cs = CMP_STRIDE * np.arange(n_half)[:, None]
    ss = SEL_BLOCK * np.arange(LANES)[None, :]
    overlap = np.clip(np.minimum(cs + CMP_BLOCK, ss + SEL_BLOCK) - np.maximum(cs, ss), 0, None)
    mmap = (overlap / CMP_BLOCK) * (np.arange(n_half)[:, None] < n_cmp) * (np.arange(LANES)[None, :] < n_sel)
    expand = (np.arange(LANES)[:, None] == (np.arange(S)[None, :] // SEL_BLOCK))
    slopes = tuple(float(s) for s in (2.0 ** (-8.0 * np.arange(1, heads + 1) / heads)).astype(np.float32))

    o_a = _nsa(proj, gate, kc, vc, jnp.asarray(mmap, dtype=BF16), jnp.asarray(expand, dtype=BF16),
               nsa_out_norm, batch=B, seq=S, cols=cols, slopes=slopes, n_cmp=n_cmp)
    o_h = _hgrn(proj, lower_bounds, hgrn_out_norm, batch=B, seq=S, cols=cols)

    wo = w_out[0].astype(BF16)
    out = _outproj(x2, o_a, o_h, wo[:nsa_w], wo[nsa_w:], final_norm.reshape(1, D), tm=512)
    return out.reshape(B, S, D)
```

```python
import functools

import ml_dtypes
import numpy as np
import jax
import jax.numpy as jnp
from jax import lax
from jax.experimental import pallas as pl
from jax.experimental.pallas import tpu as pltpu

F32 = jnp.float32
BF16 = jnp.bfloat16

EPS = 1e-6
NEG_INF = -1e30
LOG2E = 1.4426950408889634

NSA_HEAD_DIM = 64
NSA_KV_HEADS = 4
CMP_BLOCK = 32
CMP_STRIDE = 16
SEL_BLOCK = 64
SEL_TOP = 8
SEL_BONUS = 1.0e4
WINDOW = 512
HGRN_HEAD_DIM = 128
HGRN_CHUNK = 64
HGRN_SUB = 16

LANES = 128
VMEM_LIMIT = 56 * 1024 * 1024
NSA_TILE = 256

SEL_LANE0 = NSA_HEAD_DIM
MAX_SEL_BLOCKS = 32
POS_LANE0 = SEL_LANE0 + MAX_SEL_BLOCKS
N_SPLIT = 4
FLAG_LANE = POS_LANE0 + 2 * N_SPLIT


def _nt_dot(a, b):
    return lax.dot_general(a, b, (((1,), (1,)), ((), ())), preferred_element_type=F32)


def _tn_dot(a, b):
    return lax.dot_general(a, b, (((0,), (0,)), ((), ())), preferred_element_type=F32)


def _split3(x):
    hi = x.astype(BF16)
    r1 = x - hi.astype(F32)
    mid = r1.astype(BF16)
    lo = (r1 - mid.astype(F32)).astype(BF16)
    return hi, mid, lo


def _proj_kernel(x_ref, g_ref, w_ref, wg_ref, o_ref, og_ref, h_ref, *, row_chunk):
    j = pl.program_id(1)

    @pl.when(j == 0)
    def _():
        n_chunks = x_ref.shape[0] // row_chunk

        def body(c, carry):
            rows = pl.ds(pl.multiple_of(c * row_chunk, row_chunk), row_chunk)
            x = x_ref[rows, :]
            ms = jnp.mean(x * x, axis=-1, keepdims=True)
            h_ref[rows, :] = (x * lax.rsqrt(ms + EPS) * g_ref[...]).astype(BF16)
            return carry

        lax.fori_loop(0, n_chunks, body, 0)
        og_ref[...] = jnp.dot(h_ref[...], wg_ref[...], preferred_element_type=F32)

    o_ref[...] = jnp.dot(h_ref[...], w_ref[...], preferred_element_type=F32)


def _proj(x2, g, w, wg, *, tm, tn):
    M, D = x2.shape
    N = w.shape[1]
    NG = wg.shape[1]
    return pl.pallas_call(
        functools.partial(_proj_kernel, row_chunk=128),
        grid=(M // tm, N // tn),
        in_specs=[
            pl.BlockSpec((tm, D), lambda i, j: (i, 0)),
            pl.BlockSpec((1, D), lambda i, j: (0, 0)),
            pl.BlockSpec((D, tn), lambda i, j: (0, j)),
            pl.BlockSpec((D, NG), lambda i, j: (0, 0)),
        ],
        out_specs=[
            pl.BlockSpec((tm, tn), lambda i, j: (i, j)),
            pl.BlockSpec((tm, NG), lambda i, j: (i, 0)),
        ],
        out_shape=[
            jax.ShapeDtypeStruct((M, N), F32),
            jax.ShapeDtypeStruct((M, NG), F32),
        ],
        scratch_shapes=[pltpu.VMEM((tm, D), BF16)],
        compiler_params=pltpu.CompilerParams(
            dimension_semantics=("arbitrary", "arbitrary"),
            vmem_limit_bytes=VMEM_LIMIT),
        name="proj",
    )(x2, g, w, wg)


def _compress_kernel(c0_ref, c1_ref, c2_ref, c3_ref, pe_ref, wp_ref, wq_ref, w2_ref, feat_ref,
                     kc_ref, vc_ref, x_ref, *, n_half):
    for g, c_ref in enumerate((c0_ref, c1_ref, c2_ref, c3_ref)):
        for l in range(CMP_STRIDE):
            x_ref[g * n_half:(g + 1) * n_half, l * LANES:(l + 1) * LANES] = (
                c_ref[pl.ds(l, n_half, stride=CMP_STRIDE), :])
    x = x_ref[...]
    rows = x.shape[0]
    first = jnp.dot((x + pe_ref[0:1, :]).astype(BF16), wp_ref[...], preferred_element_type=F32)
    second = jnp.dot((x + pe_ref[1:2, :]).astype(BF16), wq_ref[...], preferred_element_type=F32)
    hidden = first + pltpu.roll(second, rows - 1, 0)
    out = jnp.dot(jax.nn.gelu(hidden).astype(BF16), w2_ref[...], preferred_element_type=F32)
    lane = lax.broadcasted_iota(jnp.int32, out.shape, 1)
    kc_ref[...] = jnp.where(lane < NSA_HEAD_DIM, out.astype(BF16), feat_ref[...])
    vc_ref[...] = jnp.where(lane < NSA_HEAD_DIM, pltpu.roll(out, NSA_HEAD_DIM, 1), out).astype(BF16)


def _compress(proj, pe, wp, wq, w2, feat, *, batch, seq, col0, n_half):
    rows = NSA_KV_HEADS * n_half
    const = lambda a: pl.BlockSpec(a.shape, lambda b: (0,) * a.ndim)
    stream = lambda g: pl.BlockSpec((seq, LANES), lambda b, o=col0 // LANES + g: (b, o))
    return pl.pallas_call(
        functools.partial(_compress_kernel, n_half=n_half),
        grid=(batch,),
        in_specs=[stream(0), stream(1), stream(2), stream(3),
                  const(pe), const(wp), const(wq), const(w2), const(feat)],
        out_specs=[
            pl.BlockSpec((rows, LANES), lambda b: (b, 0)),
            pl.BlockSpec((rows, LANES), lambda b: (b, 0)),
        ],
        out_shape=[
            jax.ShapeDtypeStruct((batch * rows, LANES), BF16),
            jax.ShapeDtypeStruct((batch * rows, LANES), BF16),
        ],
        scratch_shapes=[pltpu.VMEM((rows, CMP_STRIDE * LANES), F32)],
        compiler_params=pltpu.CompilerParams(
            dimension_semantics=("arbitrary",),
            vmem_limit_bytes=VMEM_LIMIT),
        name="compress",
    )(proj, proj, proj, proj, pe, wp, wq, w2, feat)


def _nsa_kernel(q_ref, z_ref, gate_ref, slc_ref, win_ref, kc_ref, vc_ref, featk_ref, featw_ref,
                qfeat_ref, mmap_ref, place_ref, norm_ref, o_ref,
                ksel_ref, vsel_ref, kwin_ref, vwin_ref, *, n_cmp):
    g = pl.program_id(1)
    qi = pl.program_id(2)
    tq = q_ref.shape[0]
    tk = tq
    seq = slc_ref.shape[0]
    dh = NSA_HEAD_DIM
    rep = q_ref.shape[1] // dh
    n_half = kc_ref.shape[0]
    t0 = qi * tq

    @pl.when(qi == 0)
    def _():
        lane = lax.broadcasted_iota(jnp.int32, (seq, LANES), 1)
        kv = slc_ref[...]
        ksel_ref[...] = jnp.where(lane < dh, kv.astype(BF16), featk_ref[...])
        vsel_ref[...] = jnp.where(lane < dh, pltpu.roll(kv, dh, 1), kv).astype(BF16)
        kv = win_ref[...]
        kwin_ref[0:WINDOW, :] = featw_ref[0:WINDOW, :]
        kwin_ref[WINDOW:, :] = jnp.where(lane < dh, kv.astype(BF16), featw_ref[WINDOW:, :])
        vwin_ref[0:WINDOW, :] = jnp.zeros((WINDOW, LANES), BF16)
        vwin_ref[WINDOW:, :] = jnp.where(lane < dh, pltpu.roll(kv, dh, 1), kv).astype(BF16)

    lane = lax.broadcasted_iota(jnp.int32, (tq, LANES), 1)
    row = lax.broadcasted_iota(jnp.int32, (tq, tk), 0)
    col = lax.broadcasted_iota(jnp.int32, (tq, tk), 1)
    stack = lambda a: jnp.concatenate([a] * rep, axis=0)
    causal_bias = stack(jnp.where(col <= row, 0.0, NEG_INF))
    band_bias = stack(jnp.where(col > row, 0.0, NEG_INF))

    qfeat = qfeat_ref[0]
    qa = []
    for r in range(rep):
        qcol = q_ref[:, (r // 2) * LANES:(r // 2 + 1) * LANES]
        if r % 2:
            qcol = pltpu.roll(qcol, dh, 1)
        qa.append(jnp.where(lane < dh, qcol * LOG2E, qfeat[r:r + 1, :]))
    q0 = jnp.concatenate(qa, axis=0).astype(BF16)

    n_idx = lax.broadcasted_iota(jnp.int32, (tq, n_half), 1)
    t_idx = t0 + lax.broadcasted_iota(jnp.int32, (tq, n_half), 0)
    valid_c = stack((CMP_STRIDE * n_idx + (CMP_BLOCK - 1) <= t_idx) & (n_idx < n_cmp))
    lg = jnp.where(valid_c, _nt_dot(q0, kc_ref[...]), NEG_INF)
    e = jnp.exp2(lg - jnp.max(lg, axis=-1, keepdims=True))
    p = jnp.where(valid_c, e / jnp.sum(e, axis=-1, keepdims=True), 0.0)
    o_cmp = jnp.dot(p.astype(BF16), vc_ref[...], preferred_element_type=F32)
    p_sum = p[0:tq]
    for r in range(1, rep):
        p_sum = p_sum + p[r * tq:(r + 1) * tq]

    mm = mmap_ref[...]
    hi, mid, lo = _split3(p_sum)
    p_slc = (_nt_dot(mm, hi) + _nt_dot(mm, mid) + _nt_dot(mm, lo))[0:MAX_SEL_BLOCKS]
    jj = lax.broadcasted_iota(jnp.int32, (MAX_SEL_BLOCKS, tq), 0)
    tt = t0 + lax.broadcasted_iota(jnp.int32, (MAX_SEL_BLOCKS, tq), 1)
    cur = lax.shift_right_logical(tt, int(np.log2(SEL_BLOCK)))
    forced = (jj == 0) | (jj == cur) | (jj == cur - 1)
    future = jj > cur
    score = jnp.where(future, -1.0, p_slc + jnp.where(forced, SEL_BONUS, 0.0))
    rank = jnp.zeros((MAX_SEL_BLOCKS, tq), jnp.int32)
    for i in range(MAX_SEL_BLOCKS):
        other = jnp.broadcast_to(score[i:i + 1, :], score.shape)
        beats = (other > score) | ((other == score) & (jj > i))
        rank = rank + beats.astype(jnp.int32)
    chosen = jnp.where((rank < SEL_TOP) & jnp.logical_not(future), 1.0, 0.0).astype(BF16)
    placed = _tn_dot(chosen, place_ref[...])
    sel_lanes = (lane >= SEL_LANE0) & (lane < SEL_LANE0 + MAX_SEL_BLOCKS)
    sel_bias = (placed - 1.0) * (-NEG_INF)
    qs = jnp.concatenate([jnp.where(sel_lanes, sel_bias, a) for a in qa], axis=0).astype(BF16)

    def sel_tile(kt, carry, bias):
        m, l, acc = carry
        rows = pl.ds(pl.multiple_of(kt * tk, tk), tk)
        lg = _nt_dot(qs, ksel_ref[rows, :])
        if bias is not None:
            lg = lg + bias
        m_new = jnp.maximum(m, jnp.max(lg, axis=-1, keepdims=True))
        alpha = jnp.exp2(m - m_new)
        pt = jnp.exp2(lg - m_new)
        l = alpha * l + jnp.sum(pt, axis=-1, keepdims=True)
        acc = alpha * acc + jnp.dot(pt.astype(BF16), vsel_ref[rows, :], preferred_element_type=F32)
        return m_new, l, acc

    init = (jnp.full((rep * tq, 1), NEG_INF, F32), jnp.zeros((rep * tq, 1), F32),
            jnp.zeros((rep * tq, LANES), F32))
    carry = lax.fori_loop(0, qi, lambda kt, c: sel_tile(kt, c, None), init)
    _, l, acc = sel_tile(qi, carry, causal_bias)
    o_slc = acc / l

    n_wt = WINDOW // tk + 1
    lgs = []
    for w in range(n_wt):
        rows = pl.ds(pl.multiple_of(t0 + w * tk, tk), tk)
        lg = _nt_dot(qs, kwin_ref[rows, :])
        if w == 0:
            lg = lg + band_bias
        if w == n_wt - 1:
            lg = lg + causal_bias
        lgs.append(lg)
    m = functools.reduce(jnp.maximum, [jnp.max(lg, axis=-1, keepdims=True) for lg in lgs])
    l = jnp.zeros((rep * tq, 1), F32)
    acc = jnp.zeros((rep * tq, LANES), F32)
    for w in range(n_wt):
        rows = pl.ds(pl.multiple_of(t0 + w * tk, tk), tk)
        pt = jnp.exp2(lgs[w] - m)
        l = l + jnp.sum(pt, axis=-1, keepdims=True)
        acc = acc + jnp.dot(pt.astype(BF16), vwin_ref[rows, :], preferred_element_type=F32)
    o_win = acc / l

    n_gate = 3 * rep
    gates = pltpu.roll(jax.nn.sigmoid(gate_ref[...]), jnp.where(g == 0, 0, LANES - n_gate * g), 1)
    ys = []
    for r in range(rep):
        sl = slice(r * tq, (r + 1) * tq)
        o = (gates[:, 3 * r:3 * r + 1] * o_cmp[sl] + gates[:, 3 * r + 1:3 * r + 2] * o_slc[sl]
             + gates[:, 3 * r + 2:3 * r + 3] * o_win[sl])
        ys.append(o * lax.rsqrt(jnp.mean(o * o, axis=-1, keepdims=True) + EPS))
    for pair in range(rep // 2):
        cols = slice(pair * LANES, (pair + 1) * LANES)
        y = jnp.where(lane < dh, ys[2 * pair], ys[2 * pair + 1]) * norm_ref[:, cols]
        z = z_ref[:, cols]
        o_ref[:, cols] = (y * (z * jax.nn.sigmoid(z))).astype(o_ref.dtype)


def _nsa(proj, gate, kc, vc, featk, featw, qfeat, mmap, place, norm, *, batch, seq, cols, n_cmp):
    tq = NSA_TILE
    nq = seq // tq
    G = NSA_KV_HEADS
    gw = norm.shape[1] // G
    n_half = kc.shape[0] // (batch * G)
    const = lambda a: pl.BlockSpec(a.shape, lambda b, g, i: (0,) * a.ndim)
    stream = lambda name: pl.BlockSpec((seq, LANES), lambda b, g, i, o=cols[name] // LANES: (b, o + g))
    return pl.pallas_call(
        functools.partial(_nsa_kernel, n_cmp=n_cmp),
        grid=(batch, G, nq),
        in_specs=[
            pl.BlockSpec((tq, gw), lambda b, g, i, o=cols["q_a"] // gw: (b * nq + i, o + g)),
            pl.BlockSpec((tq, gw), lambda b, g, i, o=cols["z_a"] // gw: (b * nq + i, o + g)),
            pl.BlockSpec((tq, gate.shape[1]), lambda b, g, i: (b * nq + i, 0)),
            stream("slc"), stream("win"),
            pl.BlockSpec((n_half, LANES), lambda b, g, i: (b * G + g, 0)),
            pl.BlockSpec((n_half, LANES), lambda b, g, i: (b * G + g, 0)),
            const(featk), const(featw),
            pl.BlockSpec((1,) + qfeat.shape[1:], lambda b, g, i: (g, 0, 0)),
            const(mmap), const(place),
            pl.BlockSpec((1, gw), lambda b, g, i: (0, g)),
        ],
        out_specs=pl.BlockSpec((tq, gw), lambda b, g, i: (b * nq + i, g)),
        out_shape=jax.ShapeDtypeStruct((batch * seq, G * gw), BF16),
        scratch_shapes=[
            pltpu.VMEM((seq, LANES), BF16), pltpu.VMEM((seq, LANES), BF16),
            pltpu.VMEM((seq + WINDOW, LANES), BF16), pltpu.VMEM((seq + WINDOW, LANES), BF16),
        ],
        compiler_params=pltpu.CompilerParams(
            dimension_semantics=("arbitrary", "arbitrary", "arbitrary"),
            vmem_limit_bytes=VMEM_LIMIT),
        name="nsa",
    )(proj, proj, gate, proj, proj, kc, vc, featk, featw, qfeat, mmap, place, norm)


def _bf16_terms(x, n):
    terms, rest = [], np.asarray(x, np.float64)
    for _ in range(n):
        t = rest.astype(np.float32).astype(ml_dtypes.bfloat16).astype(np.float64)
        terms.append(t.astype(np.float32))
        rest = rest - t
    return terms


def _nsa_tables(seq, n_half, n_cmp, heads):
    assert seq // SEL_BLOCK <= MAX_SEL_BLOCKS and FLAG_LANE < LANES

    def key_features(pos, onehot_blocks):
        f = np.zeros((len(pos), LANES), np.float32)
        if onehot_blocks:
            f[np.arange(len(pos)), SEL_LANE0 + pos // SEL_BLOCK] = 1.0
        f[:, POS_LANE0:POS_LANE0 + N_SPLIT] = ((pos // 64) * 64)[:, None]
        f[:, POS_LANE0 + N_SPLIT:POS_LANE0 + 2 * N_SPLIT] = (pos % 64)[:, None]
        return f

    featk = key_features(np.arange(seq), True)
    featw = np.concatenate([np.zeros((WINDOW, LANES), np.float32), key_features(np.arange(seq), False)])
    featw[:WINDOW, FLAG_LANE] = NEG_INF
    featc = key_features(CMP_STRIDE * np.arange(n_half) + CMP_BLOCK - 1, False)
    featc[n_cmp:, FLAG_LANE] = NEG_INF

    slopes = (2.0 ** (-8.0 * np.arange(1, heads + 1) / heads)).astype(np.float32).astype(np.float64)
    terms = _bf16_terms(slopes * LOG2E, N_SPLIT)
    rep = heads // NSA_KV_HEADS
    qfeat = np.zeros((NSA_KV_HEADS, 8, LANES), np.float32)
    for h in range(heads):
        for i, t in enumerate(terms):
            qfeat[h // rep, h % rep, POS_LANE0 + i] = t[h]
            qfeat[h // rep, h % rep, POS_LANE0 + N_SPLIT + i] = t[h]
        qfeat[h // rep, h % rep, FLAG_LANE] = 1.0

    cs = CMP_STRIDE * np.arange(n_half)[None, :]
    ss = SEL_BLOCK * np.arange(LANES)[:, None]
    overlap = np.clip(np.minimum(cs + CMP_BLOCK, ss + SEL_BLOCK) - np.maximum(cs, ss), 0, None)
    mmap = (overlap / CMP_BLOCK) * (np.arange(n_half)[None, :] < n_cmp) * (ss < seq)
    place = np.zeros((MAX_SEL_BLOCKS, LANES), np.float32)
    place[np.arange(MAX_SEL_BLOCKS), SEL_LANE0 + np.arange(MAX_SEL_BLOCKS)] = 1.0
    bf = lambda a: jnp.asarray(a, dtype=BF16)
    return bf(featk), bf(featw), bf(featc), jnp.asarray(qfeat), bf(mmap), bf(place)


def _hgrn_kernel(q_ref, f_ref, v_ref, z_ref, lb_ref, norm_ref, cum_ref, o_ref, st_ref):
    C, SUB = HGRN_CHUNK, HGRN_SUB
    n_sub = C // SUB
    seq = q_ref.shape[0]

    lbr = lb_ref[...]
    e = jnp.exp(lbr - jnp.max(lbr, axis=0, keepdims=True))
    lb = e[0:1, :] / jnp.sum(e, axis=0, keepdims=True)
    gain = norm_ref[...]
    cum = cum_ref[...]

    st_ref[...] = jnp.zeros_like(st_ref)

    def chunk(c, carry):
        rows = pl.ds(pl.multiple_of(c * C, C), C)
        q = q_ref[rows, :]
        v = v_ref[rows, :]
        f = lb + (1.0 - lb) * jax.nn.sigmoid(f_ref[rows, :])
        k = 1.0 - f
        lf = jnp.log(f)
        hi, mid, lo = _split3(lf)
        sums = (jnp.dot(cum, hi, preferred_element_type=F32)
                + jnp.dot(cum, mid, preferred_element_type=F32)
                + jnp.dot(cum, lo, preferred_element_type=F32))
        b = sums[0:C]
        a = sums[C:2 * C]
        to_sub_end = sums[2 * C:3 * C]
        to_end = sums[3 * C:4 * C]

        st = st_ref[...]
        v16 = v.astype(BF16)
        o_inter = _nt_dot((q * jnp.exp(b)).astype(BF16), st.astype(BF16))
        q1 = q * jnp.exp(a)
        k1 = k * jnp.exp(-a)
        k2 = k * jnp.exp(to_sub_end)
        outs = []
        for i in range(n_sub):
            parts = []
            for jb in range(i):
                kj = k2[jb * SUB:(jb + 1) * SUB]
                if jb < i - 1:
                    kj = kj * jnp.exp(b[i * SUB - 1:i * SUB] - b[(jb + 1) * SUB - 1:(jb + 1) * SUB])
                parts.append(kj)
            parts.append(k1[i * SUB:(i + 1) * SUB])
            rhs = jnp.concatenate(parts, axis=0) if len(parts) > 1 else parts[0]
            att = _nt_dot(q1[i * SUB:(i + 1) * SUB].astype(BF16), rhs.astype(BF16))
            width = (i + 1) * SUB
            keep = (lax.broadcasted_iota(jnp.int32, (SUB, width), 1)
                    <= lax.broadcasted_iota(jnp.int32, (SUB, width), 0) + i * SUB)
            att = jnp.where(keep, att, 0.0)
            outs.append(jnp.dot(att.astype(BF16), v16[0:(i + 1) * SUB], preferred_element_type=F32))
        o = o_inter + jnp.concatenate(outs, axis=0)

        k3 = (k * jnp.exp(to_end)).astype(BF16)
        st_ref[...] = st * jnp.exp(b[C - 1:C]) + _tn_dot(v16, k3)

        y = o * lax.rsqrt(jnp.mean(o * o, axis=-1, keepdims=True) + EPS) * gain
        z = z_ref[rows, :]
        o_ref[rows, :] = (y * (z * jax.nn.sigmoid(z))).astype(o_ref.dtype)
        return carry

    lax.fori_loop(0, seq // C, chunk, 0)


def _hgrn_patterns():
    C, SUB = HGRN_CHUNK, HGRN_SUB
    t = np.arange(C)[:, None]
    s = np.arange(C)[None, :]
    same = (t // SUB) == (s // SUB)
    incl = s <= t
    sub_incl = same & (s <= t)
    sub_after = same & (s > t)
    after = s > t
    return np.concatenate([incl, sub_incl, sub_after, after], axis=0).astype(np.float32)


def _hgrn(proj, lower_bounds, norm, *, batch, seq, cols):
    dk = HGRN_HEAD_DIM
    heads = norm.shape[1] // dk
    cum = jnp.asarray(_hgrn_patterns(), dtype=BF16)
    col = lambda name: (lambda b, h, o=cols[name] // dk: (b, o + h))
    return pl.pallas_call(
        _hgrn_kernel,
        grid=(batch, heads),
        in_specs=[
            pl.BlockSpec((seq, dk), col("q_h")),
            pl.BlockSpec((seq, dk), col("f_h")),
            pl.BlockSpec((seq, dk), col("i_h")),
            pl.BlockSpec((seq, dk), col("z_h")),
            pl.BlockSpec((lower_bounds.shape[0], dk), lambda b, h: (0, h)),
            pl.BlockSpec((1, dk), lambda b, h: (0, h)),
            pl.BlockSpec(cum.shape, lambda b, h: (0, 0)),
        ],
        out_specs=pl.BlockSpec((seq, dk), lambda b, h: (b, h)),
        out_shape=jax.ShapeDtypeStruct((batch * seq, heads * dk), BF16),
        scratch_shapes=[pltpu.VMEM((dk, dk), F32)],
        compiler_params=pltpu.CompilerParams(
            dimension_semantics=("arbitrary", "arbitrary"),
            vmem_limit_bytes=VMEM_LIMIT),
        name="hgrn",
    )(proj, proj, proj, proj, lower_bounds, norm, cum)


def _outproj_kernel(x_ref, oa_ref, oh_ref, wa_ref, wh_ref, g_ref, o_ref):
    y = x_ref[...] + jnp.dot(oa_ref[...], wa_ref[...], preferred_element_type=F32)
    y = y + jnp.dot(oh_ref[...], wh_ref[...], preferred_element_type=F32)
    ms = jnp.mean(y * y, axis=-1, keepdims=True)
    o_ref[...] = y * lax.rsqrt(ms + EPS) * g_ref[...]


def _outproj(x2, o_a, o_h, wa, wh, g, *, tm):
    M, D = x2.shape
    const = lambda a: pl.BlockSpec(a.shape, lambda i: (0,) * a.ndim)
    return pl.pallas_call(
        _outproj_kernel,
        grid=(M // tm,),
        in_specs=[
            pl.BlockSpec((tm, D), lambda i: (i, 0)),
            pl.BlockSpec((tm, o_a.shape[1]), lambda i: (i, 0)),
            pl.BlockSpec((tm, o_h.shape[1]), lambda i: (i, 0)),
            const(wa), const(wh), const(g),
        ],
        out_specs=pl.BlockSpec((tm, D), lambda i: (i, 0)),
        out_shape=jax.ShapeDtypeStruct((M, D), F32),
        compiler_params=pltpu.CompilerParams(
            dimension_semantics=("arbitrary",),
            vmem_limit_bytes=VMEM_LIMIT),
        name="outproj",
    )(x2, o_a, o_h, wa, wh, g)


def kernel(x, norm_in, w_in, cmp_pe_k, cmp_w1_k, cmp_w2_k, cmp_pe_v, cmp_w1_v, cmp_w2_v,
           lower_bounds, nsa_out_norm, hgrn_out_norm, w_out, final_norm):
    B, S, D = x.shape
    assert norm_in.shape[0] == 1, "single-layer problem"
    nsa_w = nsa_out_norm.shape[1]
    hgrn_w = hgrn_out_norm.shape[1]
    dh = NSA_HEAD_DIM
    G = NSA_KV_HEADS
    heads = nsa_w // dh
    kvw = G * dh
    n_gate = 3 * heads
    n_cmp = (S - CMP_BLOCK) // CMP_STRIDE + 1
    n_half = S // CMP_STRIDE
    assert S % NSA_TILE == 0 and WINDOW % NSA_TILE == 0 and n_half <= LANES and 2 * dh == LANES

    names = ["q_a", "k_cmp", "v_cmp", "k_slc", "v_slc", "k_win", "v_win", "gate", "z_a",
             "q_h", "f_h", "i_h", "z_h"]
    widths = [nsa_w] + [kvw] * 6 + [n_gate, nsa_w] + [hgrn_w] * 4
    starts = dict(zip(names, np.cumsum([0] + widths[:-1]).tolist()))
    wd = dict(zip(names, widths))
    w = w_in[0]
    piece = lambda n: w[:, starts[n]:starts[n] + wd[n]]

    def paired(kname, vname):
        k = piece(kname).reshape(D, G, dh)
        v = piece(vname).reshape(D, G, dh)
        return jnp.concatenate([k, v], axis=-1).reshape(D, G * 2 * dh)

    blocks = [("q_a", piece("q_a") * (dh ** -0.5)), ("z_a", piece("z_a")),
              ("cmp", paired("k_cmp", "v_cmp")), ("slc", paired("k_slc", "v_slc")),
              ("win", paired("k_win", "v_win")),
              ("q_h", piece("q_h")), ("f_h", piece("f_h")), ("i_h", piece("i_h")), ("z_h", piece("z_h"))]
    cols, off = {}, 0
    for n, blk in blocks:
        cols[n] = off
        off += blk.shape[1]
    w_main = jnp.concatenate([blk for _, blk in blocks], axis=1).astype(BF16)
    w_gate = jnp.pad(piece("gate"), ((0, 0), (0, LANES - n_gate))).astype(BF16)

    x2 = x.reshape(B * S, D)
    proj, gate = _proj(x2, norm_in, w_main, w_gate, tm=min(1024, B * S), tn=1536)

    featk, featw, featc, qfeat, mmap, place = _nsa_tables(S, n_half, n_cmp, heads)

    def w1_halves(w1k, w1v):
        hk = w1k.shape[1]
        k3 = w1k.reshape(2, CMP_STRIDE, dh, hk)
        v3 = w1v.reshape(2, CMP_STRIDE, dh, hk)
        zk = jnp.zeros_like(k3[0])
        top = lambda a: jnp.concatenate([a, zk], axis=-1)
        bot = lambda a: jnp.concatenate([zk, a], axis=-1)
        half = lambda i: jnp.concatenate([top(k3[i]), bot(v3[i])], axis=1).reshape(CMP_STRIDE * 2 * dh, 2 * hk)
        return half(0).astype(BF16), half(1).astype(BF16)

    wp, wq = w1_halves(cmp_w1_k[0], cmp_w1_v[0])
    zk = jnp.zeros_like(cmp_w2_k[0])
    w2 = jnp.concatenate([jnp.concatenate([cmp_w2_k[0], zk], axis=1),
                          jnp.concatenate([zk, cmp_w2_v[0]], axis=1)], axis=0).astype(BF16)
    pe = jnp.concatenate([cmp_pe_k[0].reshape(2, CMP_STRIDE, dh), cmp_pe_v[0].reshape(2, CMP_STRIDE, dh)],
                         axis=-1).reshape(2, CMP_STRIDE * 2 * dh)
    kc, vc = _compress(proj, pe, wp, wq, w2, jnp.tile(featc, (G, 1)),
                       batch=B, seq=S, col0=cols["cmp"], n_half=n_half)

    o_a = _nsa(proj, gate, kc, vc, featk, featw, qfeat, mmap, place, nsa_out_norm,
               batch=B, seq=S, cols=cols, n_cmp=n_cmp)
    o_h = _hgrn(proj, lower_bounds, hgrn_out_norm, batch=B, seq=S, cols=cols)

    wo = w_out[0].astype(BF16)
    out = _outproj(x2, o_a, o_h, wo[:nsa_w], wo[nsa_w:], final_norm.reshape(1, D), tm=512)
    return out.reshape(B, S, D)
```

```python
import functools

import ml_dtypes
import numpy as np
import jax
import jax.numpy as jnp
from jax import lax
from jax.experimental import pallas as pl
from jax.experimental.pallas import tpu as pltpu

F32 = jnp.float32
BF16 = jnp.bfloat16

EPS = 1e-6
NEG_INF = -1e30
LOG2E = 1.4426950408889634

NSA_HEAD_DIM = 64
NSA_KV_HEADS = 4
CMP_BLOCK = 32
CMP_STRIDE = 16
SEL_BLOCK = 64
SEL_TOP = 8
SEL_BONUS = 1.0e4
WINDOW = 512
HGRN_HEAD_DIM = 128
HGRN_CHUNK = 64
HGRN_SUB = 16

LANES = 128
VMEM_LIMIT = 56 * 1024 * 1024
NSA_TILE = 256

SEL_LANE0 = NSA_HEAD_DIM
MAX_SEL_BLOCKS = 32
POS_LANE0 = SEL_LANE0 + MAX_SEL_BLOCKS
N_SPLIT = 4
FLAG_LANE = POS_LANE0 + 2 * N_SPLIT


def _nt_dot(a, b):
    return lax.dot_general(a, b, (((1,), (1,)), ((), ())), preferred_element_type=F32)


def _tn_dot(a, b):
    return lax.dot_general(a, b, (((0,), (0,)), ((), ())), preferred_element_type=F32)


def _split3(x):
    hi = x.astype(BF16)
    r1 = x - hi.astype(F32)
    mid = r1.astype(BF16)
    lo = (r1 - mid.astype(F32)).astype(BF16)
    return hi, mid, lo


def _proj_kernel(x_ref, g_ref, w_ref, wg_ref, o_ref, og_ref, h_ref, *, row_chunk):
    j = pl.program_id(1)

    @pl.when(j == 0)
    def _():
        n_chunks = x_ref.shape[0] // row_chunk

        def body(c, carry):
            rows = pl.ds(pl.multiple_of(c * row_chunk, row_chunk), row_chunk)
            x = x_ref[rows, :]
            ms = jnp.mean(x * x, axis=-1, keepdims=True)
            h_ref[rows, :] = (x * lax.rsqrt(ms + EPS) * g_ref[...]).astype(BF16)
            return carry

        lax.fori_loop(0, n_chunks, body, 0)
        og_ref[...] = jnp.dot(h_ref[...], wg_ref[...], preferred_element_type=F32)

    o_ref[...] = jnp.dot(h_ref[...], w_ref[...], preferred_element_type=F32)


def _proj(x2, g, w, wg, *, tm, tn):
    M, D = x2.shape
    N = w.shape[1]
    NG = wg.shape[1]
    return pl.pallas_call(
        functools.partial(_proj_kernel, row_chunk=128),
        grid=(M // tm, N // tn),
        in_specs=[
            pl.BlockSpec((tm, D), lambda i, j: (i, 0)),
            pl.BlockSpec((1, D), lambda i, j: (0, 0)),
            pl.BlockSpec((D, tn), lambda i, j: (0, j)),
            pl.BlockSpec((D, NG), lambda i, j: (0, 0)),
        ],
        out_specs=[
            pl.BlockSpec((tm, tn), lambda i, j: (i, j)),
            pl.BlockSpec((tm, NG), lambda i, j: (i, 0)),
        ],
        out_shape=[
            jax.ShapeDtypeStruct((M, N), F32),
            jax.ShapeDtypeStruct((M, NG), F32),
        ],
        scratch_shapes=[pltpu.VMEM((tm, D), BF16)],
        compiler_params=pltpu.CompilerParams(
            dimension_semantics=("arbitrary", "arbitrary"),
            vmem_limit_bytes=VMEM_LIMIT),
        name="proj",
    )(x2, g, w, wg)


def _compress_kernel(c0_ref, c1_ref, c2_ref, c3_ref, pe_ref, wp_ref, wq_ref, w2_ref, feat_ref,
                     kc_ref, vc_ref, x_ref, *, n_half):
    for g, c_ref in enumerate((c0_ref, c1_ref, c2_ref, c3_ref)):
        for l in range(CMP_STRIDE):
            x_ref[g * n_half:(g + 1) * n_half, l * LANES:(l + 1) * LANES] = (
                c_ref[pl.ds(l, n_half, stride=CMP_STRIDE), :])
    x = x_ref[...]
    rows = x.shape[0]
    first = jnp.dot((x + pe_ref[0:1, :]).astype(BF16), wp_ref[...], preferred_element_type=F32)
    second = jnp.dot((x + pe_ref[1:2, :]).astype(BF16), wq_ref[...], preferred_element_type=F32)
    hidden = first + pltpu.roll(second, rows - 1, 0)
    out = jnp.dot(jax.nn.gelu(hidden).astype(BF16), w2_ref[...], preferred_element_type=F32)
    lane = lax.broadcasted_iota(jnp.int32, out.shape, 1)
    kc_ref[...] = jnp.where(lane < NSA_HEAD_DIM, out.astype(BF16), feat_ref[...])
    vc_ref[...] = jnp.where(lane < NSA_HEAD_DIM, pltpu.roll(out, NSA_HEAD_DIM, 1), out).astype(BF16)


def _compress(proj, pe, wp, wq, w2, feat, *, batch, seq, col0, n_half):
    rows = NSA_KV_HEADS * n_half
    const = lambda a: pl.BlockSpec(a.shape, lambda b: (0,) * a.ndim)
    stream = lambda g: pl.BlockSpec((seq, LANES), lambda b, o=col0 // LANES + g: (b, o))
    return pl.pallas_call(
        functools.partial(_compress_kernel, n_half=n_half),
        grid=(batch,),
        in_specs=[stream(0), stream(1), stream(2), stream(3),
                  const(pe), const(wp), const(wq), const(w2), const(feat)],
        out_specs=[
            pl.BlockSpec((rows, LANES), lambda b: (b, 0)),
            pl.BlockSpec((rows, LANES), lambda b: (b, 0)),
        ],
        out_shape=[
            jax.ShapeDtypeStruct((batch * rows, LANES), BF16),
            jax.ShapeDtypeStruct((batch * rows, LANES), BF16),
        ],
        scratch_shapes=[pltpu.VMEM((rows, CMP_STRIDE * LANES), F32)],
        compiler_params=pltpu.CompilerParams(
            dimension_semantics=("arbitrary",),
            vmem_limit_bytes=VMEM_LIMIT),
        name="compress",
    )(proj, proj, proj, proj, pe, wp, wq, w2, feat)


def _nsa_kernel(q_ref, z_ref, gate_ref, slc_ref, win_ref, kc_ref, vc_ref, featk_ref, featw_ref,
                qfeat_ref, mmap_ref, place_ref, norm_ref, o_ref,
                ksel_ref, vsel_ref, kwin_ref, vwin_ref, *, n_cmp):
    g = pl.program_id(1)
    qi = pl.program_id(2)
    tq = q_ref.shape[0]
    tk = tq
    seq = slc_ref.shape[0]
    dh = NSA_HEAD_DIM
    rep = q_ref.shape[1] // dh
    n_half = kc_ref.shape[0]
    t0 = qi * tq

    @pl.when(qi == 0)
    def _():
        lane = lax.broadcasted_iota(jnp.int32, (seq, LANES), 1)
        kv = slc_ref[...]
        ksel_ref[...] = jnp.where(lane < dh, kv.astype(BF16), featk_ref[...])
        vsel_ref[...] = jnp.where(lane < dh, pltpu.roll(kv, dh, 1), kv).astype(BF16)
        kv = win_ref[...]
        kwin_ref[0:WINDOW, :] = featw_ref[0:WINDOW, :]
        kwin_ref[WINDOW:, :] = jnp.where(lane < dh, kv.astype(BF16), featw_ref[WINDOW:, :])
        vwin_ref[0:WINDOW, :] = jnp.zeros((WINDOW, LANES), BF16)
        vwin_ref[WINDOW:, :] = jnp.where(lane < dh, pltpu.roll(kv, dh, 1), kv).astype(BF16)

    lane = lax.broadcasted_iota(jnp.int32, (tq, LANES), 1)
    row = lax.broadcasted_iota(jnp.int32, (tq, tk), 0)
    col = lax.broadcasted_iota(jnp.int32, (tq, tk), 1)
    stack = lambda a: jnp.concatenate([a] * rep, axis=0)
    causal_bias = stack(jnp.where(col <= row, 0.0, NEG_INF))
    band_bias = stack(jnp.where(col > row, 0.0, NEG_INF))

    qfeat = qfeat_ref[0]
    qa = []
    for r in range(rep):
        qcol = q_ref[:, (r // 2) * LANES:(r // 2 + 1) * LANES]
        if r % 2:
            qcol = pltpu.roll(qcol, dh, 1)
        qa.append(jnp.where(lane < dh, qcol * LOG2E, qfeat[r:r + 1, :]))
    q0 = jnp.concatenate(qa, axis=0).astype(BF16)

    n_idx = lax.broadcasted_iota(jnp.int32, (tq, n_half), 1)
    t_idx = t0 + lax.broadcasted_iota(jnp.int32, (tq, n_half), 0)
    valid_c = stack((CMP_STRIDE * n_idx + (CMP_BLOCK - 1) <= t_idx) & (n_idx < n_cmp))
    lg = jnp.where(valid_c, _nt_dot(q0, kc_ref[...]), NEG_INF)
    e = jnp.exp2(lg - jnp.max(lg, axis=-1, keepdims=True))
    p = jnp.where(valid_c, e / jnp.sum(e, axis=-1, keepdims=True), 0.0)
    o_cmp = jnp.dot(p.astype(BF16), vc_ref[...], preferred_element_type=F32)
    p_sum = p[0:tq]
    for r in range(1, rep):
        p_sum = p_sum + p[r * tq:(r + 1) * tq]

    mm = mmap_ref[...]
    hi, mid, lo = _split3(p_sum)
    p_slc = (_nt_dot(mm, hi) + _nt_dot(mm, mid) + _nt_dot(mm, lo))[0:MAX_SEL_BLOCKS]
    jj = lax.broadcasted_iota(jnp.int32, (MAX_SEL_BLOCKS, tq), 0)
    tt = t0 + lax.broadcasted_iota(jnp.int32, (MAX_SEL_BLOCKS, tq), 1)
    cur = lax.shift_right_logical(tt, int(np.log2(SEL_BLOCK)))
    forced = (jj == 0) | (jj == cur) | (jj == cur - 1)
    future = jj > cur
    score = jnp.where(future, -1.0, p_slc + jnp.where(forced, SEL_BONUS, 0.0))
    rank = jnp.zeros((MAX_SEL_BLOCKS, tq), jnp.int32)
    for i in range(MAX_SEL_BLOCKS):
        other = jnp.broadcast_to(score[i:i + 1, :], score.shape)
        beats = (other > score) | ((other == score) & (jj > i))
        rank = rank + beats.astype(jnp.int32)
    chosen = jnp.where((rank < SEL_TOP) & jnp.logical_not(future), 1.0, 0.0).astype(BF16)
    placed = _tn_dot(chosen, place_ref[...])
    sel_lanes = (lane >= SEL_LANE0) & (lane < SEL_LANE0 + MAX_SEL_BLOCKS)
    sel_bias = (placed - 1.0) * (-NEG_INF)
    qs = jnp.concatenate([jnp.where(sel_lanes, sel_bias, a) for a in qa], axis=0).astype(BF16)

    def sel_tile(kt, carry, bias):
        m, l, acc = carry
        rows = pl.ds(pl.multiple_of(kt * tk, tk), tk)
        lg = _nt_dot(qs, ksel_ref[rows, :])
        if bias is not None:
            lg = lg + bias
        m_new = jnp.maximum(m, jnp.max(lg, axis=-1, keepdims=True))
        alpha = jnp.exp2(m - m_new)
        pt = jnp.exp2(lg - m_new)
        l = alpha * l + jnp.sum(pt, axis=-1, keepdims=True)
        acc = alpha * acc + jnp.dot(pt.astype(BF16), vsel_ref[rows, :], preferred_element_type=F32)
        return m_new, l, acc

    init = (jnp.full((rep * tq, 1), NEG_INF, F32), jnp.zeros((rep * tq, 1), F32),
            jnp.zeros((rep * tq, LANES), F32))
    carry = lax.fori_loop(0, qi, lambda kt, c: sel_tile(kt, c, None), init)
    _, l, acc = sel_tile(qi, carry, causal_bias)
    o_slc = acc / l

    n_wt = WINDOW // tk + 1
    lgs = []
    for w in range(n_wt):
        rows = pl.ds(pl.multiple_of(t0 + w * tk, tk), tk)
        lg = _nt_dot(qs, kwin_ref[rows, :])
        if w == 0:
            lg = lg + band_bias
        if w == n_wt - 1:
            lg = lg + causal_bias
        lgs.append(lg)
    m = functools.reduce(jnp.maximum, [jnp.max(lg, axis=-1, keepdims=True) for lg in lgs])
    l = jnp.zeros((rep * tq, 1), F32)
    acc = jnp.zeros((rep * tq, LANES), F32)
    for w in range(n_wt):
        rows = pl.ds(pl.multiple_of(t0 + w * tk, tk), tk)
        pt = jnp.exp2(lgs[w] - m)
        l = l + jnp.sum(pt, axis=-1, keepdims=True)
        acc = acc + jnp.dot(pt.astype(BF16), vwin_ref[rows, :], preferred_element_type=F32)
    o_win = acc / l

    n_gate = 3 * rep
    gates = pltpu.roll(jax.nn.sigmoid(gate_ref[...]), jnp.where(g == 0, 0, LANES - n_gate * g), 1)
    ys = []
    for r in range(rep):
        sl = slice(r * tq, (r + 1) * tq)
        o = (gates[:, 3 * r:3 * r + 1] * o_cmp[sl] + gates[:, 3 * r + 1:3 * r + 2] * o_slc[sl]
             + gates[:, 3 * r + 2:3 * r + 3] * o_win[sl])
        ys.append(o * lax.rsqrt(jnp.mean(o * o, axis=-1, keepdims=True) + EPS))
    for pair in range(rep // 2):
        cols = slice(pair * LANES, (pair + 1) * LANES)
        y = jnp.where(lane < dh, ys[2 * pair], ys[2 * pair + 1]) * norm_ref[:, cols]
        z = z_ref[:, cols]
        o_ref[:, cols] = (y * (z * jax.nn.sigmoid(z))).astype(o_ref.dtype)


def _nsa(proj, gate, kc, vc, featk, featw, qfeat, mmap, place, norm, *, batch, seq, cols, n_cmp):
    tq = NSA_TILE
    nq = seq // tq
    G = NSA_KV_HEADS
    gw = norm.shape[1] // G
    n_half = kc.shape[0] // (batch * G)
    const = lambda a: pl.BlockSpec(a.shape, lambda b, g, i: (0,) * a.ndim)
    stream = lambda name: pl.BlockSpec((seq, LANES), lambda b, g, i, o=cols[name] // LANES: (b, o + g))
    return pl.pallas_call(
        functools.partial(_nsa_kernel, n_cmp=n_cmp),
        grid=(batch, G, nq),
        in_specs=[
            pl.BlockSpec((tq, gw), lambda b, g, i, o=cols["q_a"] // gw: (b * nq + i, o + g)),
            pl.BlockSpec((tq, gw), lambda b, g, i, o=cols["z_a"] // gw: (b * nq + i, o + g)),
            pl.BlockSpec((tq, gate.shape[1]), lambda b, g, i: (b * nq + i, 0)),
            stream("slc"), stream("win"),
            pl.BlockSpec((n_half, LANES), lambda b, g, i: (b * G + g, 0)),
            pl.BlockSpec((n_half, LANES), lambda b, g, i: (b * G + g, 0)),
            const(featk), const(featw),
            pl.BlockSpec((1,) + qfeat.shape[1:], lambda b, g, i: (g, 0, 0)),
            const(mmap), const(place),
            pl.BlockSpec((1, gw), lambda b, g, i: (0, g)),
        ],
        out_specs=pl.BlockSpec((tq, gw), lambda b, g, i: (b * nq + i, g)),
        out_shape=jax.ShapeDtypeStruct((batch * seq, G * gw), BF16),
        scratch_shapes=[
            pltpu.VMEM((seq, LANES), BF16), pltpu.VMEM((seq, LANES), BF16),
            pltpu.VMEM((seq + WINDOW, LANES), BF16), pltpu.VMEM((seq + WINDOW, LANES), BF16),
        ],
        compiler_params=pltpu.CompilerParams(
            dimension_semantics=("arbitrary", "arbitrary", "arbitrary"),
            vmem_limit_bytes=VMEM_LIMIT),
        name="nsa",
    )(proj, proj, gate, proj, proj, kc, vc, featk, featw, qfeat, mmap, place, norm)


def _bf16_terms(x, n):
    terms, rest = [], np.asarray(x, np.float64)
    for _ in range(n):
        t = rest.astype(np.float32).astype(ml_dtypes.bfloat16).astype(np.float64)
        terms.append(t.astype(np.float32))
        rest = rest - t
    return terms


def _nsa_tables(seq, n_half, n_cmp, heads):
    assert seq // SEL_BLOCK <= MAX_SEL_BLOCKS and FLAG_LANE < LANES

    def key_features(pos, onehot_blocks):
        f = np.zeros((len(pos), LANES), np.float32)
        if onehot_blocks:
            f[np.arange(len(pos)), SEL_LANE0 + pos // SEL_BLOCK] = 1.0
        f[:, POS_LANE0:POS_LANE0 + N_SPLIT] = ((pos // 64) * 64)[:, None]
        f[:, POS_LANE0 + N_SPLIT:POS_LANE0 + 2 * N_SPLIT] = (pos % 64)[:, None]
        return f

    featk = key_features(np.arange(seq), True)
    featw = np.concatenate([np.zeros((WINDOW, LANES), np.float32), key_features(np.arange(seq), False)])
    featw[:WINDOW, FLAG_LANE] = NEG_INF
    featc = key_features(CMP_STRIDE * np.arange(n_half) + CMP_BLOCK - 1, False)
    featc[n_cmp:, FLAG_LANE] = NEG_INF

    slopes = (2.0 ** (-8.0 * np.arange(1, heads + 1) / heads)).astype(np.float32).astype(np.float64)
    terms = _bf16_terms(slopes * LOG2E, N_SPLIT)
    rep = heads // NSA_KV_HEADS
    qfeat = np.zeros((NSA_KV_HEADS, 8, LANES), np.float32)
    for h in range(heads):
        for i, t in enumerate(terms):
            qfeat[h // rep, h % rep, POS_LANE0 + i] = t[h]
            qfeat[h // rep, h % rep, POS_LANE0 + N_SPLIT + i] = t[h]
        qfeat[h // rep, h % rep, FLAG_LANE] = 1.0

    cs = CMP_STRIDE * np.arange(n_half)[None, :]
    ss = SEL_BLOCK * np.arange(LANES)[:, None]
    overlap = np.clip(np.minimum(cs + CMP_BLOCK, ss + SEL_BLOCK) - np.maximum(cs, ss), 0, None)
    mmap = (overlap / CMP_BLOCK) * (np.arange(n_half)[None, :] < n_cmp) * (ss < seq)
    place = np.zeros((MAX_SEL_BLOCKS, LANES), np.float32)
    place[np.arange(MAX_SEL_BLOCKS), SEL_LANE0 + np.arange(MAX_SEL_BLOCKS)] = 1.0
    bf = lambda a: jnp.asarray(a, dtype=BF16)
    return bf(featk), bf(featw), bf(featc), jnp.asarray(qfeat), bf(mmap), bf(place)


def _hgrn_kernel(q_ref, f_ref, v_ref, z_ref, lb_ref, norm_ref, cum_ref, o_ref, *, group):
    C, SUB = HGRN_CHUNK, HGRN_SUB
    n_sub = C // SUB
    seq, dk = q_ref.shape

    lbr = lb_ref[...]
    e = jnp.exp(lbr - jnp.max(lbr, axis=0, keepdims=True))
    lb = e[0:1, :] / jnp.sum(e, axis=0, keepdims=True)
    gain = norm_ref[...]
    cum = cum_ref[...]
    sub = lambda x, i: x[i * SUB:(i + 1) * SUB]
    masks = []
    for i in range(n_sub):
        width = (i + 1) * SUB
        masks.append(lax.broadcasted_iota(jnp.int32, (SUB, width), 1)
                     <= lax.broadcasted_iota(jnp.int32, (SUB, width), 0) + i * SUB)

    def block(cb, st):
        chunks = range(group)
        rows = [pl.ds(pl.multiple_of((cb * group + j) * C, C), C) for j in chunks]
        v16 = [v_ref[rows[j], :].astype(BF16) for j in chunks]
        f = [lb + (1.0 - lb) * jax.nn.sigmoid(f_ref[rows[j], :]) for j in chunks]
        k = [1.0 - f[j] for j in chunks]
        parts = [_split3(jnp.log(f[j])) for j in chunks]
        a = [(jnp.dot(cum, parts[j][0], preferred_element_type=F32)
              + jnp.dot(cum, parts[j][1], preferred_element_type=F32)
              + jnp.dot(cum, parts[j][2], preferred_element_type=F32)) for j in chunks]
        q1, k1, k2, qb, k3, start, dec = [], [], [], [], [], [], []
        for j in chunks:
            q1.append(q_ref[rows[j], :] * jnp.exp(a[j]))
            k1.append(k[j] * jnp.exp(-a[j]))
            tot = [a[j][(i + 1) * SUB - 1:(i + 1) * SUB] for i in range(n_sub)]
            s = [jnp.zeros_like(tot[0])]
            for i in range(n_sub):
                s.append(s[i] + tot[i])
            start.append(s)
            dec.append(jnp.exp(s[n_sub]))
            qb.append(jnp.concatenate([sub(q1[j], i) * jnp.exp(s[i]) for i in range(n_sub)], axis=0)
                      .astype(BF16))
            k2.append([sub(k1[j], i) * jnp.exp(tot[i]) for i in range(n_sub)])
            k3.append(jnp.concatenate([sub(k1[j], i) * jnp.exp(s[n_sub] - s[i]) for i in range(n_sub)],
                                      axis=0).astype(BF16))
        att = []
        for j in chunks:
            row_blocks = []
            for i in range(n_sub):
                rhs = [k2[j][jb] * jnp.exp(start[j][i] - start[j][jb + 1]) if jb < i - 1 else k2[j][jb]
                       for jb in range(i)]
                rhs.append(sub(k1[j], i))
                rhs = jnp.concatenate(rhs, axis=0) if len(rhs) > 1 else rhs[0]
                sc = _nt_dot(sub(q1[j], i).astype(BF16), rhs.astype(BF16))
                row_blocks.append(jnp.where(masks[i], sc, 0.0).astype(BF16))
            att.append(row_blocks)
        intra = [jnp.concatenate([jnp.dot(att[j][i], v16[j][0:(i + 1) * SUB], preferred_element_type=F32)
                                  for i in range(n_sub)], axis=0) for j in chunks]
        incr = [_tn_dot(v16[j], k3[j]) for j in chunks]
        inter = []
        for j in chunks:
            inter.append(_nt_dot(qb[j], st.astype(BF16)))
            st = st * dec[j] + incr[j]
        for j in chunks:
            o = inter[j] + intra[j]
            y = o * lax.rsqrt(jnp.mean(o * o, axis=-1, keepdims=True) + EPS) * gain
            z = z_ref[rows[j], :]
            o_ref[rows[j], :] = (y * (z * jax.nn.sigmoid(z))).astype(o_ref.dtype)
        return st

    lax.fori_loop(0, seq // (C * group), block, jnp.zeros((dk, dk), F32))


def _hgrn_patterns():
    C, SUB = HGRN_CHUNK, HGRN_SUB
    t = np.arange(C)[:, None]
    s = np.arange(C)[None, :]
    return (((t // SUB) == (s // SUB)) & (s <= t)).astype(np.float32)


def _hgrn(proj, lower_bounds, norm, *, batch, seq, cols):
    dk = HGRN_HEAD_DIM
    heads = norm.shape[1] // dk
    cum = jnp.asarray(_hgrn_patterns(), dtype=BF16)
    col = lambda name: (lambda b, h, o=cols[name] // dk: (b, o + h))
    return pl.pallas_call(
        functools.partial(_hgrn_kernel, group=8),
        grid=(batch, heads),
        in_specs=[
            pl.BlockSpec((seq, dk), col("q_h")),
            pl.BlockSpec((seq, dk), col("f_h")),
            pl.BlockSpec((seq, dk), col("i_h")),
            pl.BlockSpec((seq, dk), col("z_h")),
            pl.BlockSpec((lower_bounds.shape[0], dk), lambda b, h: (0, h)),
            pl.BlockSpec((1, dk), lambda b, h: (0, h)),
            pl.BlockSpec(cum.shape, lambda b, h: (0, 0)),
        ],
        out_specs=pl.BlockSpec((seq, dk), lambda b, h: (b, h)),
        out_shape=jax.ShapeDtypeStruct((batch * seq, heads * dk), BF16),
        compiler_params=pltpu.CompilerParams(
            dimension_semantics=("arbitrary", "arbitrary"),
            vmem_limit_bytes=VMEM_LIMIT),
        name="hgrn",
    )(proj, proj, proj, proj, lower_bounds, norm, cum)


def _outproj_kernel(x_ref, oa_ref, oh_ref, wa_ref, wh_ref, g_ref, o_ref):
    y = x_ref[...] + jnp.dot(oa_ref[...], wa_ref[...], preferred_element_type=F32)
    y = y + jnp.dot(oh_ref[...], wh_ref[...], preferred_element_type=F32)
    ms = jnp.mean(y * y, axis=-1, keepdims=True)
    o_ref[...] = y * lax.rsqrt(ms + EPS) * g_ref[...]


def _outproj(x2, o_a, o_h, wa, wh, g, *, tm):
    M, D = x2.shape
    const = lambda a: pl.BlockSpec(a.shape, lambda i: (0,) * a.ndim)
    return pl.pallas_call(
        _outproj_kernel,
        grid=(M // tm,),
        in_specs=[
            pl.BlockSpec((tm, D), lambda i: (i, 0)),
            pl.BlockSpec((tm, o_a.shape[1]), lambda i: (i, 0)),
            pl.BlockSpec((tm, o_h.shape[1]), lambda i: (i, 0)),
            const(wa), const(wh), const(g),
        ],
        out_specs=pl.BlockSpec((tm, D), lambda i: (i, 0)),
        out_shape=jax.ShapeDtypeStruct((M, D), F32),
        compiler_params=pltpu.CompilerParams(
            dimension_semantics=("arbitrary",),
            vmem_limit_bytes=VMEM_LIMIT),
        name="outproj",
    )(x2, o_a, o_h, wa, wh, g)


def kernel(x, norm_in, w_in, cmp_pe_k, cmp_w1_k, cmp_w2_k, cmp_pe_v, cmp_w1_v, cmp_w2_v,
           lower_bounds, nsa_out_norm, hgrn_out_norm, w_out, final_norm):
    B, S, D = x.shape
    assert norm_in.shape[0] == 1, "single-layer problem"
    nsa_w = nsa_out_norm.shape[1]
    hgrn_w = hgrn_out_norm.shape[1]
    dh = NSA_HEAD_DIM
    G = NSA_KV_HEADS
    heads = nsa_w // dh
    kvw = G * dh
    n_gate = 3 * heads
    n_cmp = (S - CMP_BLOCK) // CMP_STRIDE + 1
    n_half = S // CMP_STRIDE
    assert S % NSA_TILE == 0 and WINDOW % NSA_TILE == 0 and n_half <= LANES and 2 * dh == LANES

    names = ["q_a", "k_cmp", "v_cmp", "k_slc", "v_slc", "k_win", "v_win", "gate", "z_a",
             "q_h", "f_h", "i_h", "z_h"]
    widths = [nsa_w] + [kvw] * 6 + [n_gate, nsa_w] + [hgrn_w] * 4
    starts = dict(zip(names, np.cumsum([0] + widths[:-1]).tolist()))
    wd = dict(zip(names, widths))
    w = w_in[0]
    piece = lambda n: w[:, starts[n]:starts[n] + wd[n]]

    def paired(kname, vname):
        k = piece(kname).reshape(D, G, dh)
        v = piece(vname).reshape(D, G, dh)
        return jnp.concatenate([k, v], axis=-1).reshape(D, G * 2 * dh)

    blocks = [("q_a", piece("q_a") * (dh ** -0.5)), ("z_a", piece("z_a")),
              ("cmp", paired("k_cmp", "v_cmp")), ("slc", paired("k_slc", "v_slc")),
              ("win", paired("k_win", "v_win")),
              ("q_h", piece("q_h")), ("f_h", piece("f_h")), ("i_h", piece("i_h")), ("z_h", piece("z_h"))]
    cols, off = {}, 0
    for n, blk in blocks:
        cols[n] = off
        off += blk.shape[1]
    w_main = jnp.concatenate([blk for _, blk in blocks], axis=1).astype(BF16)
    w_gate = jnp.pad(piece("gate"), ((0, 0), (0, LANES - n_gate))).astype(BF16)

    x2 = x.reshape(B * S, D)
    proj, gate = _proj(x2, norm_in, w_main, w_gate, tm=min(1024, B * S), tn=1536)

    featk, featw, featc, qfeat, mmap, place = _nsa_tables(S, n_half, n_cmp, heads)

    def w1_halves(w1k, w1v):
        hk = w1k.shape[1]
        k3 = w1k.reshape(2, CMP_STRIDE, dh, hk)
        v3 = w1v.reshape(2, CMP_STRIDE, dh, hk)
        zk = jnp.zeros_like(k3[0])
        top = lambda a: jnp.concatenate([a, zk], axis=-1)
        bot = lambda a: jnp.concatenate([zk, a], axis=-1)
        half = lambda i: jnp.concatenate([top(k3[i]), bot(v3[i])], axis=1).reshape(CMP_STRIDE * 2 * dh, 2 * hk)
        return half(0).astype(BF16), half(1).astype(BF16)

    wp, wq = w1_halves(cmp_w1_k[0], cmp_w1_v[0])
    zk = jnp.zeros_like(cmp_w2_k[0])
    w2 = jnp.concatenate([jnp.concatenate([cmp_w2_k[0], zk], axis=1),
                          jnp.concatenate([zk, cmp_w2_v[0]], axis=1)], axis=0).astype(BF16)
    pe = jnp.concatenate([cmp_pe_k[0].reshape(2, CMP_STRIDE, dh), cmp_pe_v[0].reshape(2, CMP_STRIDE, dh)],
                         axis=-1).reshape(2, CMP_STRIDE * 2 * dh)
    kc, vc = _compress(proj, pe, wp, wq, w2, jnp.tile(featc, (G, 1)),
                       batch=B, seq=S, col0=cols["cmp"], n_half=n_half)

    o_a = _nsa(proj, gate, kc, vc, featk, featw, qfeat, mmap, place, nsa_out_norm,
               batch=B, seq=S, cols=cols, n_cmp=n_cmp)
    o_h = _hgrn(proj, lower_bounds, hgrn_out_norm, batch=B, seq=S, cols=cols)

    wo = w_out[0].astype(BF16)
    out = _outproj(x2, o_a, o_h, wo[:nsa_w], wo[nsa_w:], final_norm.reshape(1, D), tm=512)
    return out.reshape(B, S, D)
```

```python
import functools

import ml_dtypes
import numpy as np
import jax
import jax.numpy as jnp
from jax import lax
from jax.experimental import pallas as pl
from jax.experimental.pallas import tpu as pltpu

F32 = jnp.float32
BF16 = jnp.bfloat16

EPS = 1e-6
NEG_INF = -1e30
LOG2E = 1.4426950408889634

NSA_HEAD_DIM = 64
NSA_KV_HEADS = 4
CMP_BLOCK = 32
CMP_STRIDE = 16
SEL_BLOCK = 64
SEL_TOP = 8
SEL_BONUS = 1.0e4
WINDOW = 512
HGRN_HEAD_DIM = 128
HGRN_CHUNK = 64
HGRN_SUB = 16

LANES = 128
VMEM_LIMIT = 56 * 1024 * 1024
NSA_TILE = 256

SEL_LANE0 = NSA_HEAD_DIM
MAX_SEL_BLOCKS = 32
POS_LANE0 = SEL_LANE0 + MAX_SEL_BLOCKS
N_SPLIT = 4
FLAG_LANE = POS_LANE0 + 2 * N_SPLIT


def _nt_dot(a, b):
    return lax.dot_general(a, b, (((1,), (1,)), ((), ())), preferred_element_type=F32)


def _tn_dot(a, b):
    return lax.dot_general(a, b, (((0,), (0,)), ((), ())), preferred_element_type=F32)


def _split3(x):
    hi = x.astype(BF16)
    r1 = x - hi.astype(F32)
    mid = r1.astype(BF16)
    lo = (r1 - mid.astype(F32)).astype(BF16)
    return hi, mid, lo


def _proj_kernel(x_ref, g_ref, w_ref, wg_ref, o_ref, og_ref, h_ref, *, row_chunk):
    j = pl.program_id(1)

    @pl.when(j == 0)
    def _():
        n_chunks = x_ref.shape[0] // row_chunk

        def body(c, carry):
            rows = pl.ds(pl.multiple_of(c * row_chunk, row_chunk), row_chunk)
            x = x_ref[rows, :]
            ms = jnp.mean(x * x, axis=-1, keepdims=True)
            h_ref[rows, :] = (x * lax.rsqrt(ms + EPS) * g_ref[...]).astype(BF16)
            return carry

        lax.fori_loop(0, n_chunks, body, 0)
        og_ref[...] = jnp.dot(h_ref[...], wg_ref[...], preferred_element_type=F32)

    o_ref[...] = jnp.dot(h_ref[...], w_ref[...], preferred_element_type=F32)


def _proj(x2, g, w, wg, *, tm, tn):
    M, D = x2.shape
    N = w.shape[1]
    NG = wg.shape[1]
    return pl.pallas_call(
        functools.partial(_proj_kernel, row_chunk=128),
        grid=(M // tm, N // tn),
        in_specs=[
            pl.BlockSpec((tm, D), lambda i, j: (i, 0)),
            pl.BlockSpec((1, D), lambda i, j: (0, 0)),
            pl.BlockSpec((D, tn), lambda i, j: (0, j)),
            pl.BlockSpec((D, NG), lambda i, j: (0, 0)),
        ],
        out_specs=[
            pl.BlockSpec((tm, tn), lambda i, j: (i, j)),
            pl.BlockSpec((tm, NG), lambda i, j: (i, 0)),
        ],
        out_shape=[
            jax.ShapeDtypeStruct((M, N), F32),
            jax.ShapeDtypeStruct((M, NG), F32),
        ],
        scratch_shapes=[pltpu.VMEM((tm, D), BF16)],
        compiler_params=pltpu.CompilerParams(
            dimension_semantics=("arbitrary", "arbitrary"),
            vmem_limit_bytes=VMEM_LIMIT),
        name="proj",
    )(x2, g, w, wg)


def _compress_kernel(c0_ref, c1_ref, c2_ref, c3_ref, pe_ref, wp_ref, wq_ref, w2_ref, feat_ref,
                     kc_ref, vc_ref, x_ref, *, n_half):
    for g, c_ref in enumerate((c0_ref, c1_ref, c2_ref, c3_ref)):
        for l in range(CMP_STRIDE):
            x_ref[g * n_half:(g + 1) * n_half, l * LANES:(l + 1) * LANES] = (
                c_ref[pl.ds(l, n_half, stride=CMP_STRIDE), :])
    x = x_ref[...]
    rows = x.shape[0]
    first = jnp.dot((x + pe_ref[0:1, :]).astype(BF16), wp_ref[...], preferred_element_type=F32)
    second = jnp.dot((x + pe_ref[1:2, :]).astype(BF16), wq_ref[...], preferred_element_type=F32)
    hidden = first + pltpu.roll(second, rows - 1, 0)
    out = jnp.dot(jax.nn.gelu(hidden).astype(BF16), w2_ref[...], preferred_element_type=F32)
    lane = lax.broadcasted_iota(jnp.int32, out.shape, 1)
    dh = NSA_HEAD_DIM
    kc_ref[...] = jnp.where(lane < dh, out.astype(BF16), feat_ref[...])
    for g in range(NSA_KV_HEADS):
        vc_ref[g * dh:(g + 1) * dh, :] = out[g * n_half:(g + 1) * n_half, :].T[dh:, :].astype(BF16)


def _compress(proj, pe, wp, wq, w2, feat, *, batch, seq, col0, n_half):
    rows = NSA_KV_HEADS * n_half
    rows_t = NSA_KV_HEADS * NSA_HEAD_DIM
    const = lambda a: pl.BlockSpec(a.shape, lambda b: (0,) * a.ndim)
    stream = lambda g: pl.BlockSpec((seq, LANES), lambda b, o=col0 // LANES + g: (b, o))
    return pl.pallas_call(
        functools.partial(_compress_kernel, n_half=n_half),
        grid=(batch,),
        in_specs=[stream(0), stream(1), stream(2), stream(3),
                  const(pe), const(wp), const(wq), const(w2), const(feat)],
        out_specs=[
            pl.BlockSpec((rows, LANES), lambda b: (b, 0)),
            pl.BlockSpec((rows_t, n_half), lambda b: (b, 0)),
        ],
        out_shape=[
            jax.ShapeDtypeStruct((batch * rows, LANES), BF16),
            jax.ShapeDtypeStruct((batch * rows_t, n_half), BF16),
        ],
        scratch_shapes=[pltpu.VMEM((rows, CMP_STRIDE * LANES), F32)],
        compiler_params=pltpu.CompilerParams(
            dimension_semantics=("arbitrary",),
            vmem_limit_bytes=VMEM_LIMIT),
        name="compress",
    )(proj, proj, proj, proj, pe, wp, wq, w2, feat)


def _nsa_kernel(q_ref, z_ref, gate_ref, slc_ref, win_ref, kc_ref, vct_ref, featk_ref, featw_ref,
                qfeat_ref, mmap_ref, place_ref, norm_ref, o_ref,
                ksel_ref, vselt_ref, kwin_ref, vwint_ref, gt_ref, *, n_cmp):
    g = pl.program_id(1)
    qi = pl.program_id(2)
    tq = q_ref.shape[0]
    tk = tq
    seq = slc_ref.shape[0]
    dh = NSA_HEAD_DIM
    rep = q_ref.shape[1] // dh
    n_half = kc_ref.shape[0]
    n_pad = WINDOW // tk
    t0 = qi * tq

    @pl.when(qi == 0)
    def _():
        lane = lax.broadcasted_iota(jnp.int32, (seq, LANES), 1)
        kv = slc_ref[...]
        ksel_ref[...] = jnp.where(lane < dh, kv.astype(BF16), featk_ref[...])
        vt = kv.T[dh:, :].astype(BF16)
        for kt in range(seq // tk):
            vselt_ref[kt] = vt[:, kt * tk:(kt + 1) * tk]
        kv = win_ref[...]
        kwin_ref[0:WINDOW, :] = featw_ref[0:WINDOW, :]
        kwin_ref[WINDOW:, :] = jnp.where(lane < dh, kv.astype(BF16), featw_ref[WINDOW:, :])
        vt = kv.T[dh:, :].astype(BF16)
        for kt in range(n_pad):
            vwint_ref[kt] = jnp.zeros((dh, tk), BF16)
        for kt in range(seq // tk):
            vwint_ref[n_pad + kt] = vt[:, kt * tk:(kt + 1) * tk]

    lane = lax.broadcasted_iota(jnp.int32, (tq, LANES), 1)
    key_i = lax.broadcasted_iota(jnp.int32, (tk, tq), 0)
    qry_i = lax.broadcasted_iota(jnp.int32, (tk, tq), 1)
    heads = lambda a: jnp.concatenate([a] * rep, axis=1)
    causal_bias = heads(jnp.where(key_i <= qry_i, 0.0, NEG_INF))
    band_bias = heads(jnp.where(key_i > qry_i, 0.0, NEG_INF))

    qfeat = qfeat_ref[0]
    qa = []
    for r in range(rep):
        qcol = q_ref[:, (r // 2) * LANES:(r // 2 + 1) * LANES]
        if r % 2:
            qcol = pltpu.roll(qcol, dh, 1)
        qa.append(jnp.where(lane < dh, qcol * LOG2E, qfeat[r:r + 1, :]))
    q0 = jnp.concatenate(qa, axis=0).astype(BF16)

    n_idx = lax.broadcasted_iota(jnp.int32, (n_half, tq), 0)
    t_idx = t0 + lax.broadcasted_iota(jnp.int32, (n_half, tq), 1)
    valid_c = heads((CMP_STRIDE * n_idx + (CMP_BLOCK - 1) <= t_idx) & (n_idx < n_cmp))
    lg = jnp.where(valid_c, _nt_dot(kc_ref[...], q0), NEG_INF)
    e = jnp.exp2(lg - jnp.max(lg, axis=0, keepdims=True))
    p = jnp.where(valid_c, e / jnp.sum(e, axis=0, keepdims=True), 0.0)
    o_cmp = jnp.dot(vct_ref[...], p.astype(BF16), preferred_element_type=F32)
    p_sum = p[:, 0:tq]
    for r in range(1, rep):
        p_sum = p_sum + p[:, r * tq:(r + 1) * tq]

    mm = mmap_ref[...]
    hi, mid, lo = _split3(p_sum)
    p_slc = (jnp.dot(mm, hi, preferred_element_type=F32) + jnp.dot(mm, mid, preferred_element_type=F32)
             + jnp.dot(mm, lo, preferred_element_type=F32))[0:MAX_SEL_BLOCKS]
    jj = lax.broadcasted_iota(jnp.int32, (MAX_SEL_BLOCKS, tq), 0)
    tt = t0 + lax.broadcasted_iota(jnp.int32, (MAX_SEL_BLOCKS, tq), 1)
    cur = lax.shift_right_logical(tt, int(np.log2(SEL_BLOCK)))
    forced = (jj == 0) | (jj == cur) | (jj == cur - 1)
    future = jj > cur
    score = jnp.where(future, -1.0, p_slc + jnp.where(forced, SEL_BONUS, 0.0))
    rank = jnp.zeros((MAX_SEL_BLOCKS, tq), jnp.int32)
    for i in range(MAX_SEL_BLOCKS):
        other = jnp.broadcast_to(score[i:i + 1, :], score.shape)
        beats = (other > score) | ((other == score) & (jj > i))
        rank = rank + beats.astype(jnp.int32)
    chosen = jnp.where((rank < SEL_TOP) & jnp.logical_not(future), 1.0, 0.0).astype(BF16)
    placed = _tn_dot(chosen, place_ref[...])
    sel_lanes = (lane >= SEL_LANE0) & (lane < SEL_LANE0 + MAX_SEL_BLOCKS)
    sel_bias = (placed - 1.0) * (-NEG_INF)
    qs = jnp.concatenate([jnp.where(sel_lanes, sel_bias, a) for a in qa], axis=0).astype(BF16)

    sub8 = lambda a: a.reshape(tk // 8, 8, rep * tq)

    def softmax_pv(lgs, k_tiles):
        m8 = functools.reduce(jnp.maximum, [jnp.max(sub8(lg), axis=0) for lg in lgs])
        m = jnp.max(m8, axis=0, keepdims=True)
        l8 = jnp.zeros((8, rep * tq), F32)
        acc = jnp.zeros((dh, rep * tq), F32)
        for lg, vt in zip(lgs, k_tiles):
            pt = jnp.exp2(lg - m)
            l8 = l8 + jnp.sum(sub8(pt), axis=0)
            acc = acc + jnp.dot(vt, pt.astype(BF16), preferred_element_type=F32)
        return acc / jnp.sum(l8, axis=0, keepdims=True)

    def sel_variant(n):
        def run():
            lgs = [_nt_dot(ksel_ref[kt * tk:(kt + 1) * tk, :], qs) for kt in range(n + 1)]
            lgs[n] = lgs[n] + causal_bias
            return softmax_pv(lgs, [vselt_ref[kt] for kt in range(n + 1)])
        return run

    o_slc = lax.switch(qi, [sel_variant(n) for n in range(seq // tq)])

    n_wt = n_pad + 1
    lgs = []
    for w in range(n_wt):
        rows = pl.ds(pl.multiple_of(t0 + w * tk, tk), tk)
        lg = _nt_dot(kwin_ref[rows, :], qs)
        if w == 0:
            lg = lg + band_bias
        if w == n_wt - 1:
            lg = lg + causal_bias
        lgs.append(lg)
    o_win = softmax_pv(lgs, [vwint_ref[qi + w] for w in range(n_wt)])

    gt_ref[...] = jax.nn.sigmoid(gate_ref[...]).T
    ys = []
    for r in range(rep):
        cols = slice(r * tq, (r + 1) * tq)
        gate = lambda c: gt_ref[pl.ds(3 * (g * rep + r) + c, 1), :]
        o = gate(0) * o_cmp[:, cols] + gate(1) * o_slc[:, cols] + gate(2) * o_win[:, cols]
        ys.append(o * lax.rsqrt(jnp.mean(o * o, axis=0, keepdims=True) + EPS))
    for pair in range(rep // 2):
        cols = slice(pair * LANES, (pair + 1) * LANES)
        y = jnp.concatenate([ys[2 * pair], ys[2 * pair + 1]], axis=0).T * norm_ref[:, cols]
        z = z_ref[:, cols]
        o_ref[:, cols] = (y * (z * jax.nn.sigmoid(z))).astype(o_ref.dtype)


def _nsa(proj, gate, kc, vc, featk, featw, qfeat, mmap, place, norm, *, batch, seq, cols, n_cmp):
    tq = NSA_TILE
    nq = seq // tq
    G = NSA_KV_HEADS
    gw = norm.shape[1] // G
    n_half = kc.shape[0] // (batch * G)
    const = lambda a: pl.BlockSpec(a.shape, lambda b, g, i: (0,) * a.ndim)
    stream = lambda name: pl.BlockSpec((seq, LANES), lambda b, g, i, o=cols[name] // LANES: (b, o + g))
    return pl.pallas_call(
        functools.partial(_nsa_kernel, n_cmp=n_cmp),
        grid=(batch, G, nq),
        in_specs=[
            pl.BlockSpec((tq, gw), lambda b, g, i, o=cols["q_a"] // gw: (b * nq + i, o + g)),
            pl.BlockSpec((tq, gw), lambda b, g, i, o=cols["z_a"] // gw: (b * nq + i, o + g)),
            pl.BlockSpec((tq, gate.shape[1]), lambda b, g, i: (b * nq + i, 0)),
            stream("slc"), stream("win"),
            pl.BlockSpec((n_half, LANES), lambda b, g, i: (b * G + g, 0)),
            pl.BlockSpec((NSA_HEAD_DIM, n_half), lambda b, g, i: (b * G + g, 0)),
            const(featk), const(featw),
            pl.BlockSpec((1,) + qfeat.shape[1:], lambda b, g, i: (g, 0, 0)),
            const(mmap), const(place),
            pl.BlockSpec((1, gw), lambda b, g, i: (0, g)),
        ],
        out_specs=pl.BlockSpec((tq, gw), lambda b, g, i: (b * nq + i, g)),
        out_shape=jax.ShapeDtypeStruct((batch * seq, G * gw), BF16),
        scratch_shapes=[
            pltpu.VMEM((seq, LANES), BF16),
            pltpu.VMEM((seq // tq, NSA_HEAD_DIM, tq), BF16),
            pltpu.VMEM((seq + WINDOW, LANES), BF16),
            pltpu.VMEM(((seq + WINDOW) // tq, NSA_HEAD_DIM, tq), BF16),
            pltpu.VMEM((gate.shape[1], tq), F32),
        ],
        compiler_params=pltpu.CompilerParams(
            dimension_semantics=("arbitrary", "arbitrary", "arbitrary"),
            vmem_limit_bytes=VMEM_LIMIT),
        name="nsa",
    )(proj, proj, gate, proj, proj, kc, vc, featk, featw, qfeat, mmap, place, norm)


def _bf16_terms(x, n):
    terms, rest = [], np.asarray(x, np.float64)
    for _ in range(n):
        t = rest.astype(np.float32).astype(ml_dtypes.bfloat16).astype(np.float64)
        terms.append(t.astype(np.float32))
        rest = rest - t
    return terms


def _nsa_tables(seq, n_half, n_cmp, heads):
    assert seq // SEL_BLOCK <= MAX_SEL_BLOCKS and FLAG_LANE < LANES

    def key_features(pos, onehot_blocks):
        f = np.zeros((len(pos), LANES), np.float32)
        if onehot_blocks:
            f[np.arange(len(pos)), SEL_LANE0 + pos // SEL_BLOCK] = 1.0
        f[:, POS_LANE0:POS_LANE0 + N_SPLIT] = ((pos // 64) * 64)[:, None]
        f[:, POS_LANE0 + N_SPLIT:POS_LANE0 + 2 * N_SPLIT] = (pos % 64)[:, None]
        return f

    featk = key_features(np.arange(seq), True)
    featw = np.concatenate([np.zeros((WINDOW, LANES), np.float32), key_features(np.arange(seq), False)])
    featw[:WINDOW, FLAG_LANE] = NEG_INF
    featc = key_features(CMP_STRIDE * np.arange(n_half) + CMP_BLOCK - 1, False)
    featc[n_cmp:, FLAG_LANE] = NEG_INF

    slopes = (2.0 ** (-8.0 * np.arange(1, heads + 1) / heads)).astype(np.float32).astype(np.float64)
    terms = _bf16_terms(slopes * LOG2E, N_SPLIT)
    rep = heads // NSA_KV_HEADS
    qfeat = np.zeros((NSA_KV_HEADS, 8, LANES), np.float32)
    for h in range(heads):
        for i, t in enumerate(terms):
            qfeat[h // rep, h % rep, POS_LANE0 + i] = t[h]
            qfeat[h // rep, h % rep, POS_LANE0 + N_SPLIT + i] = t[h]
        qfeat[h // rep, h % rep, FLAG_LANE] = 1.0

    cs = CMP_STRIDE * np.arange(n_half)[None, :]
    ss = SEL_BLOCK * np.arange(LANES)[:, None]
    overlap = np.clip(np.minimum(cs + CMP_BLOCK, ss + SEL_BLOCK) - np.maximum(cs, ss), 0, None)
    mmap = (overlap / CMP_BLOCK) * (np.arange(n_half)[None, :] < n_cmp) * (ss < seq)
    place = np.zeros((MAX_SEL_BLOCKS, LANES), np.float32)
    place[np.arange(MAX_SEL_BLOCKS), SEL_LANE0 + np.arange(MAX_SEL_BLOCKS)] = 1.0
    bf = lambda a: jnp.asarray(a, dtype=BF16)
    return bf(featk), bf(featw), bf(featc), jnp.asarray(qfeat), bf(mmap), bf(place)


def _hgrn_kernel(q_ref, f_ref, v_ref, z_ref, lb_ref, norm_ref, cum_ref, o_ref, *, group):
    C, SUB = HGRN_CHUNK, HGRN_SUB
    n_sub = C // SUB
    seq, dk = q_ref.shape

    lbr = lb_ref[...]
    e = jnp.exp(lbr - jnp.max(lbr, axis=0, keepdims=True))
    lb = e[0:1, :] / jnp.sum(e, axis=0, keepdims=True)
    gain = norm_ref[...]
    cum = cum_ref[...]
    sub = lambda x, i: x[i * SUB:(i + 1) * SUB]
    masks = []
    for i in range(n_sub):
        width = (i + 1) * SUB
        masks.append(lax.broadcasted_iota(jnp.int32, (SUB, width), 1)
                     <= lax.broadcasted_iota(jnp.int32, (SUB, width), 0) + i * SUB)

    def block(cb, st):
        chunks = range(group)
        rows = [pl.ds(pl.multiple_of((cb * group + j) * C, C), C) for j in chunks]
        v16 = [v_ref[rows[j], :].astype(BF16) for j in chunks]
        f = [lb + (1.0 - lb) * jax.nn.sigmoid(f_ref[rows[j], :]) for j in chunks]
        k = [1.0 - f[j] for j in chunks]
        parts = [_split3(jnp.log(f[j])) for j in chunks]
        a = [(jnp.dot(cum, parts[j][0], preferred_element_type=F32)
              + jnp.dot(cum, parts[j][1], preferred_element_type=F32)
              + jnp.dot(cum, parts[j][2], preferred_element_type=F32)) for j in chunks]
        q1, k1, k2, qb, k3, start, dec = [], [], [], [], [], [], []
        for j in chunks:
            q1.append(q_ref[rows[j], :] * jnp.exp(a[j]))
            k1.append(k[j] * jnp.exp(-a[j]))
            tot = [a[j][(i + 1) * SUB - 1:(i + 1) * SUB] for i in range(n_sub)]
            s = [jnp.zeros_like(tot[0])]
            for i in range(n_sub):
                s.append(s[i] + tot[i])
            start.append(s)
            dec.append(jnp.exp(s[n_sub]))
            qb.append(jnp.concatenate([sub(q1[j], i) * jnp.exp(s[i]) for i in range(n_sub)], axis=0)
                      .astype(BF16))
            k2.append([sub(k1[j], i) * jnp.exp(tot[i]) for i in range(n_sub)])
            k3.append(jnp.concatenate([sub(k1[j], i) * jnp.exp(s[n_sub] - s[i]) for i in range(n_sub)],
                                      axis=0).astype(BF16))
        att = []
        for j in chunks:
            row_blocks = []
            for i in range(n_sub):
                rhs = [k2[j][jb] * jnp.exp(start[j][i] - start[j][jb + 1]) if jb < i - 1 else k2[j][jb]
                       for jb in range(i)]
                rhs.append(sub(k1[j], i))
                rhs = jnp.concatenate(rhs, axis=0) if len(rhs) > 1 else rhs[0]
                sc = _nt_dot(sub(q1[j], i).astype(BF16), rhs.astype(BF16))
                row_blocks.append(jnp.where(masks[i], sc, 0.0).astype(BF16))
            att.append(row_blocks)
        intra = [jnp.concatenate([jnp.dot(att[j][i], v16[j][0:(i + 1) * SUB], preferred_element_type=F32)
                                  for i in range(n_sub)], axis=0) for j in chunks]
        incr = [_tn_dot(v16[j], k3[j]) for j in chunks]
        inter = []
        for j in chunks:
            inter.append(_nt_dot(qb[j], st.astype(BF16)))
            st = st * dec[j] + incr[j]
        for j in chunks:
            o = inter[j] + intra[j]
            y = o * lax.rsqrt(jnp.mean(o * o, axis=-1, keepdims=True) + EPS) * gain
            z = z_ref[rows[j], :]
            o_ref[rows[j], :] = (y * (z * jax.nn.sigmoid(z))).astype(o_ref.dtype)
        return st

    lax.fori_loop(0, seq // (C * group), block, jnp.zeros((dk, dk), F32))


def _hgrn_patterns():
    C, SUB = HGRN_CHUNK, HGRN_SUB
    t = np.arange(C)[:, None]
    s = np.arange(C)[None, :]
    return (((t // SUB) == (s // SUB)) & (s <= t)).astype(np.float32)


def _hgrn(proj, lower_bounds, norm, *, batch, seq, cols):
    dk = HGRN_HEAD_DIM
    heads = norm.shape[1] // dk
    cum = jnp.asarray(_hgrn_patterns(), dtype=BF16)
    col = lambda name: (lambda b, h, o=cols[name] // dk: (b, o + h))
    return pl.pallas_call(
        functools.partial(_hgrn_kernel, group=8),
        grid=(batch, heads),
        in_specs=[
            pl.BlockSpec((seq, dk), col("q_h")),
            pl.BlockSpec((seq, dk), col("f_h")),
            pl.BlockSpec((seq, dk), col("i_h")),
            pl.BlockSpec((seq, dk), col("z_h")),
            pl.BlockSpec((lower_bounds.shape[0], dk), lambda b, h: (0, h)),
            pl.BlockSpec((1, dk), lambda b, h: (0, h)),
            pl.BlockSpec(cum.shape, lambda b, h: (0, 0)),
        ],
        out_specs=pl.BlockSpec((seq, dk), lambda b, h: (b, h)),
        out_shape=jax.ShapeDtypeStruct((batch * seq, heads * dk), BF16),
        compiler_params=pltpu.CompilerParams(
            dimension_semantics=("arbitrary", "arbitrary"),
            vmem_limit_bytes=VMEM_LIMIT),
        name="hgrn",
    )(proj, proj, proj, proj, lower_bounds, norm, cum)


def _outproj_kernel(x_ref, oa_ref, oh_ref, wa_ref, wh_ref, g_ref, o_ref):
    y = x_ref[...] + jnp.dot(oa_ref[...], wa_ref[...], preferred_element_type=F32)
    y = y + jnp.dot(oh_ref[...], wh_ref[...], preferred_element_type=F32)
    ms = jnp.mean(y * y, axis=-1, keepdims=True)
    o_ref[...] = y * lax.rsqrt(ms + EPS) * g_ref[...]


def _outproj(x2, o_a, o_h, wa, wh, g, *, tm):
    M, D = x2.shape
    const = lambda a: pl.BlockSpec(a.shape, lambda i: (0,) * a.ndim)
    return pl.pallas_call(
        _outproj_kernel,
        grid=(M // tm,),
        in_specs=[
            pl.BlockSpec((tm, D), lambda i: (i, 0)),
            pl.BlockSpec((tm, o_a.shape[1]), lambda i: (i, 0)),
            pl.BlockSpec((tm, o_h.shape[1]), lambda i: (i, 0)),
            const(wa), const(wh), const(g),
        ],
        out_specs=pl.BlockSpec((tm, D), lambda i: (i, 0)),
        out_shape=jax.ShapeDtypeStruct((M, D), F32),
        compiler_params=pltpu.CompilerParams(
            dimension_semantics=("arbitrary",),
            vmem_limit_bytes=VMEM_LIMIT),
        name="outproj",
    )(x2, o_a, o_h, wa, wh, g)


def kernel(x, norm_in, w_in, cmp_pe_k, cmp_w1_k, cmp_w2_k, cmp_pe_v, cmp_w1_v, cmp_w2_v,
           lower_bounds, nsa_out_norm, hgrn_out_norm, w_out, final_norm):
    B, S, D = x.shape
    assert norm_in.shape[0] == 1, "single-layer problem"
    nsa_w = nsa_out_norm.shape[1]
    hgrn_w = hgrn_out_norm.shape[1]
    dh = NSA_HEAD_DIM
    G = NSA_KV_HEADS
    heads = nsa_w // dh
    kvw = G * dh
    n_gate = 3 * heads
    n_cmp = (S - CMP_BLOCK) // CMP_STRIDE + 1
    n_half = S // CMP_STRIDE
    assert S % NSA_TILE == 0 and WINDOW % NSA_TILE == 0 and n_half <= LANES and 2 * dh == LANES

    names = ["q_a", "k_cmp", "v_cmp", "k_slc", "v_slc", "k_win", "v_win", "gate", "z_a",
             "q_h", "f_h", "i_h", "z_h"]
    widths = [nsa_w] + [kvw] * 6 + [n_gate, nsa_w] + [hgrn_w] * 4
    starts = dict(zip(names, np.cumsum([0] + widths[:-1]).tolist()))
    wd = dict(zip(names, widths))
    w = w_in[0]
    piece = lambda n: w[:, starts[n]:starts[n] + wd[n]]

    def paired(kname, vname):
        k = piece(kname).reshape(D, G, dh)
        v = piece(vname).reshape(D, G, dh)
        return jnp.concatenate([k, v], axis=-1).reshape(D, G * 2 * dh)

    blocks = [("q_a", piece("q_a") * (dh ** -0.5)), ("z_a", piece("z_a")),
              ("cmp", paired("k_cmp", "v_cmp")), ("slc", paired("k_slc", "v_slc")),
              ("win", paired("k_win", "v_win")),
              ("q_h", piece("q_h")), ("f_h", piece("f_h")), ("i_h", piece("i_h")), ("z_h", piece("z_h"))]
    cols, off = {}, 0
    for n, blk in blocks:
        cols[n] = off
        off += blk.shape[1]
    w_main = jnp.concatenate([blk for _, blk in blocks], axis=1).astype(BF16)
    w_gate = jnp.pad(piece("gate"), ((0, 0), (0, LANES - n_gate))).astype(BF16)

    x2 = x.reshape(B * S, D)
    proj, gate = _proj(x2, norm_in, w_main, w_gate, tm=min(1024, B * S), tn=1536)

    featk, featw, featc, qfeat, mmap, place = _nsa_tables(S, n_half, n_cmp, heads)

    def w1_halves(w1k, w1v):
        hk = w1k.shape[1]
        k3 = w1k.reshape(2, CMP_STRIDE, dh, hk)
        v3 = w1v.reshape(2, CMP_STRIDE, dh, hk)
        zk = jnp.zeros_like(k3[0])
        top = lambda a: jnp.concatenate([a, zk], axis=-1)
        bot = lambda a: jnp.concatenate([zk, a], axis=-1)
        half = lambda i: jnp.concatenate([top(k3[i]), bot(v3[i])], axis=1).reshape(CMP_STRIDE * 2 * dh, 2 * hk)
        return half(0).astype(BF16), half(1).astype(BF16)

    wp, wq = w1_halves(cmp_w1_k[0], cmp_w1_v[0])
    zk = jnp.zeros_like(cmp_w2_k[0])
    w2 = jnp.concatenate([jnp.concatenate([cmp_w2_k[0], zk], axis=1),
                          jnp.concatenate([zk, cmp_w2_v[0]], axis=1)], axis=0).astype(BF16)
    pe = jnp.concatenate([cmp_pe_k[0].reshape(2, CMP_STRIDE, dh), cmp_pe_v[0].reshape(2, CMP_STRIDE, dh)],
                         axis=-1).reshape(2, CMP_STRIDE * 2 * dh)
    kc, vc = _compress(proj, pe, wp, wq, w2, jnp.tile(featc, (G, 1)),
                       batch=B, seq=S, col0=cols["cmp"], n_half=n_half)

    o_a = _nsa(proj, gate, kc, vc, featk, featw, qfeat, mmap, place, nsa_out_norm,
               batch=B, seq=S, cols=cols, n_cmp=n_cmp)
    o_h = _hgrn(proj, lower_bounds, hgrn_out_norm, batch=B, seq=S, cols=cols)

    wo = w_out[0].astype(BF16)
    out = _outproj(x2, o_a, o_h, wo[:nsa_w], wo[nsa_w:], final_norm.reshape(1, D), tm=512)
    return out.reshape(B, S, D)
```

```python
import functools

import ml_dtypes
import numpy as np
import jax
import jax.numpy as jnp
from jax import lax
from jax.experimental import pallas as pl
from jax.experimental.pallas import tpu as pltpu

F32 = jnp.float32
BF16 = jnp.bfloat16

EPS = 1e-6
NEG_INF = -1e30
LOG2E = 1.4426950408889634

NSA_HEAD_DIM = 64
NSA_KV_HEADS = 4
CMP_BLOCK = 32
CMP_STRIDE = 16
SEL_BLOCK = 64
SEL_TOP = 8
SEL_BONUS = 1.0e4
WINDOW = 512
HGRN_HEAD_DIM = 128
HGRN_CHUNK = 64
HGRN_SUB = 16

LANES = 128
VMEM_LIMIT = 56 * 1024 * 1024
NSA_TILE = 256

SEL_LANE0 = NSA_HEAD_DIM
MAX_SEL_BLOCKS = 32
POS_LANE0 = SEL_LANE0 + MAX_SEL_BLOCKS
N_SPLIT = 4
FLAG_LANE = POS_LANE0 + 2 * N_SPLIT


def _nt_dot(a, b):
    return lax.dot_general(a, b, (((1,), (1,)), ((), ())), preferred_element_type=F32)


def _tn_dot(a, b):
    return lax.dot_general(a, b, (((0,), (0,)), ((), ())), preferred_element_type=F32)


def _split3(x):
    hi = x.astype(BF16)
    r1 = x - hi.astype(F32)
    mid = r1.astype(BF16)
    lo = (r1 - mid.astype(F32)).astype(BF16)
    return hi, mid, lo


def _proj_kernel(x_ref, g_ref, w_ref, wg_ref, o_ref, og_ref, h_ref, *, row_chunk):
    j = pl.program_id(1)

    @pl.when(j == 0)
    def _():
        n_chunks = x_ref.shape[0] // row_chunk

        def body(c, carry):
            rows = pl.ds(pl.multiple_of(c * row_chunk, row_chunk), row_chunk)
            x = x_ref[rows, :]
            ms = jnp.mean(x * x, axis=-1, keepdims=True)
            h_ref[rows, :] = (x * lax.rsqrt(ms + EPS) * g_ref[...]).astype(BF16)
            return carry

        lax.fori_loop(0, n_chunks, body, 0)
        og_ref[...] = jnp.dot(h_ref[...], wg_ref[...], preferred_element_type=F32)

    o_ref[...] = jnp.dot(h_ref[...], w_ref[...], preferred_element_type=F32)


def _proj(x2, g, w, wg, *, tm, tn):
    M, D = x2.shape
    N = w.shape[1]
    NG = wg.shape[1]
    return pl.pallas_call(
        functools.partial(_proj_kernel, row_chunk=128),
        grid=(M // tm, N // tn),
        in_specs=[
            pl.BlockSpec((tm, D), lambda i, j: (i, 0)),
            pl.BlockSpec((1, D), lambda i, j: (0, 0)),
            pl.BlockSpec((D, tn), lambda i, j: (0, j)),
            pl.BlockSpec((D, NG), lambda i, j: (0, 0)),
        ],
        out_specs=[
            pl.BlockSpec((tm, tn), lambda i, j: (i, j)),
            pl.BlockSpec((tm, NG), lambda i, j: (i, 0)),
        ],
        out_shape=[
            jax.ShapeDtypeStruct((M, N), F32),
            jax.ShapeDtypeStruct((M, NG), F32),
        ],
        scratch_shapes=[pltpu.VMEM((tm, D), BF16)],
        compiler_params=pltpu.CompilerParams(
            dimension_semantics=("arbitrary", "arbitrary"),
            vmem_limit_bytes=VMEM_LIMIT),
        name="proj",
    )(x2, g, w, wg)


def _compress_kernel(c0_ref, c1_ref, c2_ref, c3_ref, pe_ref, wp_ref, wq_ref, w2_ref, feat_ref,
                     kc_ref, vc_ref, x_ref, *, n_half):
    for g, c_ref in enumerate((c0_ref, c1_ref, c2_ref, c3_ref)):
        for l in range(CMP_STRIDE):
            x_ref[g * n_half:(g + 1) * n_half, l * LANES:(l + 1) * LANES] = (
                c_ref[pl.ds(l, n_half, stride=CMP_STRIDE), :])
    x = x_ref[...]
    rows = x.shape[0]
    first = jnp.dot((x + pe_ref[0:1, :]).astype(BF16), wp_ref[...], preferred_element_type=F32)
    second = jnp.dot((x + pe_ref[1:2, :]).astype(BF16), wq_ref[...], preferred_element_type=F32)
    hidden = first + pltpu.roll(second, rows - 1, 0)
    out = jnp.dot(jax.nn.gelu(hidden).astype(BF16), w2_ref[...], preferred_element_type=F32)
    lane = lax.broadcasted_iota(jnp.int32, out.shape, 1)
    dh = NSA_HEAD_DIM
    kc_ref[...] = jnp.where(lane < dh, out.astype(BF16), feat_ref[...])
    for g in range(NSA_KV_HEADS):
        vc_ref[g * dh:(g + 1) * dh, :] = out[g * n_half:(g + 1) * n_half, :].T[dh:, :].astype(BF16)


def _compress(proj, pe, wp, wq, w2, feat, *, batch, seq, col0, n_half):
    rows = NSA_KV_HEADS * n_half
    rows_t = NSA_KV_HEADS * NSA_HEAD_DIM
    const = lambda a: pl.BlockSpec(a.shape, lambda b: (0,) * a.ndim)
    stream = lambda g: pl.BlockSpec((seq, LANES), lambda b, o=col0 // LANES + g: (b, o))
    return pl.pallas_call(
        functools.partial(_compress_kernel, n_half=n_half),
        grid=(batch,),
        in_specs=[stream(0), stream(1), stream(2), stream(3),
                  const(pe), const(wp), const(wq), const(w2), const(feat)],
        out_specs=[
            pl.BlockSpec((rows, LANES), lambda b: (b, 0)),
            pl.BlockSpec((rows_t, n_half), lambda b: (b, 0)),
        ],
        out_shape=[
            jax.ShapeDtypeStruct((batch * rows, LANES), BF16),
            jax.ShapeDtypeStruct((batch * rows_t, n_half), BF16),
        ],
        scratch_shapes=[pltpu.VMEM((rows, CMP_STRIDE * LANES), F32)],
        compiler_params=pltpu.CompilerParams(
            dimension_semantics=("arbitrary",),
            vmem_limit_bytes=VMEM_LIMIT),
        name="compress",
    )(proj, proj, proj, proj, pe, wp, wq, w2, feat)


def _nsa_kernel(q_ref, z_ref, gate_ref, slc_ref, win_ref, kc_ref, vct_ref, featk_ref, featw_ref,
                qfeat_ref, mmap_ref, place_ref, norm_ref, o_ref,
                ksel_ref, vselt_ref, kwin_ref, vwint_ref, gt_ref, *, n_cmp):
    g = pl.program_id(1)
    qi = pl.program_id(2)
    tq = q_ref.shape[0]
    tk = tq
    seq = slc_ref.shape[0]
    dh = NSA_HEAD_DIM
    rep = q_ref.shape[1] // dh
    n_half = kc_ref.shape[0]
    n_pad = WINDOW // tk
    t0 = qi * tq

    @pl.when(qi == 0)
    def _():
        lane = lax.broadcasted_iota(jnp.int32, (seq, LANES), 1)
        kv = slc_ref[...]
        ksel_ref[...] = jnp.where(lane < dh, kv.astype(BF16), featk_ref[...])
        vt = kv.T[dh:, :].astype(BF16)
        for kt in range(seq // tk):
            vselt_ref[kt] = vt[:, kt * tk:(kt + 1) * tk]
        kv = win_ref[...]
        kwin_ref[0:WINDOW, :] = featw_ref[0:WINDOW, :]
        kwin_ref[WINDOW:, :] = jnp.where(lane < dh, kv.astype(BF16), featw_ref[WINDOW:, :])
        vt = kv.T[dh:, :].astype(BF16)
        for kt in range(n_pad):
            vwint_ref[kt] = jnp.zeros((dh, tk), BF16)
        for kt in range(seq // tk):
            vwint_ref[n_pad + kt] = vt[:, kt * tk:(kt + 1) * tk]

    heads = lambda a: jnp.concatenate([a] * rep, axis=1)
    sub8 = lambda a: a.reshape(a.shape[0] // 8, 8, a.shape[1])

    def softmax_pv(lgs, v_tiles):
        m8 = functools.reduce(jnp.maximum, [jnp.max(sub8(lg), axis=0) for lg in lgs])
        m = jnp.max(m8, axis=0, keepdims=True)
        l8 = jnp.zeros((8, rep * tq), F32)
        acc = jnp.zeros((dh, rep * tq), F32)
        for lg, vt in zip(lgs, v_tiles):
            pt = jnp.exp2(lg - m)
            l8 = l8 + jnp.sum(sub8(pt), axis=0)
            acc = acc + jnp.dot(vt, pt.astype(BF16), preferred_element_type=F32)
        return acc / jnp.sum(l8, axis=0, keepdims=True)

    def step(n):
        t0 = n * tq
        lane = lax.broadcasted_iota(jnp.int32, (tq, LANES), 1)
        key_i = lax.broadcasted_iota(jnp.int32, (tk, tq), 0)
        qry_i = lax.broadcasted_iota(jnp.int32, (tk, tq), 1)
        causal_bias = heads(jnp.where(key_i <= qry_i, 0.0, NEG_INF))
        band_bias = heads(jnp.where(key_i > qry_i, 0.0, NEG_INF))

        qfeat = qfeat_ref[0]
        qa = []
        for r in range(rep):
            qcol = q_ref[:, (r // 2) * LANES:(r // 2 + 1) * LANES]
            if r % 2:
                qcol = pltpu.roll(qcol, dh, 1)
            qa.append(jnp.where(lane < dh, qcol * LOG2E, qfeat[r:r + 1, :]))
        q0 = jnp.concatenate(qa, axis=0).astype(BF16)

        n_wt = n_pad + 1
        lgs = []
        for w in range(n_wt):
            lg = _nt_dot(kwin_ref[t0 + w * tk:t0 + (w + 1) * tk, :], q0)
            if w == 0:
                lg = lg + band_bias
            if w == n_wt - 1:
                lg = lg + causal_bias
            lgs.append(lg)
        o_win = softmax_pv(lgs, [vwint_ref[n + w] for w in range(n_wt)])

        nc = min(n_half, (t0 + tq) // CMP_STRIDE)
        n_idx = lax.broadcasted_iota(jnp.int32, (nc, tq), 0)
        t_idx = t0 + lax.broadcasted_iota(jnp.int32, (nc, tq), 1)
        valid_c = heads((CMP_STRIDE * n_idx + (CMP_BLOCK - 1) <= t_idx) & (n_idx < n_cmp))
        lg = jnp.where(valid_c, _nt_dot(kc_ref[0:nc, :], q0), NEG_INF)
        e = jnp.exp2(lg - jnp.max(lg, axis=0, keepdims=True))
        p = jnp.where(valid_c, e / jnp.sum(e, axis=0, keepdims=True), 0.0)
        if nc < n_half:
            p = jnp.concatenate([p, jnp.zeros((n_half - nc, rep * tq), F32)], axis=0)
        o_cmp = jnp.dot(vct_ref[...], p.astype(BF16), preferred_element_type=F32)
        p_sum = p[:, 0:tq]
        for r in range(1, rep):
            p_sum = p_sum + p[:, r * tq:(r + 1) * tq]

        nb = min(MAX_SEL_BLOCKS, (t0 + tq) // SEL_BLOCK)
        mm = mmap_ref[...]
        hi, mid, lo = _split3(p_sum)
        p_slc = (jnp.dot(mm, hi, preferred_element_type=F32) + jnp.dot(mm, mid, preferred_element_type=F32)
                 + jnp.dot(mm, lo, preferred_element_type=F32))[0:MAX_SEL_BLOCKS]
        jj = lax.broadcasted_iota(jnp.int32, (MAX_SEL_BLOCKS, tq), 0)
        tt = t0 + lax.broadcasted_iota(jnp.int32, (MAX_SEL_BLOCKS, tq), 1)
        cur = lax.shift_right_logical(tt, int(np.log2(SEL_BLOCK)))
        forced = (jj == 0) | (jj == cur) | (jj == cur - 1)
        future = jj > cur
        score = jnp.where(future, -1.0, p_slc + jnp.where(forced, SEL_BONUS, 0.0))
        rank = jnp.zeros((MAX_SEL_BLOCKS, tq), jnp.int32)
        for i in range(nb):
            other = jnp.broadcast_to(score[i:i + 1, :], score.shape)
            beats = (other > score) | ((other == score) & (jj > i))
            rank = rank + beats.astype(jnp.int32)
        chosen = jnp.where((rank < SEL_TOP) & jnp.logical_not(future), 1.0, 0.0).astype(BF16)
        placed = _tn_dot(chosen, place_ref[...])
        sel_lanes = (lane >= SEL_LANE0) & (lane < SEL_LANE0 + MAX_SEL_BLOCKS)
        sel_bias = (placed - 1.0) * (-NEG_INF)
        qs = jnp.concatenate([jnp.where(sel_lanes, sel_bias, a) for a in qa], axis=0).astype(BF16)

        lgs = [_nt_dot(ksel_ref[kt * tk:(kt + 1) * tk, :], qs) for kt in range(n + 1)]
        lgs[n] = lgs[n] + causal_bias
        o_slc = softmax_pv(lgs, [vselt_ref[kt] for kt in range(n + 1)])

        gt_ref[...] = jax.nn.sigmoid(gate_ref[...]).T
        ys = []
        for r in range(rep):
            cols = slice(r * tq, (r + 1) * tq)
            gate = lambda c: gt_ref[pl.ds(3 * (g * rep + r) + c, 1), :]
            o = gate(0) * o_cmp[:, cols] + gate(1) * o_slc[:, cols] + gate(2) * o_win[:, cols]
            ys.append(o * lax.rsqrt(jnp.mean(o * o, axis=0, keepdims=True) + EPS))
        for pair in range(rep // 2):
            cols = slice(pair * LANES, (pair + 1) * LANES)
            y = jnp.concatenate([ys[2 * pair], ys[2 * pair + 1]], axis=0).T * norm_ref[:, cols]
            z = z_ref[:, cols]
            o_ref[:, cols] = (y * (z * jax.nn.sigmoid(z))).astype(o_ref.dtype)

    lax.switch(qi, [functools.partial(step, n) for n in range(seq // tq)])


def _nsa(proj, gate, kc, vc, featk, featw, qfeat, mmap, place, norm, *, batch, seq, cols, n_cmp):
    tq = NSA_TILE
    nq = seq // tq
    G = NSA_KV_HEADS
    gw = norm.shape[1] // G
    n_half = kc.shape[0] // (batch * G)
    const = lambda a: pl.BlockSpec(a.shape, lambda b, g, i: (0,) * a.ndim)
    stream = lambda name: pl.BlockSpec((seq, LANES), lambda b, g, i, o=cols[name] // LANES: (b, o + g))
    return pl.pallas_call(
        functools.partial(_nsa_kernel, n_cmp=n_cmp),
        grid=(batch, G, nq),
        in_specs=[
            pl.BlockSpec((tq, gw), lambda b, g, i, o=cols["q_a"] // gw: (b * nq + i, o + g)),
            pl.BlockSpec((tq, gw), lambda b, g, i, o=cols["z_a"] // gw: (b * nq + i, o + g)),
            pl.BlockSpec((tq, gate.shape[1]), lambda b, g, i: (b * nq + i, 0)),
            stream("slc"), stream("win"),
            pl.BlockSpec((n_half, LANES), lambda b, g, i: (b * G + g, 0)),
            pl.BlockSpec((NSA_HEAD_DIM, n_half), lambda b, g, i: (b * G + g, 0)),
            const(featk), const(featw),
            pl.BlockSpec((1,) + qfeat.shape[1:], lambda b, g, i: (g, 0, 0)),
            const(mmap), const(place),
            pl.BlockSpec((1, gw), lambda b, g, i: (0, g)),
        ],
        out_specs=pl.BlockSpec((tq, gw), lambda b, g, i: (b * nq + i, g)),
        out_shape=jax.ShapeDtypeStruct((batch * seq, G * gw), BF16),
        scratch_shapes=[
            pltpu.VMEM((seq, LANES), BF16),
            pltpu.VMEM((seq // tq, NSA_HEAD_DIM, tq), BF16),
            pltpu.VMEM((seq + WINDOW, LANES), BF16),
            pltpu.VMEM(((seq + WINDOW) // tq, NSA_HEAD_DIM, tq), BF16),
            pltpu.VMEM((gate.shape[1], tq), F32),
        ],
        compiler_params=pltpu.CompilerParams(
            dimension_semantics=("arbitrary", "arbitrary", "arbitrary"),
            vmem_limit_bytes=VMEM_LIMIT),
        name="nsa",
    )(proj, proj, gate, proj, proj, kc, vc, featk, featw, qfeat, mmap, place, norm)


def _bf16_terms(x, n):
    terms, rest = [], np.asarray(x, np.float64)
    for _ in range(n):
        t = rest.astype(np.float32).astype(ml_dtypes.bfloat16).astype(np.float64)
        terms.append(t.astype(np.float32))
        rest = rest - t
    return terms


def _nsa_tables(seq, n_half, n_cmp, heads):
    assert seq // SEL_BLOCK <= MAX_SEL_BLOCKS and FLAG_LANE < LANES

    def key_features(pos, onehot_blocks):
        f = np.zeros((len(pos), LANES), np.float32)
        if onehot_blocks:
            f[np.arange(len(pos)), SEL_LANE0 + pos // SEL_BLOCK] = 1.0
        f[:, POS_LANE0:POS_LANE0 + N_SPLIT] = ((pos // 64) * 64)[:, None]
        f[:, POS_LANE0 + N_SPLIT:POS_LANE0 + 2 * N_SPLIT] = (pos % 64)[:, None]
        return f

    featk = key_features(np.arange(seq), True)
    featw = np.concatenate([np.zeros((WINDOW, LANES), np.float32), key_features(np.arange(seq), False)])
    featw[:WINDOW, FLAG_LANE] = NEG_INF
    featc = key_features(CMP_STRIDE * np.arange(n_half) + CMP_BLOCK - 1, False)
    featc[n_cmp:, FLAG_LANE] = NEG_INF

    slopes = (2.0 ** (-8.0 * np.arange(1, heads + 1) / heads)).astype(np.float32).astype(np.float64)
    terms = _bf16_terms(slopes * LOG2E, N_SPLIT)
    rep = heads // NSA_KV_HEADS
    qfeat = np.zeros((NSA_KV_HEADS, 8, LANES), np.float32)
    for h in range(heads):
        for i, t in enumerate(terms):
            qfeat[h // rep, h % rep, POS_LANE0 + i] = t[h]
            qfeat[h // rep, h % rep, POS_LANE0 + N_SPLIT + i] = t[h]
        qfeat[h // rep, h % rep, FLAG_LANE] = 1.0

    cs = CMP_STRIDE * np.arange(n_half)[None, :]
    ss = SEL_BLOCK * np.arange(LANES)[:, None]
    overlap = np.clip(np.minimum(cs + CMP_BLOCK, ss + SEL_BLOCK) - np.maximum(cs, ss), 0, None)
    mmap = (overlap / CMP_BLOCK) * (np.arange(n_half)[None, :] < n_cmp) * (ss < seq)
    place = np.zeros((MAX_SEL_BLOCKS, LANES), np.float32)
    place[np.arange(MAX_SEL_BLOCKS), SEL_LANE0 + np.arange(MAX_SEL_BLOCKS)] = 1.0
    bf = lambda a: jnp.asarray(a, dtype=BF16)
    return bf(featk), bf(featw), bf(featc), jnp.asarray(qfeat), bf(mmap), bf(place)


def _hgrn_kernel(q_ref, f_ref, v_ref, z_ref, lb_ref, norm_ref, cum_ref, o_ref, *, group):
    C, SUB = HGRN_CHUNK, HGRN_SUB
    n_sub = C // SUB
    seq, dk = q_ref.shape

    lbr = lb_ref[...]
    e = jnp.exp(lbr - jnp.max(lbr, axis=0, keepdims=True))
    lb = e[0:1, :] / jnp.sum(e, axis=0, keepdims=True)
    gain = norm_ref[...]
    cum = cum_ref[...]
    sub = lambda x, i: x[i * SUB:(i + 1) * SUB]
    masks = []
    for i in range(n_sub):
        width = (i + 1) * SUB
        masks.append(lax.broadcasted_iota(jnp.int32, (SUB, width), 1)
                     <= lax.broadcasted_iota(jnp.int32, (SUB, width), 0) + i * SUB)

    def block(cb, st):
        chunks = range(group)
        rows = [pl.ds(pl.multiple_of((cb * group + j) * C, C), C) for j in chunks]
        v16 = [v_ref[rows[j], :].astype(BF16) for j in chunks]
        f = [lb + (1.0 - lb) * jax.nn.sigmoid(f_ref[rows[j], :]) for j in chunks]
        k = [1.0 - f[j] for j in chunks]
        parts = [_split3(jnp.log(f[j])) for j in chunks]
        a = [(jnp.dot(cum, parts[j][0], preferred_element_type=F32)
              + jnp.dot(cum, parts[j][1], preferred_element_type=F32)
              + jnp.dot(cum, parts[j][2], preferred_element_type=F32)) for j in chunks]
        q1, k1, k2, qb, k3, start, dec = [], [], [], [], [], [], []
        for j in chunks:
            q1.append(q_ref[rows[j], :] * jnp.exp(a[j]))
            k1.append(k[j] * jnp.exp(-a[j]))
            tot = [a[j][(i + 1) * SUB - 1:(i + 1) * SUB] for i in range(n_sub)]
            s = [jnp.zeros_like(tot[0])]
            for i in range(n_sub):
                s.append(s[i] + tot[i])
            start.append(s)
            dec.append(jnp.exp(s[n_sub]))
            qb.append(jnp.concatenate([sub(q1[j], i) * jnp.exp(s[i]) for i in range(n_sub)], axis=0)
                      .astype(BF16))
            k2.append([sub(k1[j], i) * jnp.exp(tot[i]) for i in range(n_sub)])
            k3.append(jnp.concatenate([sub(k1[j], i) * jnp.exp(s[n_sub] - s[i]) for i in range(n_sub)],
                                      axis=0).astype(BF16))
        att = []
        for j in chunks:
            row_blocks = []
            for i in range(n_sub):
                rhs = [k2[j][jb] * jnp.exp(start[j][i] - start[j][jb + 1]) if jb < i - 1 else k2[j][jb]
                       for jb in range(i)]
                rhs.append(sub(k1[j], i))
                rhs = jnp.concatenate(rhs, axis=0) if len(rhs) > 1 else rhs[0]
                sc = _nt_dot(sub(q1[j], i).astype(BF16), rhs.astype(BF16))
                row_blocks.append(jnp.where(masks[i], sc, 0.0).astype(BF16))
            att.append(row_blocks)
        intra = [jnp.concatenate([jnp.dot(att[j][i], v16[j][0:(i + 1) * SUB], preferred_element_type=F32)
                                  for i in range(n_sub)], axis=0) for j in chunks]
        incr = [_tn_dot(v16[j], k3[j]) for j in chunks]
        inter = []
        for j in chunks:
            inter.append(_nt_dot(qb[j], st.astype(BF16)))
            st = st * dec[j] + incr[j]
        for j in chunks:
            o = inter[j] + intra[j]
            y = o * lax.rsqrt(jnp.mean(o * o, axis=-1, keepdims=True) + EPS) * gain
            z = z_ref[rows[j], :]
            o_ref[rows[j], :] = (y * (z * jax.nn.sigmoid(z))).astype(o_ref.dtype)
        return st

    lax.fori_loop(0, seq // (C * group), block, jnp.zeros((dk, dk), F32))


def _hgrn_patterns():
    C, SUB = HGRN_CHUNK, HGRN_SUB
    t = np.arange(C)[:, None]
    s = np.arange(C)[None, :]
    return (((t // SUB) == (s // SUB)) & (s <= t)).astype(np.float32)


def _hgrn(proj, lower_bounds, norm, *, batch, seq, cols):
    dk = HGRN_HEAD_DIM
    heads = norm.shape[1] // dk
    cum = jnp.asarray(_hgrn_patterns(), dtype=BF16)
    col = lambda name: (lambda b, h, o=cols[name] // dk: (b, o + h))
    return pl.pallas_call(
        functools.partial(_hgrn_kernel, group=8),
        grid=(batch, heads),
        in_specs=[
            pl.BlockSpec((seq, dk), col("q_h")),
            pl.BlockSpec((seq, dk), col("f_h")),
            pl.BlockSpec((seq, dk), col("i_h")),
            pl.BlockSpec((seq, dk), col("z_h")),
            pl.BlockSpec((lower_bounds.shape[0], dk), lambda b, h: (0, h)),
            pl.BlockSpec((1, dk), lambda b, h: (0, h)),
            pl.BlockSpec(cum.shape, lambda b, h: (0, 0)),
        ],
        out_specs=pl.BlockSpec((seq, dk), lambda b, h: (b, h)),
        out_shape=jax.ShapeDtypeStruct((batch * seq, heads * dk), BF16),
        compiler_params=pltpu.CompilerParams(
            dimension_semantics=("arbitrary", "arbitrary"),
            vmem_limit_bytes=VMEM_LIMIT),
        name="hgrn",
    )(proj, proj, proj, proj, lower_bounds, norm, cum)


def _outproj_kernel(x_ref, oa_ref, oh_ref, wa_ref, wh_ref, g_ref, o_ref):
    y = x_ref[...] + jnp.dot(oa_ref[...], wa_ref[...], preferred_element_type=F32)
    y = y + jnp.dot(oh_ref[...], wh_ref[...], preferred_element_type=F32)
    ms = jnp.mean(y * y, axis=-1, keepdims=True)
    o_ref[...] = y * lax.rsqrt(ms + EPS) * g_ref[...]


def _outproj(x2, o_a, o_h, wa, wh, g, *, tm):
    M, D = x2.shape
    const = lambda a: pl.BlockSpec(a.shape, lambda i: (0,) * a.ndim)
    return pl.pallas_call(
        _outproj_kernel,
        grid=(M // tm,),
        in_specs=[
            pl.BlockSpec((tm, D), lambda i: (i, 0)),
            pl.BlockSpec((tm, o_a.shape[1]), lambda i: (i, 0)),
            pl.BlockSpec((tm, o_h.shape[1]), lambda i: (i, 0)),
            const(wa), const(wh), const(g),
        ],
        out_specs=pl.BlockSpec((tm, D), lambda i: (i, 0)),
        out_shape=jax.ShapeDtypeStruct((M, D), F32),
        compiler_params=pltpu.CompilerParams(
            dimension_semantics=("arbitrary",),
            vmem_limit_bytes=VMEM_LIMIT),
        name="outproj",
    )(x2, o_a, o_h, wa, wh, g)


def kernel(x, norm_in, w_in, cmp_pe_k, cmp_w1_k, cmp_w2_k, cmp_pe_v, cmp_w1_v, cmp_w2_v,
           lower_bounds, nsa_out_norm, hgrn_out_norm, w_out, final_norm):
    B, S, D = x.shape
    assert norm_in.shape[0] == 1, "single-layer problem"
    nsa_w = nsa_out_norm.shape[1]
    hgrn_w = hgrn_out_norm.shape[1]
    dh = NSA_HEAD_DIM
    G = NSA_KV_HEADS
    heads = nsa_w // dh
    kvw = G * dh
    n_gate = 3 * heads
    n_cmp = (S - CMP_BLOCK) // CMP_STRIDE + 1
    n_half = S // CMP_STRIDE
    assert S % NSA_TILE == 0 and WINDOW % NSA_TILE == 0 and n_half <= LANES and 2 * dh == LANES

    names = ["q_a", "k_cmp", "v_cmp", "k_slc", "v_slc", "k_win", "v_win", "gate", "z_a",
             "q_h", "f_h", "i_h", "z_h"]
    widths = [nsa_w] + [kvw] * 6 + [n_gate, nsa_w] + [hgrn_w] * 4
    starts = dict(zip(names, np.cumsum([0] + widths[:-1]).tolist()))
    wd = dict(zip(names, widths))
    w = w_in[0]
    piece = lambda n: w[:, starts[n]:starts[n] + wd[n]]

    def paired(kname, vname):
        k = piece(kname).reshape(D, G, dh)
        v = piece(vname).reshape(D, G, dh)
        return jnp.concatenate([k, v], axis=-1).reshape(D, G * 2 * dh)

    blocks = [("q_a", piece("q_a") * (dh ** -0.5)), ("z_a", piece("z_a")),
              ("cmp", paired("k_cmp", "v_cmp")), ("slc", paired("k_slc", "v_slc")),
              ("win", paired("k_win", "v_win")),
              ("q_h", piece("q_h")), ("f_h", piece("f_h")), ("i_h", piece("i_h")), ("z_h", piece("z_h"))]
    cols, off = {}, 0
    for n, blk in blocks:
        cols[n] = off
        off += blk.shape[1]
    w_main = jnp.concatenate([blk for _, blk in blocks], axis=1).astype(BF16)
    w_gate = jnp.pad(piece("gate"), ((0, 0), (0, LANES - n_gate))).astype(BF16)

    x2 = x.reshape(B * S, D)
    proj, gate = _proj(x2, norm_in, w_main, w_gate, tm=min(1024, B * S), tn=1536)

    featk, featw, featc, qfeat, mmap, place = _nsa_tables(S, n_half, n_cmp, heads)

    def w1_halves(w1k, w1v):
        hk = w1k.shape[1]
        k3 = w1k.reshape(2, CMP_STRIDE, dh, hk)
        v3 = w1v.reshape(2, CMP_STRIDE, dh, hk)
        zk = jnp.zeros_like(k3[0])
        top = lambda a: jnp.concatenate([a, zk], axis=-1)
        bot = lambda a: jnp.concatenate([zk, a], axis=-1)
        half = lambda i: jnp.concatenate([top(k3[i]), bot(v3[i])], axis=1).reshape(CMP_STRIDE * 2 * dh, 2 * hk)
        return half(0).astype(BF16), half(1).astype(BF16)

    wp, wq = w1_halves(cmp_w1_k[0], cmp_w1_v[0])
    zk = jnp.zeros_like(cmp_w2_k[0])
    w2 = jnp.concatenate([jnp.concatenate([cmp_w2_k[0], zk], axis=1),
                          jnp.concatenate([zk, cmp_w2_v[0]], axis=1)], axis=0).astype(BF16)
    pe = jnp.concatenate([cmp_pe_k[0].reshape(2, CMP_STRIDE, dh), cmp_pe_v[0].reshape(2, CMP_STRIDE, dh)],
                         axis=-1).reshape(2, CMP_STRIDE * 2 * dh)
    kc, vc = _compress(proj, pe, wp, wq, w2, jnp.tile(featc, (G, 1)),
                       batch=B, seq=S, col0=cols["cmp"], n_half=n_half)

    o_a = _nsa(proj, gate, kc, vc, featk, featw, qfeat, mmap, place, nsa_out_norm,
               batch=B, seq=S, cols=cols, n_cmp=n_cmp)
    o_h = _hgrn(proj, lower_bounds, hgrn_out_norm, batch=B, seq=S, cols=cols)

    wo = w_out[0].astype(BF16)
    out = _outproj(x2, o_a, o_h, wo[:nsa_w], wo[nsa_w:], final_norm.reshape(1, D), tm=512)
    return out.reshape(B, S, D)
```

```python
import functools

import ml_dtypes
import numpy as np
import jax
import jax.numpy as jnp
from jax import lax
from jax.experimental import pallas as pl
from jax.experimental.pallas import tpu as pltpu

F32 = jnp.float32
BF16 = jnp.bfloat16

EPS = 1e-6
NEG_INF = -1e30
LOG2E = 1.4426950408889634

NSA_HEAD_DIM = 64
NSA_KV_HEADS = 4
CMP_BLOCK = 32
CMP_STRIDE = 16
SEL_BLOCK = 64
SEL_TOP = 8
SEL_BONUS = 1.0e4
WINDOW = 512
HGRN_HEAD_DIM = 128
HGRN_CHUNK = 64
HGRN_SUB = 16

LANES = 128
VMEM_LIMIT = 56 * 1024 * 1024
NSA_TILE = 256

SEL_LANE0 = NSA_HEAD_DIM
MAX_SEL_BLOCKS = 32
POS_LANE0 = SEL_LANE0 + MAX_SEL_BLOCKS
N_SPLIT = 4
FLAG_LANE = POS_LANE0 + 2 * N_SPLIT


def _nt_dot(a, b):
    return lax.dot_general(a, b, (((1,), (1,)), ((), ())), preferred_element_type=F32)


def _tn_dot(a, b):
    return lax.dot_general(a, b, (((0,), (0,)), ((), ())), preferred_element_type=F32)


def _split3(x):
    hi = x.astype(BF16)
    r1 = x - hi.astype(F32)
    mid = r1.astype(BF16)
    lo = (r1 - mid.astype(F32)).astype(BF16)
    return hi, mid, lo


def _proj_kernel(x_ref, g_ref, w_ref, wg_ref, o16_ref, o32_ref, og_ref, h_ref, *, row_chunk, n16):
    j = pl.program_id(1)

    @pl.when(j == 0)
    def _():
        n_chunks = x_ref.shape[0] // row_chunk

        def body(c, carry):
            rows = pl.ds(pl.multiple_of(c * row_chunk, row_chunk), row_chunk)
            x = x_ref[rows, :]
            ms = jnp.mean(x * x, axis=-1, keepdims=True)
            h_ref[rows, :] = (x * lax.rsqrt(ms + EPS) * g_ref[...]).astype(BF16)
            return carry

        lax.fori_loop(0, n_chunks, body, 0)
        og_ref[...] = jnp.dot(h_ref[...], wg_ref[...], preferred_element_type=F32)

    @pl.when(j < n16)
    def _():
        o16_ref[...] = jnp.dot(h_ref[...], w_ref[...], preferred_element_type=F32).astype(BF16)

    @pl.when(j >= n16)
    def _():
        o32_ref[...] = jnp.dot(h_ref[...], w_ref[...], preferred_element_type=F32)


def _proj(x2, g, w, wg, *, tm, tn, n16_cols):
    M, D = x2.shape
    N = w.shape[1]
    NG = wg.shape[1]
    n16 = n16_cols // tn
    assert n16 * tn == n16_cols and N % tn == 0
    return pl.pallas_call(
        functools.partial(_proj_kernel, row_chunk=128, n16=n16),
        grid=(M // tm, N // tn),
        in_specs=[
            pl.BlockSpec((tm, D), lambda i, j: (i, 0)),
            pl.BlockSpec((1, D), lambda i, j: (0, 0)),
            pl.BlockSpec((D, tn), lambda i, j: (0, j)),
            pl.BlockSpec((D, NG), lambda i, j: (0, 0)),
        ],
        out_specs=[
            pl.BlockSpec((tm, tn), lambda i, j: (i, jnp.minimum(j, n16 - 1))),
            pl.BlockSpec((tm, tn), lambda i, j: (i, jnp.maximum(j - n16, 0))),
            pl.BlockSpec((tm, NG), lambda i, j: (i, 0)),
        ],
        out_shape=[
            jax.ShapeDtypeStruct((M, n16_cols), BF16),
            jax.ShapeDtypeStruct((M, N - n16_cols), F32),
            jax.ShapeDtypeStruct((M, NG), F32),
        ],
        scratch_shapes=[pltpu.VMEM((tm, D), BF16)],
        compiler_params=pltpu.CompilerParams(
            dimension_semantics=("arbitrary", "arbitrary"),
            vmem_limit_bytes=VMEM_LIMIT),
        name="proj",
    )(x2, g, w, wg)


def _compress_kernel(c0_ref, c1_ref, c2_ref, c3_ref, pe_ref, wp_ref, wq_ref, w2_ref, feat_ref,
                     kc_ref, vc_ref, x_ref, c32_ref, *, n_half):
    for g, c_ref in enumerate((c0_ref, c1_ref, c2_ref, c3_ref)):
        c32_ref[...] = c_ref[...].astype(F32)
        for l in range(CMP_STRIDE):
            x_ref[g * n_half:(g + 1) * n_half, l * LANES:(l + 1) * LANES] = (
                c32_ref[pl.ds(l, n_half, stride=CMP_STRIDE), :])
    x = x_ref[...]
    rows = x.shape[0]
    first = jnp.dot((x + pe_ref[0:1, :]).astype(BF16), wp_ref[...], preferred_element_type=F32)
    second = jnp.dot((x + pe_ref[1:2, :]).astype(BF16), wq_ref[...], preferred_element_type=F32)
    hidden = first + pltpu.roll(second, rows - 1, 0)
    out = jnp.dot(jax.nn.gelu(hidden).astype(BF16), w2_ref[...], preferred_element_type=F32)
    lane = lax.broadcasted_iota(jnp.int32, out.shape, 1)
    dh = NSA_HEAD_DIM
    kc_ref[...] = jnp.where(lane < dh, out.astype(BF16), feat_ref[...])
    for g in range(NSA_KV_HEADS):
        vc_ref[g * dh:(g + 1) * dh, :] = out[g * n_half:(g + 1) * n_half, :].T[dh:, :].astype(BF16)


def _compress(proj, pe, wp, wq, w2, feat, *, batch, seq, col0, n_half):
    rows = NSA_KV_HEADS * n_half
    rows_t = NSA_KV_HEADS * NSA_HEAD_DIM
    const = lambda a: pl.BlockSpec(a.shape, lambda b: (0,) * a.ndim)
    stream = lambda g: pl.BlockSpec((seq, LANES), lambda b, o=col0 // LANES + g: (b, o))
    return pl.pallas_call(
        functools.partial(_compress_kernel, n_half=n_half),
        grid=(batch,),
        in_specs=[stream(0), stream(1), stream(2), stream(3),
                  const(pe), const(wp), const(wq), const(w2), const(feat)],
        out_specs=[
            pl.BlockSpec((rows, LANES), lambda b: (b, 0)),
            pl.BlockSpec((rows_t, n_half), lambda b: (b, 0)),
        ],
        out_shape=[
            jax.ShapeDtypeStruct((batch * rows, LANES), BF16),
            jax.ShapeDtypeStruct((batch * rows_t, n_half), BF16),
        ],
        scratch_shapes=[pltpu.VMEM((rows, CMP_STRIDE * LANES), F32), pltpu.VMEM((seq, LANES), F32)],
        compiler_params=pltpu.CompilerParams(
            dimension_semantics=("arbitrary",),
            vmem_limit_bytes=VMEM_LIMIT),
        name="compress",
    )(proj, proj, proj, proj, pe, wp, wq, w2, feat)


def _nsa_kernel(q_ref, z_ref, gate_ref, slc_ref, win_ref, kc_ref, vct_ref, featk_ref, featw_ref,
                qfeat_ref, mmap_ref, place_ref, norm_ref, o_ref,
                ksel_ref, vselt_ref, kwin_ref, vwint_ref, gt_ref, *, n_cmp):
    g = pl.program_id(1)
    qi = pl.program_id(2)
    tq = q_ref.shape[0]
    tk = tq
    seq = slc_ref.shape[0]
    dh = NSA_HEAD_DIM
    rep = q_ref.shape[1] // dh
    n_half = kc_ref.shape[0]
    n_pad = WINDOW // tk
    t0 = qi * tq

    @pl.when(qi == 0)
    def _():
        lane = lax.broadcasted_iota(jnp.int32, (seq, LANES), 1)
        kv = slc_ref[...]
        ksel_ref[...] = jnp.where(lane < dh, kv, featk_ref[...])
        vt = kv.astype(F32).T[dh:, :].astype(BF16)
        for kt in range(seq // tk):
            vselt_ref[kt] = vt[:, kt * tk:(kt + 1) * tk]
        kv = win_ref[...]
        kwin_ref[0:WINDOW, :] = featw_ref[0:WINDOW, :]
        kwin_ref[WINDOW:, :] = jnp.where(lane < dh, kv, featw_ref[WINDOW:, :])
        vt = kv.astype(F32).T[dh:, :].astype(BF16)
        for kt in range(n_pad):
            vwint_ref[kt] = jnp.zeros((dh, tk), BF16)
        for kt in range(seq // tk):
            vwint_ref[n_pad + kt] = vt[:, kt * tk:(kt + 1) * tk]

    heads = lambda a: jnp.concatenate([a] * rep, axis=1)
    sub8 = lambda a: a.reshape(a.shape[0] // 8, 8, a.shape[1])

    def softmax_pv(lgs, v_tiles):
        m8 = functools.reduce(jnp.maximum, [jnp.max(sub8(lg), axis=0) for lg in lgs])
        m = jnp.max(m8, axis=0, keepdims=True)
        l8 = jnp.zeros((8, rep * tq), F32)
        acc = jnp.zeros((dh, rep * tq), F32)
        for lg, vt in zip(lgs, v_tiles):
            pt = jnp.exp2(lg - m)
            l8 = l8 + jnp.sum(sub8(pt), axis=0)
            acc = acc + jnp.dot(vt, pt.astype(BF16), preferred_element_type=F32)
        return acc / jnp.sum(l8, axis=0, keepdims=True)

    def step(n):
        t0 = n * tq
        lane = lax.broadcasted_iota(jnp.int32, (tq, LANES), 1)
        key_i = lax.broadcasted_iota(jnp.int32, (tk, tq), 0)
        qry_i = lax.broadcasted_iota(jnp.int32, (tk, tq), 1)
        causal_bias = heads(jnp.where(key_i <= qry_i, 0.0, NEG_INF))
        band_bias = heads(jnp.where(key_i > qry_i, 0.0, NEG_INF))

        qfeat = qfeat_ref[0]
        qa = []
        for r in range(rep):
            qcol = q_ref[:, (r // 2) * LANES:(r // 2 + 1) * LANES].astype(F32)
            if r % 2:
                qcol = pltpu.roll(qcol, dh, 1)
            qa.append(jnp.where(lane < dh, qcol * LOG2E, qfeat[r:r + 1, :]))
        q0 = jnp.concatenate(qa, axis=0).astype(BF16)

        n_wt = n_pad + 1
        lgs = []
        for w in range(n_wt):
            lg = _nt_dot(kwin_ref[t0 + w * tk:t0 + (w + 1) * tk, :], q0)
            if w == 0:
                lg = lg + band_bias
            if w == n_wt - 1:
                lg = lg + causal_bias
            lgs.append(lg)
        o_win = softmax_pv(lgs, [vwint_ref[n + w] for w in range(n_wt)])

        nc = min(n_half, (t0 + tq) // CMP_STRIDE)
        n_idx = lax.broadcasted_iota(jnp.int32, (nc, tq), 0)
        t_idx = t0 + lax.broadcasted_iota(jnp.int32, (nc, tq), 1)
        valid_c = heads((CMP_STRIDE * n_idx + (CMP_BLOCK - 1) <= t_idx) & (n_idx < n_cmp))
        lg = jnp.where(valid_c, _nt_dot(kc_ref[0:nc, :], q0), NEG_INF)
        e = jnp.exp2(lg - jnp.max(lg, axis=0, keepdims=True))
        p = jnp.where(valid_c, e / jnp.sum(e, axis=0, keepdims=True), 0.0)
        if nc < n_half:
            p = jnp.concatenate([p, jnp.zeros((n_half - nc, rep * tq), F32)], axis=0)
        o_cmp = jnp.dot(vct_ref[...], p.astype(BF16), preferred_element_type=F32)
        p_sum = p[:, 0:tq]
        for r in range(1, rep):
            p_sum = p_sum + p[:, r * tq:(r + 1) * tq]

        nb = min(MAX_SEL_BLOCKS, (t0 + tq) // SEL_BLOCK)
        mm = mmap_ref[...]
        hi, mid, lo = _split3(p_sum)
        p_slc = (jnp.dot(mm, hi, preferred_element_type=F32) + jnp.dot(mm, mid, preferred_element_type=F32)
                 + jnp.dot(mm, lo, preferred_element_type=F32))[0:MAX_SEL_BLOCKS]
        jj = lax.broadcasted_iota(jnp.int32, (MAX_SEL_BLOCKS, tq), 0)
        tt = t0 + lax.broadcasted_iota(jnp.int32, (MAX_SEL_BLOCKS, tq), 1)
        cur = lax.shift_right_logical(tt, int(np.log2(SEL_BLOCK)))
        forced = (jj == 0) | (jj == cur) | (jj == cur - 1)
        future = jj > cur
        score = jnp.where(future, -1.0, p_slc + jnp.where(forced, SEL_BONUS, 0.0))
        rank = jnp.zeros((MAX_SEL_BLOCKS, tq), jnp.int32)
        for i in range(nb):
            other = jnp.broadcast_to(score[i:i + 1, :], score.shape)
            beats = (other > score) | ((other == score) & (jj > i))
            rank = rank + beats.astype(jnp.int32)
        chosen = jnp.where((rank < SEL_TOP) & jnp.logical_not(future), 1.0, 0.0).astype(BF16)
        placed = _tn_dot(chosen, place_ref[...])
        sel_lanes = (lane >= SEL_LANE0) & (lane < SEL_LANE0 + MAX_SEL_BLOCKS)
        sel_bias = (placed - 1.0) * (-NEG_INF)
        qs = jnp.concatenate([jnp.where(sel_lanes, sel_bias, a) for a in qa], axis=0).astype(BF16)

        lgs = [_nt_dot(ksel_ref[kt * tk:(kt + 1) * tk, :], qs) for kt in range(n + 1)]
        lgs[n] = lgs[n] + causal_bias
        o_slc = softmax_pv(lgs, [vselt_ref[kt] for kt in range(n + 1)])

        gt_ref[...] = jax.nn.sigmoid(gate_ref[...]).T
        ys = []
        for r in range(rep):
            cols = slice(r * tq, (r + 1) * tq)
            gate = lambda c: gt_ref[pl.ds(3 * (g * rep + r) + c, 1), :]
            o = gate(0) * o_cmp[:, cols] + gate(1) * o_slc[:, cols] + gate(2) * o_win[:, cols]
            ys.append(o * lax.rsqrt(jnp.mean(o * o, axis=0, keepdims=True) + EPS))
        for pair in range(rep // 2):
            cols = slice(pair * LANES, (pair + 1) * LANES)
            y = jnp.concatenate([ys[2 * pair], ys[2 * pair + 1]], axis=0).T * norm_ref[:, cols]
            z = z_ref[:, cols]
            o_ref[:, cols] = (y * (z * jax.nn.sigmoid(z))).astype(o_ref.dtype)

    lax.switch(qi, [functools.partial(step, n) for n in range(seq // tq)])


def _nsa(streams, gate, kc, vc, featk, featw, qfeat, mmap, place, norm, *, batch, seq, n_cmp):
    tq = NSA_TILE
    nq = seq // tq
    G = NSA_KV_HEADS
    gw = norm.shape[1] // G
    n_half = kc.shape[0] // (batch * G)
    const = lambda a: pl.BlockSpec(a.shape, lambda b, g, i: (0,) * a.ndim)
    stream = lambda name: pl.BlockSpec((seq, LANES),
                                       lambda b, g, i, o=streams[name][1] // LANES: (b, o + g))
    tile = lambda name: pl.BlockSpec((tq, gw), lambda b, g, i, o=streams[name][1] // gw: (b * nq + i, o + g))
    return pl.pallas_call(
        functools.partial(_nsa_kernel, n_cmp=n_cmp),
        grid=(batch, G, nq),
        in_specs=[
            tile("q_a"), tile("z_a"),
            pl.BlockSpec((tq, gate.shape[1]), lambda b, g, i: (b * nq + i, 0)),
            stream("slc"), stream("win"),
            pl.BlockSpec((n_half, LANES), lambda b, g, i: (b * G + g, 0)),
            pl.BlockSpec((NSA_HEAD_DIM, n_half), lambda b, g, i: (b * G + g, 0)),
            const(featk), const(featw),
            pl.BlockSpec((1,) + qfeat.shape[1:], lambda b, g, i: (g, 0, 0)),
            const(mmap), const(place),
            pl.BlockSpec((1, gw), lambda b, g, i: (0, g)),
        ],
        out_specs=pl.BlockSpec((tq, gw), lambda b, g, i: (b * nq + i, g)),
        out_shape=jax.ShapeDtypeStruct((batch * seq, G * gw), BF16),
        scratch_shapes=[
            pltpu.VMEM((seq, LANES), BF16),
            pltpu.VMEM((seq // tq, NSA_HEAD_DIM, tq), BF16),
            pltpu.VMEM((seq + WINDOW, LANES), BF16),
            pltpu.VMEM(((seq + WINDOW) // tq, NSA_HEAD_DIM, tq), BF16),
            pltpu.VMEM((gate.shape[1], tq), F32),
        ],
        compiler_params=pltpu.CompilerParams(
            dimension_semantics=("arbitrary", "arbitrary", "arbitrary"),
            vmem_limit_bytes=VMEM_LIMIT),
        name="nsa",
    )(streams["q_a"][0], streams["z_a"][0], gate, streams["slc"][0], streams["win"][0],
      kc, vc, featk, featw, qfeat, mmap, place, norm)


def _bf16_terms(x, n):
    terms, rest = [], np.asarray(x, np.float64)
    for _ in range(n):
        t = rest.astype(np.float32).astype(ml_dtypes.bfloat16).astype(np.float64)
        terms.append(t.astype(np.float32))
        rest = rest - t
    return terms


def _nsa_tables(seq, n_half, n_cmp, heads):
    assert seq // SEL_BLOCK <= MAX_SEL_BLOCKS and FLAG_LANE < LANES

    def key_features(pos, onehot_blocks):
        f = np.zeros((len(pos), LANES), np.float32)
        if onehot_blocks:
            f[np.arange(len(pos)), SEL_LANE0 + pos // SEL_BLOCK] = 1.0
        f[:, POS_LANE0:POS_LANE0 + N_SPLIT] = ((pos // 64) * 64)[:, None]
        f[:, POS_LANE0 + N_SPLIT:POS_LANE0 + 2 * N_SPLIT] = (pos % 64)[:, None]
        return f

    featk = key_features(np.arange(seq), True)
    featw = np.concatenate([np.zeros((WINDOW, LANES), np.float32), key_features(np.arange(seq), False)])
    featw[:WINDOW, FLAG_LANE] = NEG_INF
    featc = key_features(CMP_STRIDE * np.arange(n_half) + CMP_BLOCK - 1, False)
    featc[n_cmp:, FLAG_LANE] = NEG_INF

    slopes = (2.0 ** (-8.0 * np.arange(1, heads + 1) / heads)).astype(np.float32).astype(np.float64)
    terms = _bf16_terms(slopes * LOG2E, N_SPLIT)
    rep = heads // NSA_KV_HEADS
    qfeat = np.zeros((NSA_KV_HEADS, 8, LANES), np.float32)
    for h in range(heads):
        for i, t in enumerate(terms):
            qfeat[h // rep, h % rep, POS_LANE0 + i] = t[h]
            qfeat[h // rep, h % rep, POS_LANE0 + N_SPLIT + i] = t[h]
        qfeat[h // rep, h % rep, FLAG_LANE] = 1.0

    cs = CMP_STRIDE * np.arange(n_half)[None, :]
    ss = SEL_BLOCK * np.arange(LANES)[:, None]
    overlap = np.clip(np.minimum(cs + CMP_BLOCK, ss + SEL_BLOCK) - np.maximum(cs, ss), 0, None)
    mmap = (overlap / CMP_BLOCK) * (np.arange(n_half)[None, :] < n_cmp) * (ss < seq)
    place = np.zeros((MAX_SEL_BLOCKS, LANES), np.float32)
    place[np.arange(MAX_SEL_BLOCKS), SEL_LANE0 + np.arange(MAX_SEL_BLOCKS)] = 1.0
    bf = lambda a: jnp.asarray(a, dtype=BF16)
    return bf(featk), bf(featw), bf(featc), jnp.asarray(qfeat), bf(mmap), bf(place)


def _hgrn_kernel(q_ref, f_ref, v_ref, z_ref, lb_ref, norm_ref, cum_ref, o_ref, *, group):
    C, SUB = HGRN_CHUNK, HGRN_SUB
    n_sub = C // SUB
    seq, dk = q_ref.shape

    lbr = lb_ref[...]
    e = jnp.exp(lbr - jnp.max(lbr, axis=0, keepdims=True))
    lb = e[0:1, :] / jnp.sum(e, axis=0, keepdims=True)
    gain = norm_ref[...]
    cum = cum_ref[...]
    sub = lambda x, i: x[i * SUB:(i + 1) * SUB]
    masks = []
    for i in range(n_sub):
        width = (i + 1) * SUB
        masks.append(lax.broadcasted_iota(jnp.int32, (SUB, width), 1)
                     <= lax.broadcasted_iota(jnp.int32, (SUB, width), 0) + i * SUB)

    def block(cb, st):
        chunks = range(group)
        rows = [pl.ds(pl.multiple_of((cb * group + j) * C, C), C) for j in chunks]
        v16 = [v_ref[rows[j], :].astype(BF16) for j in chunks]
        f = [lb + (1.0 - lb) * jax.nn.sigmoid(f_ref[rows[j], :]) for j in chunks]
        k = [1.0 - f[j] for j in chunks]
        parts = [_split3(jnp.log(f[j])) for j in chunks]
        a = [(jnp.dot(cum, parts[j][0], preferred_element_type=F32)
              + jnp.dot(cum, parts[j][1], preferred_element_type=F32)
              + jnp.dot(cum, parts[j][2], preferred_element_type=F32)) for j in chunks]
        q1, k1, k2, qb, k3, start, dec = [], [], [], [], [], [], []
        for j in chunks:
            q1.append(q_ref[rows[j], :] * jnp.exp(a[j]))
            k1.append(k[j] * jnp.exp(-a[j]))
            tot = [a[j][(i + 1) * SUB - 1:(i + 1) * SUB] for i in range(n_sub)]
            s = [jnp.zeros_like(tot[0])]
            for i in range(n_sub):
                s.append(s[i] + tot[i])
            start.append(s)
            dec.append(jnp.exp(s[n_sub]))
            qb.append(jnp.concatenate([sub(q1[j], i) * jnp.exp(s[i]) for i in range(n_sub)], axis=0)
                      .astype(BF16))
            k2.append([sub(k1[j], i) * jnp.exp(tot[i]) for i in range(n_sub)])
            k3.append(jnp.concatenate([sub(k1[j], i) * jnp.exp(s[n_sub] - s[i]) for i in range(n_sub)],
                                      axis=0).astype(BF16))
        att = []
        for j in chunks:
            row_blocks = []
            for i in range(n_sub):
                rhs = [k2[j][jb] * jnp.exp(start[j][i] - start[j][jb + 1]) if jb < i - 1 else k2[j][jb]
                       for jb in range(i)]
                rhs.append(sub(k1[j], i))
                rhs = jnp.concatenate(rhs, axis=0) if len(rhs) > 1 else rhs[0]
                sc = _nt_dot(sub(q1[j], i).astype(BF16), rhs.astype(BF16))
                row_blocks.append(jnp.where(masks[i], sc, 0.0).astype(BF16))
            att.append(row_blocks)
        intra = [jnp.concatenate([jnp.dot(att[j][i], v16[j][0:(i + 1) * SUB], preferred_element_type=F32)
                                  for i in range(n_sub)], axis=0) for j in chunks]
        incr = [_tn_dot(v16[j], k3[j]) for j in chunks]
        inter = []
        for j in chunks:
            inter.append(_nt_dot(qb[j], st.astype(BF16)))
            st = st * dec[j] + incr[j]
        for j in chunks:
            o = inter[j] + intra[j]
            y = o * lax.rsqrt(jnp.mean(o * o, axis=-1, keepdims=True) + EPS) * gain
            z = z_ref[rows[j], :]
            o_ref[rows[j], :] = (y * (z * jax.nn.sigmoid(z))).astype(o_ref.dtype)
        return st

    lax.fori_loop(0, seq // (C * group), block, jnp.zeros((dk, dk), F32))


def _hgrn_patterns():
    C, SUB = HGRN_CHUNK, HGRN_SUB
    t = np.arange(C)[:, None]
    s = np.arange(C)[None, :]
    return (((t // SUB) == (s // SUB)) & (s <= t)).astype(np.float32)


def _hgrn(streams, lower_bounds, norm, *, batch, seq):
    dk = HGRN_HEAD_DIM
    heads = norm.shape[1] // dk
    cum = jnp.asarray(_hgrn_patterns(), dtype=BF16)
    col = lambda name: (lambda b, h, o=streams[name][1] // dk: (b, o + h))
    return pl.pallas_call(
        functools.partial(_hgrn_kernel, group=8),
        grid=(batch, heads),
        in_specs=[
            pl.BlockSpec((seq, dk), col("q_h")),
            pl.BlockSpec((seq, dk), col("f_h")),
            pl.BlockSpec((seq, dk), col("i_h")),
            pl.BlockSpec((seq, dk), col("z_h")),
            pl.BlockSpec((lower_bounds.shape[0], dk), lambda b, h: (0, h)),
            pl.BlockSpec((1, dk), lambda b, h: (0, h)),
            pl.BlockSpec(cum.shape, lambda b, h: (0, 0)),
        ],
        out_specs=pl.BlockSpec((seq, dk), lambda b, h: (b, h)),
        out_shape=jax.ShapeDtypeStruct((batch * seq, heads * dk), BF16),
        compiler_params=pltpu.CompilerParams(
            dimension_semantics=("arbitrary", "arbitrary"),
            vmem_limit_bytes=VMEM_LIMIT),
        name="hgrn",
    )(streams["q_h"][0], streams["f_h"][0], streams["i_h"][0], streams["z_h"][0], lower_bounds, norm, cum)


def _outproj_kernel(x_ref, oa_ref, oh_ref, wa_ref, wh_ref, g_ref, o_ref):
    y = x_ref[...] + jnp.dot(oa_ref[...], wa_ref[...], preferred_element_type=F32)
    y = y + jnp.dot(oh_ref[...], wh_ref[...], preferred_element_type=F32)
    ms = jnp.mean(y * y, axis=-1, keepdims=True)
    o_ref[...] = y * lax.rsqrt(ms + EPS) * g_ref[...]


def _outproj(x2, o_a, o_h, wa, wh, g, *, tm):
    M, D = x2.shape
    const = lambda a: pl.BlockSpec(a.shape, lambda i: (0,) * a.ndim)
    return pl.pallas_call(
        _outproj_kernel,
        grid=(M // tm,),
        in_specs=[
            pl.BlockSpec((tm, D), lambda i: (i, 0)),
            pl.BlockSpec((tm, o_a.shape[1]), lambda i: (i, 0)),
            pl.BlockSpec((tm, o_h.shape[1]), lambda i: (i, 0)),
            const(wa), const(wh), const(g),
        ],
        out_specs=pl.BlockSpec((tm, D), lambda i: (i, 0)),
        out_shape=jax.ShapeDtypeStruct((M, D), F32),
        compiler_params=pltpu.CompilerParams(
            dimension_semantics=("arbitrary",),
            vmem_limit_bytes=VMEM_LIMIT),
        name="outproj",
    )(x2, o_a, o_h, wa, wh, g)


def kernel(x, norm_in, w_in, cmp_pe_k, cmp_w1_k, cmp_w2_k, cmp_pe_v, cmp_w1_v, cmp_w2_v,
           lower_bounds, nsa_out_norm, hgrn_out_norm, w_out, final_norm):
    B, S, D = x.shape
    assert norm_in.shape[0] == 1, "single-layer problem"
    nsa_w = nsa_out_norm.shape[1]
    hgrn_w = hgrn_out_norm.shape[1]
    dh = NSA_HEAD_DIM
    G = NSA_KV_HEADS
    heads = nsa_w // dh
    kvw = G * dh
    n_gate = 3 * heads
    n_cmp = (S - CMP_BLOCK) // CMP_STRIDE + 1
    n_half = S // CMP_STRIDE
    assert S % NSA_TILE == 0 and WINDOW % NSA_TILE == 0 and n_half <= LANES and 2 * dh == LANES

    names = ["q_a", "k_cmp", "v_cmp", "k_slc", "v_slc", "k_win", "v_win", "gate", "z_a",
             "q_h", "f_h", "i_h", "z_h"]
    widths = [nsa_w] + [kvw] * 6 + [n_gate, nsa_w] + [hgrn_w] * 4
    starts = dict(zip(names, np.cumsum([0] + widths[:-1]).tolist()))
    wd = dict(zip(names, widths))
    w = w_in[0]
    piece = lambda n: w[:, starts[n]:starts[n] + wd[n]]

    def paired(kname, vname):
        k = piece(kname).reshape(D, G, dh)
        v = piece(vname).reshape(D, G, dh)
        return jnp.concatenate([k, v], axis=-1).reshape(D, G * 2 * dh)

    blocks16 = [("q_a", piece("q_a") * (dh ** -0.5)),
                ("cmp", paired("k_cmp", "v_cmp")), ("slc", paired("k_slc", "v_slc")),
                ("win", paired("k_win", "v_win")), ("q_h", piece("q_h")), ("i_h", piece("i_h"))]
    blocks32 = [("z_a", piece("z_a")), ("f_h", piece("f_h")), ("z_h", piece("z_h"))]
    n16_cols = sum(blk.shape[1] for _, blk in blocks16)
    w_main = jnp.concatenate([blk for _, blk in blocks16 + blocks32], axis=1).astype(BF16)
    w_gate = jnp.pad(piece("gate"), ((0, 0), (0, LANES - n_gate))).astype(BF16)

    x2 = x.reshape(B * S, D)
    proj16, proj32, gate = _proj(x2, norm_in, w_main, w_gate, tm=min(1024, B * S), tn=1536,
                                 n16_cols=n16_cols)
    streams = {}
    for arr, blocks in ((proj16, blocks16), (proj32, blocks32)):
        off = 0
        for n, blk in blocks:
            streams[n] = (arr, off)
            off += blk.shape[1]

    featk, featw, featc, qfeat, mmap, place = _nsa_tables(S, n_half, n_cmp, heads)

    def w1_halves(w1k, w1v):
        hk = w1k.shape[1]
        k3 = w1k.reshape(2, CMP_STRIDE, dh, hk)
        v3 = w1v.reshape(2, CMP_STRIDE, dh, hk)
        zk = jnp.zeros_like(k3[0])
        top = lambda a: jnp.concatenate([a, zk], axis=-1)
        bot = lambda a: jnp.concatenate([zk, a], axis=-1)
        half = lambda i: jnp.concatenate([top(k3[i]), bot(v3[i])], axis=1).reshape(CMP_STRIDE * 2 * dh, 2 * hk)
        return half(0).astype(BF16), half(1).astype(BF16)

    wp, wq = w1_halves(cmp_w1_k[0], cmp_w1_v[0])
    zk = jnp.zeros_like(cmp_w2_k[0])
    w2 = jnp.concatenate([jnp.concatenate([cmp_w2_k[0], zk], axis=1),
                          jnp.concatenate([zk, cmp_w2_v[0]], axis=1)], axis=0).astype(BF16)
    pe = jnp.concatenate([cmp_pe_k[0].reshape(2, CMP_STRIDE, dh), cmp_pe_v[0].reshape(2, CMP_STRIDE, dh)],
                         axis=-1).reshape(2, CMP_STRIDE * 2 * dh)
    kc, vc = _compress(streams["cmp"][0], pe, wp, wq, w2, jnp.tile(featc, (G, 1)),
                       batch=B, seq=S, col0=streams["cmp"][1], n_half=n_half)

    o_a = _nsa(streams, gate, kc, vc, featk, featw, qfeat, mmap, place, nsa_out_norm,
               batch=B, seq=S, n_cmp=n_cmp)
    o_h = _hgrn(streams, lower_bounds, hgrn_out_norm, batch=B, seq=S)

    wo = w_out[0].astype(BF16)
    out = _outproj(x2, o_a, o_h, wo[:nsa_w], wo[nsa_w:], final_norm.reshape(1, D), tm=512)
    return out.reshape(B, S, D)
```

```python
import functools

import ml_dtypes
import numpy as np
import jax
import jax.numpy as jnp
from jax import lax
from jax.experimental import pallas as pl
from jax.experimental.pallas import tpu as pltpu

F32 = jnp.float32
BF16 = jnp.bfloat16

EPS = 1e-6
NEG_INF = -1e30
LOG2E = 1.4426950408889634

NSA_HEAD_DIM = 64
NSA_KV_HEADS = 4
CMP_BLOCK = 32
CMP_STRIDE = 16
SEL_BLOCK = 64
SEL_TOP = 8
SEL_BONUS = 1.0e4
WINDOW = 512
HGRN_HEAD_DIM = 128
HGRN_CHUNK = 64
HGRN_SUB = 16

LANES = 128
VMEM_LIMIT = 56 * 1024 * 1024
NSA_TILE = 256

SEL_LANE0 = NSA_HEAD_DIM
MAX_SEL_BLOCKS = 32
POS_LANE0 = SEL_LANE0 + MAX_SEL_BLOCKS
N_SPLIT = 4
FLAG_LANE = POS_LANE0 + 2 * N_SPLIT
SHIFT_LANE0 = FLAG_LANE + 1
REL_LANE0 = SHIFT_LANE0 + N_SPLIT
SCORE_BOUND = 100.0
SAFETY = 1.02


def _nt_dot(a, b):
    return lax.dot_general(a, b, (((1,), (1,)), ((), ())), preferred_element_type=F32)


def _tn_dot(a, b):
    return lax.dot_general(a, b, (((0,), (0,)), ((), ())), preferred_element_type=F32)


def _split3(x):
    hi = x.astype(BF16)
    r1 = x - hi.astype(F32)
    mid = r1.astype(BF16)
    lo = (r1 - mid.astype(F32)).astype(BF16)
    return hi, mid, lo


def _proj_kernel(x_ref, g_ref, w_ref, wg_ref, o16_ref, o32_ref, og_ref, h_ref, *, row_chunk, n16):
    j = pl.program_id(1)

    @pl.when(j == 0)
    def _():
        n_chunks = x_ref.shape[0] // row_chunk

        def body(c, carry):
            rows = pl.ds(pl.multiple_of(c * row_chunk, row_chunk), row_chunk)
            x = x_ref[rows, :]
            ms = jnp.mean(x * x, axis=-1, keepdims=True)
            h_ref[rows, :] = (x * lax.rsqrt(ms + EPS) * g_ref[...]).astype(BF16)
            return carry

        lax.fori_loop(0, n_chunks, body, 0)
        og_ref[...] = jnp.dot(h_ref[...], wg_ref[...], preferred_element_type=F32)

    @pl.when(j < n16)
    def _():
        o16_ref[...] = jnp.dot(h_ref[...], w_ref[...], preferred_element_type=F32).astype(BF16)

    @pl.when(j >= n16)
    def _():
        o32_ref[...] = jnp.dot(h_ref[...], w_ref[...], preferred_element_type=F32)


def _proj(x2, g, w, wg, *, tm, tn, n16_cols):
    M, D = x2.shape
    N = w.shape[1]
    NG = wg.shape[1]
    n16 = n16_cols // tn
    assert n16 * tn == n16_cols and N % tn == 0
    return pl.pallas_call(
        functools.partial(_proj_kernel, row_chunk=128, n16=n16),
        grid=(M // tm, N // tn),
        in_specs=[
            pl.BlockSpec((tm, D), lambda i, j: (i, 0)),
            pl.BlockSpec((1, D), lambda i, j: (0, 0)),
            pl.BlockSpec((D, tn), lambda i, j: (0, j)),
            pl.BlockSpec((D, NG), lambda i, j: (0, 0)),
        ],
        out_specs=[
            pl.BlockSpec((tm, tn), lambda i, j: (i, jnp.minimum(j, n16 - 1))),
            pl.BlockSpec((tm, tn), lambda i, j: (i, jnp.maximum(j - n16, 0))),
            pl.BlockSpec((tm, NG), lambda i, j: (i, 0)),
        ],
        out_shape=[
            jax.ShapeDtypeStruct((M, n16_cols), BF16),
            jax.ShapeDtypeStruct((M, N - n16_cols), F32),
            jax.ShapeDtypeStruct((M, NG), F32),
        ],
        scratch_shapes=[pltpu.VMEM((tm, D), BF16)],
        compiler_params=pltpu.CompilerParams(
            dimension_semantics=("arbitrary", "arbitrary"),
            vmem_limit_bytes=VMEM_LIMIT),
        name="proj",
    )(x2, g, w, wg)


def _compress_kernel(c0_ref, c1_ref, c2_ref, c3_ref, pe_ref, wp_ref, wq_ref, w2_ref, feat_ref,
                     kc_ref, vc_ref, x_ref, c32_ref, *, n_half):
    for g, c_ref in enumerate((c0_ref, c1_ref, c2_ref, c3_ref)):
        c32_ref[...] = c_ref[...].astype(F32)
        for l in range(CMP_STRIDE):
            x_ref[g * n_half:(g + 1) * n_half, l * LANES:(l + 1) * LANES] = (
                c32_ref[pl.ds(l, n_half, stride=CMP_STRIDE), :])
    x = x_ref[...]
    rows = x.shape[0]
    first = jnp.dot((x + pe_ref[0:1, :]).astype(BF16), wp_ref[...], preferred_element_type=F32)
    second = jnp.dot((x + pe_ref[1:2, :]).astype(BF16), wq_ref[...], preferred_element_type=F32)
    hidden = first + pltpu.roll(second, rows - 1, 0)
    out = jnp.dot(jax.nn.gelu(hidden).astype(BF16), w2_ref[...], preferred_element_type=F32)
    lane = lax.broadcasted_iota(jnp.int32, out.shape, 1)
    dh = NSA_HEAD_DIM
    kc_ref[...] = jnp.where(lane < dh, out.astype(BF16), feat_ref[...])
    for g in range(NSA_KV_HEADS):
        vc_ref[g * dh:(g + 1) * dh, :] = out[g * n_half:(g + 1) * n_half, :].T[dh:, :].astype(BF16)


def _compress(proj, pe, wp, wq, w2, feat, *, batch, seq, col0, n_half):
    rows = NSA_KV_HEADS * n_half
    rows_t = NSA_KV_HEADS * NSA_HEAD_DIM
    const = lambda a: pl.BlockSpec(a.shape, lambda b: (0,) * a.ndim)
    stream = lambda g: pl.BlockSpec((seq, LANES), lambda b, o=col0 // LANES + g: (b, o))
    return pl.pallas_call(
        functools.partial(_compress_kernel, n_half=n_half),
        grid=(batch,),
        in_specs=[stream(0), stream(1), stream(2), stream(3),
                  const(pe), const(wp), const(wq), const(w2), const(feat)],
        out_specs=[
            pl.BlockSpec((rows, LANES), lambda b: (b, 0)),
            pl.BlockSpec((rows_t, n_half), lambda b: (b, 0)),
        ],
        out_shape=[
            jax.ShapeDtypeStruct((batch * rows, LANES), BF16),
            jax.ShapeDtypeStruct((batch * rows_t, n_half), BF16),
        ],
        scratch_shapes=[pltpu.VMEM((rows, CMP_STRIDE * LANES), F32), pltpu.VMEM((seq, LANES), F32)],
        compiler_params=pltpu.CompilerParams(
            dimension_semantics=("arbitrary",),
            vmem_limit_bytes=VMEM_LIMIT),
        name="compress",
    )(proj, proj, proj, proj, pe, wp, wq, w2, feat)


def _nsa_kernel(q_ref, qall_ref, z_ref, gate_ref, slc_ref, win_ref, kc_ref, vct_ref, featk_ref, featw_ref,
                qfeat_ref, qrel_ref, seg_ref, mmap_ref, place_ref, norm_ref, o_ref,
                ksel_ref, vselt_ref, kwin_ref, vwint_ref, gt_ref, bounded_ref, *, n_cmp):
    g = pl.program_id(1)
    qi = pl.program_id(2)
    tq = q_ref.shape[0]
    tk = tq
    seq = slc_ref.shape[0]
    dh = NSA_HEAD_DIM
    rep = q_ref.shape[1] // dh
    n_half = kc_ref.shape[0]
    n_pad = WINDOW // tk

    @pl.when(qi == 0)
    def _():
        lane = lax.broadcasted_iota(jnp.int32, (seq, LANES), 1)

        def max_sq_norm(x, width):
            sq = x * x
            hi = sq.astype(BF16)
            lo = (sq - hi.astype(F32)).astype(BF16)
            s = (jnp.dot(hi, seg_ref[0:width, :], preferred_element_type=F32)
                 + jnp.dot(lo, seg_ref[0:width, :], preferred_element_type=F32))
            return jnp.max(s) * SAFETY

        kv = slc_ref[...]
        k2_sel = max_sq_norm(jnp.where(lane < dh, kv.astype(F32), 0.0), LANES)
        ksel_ref[...] = jnp.where(lane < dh, kv, featk_ref[...])
        vt = kv.astype(F32).T[dh:, :].astype(BF16)
        for kt in range(seq // tk):
            vselt_ref[kt] = vt[:, kt * tk:(kt + 1) * tk]
        kv = win_ref[...]
        k2_win = max_sq_norm(jnp.where(lane < dh, kv.astype(F32), 0.0), LANES)
        kwin_ref[0:WINDOW, :] = featw_ref[0:WINDOW, :]
        kwin_ref[WINDOW:, :] = jnp.where(lane < dh, kv, featw_ref[WINDOW:, :])
        vt = kv.astype(F32).T[dh:, :].astype(BF16)
        for kt in range(n_pad):
            vwint_ref[kt] = jnp.zeros((dh, tk), BF16)
        for kt in range(seq // tk):
            vwint_ref[n_pad + kt] = vt[:, kt * tk:(kt + 1) * tk]
        q2 = max_sq_norm(qall_ref[...].astype(F32), rep * dh) * (LOG2E * LOG2E)
        limit = SCORE_BOUND * SCORE_BOUND
        bounded_ref[0] = ((q2 * k2_sel <= limit) & (q2 * k2_win <= limit)).astype(jnp.int32)

    heads = lambda a: jnp.concatenate([a] * rep, axis=1)
    sub8 = lambda a: a.reshape(a.shape[0] // 8, 8, a.shape[1])

    def softmax_pv(lgs, v_tiles, bounded):
        if bounded:
            m = 0.0
        else:
            m8 = functools.reduce(jnp.maximum, [jnp.max(sub8(lg), axis=0) for lg in lgs])
            m = jnp.max(m8, axis=0, keepdims=True)
        l8 = jnp.zeros((8, rep * tq), F32)
        acc = jnp.zeros((dh, rep * tq), F32)
        for lg, vt in zip(lgs, v_tiles):
            pt = jnp.exp2(lg) if bounded else jnp.exp2(lg - m)
            l8 = l8 + jnp.sum(sub8(pt), axis=0)
            acc = acc + jnp.dot(vt, pt.astype(BF16), preferred_element_type=F32)
        return acc / jnp.sum(l8, axis=0, keepdims=True)

    def step(n, bounded):
        t0 = n * tq
        lane = lax.broadcasted_iota(jnp.int32, (tq, LANES), 1)
        key_i = lax.broadcasted_iota(jnp.int32, (tk, tq), 0)
        qry_i = lax.broadcasted_iota(jnp.int32, (tk, tq), 1)
        causal_bias = heads(jnp.where(key_i <= qry_i, 0.0, NEG_INF))
        band_bias = heads(jnp.where(key_i > qry_i, 0.0, NEG_INF))

        qfeat = qfeat_ref[0, n]
        rel_lanes = (lane >= REL_LANE0) & (lane < REL_LANE0 + 3)
        qa = []
        for r in range(rep):
            qcol = q_ref[:, (r // 2) * LANES:(r // 2 + 1) * LANES].astype(F32)
            if r % 2:
                qcol = pltpu.roll(qcol, dh, 1)
            feat = jnp.where(rel_lanes, qrel_ref[0, r * tq:(r + 1) * tq, :].astype(F32), qfeat[r:r + 1, :])
            qa.append(jnp.where(lane < dh, qcol * LOG2E, feat))
        q0 = jnp.concatenate(qa, axis=0).astype(BF16)

        n_wt = n_pad + 1
        lgs = []
        for w in range(n_wt):
            lg = _nt_dot(kwin_ref[t0 + w * tk:t0 + (w + 1) * tk, :], q0)
            if w == 0:
                lg = lg + band_bias
            if w == n_wt - 1:
                lg = lg + causal_bias
            lgs.append(lg)
        o_win = softmax_pv(lgs, [vwint_ref[n + w] for w in range(n_wt)], bounded)

        nc = min(n_half, (t0 + tq) // CMP_STRIDE)
        n_idx = lax.broadcasted_iota(jnp.int32, (nc, tq), 0)
        t_idx = t0 + lax.broadcasted_iota(jnp.int32, (nc, tq), 1)
        valid_c = heads((CMP_STRIDE * n_idx + (CMP_BLOCK - 1) <= t_idx) & (n_idx < n_cmp))
        lg = jnp.where(valid_c, _nt_dot(kc_ref[0:nc, :], q0), NEG_INF)
        e = jnp.exp2(lg - jnp.max(lg, axis=0, keepdims=True))
        p = jnp.where(valid_c, e / jnp.sum(e, axis=0, keepdims=True), 0.0)
        if nc < n_half:
            p = jnp.concatenate([p, jnp.zeros((n_half - nc, rep * tq), F32)], axis=0)
        o_cmp = jnp.dot(vct_ref[...], p.astype(BF16), preferred_element_type=F32)
        p_sum = p[:, 0:tq]
        for r in range(1, rep):
            p_sum = p_sum + p[:, r * tq:(r + 1) * tq]

        nb = min(MAX_SEL_BLOCKS, (t0 + tq) // SEL_BLOCK)
        mm = mmap_ref[...]
        hi, mid, lo = _split3(p_sum)
        p_slc = (jnp.dot(mm, hi, preferred_element_type=F32) + jnp.dot(mm, mid, preferred_element_type=F32)
                 + jnp.dot(mm, lo, preferred_element_type=F32))[0:MAX_SEL_BLOCKS]
        jj = lax.broadcasted_iota(jnp.int32, (MAX_SEL_BLOCKS, tq), 0)
        tt = t0 + lax.broadcasted_iota(jnp.int32, (MAX_SEL_BLOCKS, tq), 1)
        cur = lax.shift_right_logical(tt, int(np.log2(SEL_BLOCK)))
        forced = (jj == 0) | (jj == cur) | (jj == cur - 1)
        future = jj > cur
        score = jnp.where(future, -1.0, p_slc + jnp.where(forced, SEL_BONUS, 0.0))
        rank = jnp.zeros((MAX_SEL_BLOCKS, tq), jnp.int32)
        for i in range(nb):
            other = jnp.broadcast_to(score[i:i + 1, :], score.shape)
            beats = (other > score) | ((other == score) & (jj > i))
            rank = rank + beats.astype(jnp.int32)
        chosen = jnp.where((rank < SEL_TOP) & jnp.logical_not(future), 1.0, 0.0).astype(BF16)
        placed = _tn_dot(chosen, place_ref[...])
        sel_lanes = (lane >= SEL_LANE0) & (lane < SEL_LANE0 + MAX_SEL_BLOCKS)
        sel_bias = (placed - 1.0) * (-NEG_INF)
        qs = jnp.concatenate([jnp.where(sel_lanes, sel_bias, a) for a in qa], axis=0).astype(BF16)

        lgs = [_nt_dot(ksel_ref[kt * tk:(kt + 1) * tk, :], qs) for kt in range(n + 1)]
        lgs[n] = lgs[n] + causal_bias
        o_slc = softmax_pv(lgs, [vselt_ref[kt] for kt in range(n + 1)], bounded)

        gt_ref[...] = jax.nn.sigmoid(gate_ref[...]).T
        ys = []
        for r in range(rep):
            cols = slice(r * tq, (r + 1) * tq)
            gate = lambda c: gt_ref[pl.ds(3 * (g * rep + r) + c, 1), :]
            o = gate(0) * o_cmp[:, cols] + gate(1) * o_slc[:, cols] + gate(2) * o_win[:, cols]
            ys.append(o * lax.rsqrt(jnp.mean(o * o, axis=0, keepdims=True) + EPS))
        for pair in range(rep // 2):
            cols = slice(pair * LANES, (pair + 1) * LANES)
            y = jnp.concatenate([ys[2 * pair], ys[2 * pair + 1]], axis=0).T * norm_ref[:, cols]
            z = z_ref[:, cols]
            o_ref[:, cols] = (y * (z * jax.nn.sigmoid(z))).astype(o_ref.dtype)

    nq = seq // tq
    lax.switch(qi + nq * bounded_ref[0],
               [functools.partial(step, n, bounded) for bounded in (False, True) for n in range(nq)])


def _nsa(streams, gate, kc, vc, featk, featw, qfeat, qrel, seg, mmap, place, norm, *, batch, seq, n_cmp):
    tq = NSA_TILE
    nq = seq // tq
    G = NSA_KV_HEADS
    gw = norm.shape[1] // G
    n_half = kc.shape[0] // (batch * G)
    const = lambda a: pl.BlockSpec(a.shape, lambda b, g, i: (0,) * a.ndim)
    stream = lambda name: pl.BlockSpec((seq, LANES),
                                       lambda b, g, i, o=streams[name][1] // LANES: (b, o + g))
    tile = lambda name: pl.BlockSpec((tq, gw), lambda b, g, i, o=streams[name][1] // gw: (b * nq + i, o + g))
    return pl.pallas_call(
        functools.partial(_nsa_kernel, n_cmp=n_cmp),
        grid=(batch, G, nq),
        in_specs=[
            tile("q_a"),
            pl.BlockSpec((seq, gw), lambda b, g, i, o=streams["q_a"][1] // gw: (b, o + g)),
            tile("z_a"),
            pl.BlockSpec((tq, gate.shape[1]), lambda b, g, i: (b * nq + i, 0)),
            stream("slc"), stream("win"),
            pl.BlockSpec((n_half, LANES), lambda b, g, i: (b * G + g, 0)),
            pl.BlockSpec((NSA_HEAD_DIM, n_half), lambda b, g, i: (b * G + g, 0)),
            const(featk), const(featw),
            pl.BlockSpec((1,) + qfeat.shape[1:], lambda b, g, i: (g, 0, 0, 0)),
            pl.BlockSpec((1,) + qrel.shape[1:], lambda b, g, i: (g, 0, 0)),
            const(seg), const(mmap), const(place),
            pl.BlockSpec((1, gw), lambda b, g, i: (0, g)),
        ],
        out_specs=pl.BlockSpec((tq, gw), lambda b, g, i: (b * nq + i, g)),
        out_shape=jax.ShapeDtypeStruct((batch * seq, G * gw), BF16),
        scratch_shapes=[
            pltpu.VMEM((seq, LANES), BF16),
            pltpu.VMEM((seq // tq, NSA_HEAD_DIM, tq), BF16),
            pltpu.VMEM((seq + WINDOW, LANES), BF16),
            pltpu.VMEM(((seq + WINDOW) // tq, NSA_HEAD_DIM, tq), BF16),
            pltpu.VMEM((gate.shape[1], tq), F32),
            pltpu.SMEM((1,), jnp.int32),
        ],
        compiler_params=pltpu.CompilerParams(
            dimension_semantics=("arbitrary", "arbitrary", "arbitrary"),
            vmem_limit_bytes=VMEM_LIMIT),
        name="nsa",
    )(streams["q_a"][0], streams["q_a"][0], streams["z_a"][0], gate, streams["slc"][0], streams["win"][0],
      kc, vc, featk, featw, qfeat, qrel, seg, mmap, place, norm)


def _bf16_terms(x, n):
    terms, rest = [], np.asarray(x, np.float64)
    for _ in range(n):
        t = rest.astype(np.float32).astype(ml_dtypes.bfloat16).astype(np.float64)
        terms.append(t.astype(np.float32))
        rest = rest - t
    return terms


def _nsa_tables(seq, n_half, n_cmp, heads):
    assert seq // SEL_BLOCK <= MAX_SEL_BLOCKS and FLAG_LANE < LANES

    def key_features(pos, onehot_blocks):
        f = np.zeros((len(pos), LANES), np.float32)
        if onehot_blocks:
            f[np.arange(len(pos)), SEL_LANE0 + pos // SEL_BLOCK] = 1.0
        f[:, POS_LANE0:POS_LANE0 + N_SPLIT] = ((pos // 64) * 64)[:, None]
        f[:, POS_LANE0 + N_SPLIT:POS_LANE0 + 2 * N_SPLIT] = (pos % 64)[:, None]
        return f

    assert REL_LANE0 + 3 <= LANES
    featk = key_features(np.arange(seq), True)
    featk[:, SHIFT_LANE0:REL_LANE0 + 3] = 1.0
    featw = np.concatenate([np.zeros((WINDOW, LANES), np.float32), key_features(np.arange(seq), False)])
    featw[:WINDOW, FLAG_LANE] = NEG_INF
    featw[WINDOW:, SHIFT_LANE0:REL_LANE0 + 3] = 1.0
    featc = key_features(CMP_STRIDE * np.arange(n_half) + CMP_BLOCK - 1, False)
    featc[n_cmp:, FLAG_LANE] = NEG_INF

    slopes = (2.0 ** (-8.0 * np.arange(1, heads + 1) / heads)).astype(np.float32).astype(np.float64)
    slopes2 = slopes * LOG2E
    terms = _bf16_terms(slopes2, N_SPLIT)
    rep = heads // NSA_KV_HEADS
    nq = seq // NSA_TILE
    qfeat = np.zeros((NSA_KV_HEADS, nq, 8, LANES), np.float32)
    qrel = np.zeros((NSA_KV_HEADS, rep * NSA_TILE, LANES), np.float32)
    for h in range(heads):
        g, r = divmod(h, rep)
        for i, t in enumerate(terms):
            qfeat[g, :, r, POS_LANE0 + i] = t[h]
            qfeat[g, :, r, POS_LANE0 + N_SPLIT + i] = t[h]
        qfeat[g, :, r, FLAG_LANE] = 1.0
        for i, t in enumerate(_bf16_terms(-slopes2[h] * NSA_TILE * np.arange(nq), N_SPLIT)):
            qfeat[g, :, r, SHIFT_LANE0 + i] = t
        for i, t in enumerate(_bf16_terms(-slopes2[h] * np.arange(NSA_TILE), 3)):
            qrel[g, r * NSA_TILE:(r + 1) * NSA_TILE, REL_LANE0 + i] = t
    seg = np.zeros((2 * LANES, LANES), np.float32)
    seg[np.arange(2 * LANES), np.arange(2 * LANES) // NSA_HEAD_DIM] = 1.0

    cs = CMP_STRIDE * np.arange(n_half)[None, :]
    ss = SEL_BLOCK * np.arange(LANES)[:, None]
    overlap = np.clip(np.minimum(cs + CMP_BLOCK, ss + SEL_BLOCK) - np.maximum(cs, ss), 0, None)
    mmap = (overlap / CMP_BLOCK) * (np.arange(n_half)[None, :] < n_cmp) * (ss < seq)
    place = np.zeros((MAX_SEL_BLOCKS, LANES), np.float32)
    place[np.arange(MAX_SEL_BLOCKS), SEL_LANE0 + np.arange(MAX_SEL_BLOCKS)] = 1.0
    bf = lambda a: jnp.asarray(a, dtype=BF16)
    return bf(featk), bf(featw), bf(featc), jnp.asarray(qfeat), bf(qrel), bf(seg), bf(mmap), bf(place)


def _hgrn_kernel(q_ref, f_ref, v_ref, z_ref, lb_ref, norm_ref, cum_ref, o_ref, *, group):
    C, SUB = HGRN_CHUNK, HGRN_SUB
    n_sub = C // SUB
    seq, dk = q_ref.shape

    lbr = lb_ref[...]
    e = jnp.exp(lbr - jnp.max(lbr, axis=0, keepdims=True))
    lb = e[0:1, :] / jnp.sum(e, axis=0, keepdims=True)
    gain = norm_ref[...]
    cum = cum_ref[...]
    sub = lambda x, i: x[i * SUB:(i + 1) * SUB]
    masks = []
    for i in range(n_sub):
        width = (i + 1) * SUB
        masks.append(lax.broadcasted_iota(jnp.int32, (SUB, width), 1)
                     <= lax.broadcasted_iota(jnp.int32, (SUB, width), 0) + i * SUB)

    def block(cb, st):
        chunks = range(group)
        rows = [pl.ds(pl.multiple_of((cb * group + j) * C, C), C) for j in chunks]
        v16 = [v_ref[rows[j], :].astype(BF16) for j in chunks]
        f = [lb + (1.0 - lb) * jax.nn.sigmoid(f_ref[rows[j], :]) for j in chunks]
        k = [1.0 - f[j] for j in chunks]
        parts = [_split3(jnp.log(f[j])) for j in chunks]
        a = [(jnp.dot(cum, parts[j][0], preferred_element_type=F32)
              + jnp.dot(cum, parts[j][1], preferred_element_type=F32)
              + jnp.dot(cum, parts[j][2], preferred_element_type=F32)) for j in chunks]
        q1, k1, k2, qb, k3, start, dec = [], [], [], [], [], [], []
        for j in chunks:
            q1.append(q_ref[rows[j], :] * jnp.exp(a[j]))
            k1.append(k[j] * jnp.exp(-a[j]))
            tot = [a[j][(i + 1) * SUB - 1:(i + 1) * SUB] for i in range(n_sub)]
            s = [jnp.zeros_like(tot[0])]
            for i in range(n_sub):
                s.append(s[i] + tot[i])
            start.append(s)
            dec.append(jnp.exp(s[n_sub]))
            qb.append(jnp.concatenate([sub(q1[j], i) * jnp.exp(s[i]) for i in range(n_sub)], axis=0)
                      .astype(BF16))
            k2.append([sub(k1[j], i) * jnp.exp(tot[i]) for i in range(n_sub)])
            k3.append(jnp.concatenate([sub(k1[j], i) * jnp.exp(s[n_sub] - s[i]) for i in range(n_sub)],
                                      axis=0).astype(BF16))
        att = []
        for j in chunks:
            row_blocks = []
            for i in range(n_sub):
                rhs = [k2[j][jb] * jnp.exp(start[j][i] - start[j][jb + 1]) if jb < i - 1 else k2[j][jb]
                       for jb in range(i)]
                rhs.append(sub(k1[j], i))
                rhs = jnp.concatenate(rhs, axis=0) if len(rhs) > 1 else rhs[0]
                sc = _nt_dot(sub(q1[j], i).astype(BF16), rhs.astype(BF16))
                row_blocks.append(jnp.where(masks[i], sc, 0.0).astype(BF16))
            att.append(row_blocks)
        intra = [jnp.concatenate([jnp.dot(att[j][i], v16[j][0:(i + 1) * SUB], preferred_element_type=F32)
                                  for i in range(n_sub)], axis=0) for j in chunks]
        incr = [_tn_dot(v16[j], k3[j]) for j in chunks]
        inter = []
        for j in chunks:
            inter.append(_nt_dot(qb[j], st.astype(BF16)))
            st = st * dec[j] + incr[j]
        for j in chunks:
            o = inter[j] + intra[j]
            y = o * lax.rsqrt(jnp.mean(o * o, axis=-1, keepdims=True) + EPS) * gain
            z = z_ref[rows[j], :]
            o_ref[rows[j], :] = (y * (z * jax.nn.sigmoid(z))).astype(o_ref.dtype)
        return st

    lax.fori_loop(0, seq // (C * group), block, jnp.zeros((dk, dk), F32))


def _hgrn_patterns():
    C, SUB = HGRN_CHUNK, HGRN_SUB
    t = np.arange(C)[:, None]
    s = np.arange(C)[None, :]
    return (((t // SUB) == (s // SUB)) & (s <= t)).astype(np.float32)


def _hgrn(streams, lower_bounds, norm, *, batch, seq):
    dk = HGRN_HEAD_DIM
    heads = norm.shape[1] // dk
    cum = jnp.asarray(_hgrn_patterns(), dtype=BF16)
    col = lambda name: (lambda b, h, o=streams[name][1] // dk: (b, o + h))
    return pl.pallas_call(
        functools.partial(_hgrn_kernel, group=8),
        grid=(batch, heads),
        in_specs=[
            pl.BlockSpec((seq, dk), col("q_h")),
            pl.BlockSpec((seq, dk), col("f_h")),
            pl.BlockSpec((seq, dk), col("i_h")),
            pl.BlockSpec((seq, dk), col("z_h")),
            pl.BlockSpec((lower_bounds.shape[0], dk), lambda b, h: (0, h)),
            pl.BlockSpec((1, dk), lambda b, h: (0, h)),
            pl.BlockSpec(cum.shape, lambda b, h: (0, 0)),
        ],
        out_specs=pl.BlockSpec((seq, dk), lambda b, h: (b, h)),
        out_shape=jax.ShapeDtypeStruct((batch * seq, heads * dk), BF16),
        compiler_params=pltpu.CompilerParams(
            dimension_semantics=("arbitrary", "arbitrary"),
            vmem_limit_bytes=VMEM_LIMIT),
        name="hgrn",
    )(streams["q_h"][0], streams["f_h"][0], streams["i_h"][0], streams["z_h"][0], lower_bounds, norm, cum)


def _outproj_kernel(x_ref, oa_ref, oh_ref, wa_ref, wh_ref, g_ref, o_ref):
    y = x_ref[...] + jnp.dot(oa_ref[...], wa_ref[...], preferred_element_type=F32)
    y = y + jnp.dot(oh_ref[...], wh_ref[...], preferred_element_type=F32)
    ms = jnp.mean(y * y, axis=-1, keepdims=True)
    o_ref[...] = y * lax.rsqrt(ms + EPS) * g_ref[...]


def _outproj(x2, o_a, o_h, wa, wh, g, *, tm):
    M, D = x2.shape
    const = lambda a: pl.BlockSpec(a.shape, lambda i: (0,) * a.ndim)
    return pl.pallas_call(
        _outproj_kernel,
        grid=(M // tm,),
        in_specs=[
            pl.BlockSpec((tm, D), lambda i: (i, 0)),
            pl.BlockSpec((tm, o_a.shape[1]), lambda i: (i, 0)),
            pl.BlockSpec((tm, o_h.shape[1]), lambda i: (i, 0)),
            const(wa), const(wh), const(g),
        ],
        out_specs=pl.BlockSpec((tm, D), lambda i: (i, 0)),
        out_shape=jax.ShapeDtypeStruct((M, D), F32),
        compiler_params=pltpu.CompilerParams(
            dimension_semantics=("arbitrary",),
            vmem_limit_bytes=VMEM_LIMIT),
        name="outproj",
    )(x2, o_a, o_h, wa, wh, g)


def kernel(x, norm_in, w_in, cmp_pe_k, cmp_w1_k, cmp_w2_k, cmp_pe_v, cmp_w1_v, cmp_w2_v,
           lower_bounds, nsa_out_norm, hgrn_out_norm, w_out, final_norm):
    B, S, D = x.shape
    assert norm_in.shape[0] == 1, "single-layer problem"
    nsa_w = nsa_out_norm.shape[1]
    hgrn_w = hgrn_out_norm.shape[1]
    dh = NSA_HEAD_DIM
    G = NSA_KV_HEADS
    heads = nsa_w // dh
    kvw = G * dh
    n_gate = 3 * heads
    n_cmp = (S - CMP_BLOCK) // CMP_STRIDE + 1
    n_half = S // CMP_STRIDE
    assert S % NSA_TILE == 0 and WINDOW % NSA_TILE == 0 and n_half <= LANES and 2 * dh == LANES

    names = ["q_a", "k_cmp", "v_cmp", "k_slc", "v_slc", "k_win", "v_win", "gate", "z_a",
             "q_h", "f_h", "i_h", "z_h"]
    widths = [nsa_w] + [kvw] * 6 + [n_gate, nsa_w] + [hgrn_w] * 4
    starts = dict(zip(names, np.cumsum([0] + widths[:-1]).tolist()))
    wd = dict(zip(names, widths))
    w = w_in[0].astype(BF16)
    piece = lambda n: w[:, starts[n]:starts[n] + wd[n]]

    def paired(kname, vname):
        k = piece(kname).reshape(D, G, dh)
        v = piece(vname).reshape(D, G, dh)
        return jnp.concatenate([k, v], axis=-1).reshape(D, G * 2 * dh)

    blocks16 = [("q_a", piece("q_a") * (dh ** -0.5)),
                ("cmp", paired("k_cmp", "v_cmp")), ("slc", paired("k_slc", "v_slc")),
                ("win", paired("k_win", "v_win")), ("q_h", piece("q_h")), ("i_h", piece("i_h"))]
    blocks32 = [("z_a", piece("z_a")), ("f_h", piece("f_h")), ("z_h", piece("z_h"))]
    n16_cols = sum(blk.shape[1] for _, blk in blocks16)
    w_main = jnp.concatenate([blk for _, blk in blocks16 + blocks32], axis=1)
    w_gate = jnp.pad(piece("gate"), ((0, 0), (0, LANES - n_gate)))

    x2 = x.reshape(B * S, D)
    proj16, proj32, gate = _proj(x2, norm_in, w_main, w_gate, tm=min(1024, B * S), tn=1536,
                                 n16_cols=n16_cols)
    streams = {}
    for arr, blocks in ((proj16, blocks16), (proj32, blocks32)):
        off = 0
        for n, blk in blocks:
            streams[n] = (arr, off)
            off += blk.shape[1]

    featk, featw, featc, qfeat, qrel, seg, mmap, place = _nsa_tables(S, n_half, n_cmp, heads)

    def w1_halves(w1k, w1v):
        hk = w1k.shape[1]
        k3 = w1k.reshape(2, CMP_STRIDE, dh, hk)
        v3 = w1v.reshape(2, CMP_STRIDE, dh, hk)
        zk = jnp.zeros_like(k3[0])
        top = lambda a: jnp.concatenate([a, zk], axis=-1)
        bot = lambda a: jnp.concatenate([zk, a], axis=-1)
        half = lambda i: jnp.concatenate([top(k3[i]), bot(v3[i])], axis=1).reshape(CMP_STRIDE * 2 * dh, 2 * hk)
        return half(0).astype(BF16), half(1).astype(BF16)

    wp, wq = w1_halves(cmp_w1_k[0], cmp_w1_v[0])
    zk = jnp.zeros_like(cmp_w2_k[0])
    w2 = jnp.concatenate([jnp.concatenate([cmp_w2_k[0], zk], axis=1),
                          jnp.concatenate([zk, cmp_w2_v[0]], axis=1)], axis=0).astype(BF16)
    pe = jnp.concatenate([cmp_pe_k[0].reshape(2, CMP_STRIDE, dh), cmp_pe_v[0].reshape(2, CMP_STRIDE, dh)],
                         axis=-1).reshape(2, CMP_STRIDE * 2 * dh)
    kc, vc = _compress(streams["cmp"][0], pe, wp, wq, w2, jnp.tile(featc, (G, 1)),
                       batch=B, seq=S, col0=streams["cmp"][1], n_half=n_half)

    o_a = _nsa(streams, gate, kc, vc, featk, featw, qfeat, qrel, seg, mmap, place, nsa_out_norm,
               batch=B, seq=S, n_cmp=n_cmp)
    o_h = _hgrn(streams, lower_bounds, hgrn_out_norm, batch=B, seq=S)

    wo = w_out[0].astype(BF16)
    out = _outproj(x2, o_a, o_h, wo[:nsa_w], wo[nsa_w:], final_norm.reshape(1, D), tm=512)
    return out.reshape(B, S, D)
```

```python
import functools

import ml_dtypes
import numpy as np
import jax
import jax.numpy as jnp
from jax import lax
from jax.experimental import pallas as pl
from jax.experimental.pallas import tpu as pltpu

F32 = jnp.float32
BF16 = jnp.bfloat16

EPS = 1e-6
NEG_INF = -1e30
LOG2E = 1.4426950408889634

NSA_HEAD_DIM = 64
NSA_KV_HEADS = 4
CMP_BLOCK = 32
CMP_STRIDE = 16
SEL_BLOCK = 64
SEL_TOP = 8
SEL_BONUS = 1.0e4
WINDOW = 512
HGRN_HEAD_DIM = 128
HGRN_CHUNK = 64
HGRN_SUB = 16

LANES = 128
VMEM_LIMIT = 56 * 1024 * 1024
NSA_TILE = 256

SEL_LANE0 = NSA_HEAD_DIM
MAX_SEL_BLOCKS = 32
POS_LANE0 = SEL_LANE0 + MAX_SEL_BLOCKS
N_SPLIT = 4
FLAG_LANE = POS_LANE0 + 2 * N_SPLIT
SHIFT_LANE0 = FLAG_LANE + 1
REL_LANE0 = SHIFT_LANE0 + N_SPLIT
SCORE_BOUND = 100.0
SAFETY = 1.02


def _nt_dot(a, b):
    return lax.dot_general(a, b, (((1,), (1,)), ((), ())), preferred_element_type=F32)


def _tn_dot(a, b):
    return lax.dot_general(a, b, (((0,), (0,)), ((), ())), preferred_element_type=F32)


def _split3(x):
    hi = x.astype(BF16)
    r1 = x - hi.astype(F32)
    mid = r1.astype(BF16)
    lo = (r1 - mid.astype(F32)).astype(BF16)
    return hi, mid, lo


def _proj_kernel(x_ref, g_ref, w_ref, wg_ref, o16_ref, o32_ref, og_ref, h_ref, *, row_chunk, n16):
    j = pl.program_id(1)

    @pl.when(j == 0)
    def _():
        n_chunks = x_ref.shape[0] // row_chunk

        def body(c, carry):
            rows = pl.ds(pl.multiple_of(c * row_chunk, row_chunk), row_chunk)
            x = x_ref[rows, :]
            ms = jnp.mean(x * x, axis=-1, keepdims=True)
            h_ref[rows, :] = (x * lax.rsqrt(ms + EPS) * g_ref[...]).astype(BF16)
            return carry

        lax.fori_loop(0, n_chunks, body, 0)
        og_ref[...] = jnp.dot(h_ref[...], wg_ref[...], preferred_element_type=F32)

    @pl.when(j < n16)
    def _():
        o16_ref[...] = jnp.dot(h_ref[...], w_ref[...], preferred_element_type=F32).astype(BF16)

    @pl.when(j >= n16)
    def _():
        o32_ref[...] = jnp.dot(h_ref[...], w_ref[...], preferred_element_type=F32)


def _proj(x2, g, w, wg, *, tm, tn, n16_cols):
    M, D = x2.shape
    N = w.shape[1]
    NG = wg.shape[1]
    n16 = n16_cols // tn
    assert n16 * tn == n16_cols and N % tn == 0
    return pl.pallas_call(
        functools.partial(_proj_kernel, row_chunk=128, n16=n16),
        grid=(M // tm, N // tn),
        in_specs=[
            pl.BlockSpec((tm, D), lambda i, j: (i, 0)),
            pl.BlockSpec((1, D), lambda i, j: (0, 0)),
            pl.BlockSpec((D, tn), lambda i, j: (0, j)),
            pl.BlockSpec((D, NG), lambda i, j: (0, 0)),
        ],
        out_specs=[
            pl.BlockSpec((tm, tn), lambda i, j: (i, jnp.minimum(j, n16 - 1))),
            pl.BlockSpec((tm, tn), lambda i, j: (i, jnp.maximum(j - n16, 0))),
            pl.BlockSpec((tm, NG), lambda i, j: (i, 0)),
        ],
        out_shape=[
            jax.ShapeDtypeStruct((M, n16_cols), BF16),
            jax.ShapeDtypeStruct((M, N - n16_cols), F32),
            jax.ShapeDtypeStruct((M, NG), F32),
        ],
        scratch_shapes=[pltpu.VMEM((tm, D), BF16)],
        compiler_params=pltpu.CompilerParams(
            dimension_semantics=("arbitrary", "arbitrary"),
            vmem_limit_bytes=VMEM_LIMIT),
        name="proj",
    )(x2, g, w, wg)


def _compress_kernel(c0_ref, c1_ref, c2_ref, c3_ref, pe_ref, wp_ref, wq_ref, w2_ref, feat_ref,
                     kc_ref, vc_ref, x_ref, c32_ref, *, n_half):
    for g, c_ref in enumerate((c0_ref, c1_ref, c2_ref, c3_ref)):
        c32_ref[...] = c_ref[...].astype(F32)
        for l in range(CMP_STRIDE):
            x_ref[g * n_half:(g + 1) * n_half, l * LANES:(l + 1) * LANES] = (
                c32_ref[pl.ds(l, n_half, stride=CMP_STRIDE), :])
    x = x_ref[...]
    rows = x.shape[0]
    first = jnp.dot((x + pe_ref[0:1, :]).astype(BF16), wp_ref[...], preferred_element_type=F32)
    second = jnp.dot((x + pe_ref[1:2, :]).astype(BF16), wq_ref[...], preferred_element_type=F32)
    hidden = first + pltpu.roll(second, rows - 1, 0)
    out = jnp.dot(jax.nn.gelu(hidden).astype(BF16), w2_ref[...], preferred_element_type=F32)
    lane = lax.broadcasted_iota(jnp.int32, out.shape, 1)
    dh = NSA_HEAD_DIM
    kc_ref[...] = jnp.where(lane < dh, out.astype(BF16), feat_ref[...])
    for g in range(NSA_KV_HEADS):
        vc_ref[g * dh:(g + 1) * dh, :] = out[g * n_half:(g + 1) * n_half, :].T[dh:, :].astype(BF16)


def _compress(proj, pe, wp, wq, w2, feat, *, batch, seq, col0, n_half):
    rows = NSA_KV_HEADS * n_half
    rows_t = NSA_KV_HEADS * NSA_HEAD_DIM
    const = lambda a: pl.BlockSpec(a.shape, lambda b: (0,) * a.ndim)
    stream = lambda g: pl.BlockSpec((seq, LANES), lambda b, o=col0 // LANES + g: (b, o))
    return pl.pallas_call(
        functools.partial(_compress_kernel, n_half=n_half),
        grid=(batch,),
        in_specs=[stream(0), stream(1), stream(2), stream(3),
                  const(pe), const(wp), const(wq), const(w2), const(feat)],
        out_specs=[
            pl.BlockSpec((rows, LANES), lambda b: (b, 0)),
            pl.BlockSpec((rows_t, n_half), lambda b: (b, 0)),
        ],
        out_shape=[
            jax.ShapeDtypeStruct((batch * rows, LANES), BF16),
            jax.ShapeDtypeStruct((batch * rows_t, n_half), BF16),
        ],
        scratch_shapes=[pltpu.VMEM((rows, CMP_STRIDE * LANES), F32), pltpu.VMEM((seq, LANES), F32)],
        compiler_params=pltpu.CompilerParams(
            dimension_semantics=("arbitrary",),
            vmem_limit_bytes=VMEM_LIMIT),
        name="compress",
    )(proj, proj, proj, proj, pe, wp, wq, w2, feat)


def _nsa_kernel(q_ref, qall_ref, z_ref, gate_ref, slc_ref, win_ref, kc_ref, vct_ref, featk_ref, featw_ref,
                qfeat_ref, qrel_ref, seg_ref, mmap_ref, place_ref, norm_ref, o_ref,
                ksel_ref, vselt_ref, kwin_ref, vwint_ref, gt_ref, bounded_ref, *, n_cmp):
    g = pl.program_id(1)
    qi = pl.program_id(2)
    tq = q_ref.shape[0]
    tk = tq
    seq = slc_ref.shape[0]
    dh = NSA_HEAD_DIM
    rep = q_ref.shape[1] // dh
    n_half = kc_ref.shape[0]
    n_pad = WINDOW // tk

    @pl.when(qi == 0)
    def _():
        lane = lax.broadcasted_iota(jnp.int32, (seq, LANES), 1)

        def max_sq_norm(x, width):
            sq = x * x
            hi = sq.astype(BF16)
            lo = (sq - hi.astype(F32)).astype(BF16)
            s = (jnp.dot(hi, seg_ref[0:width, :], preferred_element_type=F32)
                 + jnp.dot(lo, seg_ref[0:width, :], preferred_element_type=F32))
            return jnp.max(s) * SAFETY

        kv = slc_ref[...]
        k2_sel = max_sq_norm(jnp.where(lane < dh, kv.astype(F32), 0.0), LANES)
        ksel_ref[...] = jnp.where(lane < dh, kv, featk_ref[...])
        vt = kv.astype(F32).T[dh:, :].astype(BF16)
        for kt in range(seq // tk):
            vselt_ref[kt] = vt[:, kt * tk:(kt + 1) * tk]
        kv = win_ref[...]
        k2_win = max_sq_norm(jnp.where(lane < dh, kv.astype(F32), 0.0), LANES)
        kwin_ref[0:WINDOW, :] = featw_ref[0:WINDOW, :]
        kwin_ref[WINDOW:, :] = jnp.where(lane < dh, kv, featw_ref[WINDOW:, :])
        vt = kv.astype(F32).T[dh:, :].astype(BF16)
        for kt in range(n_pad):
            vwint_ref[kt] = jnp.zeros((dh, tk), BF16)
        for kt in range(seq // tk):
            vwint_ref[n_pad + kt] = vt[:, kt * tk:(kt + 1) * tk]
        q2 = max_sq_norm(qall_ref[...].astype(F32), rep * dh) * (LOG2E * LOG2E)
        limit = SCORE_BOUND * SCORE_BOUND
        bounded_ref[0] = ((q2 * k2_sel <= limit) & (q2 * k2_win <= limit)).astype(jnp.int32)

    heads = lambda a: jnp.concatenate([a] * rep, axis=1)
    sub8 = lambda a: a.reshape(a.shape[0] // 8, 8, a.shape[1])

    def softmax_pv(lgs, v_tiles, bounded):
        if bounded:
            m = 0.0
        else:
            m8 = functools.reduce(jnp.maximum, [jnp.max(sub8(lg), axis=0) for lg in lgs])
            m = jnp.max(m8, axis=0, keepdims=True)
        l8 = jnp.zeros((8, rep * tq), F32)
        acc = jnp.zeros((dh, rep * tq), F32)
        for lg, vt in zip(lgs, v_tiles):
            pt = jnp.exp2(lg) if bounded else jnp.exp2(lg - m)
            l8 = l8 + jnp.sum(sub8(pt), axis=0)
            acc = acc + jnp.dot(vt, pt.astype(BF16), preferred_element_type=F32)
        return acc / jnp.sum(l8, axis=0, keepdims=True)

    def step(n, bounded):
        static = isinstance(n, int)
        assert static or not bounded
        t0 = n * tq
        lane = lax.broadcasted_iota(jnp.int32, (tq, LANES), 1)
        key_i = lax.broadcasted_iota(jnp.int32, (tk, tq), 0)
        qry_i = lax.broadcasted_iota(jnp.int32, (tk, tq), 1)
        causal_bias = heads(jnp.where(key_i <= qry_i, 0.0, NEG_INF))
        band_bias = heads(jnp.where(key_i > qry_i, 0.0, NEG_INF))

        qfeat = qfeat_ref[0, n]
        rel_lanes = (lane >= REL_LANE0) & (lane < REL_LANE0 + 3)
        qa = []
        for r in range(rep):
            qcol = q_ref[:, (r // 2) * LANES:(r // 2 + 1) * LANES].astype(F32)
            if r % 2:
                qcol = pltpu.roll(qcol, dh, 1)
            feat = jnp.where(rel_lanes, qrel_ref[0, r * tq:(r + 1) * tq, :].astype(F32), qfeat[r:r + 1, :])
            qa.append(jnp.where(lane < dh, qcol * LOG2E, feat))
        q0 = jnp.concatenate(qa, axis=0).astype(BF16)

        n_wt = n_pad + 1
        lgs = []
        for w in range(n_wt):
            rows = (slice(t0 + w * tk, t0 + (w + 1) * tk) if static
                    else pl.ds(pl.multiple_of(t0 + w * tk, tk), tk))
            lg = _nt_dot(kwin_ref[rows, :], q0)
            if w == 0:
                lg = lg + band_bias
            if w == n_wt - 1:
                lg = lg + causal_bias
            lgs.append(lg)
        o_win = softmax_pv(lgs, [vwint_ref[n + w] for w in range(n_wt)], bounded)

        nc = min(n_half, (t0 + tq) // CMP_STRIDE) if static else n_half
        n_idx = lax.broadcasted_iota(jnp.int32, (nc, tq), 0)
        t_idx = t0 + lax.broadcasted_iota(jnp.int32, (nc, tq), 1)
        valid_c = heads((CMP_STRIDE * n_idx + (CMP_BLOCK - 1) <= t_idx) & (n_idx < n_cmp))
        lg = jnp.where(valid_c, _nt_dot(kc_ref[0:nc, :], q0), NEG_INF)
        e = jnp.exp2(lg - jnp.max(lg, axis=0, keepdims=True))
        p = jnp.where(valid_c, e / jnp.sum(e, axis=0, keepdims=True), 0.0)
        if nc < n_half:
            p = jnp.concatenate([p, jnp.zeros((n_half - nc, rep * tq), F32)], axis=0)
        o_cmp = jnp.dot(vct_ref[...], p.astype(BF16), preferred_element_type=F32)
        p_sum = p[:, 0:tq]
        for r in range(1, rep):
            p_sum = p_sum + p[:, r * tq:(r + 1) * tq]

        nb = min(MAX_SEL_BLOCKS, (t0 + tq) // SEL_BLOCK) if static else MAX_SEL_BLOCKS
        mm = mmap_ref[...]
        hi, mid, lo = _split3(p_sum)
        p_slc = (jnp.dot(mm, hi, preferred_element_type=F32) + jnp.dot(mm, mid, preferred_element_type=F32)
                 + jnp.dot(mm, lo, preferred_element_type=F32))[0:MAX_SEL_BLOCKS]
        jj = lax.broadcasted_iota(jnp.int32, (MAX_SEL_BLOCKS, tq), 0)
        tt = t0 + lax.broadcasted_iota(jnp.int32, (MAX_SEL_BLOCKS, tq), 1)
        cur = lax.shift_right_logical(tt, int(np.log2(SEL_BLOCK)))
        forced = (jj == 0) | (jj == cur) | (jj == cur - 1)
        future = jj > cur
        score = jnp.where(future, -1.0, p_slc + jnp.where(forced, SEL_BONUS, 0.0))
        rank = jnp.zeros((MAX_SEL_BLOCKS, tq), jnp.int32)
        for i in range(nb):
            other = jnp.broadcast_to(score[i:i + 1, :], score.shape)
            beats = (other > score) | ((other == score) & (jj > i))
            rank = rank + beats.astype(jnp.int32)
        chosen = jnp.where((rank < SEL_TOP) & jnp.logical_not(future), 1.0, 0.0).astype(BF16)
        placed = _tn_dot(chosen, place_ref[...])
        sel_lanes = (lane >= SEL_LANE0) & (lane < SEL_LANE0 + MAX_SEL_BLOCKS)
        sel_bias = (placed - 1.0) * (-NEG_INF)
        qs = jnp.concatenate([jnp.where(sel_lanes, sel_bias, a) for a in qa], axis=0).astype(BF16)

        if static:
            lgs = [_nt_dot(ksel_ref[kt * tk:(kt + 1) * tk, :], qs) for kt in range(n + 1)]
            lgs[n] = lgs[n] + causal_bias
            o_slc = softmax_pv(lgs, [vselt_ref[kt] for kt in range(n + 1)], bounded)
        else:
            def sel_tile(kt, carry, bias=None):
                m, l, acc = carry
                lg = _nt_dot(ksel_ref[pl.ds(pl.multiple_of(kt * tk, tk), tk), :], qs)
                if bias is not None:
                    lg = lg + bias
                m_new = jnp.maximum(m, jnp.max(lg, axis=0, keepdims=True))
                alpha = jnp.exp2(m - m_new)
                pt = jnp.exp2(lg - m_new)
                return (m_new, alpha * l + jnp.sum(pt, axis=0, keepdims=True),
                        alpha * acc + jnp.dot(vselt_ref[kt], pt.astype(BF16), preferred_element_type=F32))

            init = (jnp.full((1, rep * tq), NEG_INF, F32), jnp.zeros((1, rep * tq), F32),
                    jnp.zeros((dh, rep * tq), F32))
            _, l, acc = sel_tile(n, lax.fori_loop(0, n, sel_tile, init), causal_bias)
            o_slc = acc / l

        gt_ref[...] = jax.nn.sigmoid(gate_ref[...]).T
        ys = []
        for r in range(rep):
            cols = slice(r * tq, (r + 1) * tq)
            gate = lambda c: gt_ref[pl.ds(3 * (g * rep + r) + c, 1), :]
            o = gate(0) * o_cmp[:, cols] + gate(1) * o_slc[:, cols] + gate(2) * o_win[:, cols]
            ys.append(o * lax.rsqrt(jnp.mean(o * o, axis=0, keepdims=True) + EPS))
        for pair in range(rep // 2):
            cols = slice(pair * LANES, (pair + 1) * LANES)
            y = jnp.concatenate([ys[2 * pair], ys[2 * pair + 1]], axis=0).T * norm_ref[:, cols]
            z = z_ref[:, cols]
            o_ref[:, cols] = (y * (z * jax.nn.sigmoid(z))).astype(o_ref.dtype)

    @pl.when(bounded_ref[0] == 1)
    def _():
        lax.switch(qi, [functools.partial(step, n, True) for n in range(seq // tq)])

    @pl.when(bounded_ref[0] == 0)
    def _():
        step(qi, False)


def _nsa(streams, gate, kc, vc, featk, featw, qfeat, qrel, seg, mmap, place, norm, *, batch, seq, n_cmp):
    tq = NSA_TILE
    nq = seq // tq
    G = NSA_KV_HEADS
    gw = norm.shape[1] // G
    n_half = kc.shape[0] // (batch * G)
    const = lambda a: pl.BlockSpec(a.shape, lambda b, g, i: (0,) * a.ndim)
    stream = lambda name: pl.BlockSpec((seq, LANES),
                                       lambda b, g, i, o=streams[name][1] // LANES: (b, o + g))
    tile = lambda name: pl.BlockSpec((tq, gw), lambda b, g, i, o=streams[name][1] // gw: (b * nq + i, o + g))
    return pl.pallas_call(
        functools.partial(_nsa_kernel, n_cmp=n_cmp),
        grid=(batch, G, nq),
        in_specs=[
            tile("q_a"),
            pl.BlockSpec((seq, gw), lambda b, g, i, o=streams["q_a"][1] // gw: (b, o + g)),
            tile("z_a"),
            pl.BlockSpec((tq, gate.shape[1]), lambda b, g, i: (b * nq + i, 0)),
            stream("slc"), stream("win"),
            pl.BlockSpec((n_half, LANES), lambda b, g, i: (b * G + g, 0)),
            pl.BlockSpec((NSA_HEAD_DIM, n_half), lambda b, g, i: (b * G + g, 0)),
            const(featk), const(featw),
            pl.BlockSpec((1,) + qfeat.shape[1:], lambda b, g, i: (g, 0, 0, 0)),
            pl.BlockSpec((1,) + qrel.shape[1:], lambda b, g, i: (g, 0, 0)),
            const(seg), const(mmap), const(place),
            pl.BlockSpec((1, gw), lambda b, g, i: (0, g)),
        ],
        out_specs=pl.BlockSpec((tq, gw), lambda b, g, i: (b * nq + i, g)),
        out_shape=jax.ShapeDtypeStruct((batch * seq, G * gw), BF16),
        scratch_shapes=[
            pltpu.VMEM((seq, LANES), BF16),
            pltpu.VMEM((seq // tq, NSA_HEAD_DIM, tq), BF16),
            pltpu.VMEM((seq + WINDOW, LANES), BF16),
            pltpu.VMEM(((seq + WINDOW) // tq, NSA_HEAD_DIM, tq), BF16),
            pltpu.VMEM((gate.shape[1], tq), F32),
            pltpu.SMEM((1,), jnp.int32),
        ],
        compiler_params=pltpu.CompilerParams(
            dimension_semantics=("arbitrary", "arbitrary", "arbitrary"),
            vmem_limit_bytes=VMEM_LIMIT),
        name="nsa",
    )(streams["q_a"][0], streams["q_a"][0], streams["z_a"][0], gate, streams["slc"][0], streams["win"][0],
      kc, vc, featk, featw, qfeat, qrel, seg, mmap, place, norm)


def _bf16_terms(x, n):
    terms, rest = [], np.asarray(x, np.float64)
    for _ in range(n):
        t = rest.astype(np.float32).astype(ml_dtypes.bfloat16).astype(np.float64)
        terms.append(t.astype(np.float32))
        rest = rest - t
    return terms


def _nsa_tables(seq, n_half, n_cmp, heads):
    assert seq // SEL_BLOCK <= MAX_SEL_BLOCKS and FLAG_LANE < LANES

    def key_features(pos, onehot_blocks):
        f = np.zeros((len(pos), LANES), np.float32)
        if onehot_blocks:
            f[np.arange(len(pos)), SEL_LANE0 + pos // SEL_BLOCK] = 1.0
        f[:, POS_LANE0:POS_LANE0 + N_SPLIT] = ((pos // 64) * 64)[:, None]
        f[:, POS_LANE0 + N_SPLIT:POS_LANE0 + 2 * N_SPLIT] = (pos % 64)[:, None]
        return f

    assert REL_LANE0 + 3 <= LANES
    featk = key_features(np.arange(seq), True)
    featk[:, SHIFT_LANE0:REL_LANE0 + 3] = 1.0
    featw = np.concatenate([np.zeros((WINDOW, LANES), np.float32), key_features(np.arange(seq), False)])
    featw[:WINDOW, FLAG_LANE] = NEG_INF
    featw[WINDOW:, SHIFT_LANE0:REL_LANE0 + 3] = 1.0
    featc = key_features(CMP_STRIDE * np.arange(n_half) + CMP_BLOCK - 1, False)
    featc[n_cmp:, FLAG_LANE] = NEG_INF

    slopes = (2.0 ** (-8.0 * np.arange(1, heads + 1) / heads)).astype(np.float32).astype(np.float64)
    slopes2 = slopes * LOG2E
    terms = _bf16_terms(slopes2, N_SPLIT)
    rep = heads // NSA_KV_HEADS
    nq = seq // NSA_TILE
    qfeat = np.zeros((NSA_KV_HEADS, nq, 8, LANES), np.float32)
    qrel = np.zeros((NSA_KV_HEADS, rep * NSA_TILE, LANES), np.float32)
    for h in range(heads):
        g, r = divmod(h, rep)
        for i, t in enumerate(terms):
            qfeat[g, :, r, POS_LANE0 + i] = t[h]
            qfeat[g, :, r, POS_LANE0 + N_SPLIT + i] = t[h]
        qfeat[g, :, r, FLAG_LANE] = 1.0
        for i, t in enumerate(_bf16_terms(-slopes2[h] * NSA_TILE * np.arange(nq), N_SPLIT)):
            qfeat[g, :, r, SHIFT_LANE0 + i] = t
        for i, t in enumerate(_bf16_terms(-slopes2[h] * np.arange(NSA_TILE), 3)):
            qrel[g, r * NSA_TILE:(r + 1) * NSA_TILE, REL_LANE0 + i] = t
    seg = np.zeros((2 * LANES, LANES), np.float32)
    seg[np.arange(2 * LANES), np.arange(2 * LANES) // NSA_HEAD_DIM] = 1.0

    cs = CMP_STRIDE * np.arange(n_half)[None, :]
    ss = SEL_BLOCK * np.arange(LANES)[:, None]
    overlap = np.clip(np.minimum(cs + CMP_BLOCK, ss + SEL_BLOCK) - np.maximum(cs, ss), 0, None)
    mmap = (overlap / CMP_BLOCK) * (np.arange(n_half)[None, :] < n_cmp) * (ss < seq)
    place = np.zeros((MAX_SEL_BLOCKS, LANES), np.float32)
    place[np.arange(MAX_SEL_BLOCKS), SEL_LANE0 + np.arange(MAX_SEL_BLOCKS)] = 1.0
    bf = lambda a: jnp.asarray(a, dtype=BF16)
    return bf(featk), bf(featw), bf(featc), jnp.asarray(qfeat), bf(qrel), bf(seg), bf(mmap), bf(place)


def _hgrn_kernel(q_ref, f_ref, v_ref, z_ref, lb_ref, norm_ref, cum_ref, o_ref, *, group):
    C, SUB = HGRN_CHUNK, HGRN_SUB
    n_sub = C // SUB
    seq, dk = q_ref.shape

    lbr = lb_ref[...]
    e = jnp.exp(lbr - jnp.max(lbr, axis=0, keepdims=True))
    lb = e[0:1, :] / jnp.sum(e, axis=0, keepdims=True)
    gain = norm_ref[...]
    cum = cum_ref[...]
    sub = lambda x, i: x[i * SUB:(i + 1) * SUB]
    masks = []
    for i in range(n_sub):
        width = (i + 1) * SUB
        masks.append(lax.broadcasted_iota(jnp.int32, (SUB, width), 1)
                     <= lax.broadcasted_iota(jnp.int32, (SUB, width), 0) + i * SUB)

    def block(cb, st):
        chunks = range(group)
        rows = [pl.ds(pl.multiple_of((cb * group + j) * C, C), C) for j in chunks]
        v16 = [v_ref[rows[j], :].astype(BF16) for j in chunks]
        f = [lb + (1.0 - lb) * jax.nn.sigmoid(f_ref[rows[j], :]) for j in chunks]
        k = [1.0 - f[j] for j in chunks]
        parts = [_split3(jnp.log(f[j])) for j in chunks]
        a = [(jnp.dot(cum, parts[j][0], preferred_element_type=F32)
              + jnp.dot(cum, parts[j][1], preferred_element_type=F32)
              + jnp.dot(cum, parts[j][2], preferred_element_type=F32)) for j in chunks]
        q1, k1, k2, qb, k3, start, dec = [], [], [], [], [], [], []
        for j in chunks:
            q1.append(q_ref[rows[j], :] * jnp.exp(a[j]))
            k1.append(k[j] * jnp.exp(-a[j]))
            tot = [a[j][(i + 1) * SUB - 1:(i + 1) * SUB] for i in range(n_sub)]
            s = [jnp.zeros_like(tot[0])]
            for i in range(n_sub):
                s.append(s[i] + tot[i])
            start.append(s)
            dec.append(jnp.exp(s[n_sub]))
            qb.append(jnp.concatenate([sub(q1[j], i) * jnp.exp(s[i]) for i in range(n_sub)], axis=0)
                      .astype(BF16))
            k2.append([sub(k1[j], i) * jnp.exp(tot[i]) for i in range(n_sub)])
            k3.append(jnp.concatenate([sub(k1[j], i) * jnp.exp(s[n_sub] - s[i]) for i in range(n_sub)],
                                      axis=0).astype(BF16))
        att = []
        for j in chunks:
            row_blocks = []
            for i in range(n_sub):
                rhs = [k2[j][jb] * jnp.exp(start[j][i] - start[j][jb + 1]) if jb < i - 1 else k2[j][jb]
                       for jb in range(i)]
                rhs.append(sub(k1[j], i))
                rhs = jnp.concatenate(rhs, axis=0) if len(rhs) > 1 else rhs[0]
                sc = _nt_dot(sub(q1[j], i).astype(BF16), rhs.astype(BF16))
                row_blocks.append(jnp.where(masks[i], sc, 0.0).astype(BF16))
            att.append(row_blocks)
        intra = [jnp.concatenate([jnp.dot(att[j][i], v16[j][0:(i + 1) * SUB], preferred_element_type=F32)
                                  for i in range(n_sub)], axis=0) for j in chunks]
        incr = [_tn_dot(v16[j], k3[j]) for j in chunks]
        inter = []
        for j in chunks:
            inter.append(_nt_dot(qb[j], st.astype(BF16)))
            st = st * dec[j] + incr[j]
        for j in chunks:
            o = inter[j] + intra[j]
            y = o * lax.rsqrt(jnp.mean(o * o, axis=-1, keepdims=True) + EPS) * gain
            z = z_ref[rows[j], :]
            o_ref[rows[j], :] = (y * (z * jax.nn.sigmoid(z))).astype(o_ref.dtype)
        return st

    lax.fori_loop(0, seq // (C * group), block, jnp.zeros((dk, dk), F32))


def _hgrn_patterns():
    C, SUB = HGRN_CHUNK, HGRN_SUB
    t = np.arange(C)[:, None]
    s = np.arange(C)[None, :]
    return (((t // SUB) == (s // SUB)) & (s <= t)).astype(np.float32)


def _hgrn(streams, lower_bounds, norm, *, batch, seq):
    dk = HGRN_HEAD_DIM
    heads = norm.shape[1] // dk
    cum = jnp.asarray(_hgrn_patterns(), dtype=BF16)
    col = lambda name: (lambda b, h, o=streams[name][1] // dk: (b, o + h))
    return pl.pallas_call(
        functools.partial(_hgrn_kernel, group=8),
        grid=(batch, heads),
        in_specs=[
            pl.BlockSpec((seq, dk), col("q_h")),
            pl.BlockSpec((seq, dk), col("f_h")),
            pl.BlockSpec((seq, dk), col("i_h")),
            pl.BlockSpec((seq, dk), col("z_h")),
            pl.BlockSpec((lower_bounds.shape[0], dk), lambda b, h: (0, h)),
            pl.BlockSpec((1, dk), lambda b, h: (0, h)),
            pl.BlockSpec(cum.shape, lambda b, h: (0, 0)),
        ],
        out_specs=pl.BlockSpec((seq, dk), lambda b, h: (b, h)),
        out_shape=jax.ShapeDtypeStruct((batch * seq, heads * dk), BF16),
        compiler_params=pltpu.CompilerParams(
            dimension_semantics=("arbitrary", "arbitrary"),
            vmem_limit_bytes=VMEM_LIMIT),
        name="hgrn",
    )(streams["q_h"][0], streams["f_h"][0], streams["i_h"][0], streams["z_h"][0], lower_bounds, norm, cum)


def _outproj_kernel(x_ref, oa_ref, oh_ref, wa_ref, wh_ref, g_ref, o_ref):
    y = x_ref[...] + jnp.dot(oa_ref[...], wa_ref[...], preferred_element_type=F32)
    y = y + jnp.dot(oh_ref[...], wh_ref[...], preferred_element_type=F32)
    ms = jnp.mean(y * y, axis=-1, keepdims=True)
    o_ref[...] = y * lax.rsqrt(ms + EPS) * g_ref[...]


def _outproj(x2, o_a, o_h, wa, wh, g, *, tm):
    M, D = x2.shape
    const = lambda a: pl.BlockSpec(a.shape, lambda i: (0,) * a.ndim)
    return pl.pallas_call(
        _outproj_kernel,
        grid=(M // tm,),
        in_specs=[
            pl.BlockSpec((tm, D), lambda i: (i, 0)),
            pl.BlockSpec((tm, o_a.shape[1]), lambda i: (i, 0)),
            pl.BlockSpec((tm, o_h.shape[1]), lambda i: (i, 0)),
            const(wa), const(wh), const(g),
        ],
        out_specs=pl.BlockSpec((tm, D), lambda i: (i, 0)),
        out_shape=jax.ShapeDtypeStruct((M, D), F32),
        compiler_params=pltpu.CompilerParams(
            dimension_semantics=("arbitrary",),
            vmem_limit_bytes=VMEM_LIMIT),
        name="outproj",
    )(x2, o_a, o_h, wa, wh, g)


def kernel(x, norm_in, w_in, cmp_pe_k, cmp_w1_k, cmp_w2_k, cmp_pe_v, cmp_w1_v, cmp_w2_v,
           lower_bounds, nsa_out_norm, hgrn_out_norm, w_out, final_norm):
    B, S, D = x.shape
    assert norm_in.shape[0] == 1, "single-layer problem"
    nsa_w = nsa_out_norm.shape[1]
    hgrn_w = hgrn_out_norm.shape[1]
    dh = NSA_HEAD_DIM
    G = NSA_KV_HEADS
    heads = nsa_w // dh
    kvw = G * dh
    n_gate = 3 * heads
    n_cmp = (S - CMP_BLOCK) // CMP_STRIDE + 1
    n_half = S // CMP_STRIDE
    assert S % NSA_TILE == 0 and WINDOW % NSA_TILE == 0 and n_half <= LANES and 2 * dh == LANES

    names = ["q_a", "k_cmp", "v_cmp", "k_slc", "v_slc", "k_win", "v_win", "gate", "z_a",
             "q_h", "f_h", "i_h", "z_h"]
    widths = [nsa_w] + [kvw] * 6 + [n_gate, nsa_w] + [hgrn_w] * 4
    starts = dict(zip(names, np.cumsum([0] + widths[:-1]).tolist()))
    wd = dict(zip(names, widths))
    w = w_in[0].astype(BF16)
    piece = lambda n: w[:, starts[n]:starts[n] + wd[n]]

    def paired(kname, vname):
        k = piece(kname).reshape(D, G, dh)
        v = piece(vname).reshape(D, G, dh)
        return jnp.concatenate([k, v], axis=-1).reshape(D, G * 2 * dh)

    blocks16 = [("q_a", piece("q_a") * (dh ** -0.5)),
                ("cmp", paired("k_cmp", "v_cmp")), ("slc", paired("k_slc", "v_slc")),
                ("win", paired("k_win", "v_win")), ("q_h", piece("q_h")), ("i_h", piece("i_h"))]
    blocks32 = [("z_a", piece("z_a")), ("f_h", piece("f_h")), ("z_h", piece("z_h"))]
    n16_cols = sum(blk.shape[1] for _, blk in blocks16)
    w_main = jnp.concatenate([blk for _, blk in blocks16 + blocks32], axis=1)
    w_gate = jnp.pad(piece("gate"), ((0, 0), (0, LANES - n_gate)))

    x2 = x.reshape(B * S, D)
    proj16, proj32, gate = _proj(x2, norm_in, w_main, w_gate, tm=min(1024, B * S), tn=1536,
                                 n16_cols=n16_cols)
    streams = {}
    for arr, blocks in ((proj16, blocks16), (proj32, blocks32)):
        off = 0
        for n, blk in blocks:
            streams[n] = (arr, off)
            off += blk.shape[1]

    featk, featw, featc, qfeat, qrel, seg, mmap, place = _nsa_tables(S, n_half, n_cmp, heads)

    def w1_halves(w1k, w1v):
        hk = w1k.shape[1]
        k3 = w1k.reshape(2, CMP_STRIDE, dh, hk)
        v3 = w1v.reshape(2, CMP_STRIDE, dh, hk)
        zk = jnp.zeros_like(k3[0])
        top = lambda a: jnp.concatenate([a, zk], axis=-1)
        bot = lambda a: jnp.concatenate([zk, a], axis=-1)
        half = lambda i: jnp.concatenate([top(k3[i]), bot(v3[i])], axis=1).reshape(CMP_STRIDE * 2 * dh, 2 * hk)
        return half(0).astype(BF16), half(1).astype(BF16)

    wp, wq = w1_halves(cmp_w1_k[0], cmp_w1_v[0])
    zk = jnp.zeros_like(cmp_w2_k[0])
    w2 = jnp.concatenate([jnp.concatenate([cmp_w2_k[0], zk], axis=1),
                          jnp.concatenate([zk, cmp_w2_v[0]], axis=1)], axis=0).astype(BF16)
    pe = jnp.concatenate([cmp_pe_k[0].reshape(2, CMP_STRIDE, dh), cmp_pe_v[0].reshape(2, CMP_STRIDE, dh)],
                         axis=-1).reshape(2, CMP_STRIDE * 2 * dh)
    kc, vc = _compress(streams["cmp"][0], pe, wp, wq, w2, jnp.tile(featc, (G, 1)),
                       batch=B, seq=S, col0=streams["cmp"][1], n_half=n_half)

    o_a = _nsa(streams, gate, kc, vc, featk, featw, qfeat, qrel, seg, mmap, place, nsa_out_norm,
               batch=B, seq=S, n_cmp=n_cmp)
    o_h = _hgrn(streams, lower_bounds, hgrn_out_norm, batch=B, seq=S)

    wo = w_out[0].astype(BF16)
    out = _outproj(x2, o_a, o_h, wo[:nsa_w], wo[nsa_w:], final_norm.reshape(1, D), tm=512)
    return out.reshape(B, S, D)
```

```python
import functools

import ml_dtypes
import numpy as np
import jax
import jax.numpy as jnp
from jax import lax
from jax.experimental import pallas as pl
from jax.experimental.pallas import tpu as pltpu

F32 = jnp.float32
BF16 = jnp.bfloat16

EPS = 1e-6
NEG_INF = -1e30
LOG2E = 1.4426950408889634

NSA_HEAD_DIM = 64
NSA_KV_HEADS = 4
CMP_BLOCK = 32
CMP_STRIDE = 16
SEL_BLOCK = 64
SEL_TOP = 8
SEL_BONUS = 1.0e4
WINDOW = 512
HGRN_HEAD_DIM = 128
HGRN_CHUNK = 64
HGRN_SUB = 16

LANES = 128
VMEM_LIMIT = 56 * 1024 * 1024
NSA_TILE = 256

SEL_LANE0 = NSA_HEAD_DIM
MAX_SEL_BLOCKS = 32
POS_LANE0 = SEL_LANE0 + MAX_SEL_BLOCKS
N_SPLIT = 4
FLAG_LANE = POS_LANE0 + 2 * N_SPLIT
SHIFT_LANE0 = FLAG_LANE + 1
REL_LANE0 = SHIFT_LANE0 + N_SPLIT
SCORE_BOUND = 100.0
SAFETY = 1.02


def _nt_dot(a, b):
    return lax.dot_general(a, b, (((1,), (1,)), ((), ())), preferred_element_type=F32)


def _tn_dot(a, b):
    return lax.dot_general(a, b, (((0,), (0,)), ((), ())), preferred_element_type=F32)


def _split3(x):
    hi = x.astype(BF16)
    r1 = x - hi.astype(F32)
    mid = r1.astype(BF16)
    lo = (r1 - mid.astype(F32)).astype(BF16)
    return hi, mid, lo


def _proj_kernel(x_ref, g_ref, w_ref, wg_ref, o16_ref, o32_ref, og_ref, h_ref, *, row_chunk, n16):
    j = pl.program_id(1)

    @pl.when(j == 0)
    def _():
        n_chunks = x_ref.shape[0] // row_chunk

        def body(c, carry):
            rows = pl.ds(pl.multiple_of(c * row_chunk, row_chunk), row_chunk)
            x = x_ref[rows, :]
            ms = jnp.mean(x * x, axis=-1, keepdims=True)
            h_ref[rows, :] = (x * lax.rsqrt(ms + EPS) * g_ref[...]).astype(BF16)
            return carry

        lax.fori_loop(0, n_chunks, body, 0)
        og_ref[...] = jnp.dot(h_ref[...], wg_ref[...], preferred_element_type=F32)

    @pl.when(j < n16)
    def _():
        o16_ref[...] = jnp.dot(h_ref[...], w_ref[...], preferred_element_type=F32).astype(BF16)

    @pl.when(j >= n16)
    def _():
        o32_ref[...] = jnp.dot(h_ref[...], w_ref[...], preferred_element_type=F32)


def _proj(x2, g, w, wg, *, tm, tn, n16_cols):
    M, D = x2.shape
    N = w.shape[1]
    NG = wg.shape[1]
    n16 = n16_cols // tn
    assert n16 * tn == n16_cols and N % tn == 0
    return pl.pallas_call(
        functools.partial(_proj_kernel, row_chunk=128, n16=n16),
        grid=(M // tm, N // tn),
        in_specs=[
            pl.BlockSpec((tm, D), lambda i, j: (i, 0)),
            pl.BlockSpec((1, D), lambda i, j: (0, 0)),
            pl.BlockSpec((D, tn), lambda i, j: (0, j)),
            pl.BlockSpec((D, NG), lambda i, j: (0, 0)),
        ],
        out_specs=[
            pl.BlockSpec((tm, tn), lambda i, j: (i, jnp.minimum(j, n16 - 1))),
            pl.BlockSpec((tm, tn), lambda i, j: (i, jnp.maximum(j - n16, 0))),
            pl.BlockSpec((tm, NG), lambda i, j: (i, 0)),
        ],
        out_shape=[
            jax.ShapeDtypeStruct((M, n16_cols), BF16),
            jax.ShapeDtypeStruct((M, N - n16_cols), F32),
            jax.ShapeDtypeStruct((M, NG), F32),
        ],
        scratch_shapes=[pltpu.VMEM((tm, D), BF16)],
        compiler_params=pltpu.CompilerParams(
            dimension_semantics=("arbitrary", "arbitrary"),
            vmem_limit_bytes=VMEM_LIMIT),
        name="proj",
    )(x2, g, w, wg)


def _compress_kernel(c0_ref, c1_ref, c2_ref, c3_ref, pe_ref, wp_ref, wq_ref, w2_ref, feat_ref,
                     kc_ref, vc_ref, x_ref, c32_ref, *, n_half):
    for g, c_ref in enumerate((c0_ref, c1_ref, c2_ref, c3_ref)):
        c32_ref[...] = c_ref[...].astype(F32)
        for l in range(CMP_STRIDE):
            x_ref[g * n_half:(g + 1) * n_half, l * LANES:(l + 1) * LANES] = (
                c32_ref[pl.ds(l, n_half, stride=CMP_STRIDE), :])
    x = x_ref[...]
    rows = x.shape[0]
    first = jnp.dot((x + pe_ref[0:1, :]).astype(BF16), wp_ref[...], preferred_element_type=F32)
    second = jnp.dot((x + pe_ref[1:2, :]).astype(BF16), wq_ref[...], preferred_element_type=F32)
    hidden = first + pltpu.roll(second, rows - 1, 0)
    out = jnp.dot(jax.nn.gelu(hidden).astype(BF16), w2_ref[...], preferred_element_type=F32)
    lane = lax.broadcasted_iota(jnp.int32, out.shape, 1)
    dh = NSA_HEAD_DIM
    kc_ref[...] = jnp.where(lane < dh, out.astype(BF16), feat_ref[...])
    for g in range(NSA_KV_HEADS):
        vc_ref[g * dh:(g + 1) * dh, :] = out[g * n_half:(g + 1) * n_half, :].T[dh:, :].astype(BF16)


def _compress(proj, pe, wp, wq, w2, feat, *, batch, seq, col0, n_half):
    rows = NSA_KV_HEADS * n_half
    rows_t = NSA_KV_HEADS * NSA_HEAD_DIM
    const = lambda a: pl.BlockSpec(a.shape, lambda b: (0,) * a.ndim)
    stream = lambda g: pl.BlockSpec((seq, LANES), lambda b, o=col0 // LANES + g: (b, o))
    return pl.pallas_call(
        functools.partial(_compress_kernel, n_half=n_half),
        grid=(batch,),
        in_specs=[stream(0), stream(1), stream(2), stream(3),
                  const(pe), const(wp), const(wq), const(w2), const(feat)],
        out_specs=[
            pl.BlockSpec((rows, LANES), lambda b: (b, 0)),
            pl.BlockSpec((rows_t, n_half), lambda b: (b, 0)),
        ],
        out_shape=[
            jax.ShapeDtypeStruct((batch * rows, LANES), BF16),
            jax.ShapeDtypeStruct((batch * rows_t, n_half), BF16),
        ],
        scratch_shapes=[pltpu.VMEM((rows, CMP_STRIDE * LANES), F32), pltpu.VMEM((seq, LANES), F32)],
        compiler_params=pltpu.CompilerParams(
            dimension_semantics=("arbitrary",),
            vmem_limit_bytes=VMEM_LIMIT),
        name="compress",
    )(proj, proj, proj, proj, pe, wp, wq, w2, feat)


def _nsa_kernel(q_ref, qall_ref, z_ref, gate_ref, slc_ref, win_ref, kc_ref, vct_ref, featk_ref, featw_ref,
                qfeat_ref, qrel_ref, seg_ref, mmap_ref, place_ref, norm_ref, o_ref,
                ksel_ref, vselt_ref, kwin_ref, vwint_ref, gt_ref, bounded_ref, *, n_cmp):
    g = pl.program_id(1)
    qi = pl.program_id(2)
    tq = q_ref.shape[0]
    tk = tq
    seq = slc_ref.shape[0]
    dh = NSA_HEAD_DIM
    rep = q_ref.shape[1] // dh
    n_half = kc_ref.shape[0]
    n_pad = WINDOW // tk

    @pl.when(qi == 0)
    def _():
        lane = lax.broadcasted_iota(jnp.int32, (seq, LANES), 1)

        def max_sq_norm(x, width):
            sq = x * x
            hi = sq.astype(BF16)
            lo = (sq - hi.astype(F32)).astype(BF16)
            s = (jnp.dot(hi, seg_ref[0:width, :], preferred_element_type=F32)
                 + jnp.dot(lo, seg_ref[0:width, :], preferred_element_type=F32))
            return jnp.max(s) * SAFETY

        kv = slc_ref[...]
        k2_sel = max_sq_norm(jnp.where(lane < dh, kv.astype(F32), 0.0), LANES)
        ksel_ref[...] = jnp.where(lane < dh, kv, featk_ref[...])
        vt = kv.astype(F32).T[dh:, :].astype(BF16)
        for kt in range(seq // tk):
            vselt_ref[kt] = vt[:, kt * tk:(kt + 1) * tk]
        kv = win_ref[...]
        k2_win = max_sq_norm(jnp.where(lane < dh, kv.astype(F32), 0.0), LANES)
        kwin_ref[0:WINDOW, :] = featw_ref[0:WINDOW, :]
        kwin_ref[WINDOW:, :] = jnp.where(lane < dh, kv, featw_ref[WINDOW:, :])
        vt = kv.astype(F32).T[dh:, :].astype(BF16)
        for kt in range(n_pad):
            vwint_ref[kt] = jnp.zeros((dh, tk), BF16)
        for kt in range(seq // tk):
            vwint_ref[n_pad + kt] = vt[:, kt * tk:(kt + 1) * tk]
        q2 = max_sq_norm(qall_ref[...].astype(F32), rep * dh) * (LOG2E * LOG2E)
        limit = SCORE_BOUND * SCORE_BOUND
        bounded_ref[0] = ((q2 * k2_sel <= limit) & (q2 * k2_win <= limit)).astype(jnp.int32)

    heads = lambda a: jnp.concatenate([a] * rep, axis=1)
    sub8 = lambda a: a.reshape(a.shape[0] // 8, 8, a.shape[1])

    def softmax_pv(lgs, v_tiles, bounded):
        if bounded:
            m = 0.0
        else:
            m8 = functools.reduce(jnp.maximum, [jnp.max(sub8(lg), axis=0) for lg in lgs])
            m = jnp.max(m8, axis=0, keepdims=True)
        l8 = jnp.zeros((8, rep * tq), F32)
        acc = jnp.zeros((dh, rep * tq), F32)
        for lg, vt in zip(lgs, v_tiles):
            pt = jnp.exp2(lg) if bounded else jnp.exp2(lg - m)
            l8 = l8 + jnp.sum(sub8(pt), axis=0)
            acc = acc + jnp.dot(vt, pt.astype(BF16), preferred_element_type=F32)
        return acc / jnp.sum(l8, axis=0, keepdims=True)

    def step(n, bounded):
        static = isinstance(n, int)
        assert static or not bounded
        t0 = n * tq
        lane = lax.broadcasted_iota(jnp.int32, (tq, LANES), 1)
        key_i = lax.broadcasted_iota(jnp.int32, (tk, tq), 0)
        qry_i = lax.broadcasted_iota(jnp.int32, (tk, tq), 1)
        causal_1 = jnp.where(key_i <= qry_i, 0.0, NEG_INF)
        causal_bias = heads(causal_1)
        band_bias = heads(jnp.where(key_i > qry_i, 0.0, NEG_INF))

        qfeat = qfeat_ref[0, n]
        rel_lanes = (lane >= REL_LANE0) & (lane < REL_LANE0 + 3)
        qa = []
        for r in range(rep):
            qcol = q_ref[:, (r // 2) * LANES:(r // 2 + 1) * LANES].astype(F32)
            if r % 2:
                qcol = pltpu.roll(qcol, dh, 1)
            feat = jnp.where(rel_lanes, qrel_ref[0, r * tq:(r + 1) * tq, :].astype(F32), qfeat[r:r + 1, :])
            qa.append(jnp.where(lane < dh, qcol * LOG2E, feat))
        q0 = jnp.concatenate(qa, axis=0).astype(BF16)

        nc = min(n_half, (t0 + tq) // CMP_STRIDE) if static else n_half
        n_idx = lax.broadcasted_iota(jnp.int32, (nc, tq), 0)
        t_idx = t0 + lax.broadcasted_iota(jnp.int32, (nc, tq), 1)
        valid_c = heads((CMP_STRIDE * n_idx + (CMP_BLOCK - 1) <= t_idx) & (n_idx < n_cmp))
        lg_c = jnp.where(valid_c, _nt_dot(kc_ref[0:nc, :], q0), NEG_INF)

        n_wt = n_pad + 1
        lgs = []
        for w in range(n_wt):
            rows = (slice(t0 + w * tk, t0 + (w + 1) * tk) if static
                    else pl.ds(pl.multiple_of(t0 + w * tk, tk), tk))
            lg = _nt_dot(kwin_ref[rows, :], q0)
            if w == 0:
                lg = lg + band_bias
            if w == n_wt - 1:
                lg = lg + causal_bias
            lgs.append(lg)
        o_win = softmax_pv(lgs, [vwint_ref[n + w] for w in range(n_wt)], bounded)

        e = jnp.exp2(lg_c - jnp.max(lg_c, axis=0, keepdims=True))
        p = jnp.where(valid_c, e / jnp.sum(e, axis=0, keepdims=True), 0.0)
        if nc < n_half:
            p = jnp.concatenate([p, jnp.zeros((n_half - nc, rep * tq), F32)], axis=0)
        o_cmp = jnp.dot(vct_ref[...], p.astype(BF16), preferred_element_type=F32)
        p_sum = p[:, 0:tq]
        for r in range(1, rep):
            p_sum = p_sum + p[:, r * tq:(r + 1) * tq]

        nb = min(MAX_SEL_BLOCKS, (t0 + tq) // SEL_BLOCK) if static else MAX_SEL_BLOCKS
        mm = mmap_ref[...]
        hi, mid, lo = _split3(p_sum)
        p_slc = (jnp.dot(mm, hi, preferred_element_type=F32) + jnp.dot(mm, mid, preferred_element_type=F32)
                 + jnp.dot(mm, lo, preferred_element_type=F32))[0:MAX_SEL_BLOCKS]
        if static:
            raw = [_nt_dot(ksel_ref[kt * tk:(kt + 1) * tk, :], q0) for kt in range(n + 1)]
        jj = lax.broadcasted_iota(jnp.int32, (MAX_SEL_BLOCKS, tq), 0)
        tt = t0 + lax.broadcasted_iota(jnp.int32, (MAX_SEL_BLOCKS, tq), 1)
        cur = lax.shift_right_logical(tt, int(np.log2(SEL_BLOCK)))
        forced = (jj == 0) | (jj == cur) | (jj == cur - 1)
        future = jj > cur
        score = jnp.where(future, -1.0, p_slc + jnp.where(forced, SEL_BONUS, 0.0))
        rank = jnp.zeros((MAX_SEL_BLOCKS, tq), jnp.int32)
        for i in range(nb):
            other = jnp.broadcast_to(score[i:i + 1, :], score.shape)
            beats = (other > score) | ((other == score) & (jj > i))
            rank = rank + beats.astype(jnp.int32)
        keep = (rank < SEL_TOP) & jnp.logical_not(future)

        if static:
            blocks_per_tile = tk // SEL_BLOCK
            block_bias = jnp.where(keep, 0.0, NEG_INF)
            lgs = []
            for kt in range(n + 1):
                b = jnp.concatenate(
                    [jnp.broadcast_to(block_bias[blocks_per_tile * kt + j:blocks_per_tile * kt + j + 1, :],
                                      (SEL_BLOCK, tq)) for j in range(blocks_per_tile)], axis=0)
                lgs.append(raw[kt] + heads(b + causal_1 if kt == n else b))
            o_slc = softmax_pv(lgs, [vselt_ref[kt] for kt in range(n + 1)], bounded)
        else:
            chosen = jnp.where(keep, 1.0, 0.0).astype(BF16)
            placed = _tn_dot(chosen, place_ref[...])
            sel_lanes = (lane >= SEL_LANE0) & (lane < SEL_LANE0 + MAX_SEL_BLOCKS)
            sel_bias = (placed - 1.0) * (-NEG_INF)
            qs = jnp.concatenate([jnp.where(sel_lanes, sel_bias, a) for a in qa], axis=0).astype(BF16)

            def sel_tile(kt, carry, bias=None):
                m, l, acc = carry
                lg = _nt_dot(ksel_ref[pl.ds(pl.multiple_of(kt * tk, tk), tk), :], qs)
                if bias is not None:
                    lg = lg + bias
                m_new = jnp.maximum(m, jnp.max(lg, axis=0, keepdims=True))
                alpha = jnp.exp2(m - m_new)
                pt = jnp.exp2(lg - m_new)
                return (m_new, alpha * l + jnp.sum(pt, axis=0, keepdims=True),
                        alpha * acc + jnp.dot(vselt_ref[kt], pt.astype(BF16), preferred_element_type=F32))

            init = (jnp.full((1, rep * tq), NEG_INF, F32), jnp.zeros((1, rep * tq), F32),
                    jnp.zeros((dh, rep * tq), F32))
            _, l, acc = sel_tile(n, lax.fori_loop(0, n, sel_tile, init), causal_bias)
            o_slc = acc / l

        gt_ref[...] = jax.nn.sigmoid(gate_ref[...]).T
        ys = []
        for r in range(rep):
            cols = slice(r * tq, (r + 1) * tq)
            gate = lambda c: gt_ref[pl.ds(3 * (g * rep + r) + c, 1), :]
            o = gate(0) * o_cmp[:, cols] + gate(1) * o_slc[:, cols] + gate(2) * o_win[:, cols]
            ys.append(o * lax.rsqrt(jnp.mean(o * o, axis=0, keepdims=True) + EPS))
        for pair in range(rep // 2):
            cols = slice(pair * LANES, (pair + 1) * LANES)
            y = jnp.concatenate([ys[2 * pair], ys[2 * pair + 1]], axis=0).T * norm_ref[:, cols]
            z = z_ref[:, cols]
            o_ref[:, cols] = (y * (z * jax.nn.sigmoid(z))).astype(o_ref.dtype)

    @pl.when(bounded_ref[0] == 1)
    def _():
        lax.switch(qi, [functools.partial(step, n, True) for n in range(seq // tq)])

    @pl.when(bounded_ref[0] == 0)
    def _():
        step(qi, False)


def _nsa(streams, gate, kc, vc, featk, featw, qfeat, qrel, seg, mmap, place, norm, *, batch, seq, n_cmp):
    tq = NSA_TILE
    nq = seq // tq
    G = NSA_KV_HEADS
    gw = norm.shape[1] // G
    n_half = kc.shape[0] // (batch * G)
    const = lambda a: pl.BlockSpec(a.shape, lambda b, g, i: (0,) * a.ndim)
    stream = lambda name: pl.BlockSpec((seq, LANES),
                                       lambda b, g, i, o=streams[name][1] // LANES: (b, o + g))
    tile = lambda name: pl.BlockSpec((tq, gw), lambda b, g, i, o=streams[name][1] // gw: (b * nq + i, o + g))
    return pl.pallas_call(
        functools.partial(_nsa_kernel, n_cmp=n_cmp),
        grid=(batch, G, nq),
        in_specs=[
            tile("q_a"),
            pl.BlockSpec((seq, gw), lambda b, g, i, o=streams["q_a"][1] // gw: (b, o + g)),
            tile("z_a"),
            pl.BlockSpec((tq, gate.shape[1]), lambda b, g, i: (b * nq + i, 0)),
            stream("slc"), stream("win"),
            pl.BlockSpec((n_half, LANES), lambda b, g, i: (b * G + g, 0)),
            pl.BlockSpec((NSA_HEAD_DIM, n_half), lambda b, g, i: (b * G + g, 0)),
            const(featk), const(featw),
            pl.BlockSpec((1,) + qfeat.shape[1:], lambda b, g, i: (g, 0, 0, 0)),
            pl.BlockSpec((1,) + qrel.shape[1:], lambda b, g, i: (g, 0, 0)),
            const(seg), const(mmap), const(place),
            pl.BlockSpec((1, gw), lambda b, g, i: (0, g)),
        ],
        out_specs=pl.BlockSpec((tq, gw), lambda b, g, i: (b * nq + i, g)),
        out_shape=jax.ShapeDtypeStruct((batch * seq, G * gw), BF16),
        scratch_shapes=[
            pltpu.VMEM((seq, LANES), BF16),
            pltpu.VMEM((seq // tq, NSA_HEAD_DIM, tq), BF16),
            pltpu.VMEM((seq + WINDOW, LANES), BF16),
            pltpu.VMEM(((seq + WINDOW) // tq, NSA_HEAD_DIM, tq), BF16),
            pltpu.VMEM((gate.shape[1], tq), F32),
            pltpu.SMEM((1,), jnp.int32),
        ],
        compiler_params=pltpu.CompilerParams(
            dimension_semantics=("arbitrary", "arbitrary", "arbitrary"),
            vmem_limit_bytes=VMEM_LIMIT),
        name="nsa",
    )(streams["q_a"][0], streams["q_a"][0], streams["z_a"][0], gate, streams["slc"][0], streams["win"][0],
      kc, vc, featk, featw, qfeat, qrel, seg, mmap, place, norm)


def _bf16_terms(x, n):
    terms, rest = [], np.asarray(x, np.float64)
    for _ in range(n):
        t = rest.astype(np.float32).astype(ml_dtypes.bfloat16).astype(np.float64)
        terms.append(t.astype(np.float32))
        rest = rest - t
    return terms


def _nsa_tables(seq, n_half, n_cmp, heads):
    assert seq // SEL_BLOCK <= MAX_SEL_BLOCKS and FLAG_LANE < LANES

    def key_features(pos, onehot_blocks):
        f = np.zeros((len(pos), LANES), np.float32)
        if onehot_blocks:
            f[np.arange(len(pos)), SEL_LANE0 + pos // SEL_BLOCK] = 1.0
        f[:, POS_LANE0:POS_LANE0 + N_SPLIT] = ((pos // 64) * 64)[:, None]
        f[:, POS_LANE0 + N_SPLIT:POS_LANE0 + 2 * N_SPLIT] = (pos % 64)[:, None]
        return f

    assert REL_LANE0 + 3 <= LANES
    featk = key_features(np.arange(seq), True)
    featk[:, SHIFT_LANE0:REL_LANE0 + 3] = 1.0
    featw = np.concatenate([np.zeros((WINDOW, LANES), np.float32), key_features(np.arange(seq), False)])
    featw[:WINDOW, FLAG_LANE] = NEG_INF
    featw[WINDOW:, SHIFT_LANE0:REL_LANE0 + 3] = 1.0
    featc = key_features(CMP_STRIDE * np.arange(n_half) + CMP_BLOCK - 1, False)
    featc[n_cmp:, FLAG_LANE] = NEG_INF

    slopes = (2.0 ** (-8.0 * np.arange(1, heads + 1) / heads)).astype(np.float32).astype(np.float64)
    slopes2 = slopes * LOG2E
    terms = _bf16_terms(slopes2, N_SPLIT)
    rep = heads // NSA_KV_HEADS
    nq = seq // NSA_TILE
    qfeat = np.zeros((NSA_KV_HEADS, nq, 8, LANES), np.float32)
    qrel = np.zeros((NSA_KV_HEADS, rep * NSA_TILE, LANES), np.float32)
    for h in range(heads):
        g, r = divmod(h, rep)
        for i, t in enumerate(terms):
            qfeat[g, :, r, POS_LANE0 + i] = t[h]
            qfeat[g, :, r, POS_LANE0 + N_SPLIT + i] = t[h]
        qfeat[g, :, r, FLAG_LANE] = 1.0
        for i, t in enumerate(_bf16_terms(-slopes2[h] * NSA_TILE * np.arange(nq), N_SPLIT)):
            qfeat[g, :, r, SHIFT_LANE0 + i] = t
        for i, t in enumerate(_bf16_terms(-slopes2[h] * np.arange(NSA_TILE), 3)):
            qrel[g, r * NSA_TILE:(r + 1) * NSA_TILE, REL_LANE0 + i] = t
    seg = np.zeros((2 * LANES, LANES), np.float32)
    seg[np.arange(2 * LANES), np.arange(2 * LANES) // NSA_HEAD_DIM] = 1.0

    cs = CMP_STRIDE * np.arange(n_half)[None, :]
    ss = SEL_BLOCK * np.arange(LANES)[:, None]
    overlap = np.clip(np.minimum(cs + CMP_BLOCK, ss + SEL_BLOCK) - np.maximum(cs, ss), 0, None)
    mmap = (overlap / CMP_BLOCK) * (np.arange(n_half)[None, :] < n_cmp) * (ss < seq)
    place = np.zeros((MAX_SEL_BLOCKS, LANES), np.float32)
    place[np.arange(MAX_SEL_BLOCKS), SEL_LANE0 + np.arange(MAX_SEL_BLOCKS)] = 1.0
    bf = lambda a: jnp.asarray(a, dtype=BF16)
    return bf(featk), bf(featw), bf(featc), jnp.asarray(qfeat), bf(qrel), bf(seg), bf(mmap), bf(place)


def _hgrn_kernel(q_ref, f_ref, v_ref, z_ref, lb_ref, norm_ref, cum_ref, o_ref, *, group):
    C, SUB = HGRN_CHUNK, HGRN_SUB
    n_sub = C // SUB
    seq, dk = q_ref.shape

    lbr = lb_ref[...]
    e = jnp.exp(lbr - jnp.max(lbr, axis=0, keepdims=True))
    lb = e[0:1, :] / jnp.sum(e, axis=0, keepdims=True)
    gain = norm_ref[...]
    cum = cum_ref[...]
    sub = lambda x, i: x[i * SUB:(i + 1) * SUB]
    masks = []
    for i in range(n_sub):
        width = (i + 1) * SUB
        masks.append(lax.broadcasted_iota(jnp.int32, (SUB, width), 1)
                     <= lax.broadcasted_iota(jnp.int32, (SUB, width), 0) + i * SUB)

    def block(cb, st):
        chunks = range(group)
        rows = [pl.ds(pl.multiple_of((cb * group + j) * C, C), C) for j in chunks]
        v16 = [v_ref[rows[j], :].astype(BF16) for j in chunks]
        f = [lb + (1.0 - lb) * jax.nn.sigmoid(f_ref[rows[j], :]) for j in chunks]
        k = [1.0 - f[j] for j in chunks]
        parts = [_split3(jnp.log(f[j])) for j in chunks]
        a = [(jnp.dot(cum, parts[j][0], preferred_element_type=F32)
              + jnp.dot(cum, parts[j][1], preferred_element_type=F32)
              + jnp.dot(cum, parts[j][2], preferred_element_type=F32)) for j in chunks]
        q1, k1, k2, qb, k3, start, dec = [], [], [], [], [], [], []
        for j in chunks:
            q1.append(q_ref[rows[j], :] * jnp.exp(a[j]))
            k1.append(k[j] * jnp.exp(-a[j]))
            tot = [a[j][(i + 1) * SUB - 1:(i + 1) * SUB] for i in range(n_sub)]
            s = [jnp.zeros_like(tot[0])]
            for i in range(n_sub):
                s.append(s[i] + tot[i])
            start.append(s)
            dec.append(jnp.exp(s[n_sub]))
            qb.append(jnp.concatenate([sub(q1[j], i) * jnp.exp(s[i]) for i in range(n_sub)], axis=0)
                      .astype(BF16))
            k2.append([sub(k1[j], i) * jnp.exp(tot[i]) for i in range(n_sub)])
            k3.append(jnp.concatenate([sub(k1[j], i) * jnp.exp(s[n_sub] - s[i]) for i in range(n_sub)],
                                      axis=0).astype(BF16))
        att = []
        for j in chunks:
            row_blocks = []
            for i in range(n_sub):
                rhs = [k2[j][jb] * jnp.exp(start[j][i] - start[j][jb + 1]) if jb < i - 1 else k2[j][jb]
                       for jb in range(i)]
                rhs.append(sub(k1[j], i))
                rhs = jnp.concatenate(rhs, axis=0) if len(rhs) > 1 else rhs[0]
                sc = _nt_dot(sub(q1[j], i).astype(BF16), rhs.astype(BF16))
                row_blocks.append(jnp.where(masks[i], sc, 0.0).astype(BF16))
            att.append(row_blocks)
        intra = [jnp.concatenate([jnp.dot(att[j][i], v16[j][0:(i + 1) * SUB], preferred_element_type=F32)
                                  for i in range(n_sub)], axis=0) for j in chunks]
        incr = [_tn_dot(v16[j], k3[j]) for j in chunks]
        inter = []
        for j in chunks:
            inter.append(_nt_dot(qb[j], st.astype(BF16)))
            st = st * dec[j] + incr[j]
        for j in chunks:
            o = inter[j] + intra[j]
            y = o * lax.rsqrt(jnp.mean(o * o, axis=-1, keepdims=True) + EPS) * gain
            z = z_ref[rows[j], :]
            o_ref[rows[j], :] = (y * (z * jax.nn.sigmoid(z))).astype(o_ref.dtype)
        return st

    lax.fori_loop(0, seq // (C * group), block, jnp.zeros((dk, dk), F32))


def _hgrn_patterns():
    C, SUB = HGRN_CHUNK, HGRN_SUB
    t = np.arange(C)[:, None]
    s = np.arange(C)[None, :]
    return (((t // SUB) == (s // SUB)) & (s <= t)).astype(np.float32)


def _hgrn(streams, lower_bounds, norm, *, batch, seq):
    dk = HGRN_HEAD_DIM
    heads = norm.shape[1] // dk
    cum = jnp.asarray(_hgrn_patterns(), dtype=BF16)
    col = lambda name: (lambda b, h, o=streams[name][1] // dk: (b, o + h))
    return pl.pallas_call(
        functools.partial(_hgrn_kernel, group=8),
        grid=(batch, heads),
        in_specs=[
            pl.BlockSpec((seq, dk), col("q_h")),
            pl.BlockSpec((seq, dk), col("f_h")),
            pl.BlockSpec((seq, dk), col("i_h")),
            pl.BlockSpec((seq, dk), col("z_h")),
            pl.BlockSpec((lower_bounds.shape[0], dk), lambda b, h: (0, h)),
            pl.BlockSpec((1, dk), lambda b, h: (0, h)),
            pl.BlockSpec(cum.shape, lambda b, h: (0, 0)),
        ],
        out_specs=pl.BlockSpec((seq, dk), lambda b, h: (b, h)),
        out_shape=jax.ShapeDtypeStruct((batch * seq, heads * dk), BF16),
        compiler_params=pltpu.CompilerParams(
            dimension_semantics=("arbitrary", "arbitrary"),
            vmem_limit_bytes=VMEM_LIMIT),
        name="hgrn",
    )(streams["q_h"][0], streams["f_h"][0], streams["i_h"][0], streams["z_h"][0], lower_bounds, norm, cum)


def _outproj_kernel(x_ref, oa_ref, oh_ref, wa_ref, wh_ref, g_ref, o_ref):
    y = x_ref[...] + jnp.dot(oa_ref[...], wa_ref[...], preferred_element_type=F32)
    y = y + jnp.dot(oh_ref[...], wh_ref[...], preferred_element_type=F32)
    ms = jnp.mean(y * y, axis=-1, keepdims=True)
    o_ref[...] = y * lax.rsqrt(ms + EPS) * g_ref[...]


def _outproj(x2, o_a, o_h, wa, wh, g, *, tm):
    M, D = x2.shape
    const = lambda a: pl.BlockSpec(a.shape, lambda i: (0,) * a.ndim)
    return pl.pallas_call(
        _outproj_kernel,
        grid=(M // tm,),
        in_specs=[
            pl.BlockSpec((tm, D), lambda i: (i, 0)),
            pl.BlockSpec((tm, o_a.shape[1]), lambda i: (i, 0)),
            pl.BlockSpec((tm, o_h.shape[1]), lambda i: (i, 0)),
            const(wa), const(wh), const(g),
        ],
        out_specs=pl.BlockSpec((tm, D), lambda i: (i, 0)),
        out_shape=jax.ShapeDtypeStruct((M, D), F32),
        compiler_params=pltpu.CompilerParams(
            dimension_semantics=("arbitrary",),
            vmem_limit_bytes=VMEM_LIMIT),
        name="outproj",
    )(x2, o_a, o_h, wa, wh, g)


def kernel(x, norm_in, w_in, cmp_pe_k, cmp_w1_k, cmp_w2_k, cmp_pe_v, cmp_w1_v, cmp_w2_v,
           lower_bounds, nsa_out_norm, hgrn_out_norm, w_out, final_norm):
    B, S, D = x.shape
    assert norm_in.shape[0] == 1, "single-layer problem"
    nsa_w = nsa_out_norm.shape[1]
    hgrn_w = hgrn_out_norm.shape[1]
    dh = NSA_HEAD_DIM
    G = NSA_KV_HEADS
    heads = nsa_w // dh
    kvw = G * dh
    n_gate = 3 * heads
    n_cmp = (S - CMP_BLOCK) // CMP_STRIDE + 1
    n_half = S // CMP_STRIDE
    assert S % NSA_TILE == 0 and WINDOW % NSA_TILE == 0 and n_half <= LANES and 2 * dh == LANES

    names = ["q_a", "k_cmp", "v_cmp", "k_slc", "v_slc", "k_win", "v_win", "gate", "z_a",
             "q_h", "f_h", "i_h", "z_h"]
    widths = [nsa_w] + [kvw] * 6 + [n_gate, nsa_w] + [hgrn_w] * 4
    starts = dict(zip(names, np.cumsum([0] + widths[:-1]).tolist()))
    wd = dict(zip(names, widths))
    w = w_in[0].astype(BF16)
    piece = lambda n: w[:, starts[n]:starts[n] + wd[n]]

    def paired(kname, vname):
        k = piece(kname).reshape(D, G, dh)
        v = piece(vname).reshape(D, G, dh)
        return jnp.concatenate([k, v], axis=-1).reshape(D, G * 2 * dh)

    blocks16 = [("q_a", piece("q_a") * (dh ** -0.5)),
                ("cmp", paired("k_cmp", "v_cmp")), ("slc", paired("k_slc", "v_slc")),
                ("win", paired("k_win", "v_win")), ("q_h", piece("q_h")), ("i_h", piece("i_h"))]
    blocks32 = [("z_a", piece("z_a")), ("f_h", piece("f_h")), ("z_h", piece("z_h"))]
    n16_cols = sum(blk.shape[1] for _, blk in blocks16)
    w_main = jnp.concatenate([blk for _, blk in blocks16 + blocks32], axis=1)
    w_gate = jnp.pad(piece("gate"), ((0, 0), (0, LANES - n_gate)))

    x2 = x.reshape(B * S, D)
    proj16, proj32, gate = _proj(x2, norm_in, w_main, w_gate, tm=min(1024, B * S), tn=1536,
                                 n16_cols=n16_cols)
    streams = {}
    for arr, blocks in ((proj16, blocks16), (proj32, blocks32)):
        off = 0
        for n, blk in blocks:
            streams[n] = (arr, off)
            off += blk.shape[1]

    featk, featw, featc, qfeat, qrel, seg, mmap, place = _nsa_tables(S, n_half, n_cmp, heads)

    def w1_halves(w1k, w1v):
        hk = w1k.shape[1]
        k3 = w1k.reshape(2, CMP_STRIDE, dh, hk)
        v3 = w1v.reshape(2, CMP_STRIDE, dh, hk)
        zk = jnp.zeros_like(k3[0])
        top = lambda a: jnp.concatenate([a, zk], axis=-1)
        bot = lambda a: jnp.concatenate([zk, a], axis=-1)
        half = lambda i: jnp.concatenate([top(k3[i]), bot(v3[i])], axis=1).reshape(CMP_STRIDE * 2 * dh, 2 * hk)
        return half(0).astype(BF16), half(1).astype(BF16)

    wp, wq = w1_halves(cmp_w1_k[0], cmp_w1_v[0])
    zk = jnp.zeros_like(cmp_w2_k[0])
    w2 = jnp.concatenate([jnp.concatenate([cmp_w2_k[0], zk], axis=1),
                          jnp.concatenate([zk, cmp_w2_v[0]], axis=1)], axis=0).astype(BF16)
    pe = jnp.concatenate([cmp_pe_k[0].reshape(2, CMP_STRIDE, dh), cmp_pe_v[0].reshape(2, CMP_STRIDE, dh)],
                         axis=-1).reshape(2, CMP_STRIDE * 2 * dh)
    kc, vc = _compress(streams["cmp"][0], pe, wp, wq, w2, jnp.tile(featc, (G, 1)),
                       batch=B, seq=S, col0=streams["cmp"][1], n_half=n_half)

    o_a = _nsa(streams, gate, kc, vc, featk, featw, qfeat, qrel, seg, mmap, place, nsa_out_norm,
               batch=B, seq=S, n_cmp=n_cmp)
    o_h = _hgrn(streams, lower_bounds, hgrn_out_norm, batch=B, seq=S)

    wo = w_out[0].astype(BF16)
    out = _outproj(x2, o_a, o_h, wo[:nsa_w], wo[nsa_w:], final_norm.reshape(1, D), tm=512)
    return out.reshape(B, S, D)
```

```python
import functools

import ml_dtypes
import numpy as np
import jax
import jax.numpy as jnp
from jax import lax
from jax.experimental import pallas as pl
from jax.experimental.pallas import tpu as pltpu

F32 = jnp.float32
BF16 = jnp.bfloat16

EPS = 1e-6
NEG_INF = -1e30
LOG2E = 1.4426950408889634

NSA_HEAD_DIM = 64
NSA_KV_HEADS = 4
CMP_BLOCK = 32
CMP_STRIDE = 16
SEL_BLOCK = 64
SEL_TOP = 8
SEL_BONUS = 1.0e4
WINDOW = 512
HGRN_HEAD_DIM = 128
HGRN_CHUNK = 64
HGRN_SUB = 16

LANES = 128
VMEM_LIMIT = 56 * 1024 * 1024
NSA_TILE = 256

SEL_LANE0 = NSA_HEAD_DIM
MAX_SEL_BLOCKS = 32
POS_LANE0 = SEL_LANE0 + MAX_SEL_BLOCKS
N_SPLIT = 4
FLAG_LANE = POS_LANE0 + 2 * N_SPLIT
SHIFT_LANE0 = FLAG_LANE + 1
REL_LANE0 = SHIFT_LANE0 + N_SPLIT
SCORE_BOUND = 100.0
SAFETY = 1.02


def _nt_dot(a, b):
    return lax.dot_general(a, b, (((1,), (1,)), ((), ())), preferred_element_type=F32)


def _tn_dot(a, b):
    return lax.dot_general(a, b, (((0,), (0,)), ((), ())), preferred_element_type=F32)


def _split3(x):
    hi = x.astype(BF16)
    r1 = x - hi.astype(F32)
    mid = r1.astype(BF16)
    lo = (r1 - mid.astype(F32)).astype(BF16)
    return hi, mid, lo


def _proj_kernel(x_ref, g_ref, w_ref, wg_ref, o16_ref, o32_ref, og_ref, h_ref, *, row_chunk, n16):
    j = pl.program_id(1)

    @pl.when(j == 0)
    def _():
        n_chunks = x_ref.shape[0] // row_chunk

        def body(c, carry):
            rows = pl.ds(pl.multiple_of(c * row_chunk, row_chunk), row_chunk)
            x = x_ref[rows, :]
            ms = jnp.mean(x * x, axis=-1, keepdims=True)
            h_ref[rows, :] = (x * lax.rsqrt(ms + EPS) * g_ref[...]).astype(BF16)
            return carry

        lax.fori_loop(0, n_chunks, body, 0)
        og_ref[...] = jnp.dot(h_ref[...], wg_ref[...], preferred_element_type=F32)

    @pl.when(j < n16)
    def _():
        o16_ref[...] = jnp.dot(h_ref[...], w_ref[...], preferred_element_type=F32).astype(BF16)

    @pl.when(j >= n16)
    def _():
        o32_ref[...] = jnp.dot(h_ref[...], w_ref[...], preferred_element_type=F32)


def _proj(x2, g, w, wg, *, tm, tn, n16_cols):
    M, D = x2.shape
    N = w.shape[1]
    NG = wg.shape[1]
    n16 = n16_cols // tn
    assert n16 * tn == n16_cols and N % tn == 0
    return pl.pallas_call(
        functools.partial(_proj_kernel, row_chunk=128, n16=n16),
        grid=(M // tm, N // tn),
        in_specs=[
            pl.BlockSpec((tm, D), lambda i, j: (i, 0)),
            pl.BlockSpec((1, D), lambda i, j: (0, 0)),
            pl.BlockSpec((D, tn), lambda i, j: (0, j)),
            pl.BlockSpec((D, NG), lambda i, j: (0, 0)),
        ],
        out_specs=[
            pl.BlockSpec((tm, tn), lambda i, j: (i, jnp.minimum(j, n16 - 1))),
            pl.BlockSpec((tm, tn), lambda i, j: (i, jnp.maximum(j - n16, 0))),
            pl.BlockSpec((tm, NG), lambda i, j: (i, 0)),
        ],
        out_shape=[
            jax.ShapeDtypeStruct((M, n16_cols), BF16),
            jax.ShapeDtypeStruct((M, N - n16_cols), F32),
            jax.ShapeDtypeStruct((M, NG), F32),
        ],
        scratch_shapes=[pltpu.VMEM((tm, D), BF16)],
        compiler_params=pltpu.CompilerParams(
            dimension_semantics=("arbitrary", "arbitrary"),
            vmem_limit_bytes=VMEM_LIMIT),
        name="proj",
    )(x2, g, w, wg)


def _compress_kernel(c0_ref, c1_ref, c2_ref, c3_ref, pe_ref, wp_ref, wq_ref, w2_ref, feat_ref,
                     kc_ref, vc_ref, x_ref, c32_ref, *, n_half):
    for g, c_ref in enumerate((c0_ref, c1_ref, c2_ref, c3_ref)):
        c32_ref[...] = c_ref[...].astype(F32)
        for l in range(CMP_STRIDE):
            x_ref[g * n_half:(g + 1) * n_half, l * LANES:(l + 1) * LANES] = (
                c32_ref[pl.ds(l, n_half, stride=CMP_STRIDE), :])
    x = x_ref[...]
    rows = x.shape[0]
    first = jnp.dot((x + pe_ref[0:1, :]).astype(BF16), wp_ref[...], preferred_element_type=F32)
    second = jnp.dot((x + pe_ref[1:2, :]).astype(BF16), wq_ref[...], preferred_element_type=F32)
    hidden = first + pltpu.roll(second, rows - 1, 0)
    out = jnp.dot(jax.nn.gelu(hidden).astype(BF16), w2_ref[...], preferred_element_type=F32)
    lane = lax.broadcasted_iota(jnp.int32, out.shape, 1)
    dh = NSA_HEAD_DIM
    kc_ref[...] = jnp.where(lane < dh, out.astype(BF16), feat_ref[...])
    for g in range(NSA_KV_HEADS):
        vc_ref[g * dh:(g + 1) * dh, :] = out[g * n_half:(g + 1) * n_half, :].T[dh:, :].astype(BF16)


def _compress(proj, pe, wp, wq, w2, feat, *, batch, seq, col0, n_half):
    rows = NSA_KV_HEADS * n_half
    rows_t = NSA_KV_HEADS * NSA_HEAD_DIM
    const = lambda a: pl.BlockSpec(a.shape, lambda b: (0,) * a.ndim)
    stream = lambda g: pl.BlockSpec((seq, LANES), lambda b, o=col0 // LANES + g: (b, o))
    return pl.pallas_call(
        functools.partial(_compress_kernel, n_half=n_half),
        grid=(batch,),
        in_specs=[stream(0), stream(1), stream(2), stream(3),
                  const(pe), const(wp), const(wq), const(w2), const(feat)],
        out_specs=[
            pl.BlockSpec((rows, LANES), lambda b: (b, 0)),
            pl.BlockSpec((rows_t, n_half), lambda b: (b, 0)),
        ],
        out_shape=[
            jax.ShapeDtypeStruct((batch * rows, LANES), BF16),
            jax.ShapeDtypeStruct((batch * rows_t, n_half), BF16),
        ],
        scratch_shapes=[pltpu.VMEM((rows, CMP_STRIDE * LANES), F32), pltpu.VMEM((seq, LANES), F32)],
        compiler_params=pltpu.CompilerParams(
            dimension_semantics=("arbitrary",),
            vmem_limit_bytes=VMEM_LIMIT),
        name="compress",
    )(proj, proj, proj, proj, pe, wp, wq, w2, feat)


def _nsa_kernel(q_ref, qall_ref, z_ref, gate_ref, slc_ref, win_ref, kc_ref, vct_ref, featk_ref, featw_ref,
                qfeat_ref, qrel_ref, seg_ref, mmap_ref, place_ref, norm_ref, o_ref,
                ksel_ref, vselt_ref, kwin_ref, vwint_ref, gt_ref, bounded_ref, *, n_cmp):
    g = pl.program_id(1)
    qi = pl.program_id(2)
    tq = q_ref.shape[0]
    tk = tq
    seq = slc_ref.shape[0]
    dh = NSA_HEAD_DIM
    rep = q_ref.shape[1] // dh
    n_half = kc_ref.shape[0]
    n_pad = WINDOW // tk

    @pl.when(qi == 0)
    def _():
        lane = lax.broadcasted_iota(jnp.int32, (seq, LANES), 1)

        def max_sq_norm(x, width):
            s = jnp.dot((x * x).astype(BF16), seg_ref[0:width, :], preferred_element_type=F32)
            return jnp.max(s) * SAFETY

        kv = slc_ref[...]
        k2_sel = max_sq_norm(jnp.where(lane < dh, kv.astype(F32), 0.0), LANES)
        ksel_ref[...] = jnp.where(lane < dh, kv, featk_ref[...])
        vt = kv.astype(F32).T[dh:, :].astype(BF16)
        for kt in range(seq // tk):
            vselt_ref[kt] = vt[:, kt * tk:(kt + 1) * tk]
        kv = win_ref[...]
        k2_win = max_sq_norm(jnp.where(lane < dh, kv.astype(F32), 0.0), LANES)
        kwin_ref[0:WINDOW, :] = featw_ref[0:WINDOW, :]
        kwin_ref[WINDOW:, :] = jnp.where(lane < dh, kv, featw_ref[WINDOW:, :])
        vt = kv.astype(F32).T[dh:, :].astype(BF16)
        for kt in range(n_pad):
            vwint_ref[kt] = jnp.zeros((dh, tk), BF16)
        for kt in range(seq // tk):
            vwint_ref[n_pad + kt] = vt[:, kt * tk:(kt + 1) * tk]
        q2 = max_sq_norm(qall_ref[...].astype(F32), rep * dh) * (LOG2E * LOG2E)
        limit = SCORE_BOUND * SCORE_BOUND
        bounded_ref[0] = ((q2 * k2_sel <= limit) & (q2 * k2_win <= limit)).astype(jnp.int32)

    heads = lambda a: jnp.concatenate([a] * rep, axis=1)
    sub8 = lambda a: a.reshape(a.shape[0] // 8, 8, a.shape[1])

    def softmax_pv(lgs, v_tiles, bounded):
        if bounded:
            m = 0.0
        else:
            m8 = functools.reduce(jnp.maximum, [jnp.max(sub8(lg), axis=0) for lg in lgs])
            m = jnp.max(m8, axis=0, keepdims=True)
        l8 = jnp.zeros((8, rep * tq), F32)
        acc = jnp.zeros((dh, rep * tq), F32)
        for lg, vt in zip(lgs, v_tiles):
            pt = jnp.exp2(lg) if bounded else jnp.exp2(lg - m)
            l8 = l8 + jnp.sum(sub8(pt), axis=0)
            acc = acc + jnp.dot(vt, pt.astype(BF16), preferred_element_type=F32)
        return acc / jnp.sum(l8, axis=0, keepdims=True)

    def step(n, bounded):
        static = isinstance(n, int)
        assert static or not bounded
        t0 = n * tq
        lane = lax.broadcasted_iota(jnp.int32, (tq, LANES), 1)
        key_i = lax.broadcasted_iota(jnp.int32, (tk, tq), 0)
        qry_i = lax.broadcasted_iota(jnp.int32, (tk, tq), 1)
        causal_1 = jnp.where(key_i <= qry_i, 0.0, NEG_INF)
        causal_bias = heads(causal_1)
        band_bias = heads(jnp.where(key_i > qry_i, 0.0, NEG_INF))

        qfeat = qfeat_ref[0, n]
        rel_lanes = (lane >= REL_LANE0) & (lane < REL_LANE0 + 3)
        qa = []
        for r in range(rep):
            qcol = q_ref[:, (r // 2) * LANES:(r // 2 + 1) * LANES].astype(F32)
            if r % 2:
                qcol = pltpu.roll(qcol, dh, 1)
            feat = jnp.where(rel_lanes, qrel_ref[0, r * tq:(r + 1) * tq, :].astype(F32), qfeat[r:r + 1, :])
            qa.append(jnp.where(lane < dh, qcol * LOG2E, feat))
        q0 = jnp.concatenate(qa, axis=0).astype(BF16)

        gt_ref[...] = jax.nn.sigmoid(gate_ref[...]).T
        out_scale = []
        for pair in range(rep // 2):
            z = z_ref[:, pair * LANES:(pair + 1) * LANES]
            out_scale.append(norm_ref[:, pair * LANES:(pair + 1) * LANES] * (z * jax.nn.sigmoid(z)))

        nc = min(n_half, (t0 + tq) // CMP_STRIDE) if static else n_half
        n_idx = lax.broadcasted_iota(jnp.int32, (nc, tq), 0)
        t_idx = t0 + lax.broadcasted_iota(jnp.int32, (nc, tq), 1)
        valid_c = heads((CMP_STRIDE * n_idx + (CMP_BLOCK - 1) <= t_idx) & (n_idx < n_cmp))
        lg_c = jnp.where(valid_c, _nt_dot(kc_ref[0:nc, :], q0), NEG_INF)

        n_wt = n_pad + 1
        lgs = []
        for w in range(n_wt):
            rows = (slice(t0 + w * tk, t0 + (w + 1) * tk) if static
                    else pl.ds(pl.multiple_of(t0 + w * tk, tk), tk))
            lg = _nt_dot(kwin_ref[rows, :], q0)
            if w == 0:
                lg = lg + band_bias
            if w == n_wt - 1:
                lg = lg + causal_bias
            lgs.append(lg)
        o_win = softmax_pv(lgs, [vwint_ref[n + w] for w in range(n_wt)], bounded)

        e = jnp.exp2(lg_c - jnp.max(lg_c, axis=0, keepdims=True))
        p = jnp.where(valid_c, e / jnp.sum(e, axis=0, keepdims=True), 0.0)
        if nc < n_half:
            p = jnp.concatenate([p, jnp.zeros((n_half - nc, rep * tq), F32)], axis=0)
        o_cmp = jnp.dot(vct_ref[...], p.astype(BF16), preferred_element_type=F32)
        p_sum = p[:, 0:tq]
        for r in range(1, rep):
            p_sum = p_sum + p[:, r * tq:(r + 1) * tq]

        nb = min(MAX_SEL_BLOCKS, (t0 + tq) // SEL_BLOCK) if static else MAX_SEL_BLOCKS
        mm = mmap_ref[...]
        hi, mid, lo = _split3(p_sum)
        p_slc = (jnp.dot(mm, hi, preferred_element_type=F32) + jnp.dot(mm, mid, preferred_element_type=F32)
                 + jnp.dot(mm, lo, preferred_element_type=F32))[0:MAX_SEL_BLOCKS]
        if static:
            raw = [_nt_dot(ksel_ref[kt * tk:(kt + 1) * tk, :], q0) for kt in range(n + 1)]
        jj = lax.broadcasted_iota(jnp.int32, (MAX_SEL_BLOCKS, tq), 0)
        tt = t0 + lax.broadcasted_iota(jnp.int32, (MAX_SEL_BLOCKS, tq), 1)
        cur = lax.shift_right_logical(tt, int(np.log2(SEL_BLOCK)))
        forced = (jj == 0) | (jj == cur) | (jj == cur - 1)
        future = jj > cur
        score = jnp.where(future, -1.0, p_slc + jnp.where(forced, SEL_BONUS, 0.0))
        rank = jnp.zeros((MAX_SEL_BLOCKS, tq), jnp.int32)
        for i in range(nb):
            other = jnp.broadcast_to(score[i:i + 1, :], score.shape)
            beats = (other > score) | ((other == score) & (jj > i))
            rank = rank + beats.astype(jnp.int32)
        keep = (rank < SEL_TOP) & jnp.logical_not(future)

        if static:
            blocks_per_tile = tk // SEL_BLOCK
            block_bias = jnp.where(keep, 0.0, NEG_INF)
            lgs = []
            for kt in range(n + 1):
                b = jnp.concatenate(
                    [jnp.broadcast_to(block_bias[blocks_per_tile * kt + j:blocks_per_tile * kt + j + 1, :],
                                      (SEL_BLOCK, tq)) for j in range(blocks_per_tile)], axis=0)
                lgs.append(raw[kt] + heads(b + causal_1 if kt == n else b))
            o_slc = softmax_pv(lgs, [vselt_ref[kt] for kt in range(n + 1)], bounded)
        else:
            chosen = jnp.where(keep, 1.0, 0.0).astype(BF16)
            placed = _tn_dot(chosen, place_ref[...])
            sel_lanes = (lane >= SEL_LANE0) & (lane < SEL_LANE0 + MAX_SEL_BLOCKS)
            sel_bias = (placed - 1.0) * (-NEG_INF)
            qs = jnp.concatenate([jnp.where(sel_lanes, sel_bias, a) for a in qa], axis=0).astype(BF16)

            def sel_tile(kt, carry, bias=None):
                m, l, acc = carry
                lg = _nt_dot(ksel_ref[pl.ds(pl.multiple_of(kt * tk, tk), tk), :], qs)
                if bias is not None:
                    lg = lg + bias
                m_new = jnp.maximum(m, jnp.max(lg, axis=0, keepdims=True))
                alpha = jnp.exp2(m - m_new)
                pt = jnp.exp2(lg - m_new)
                return (m_new, alpha * l + jnp.sum(pt, axis=0, keepdims=True),
                        alpha * acc + jnp.dot(vselt_ref[kt], pt.astype(BF16), preferred_element_type=F32))

            init = (jnp.full((1, rep * tq), NEG_INF, F32), jnp.zeros((1, rep * tq), F32),
                    jnp.zeros((dh, rep * tq), F32))
            _, l, acc = sel_tile(n, lax.fori_loop(0, n, sel_tile, init), causal_bias)
            o_slc = acc / l

        ys = []
        for r in range(rep):
            cols = slice(r * tq, (r + 1) * tq)
            gate = lambda c: gt_ref[pl.ds(3 * (g * rep + r) + c, 1), :]
            o = gate(0) * o_cmp[:, cols] + gate(1) * o_slc[:, cols] + gate(2) * o_win[:, cols]
            ys.append(o * lax.rsqrt(jnp.mean(o * o, axis=0, keepdims=True) + EPS))
        for pair in range(rep // 2):
            cols = slice(pair * LANES, (pair + 1) * LANES)
            y = jnp.concatenate([ys[2 * pair], ys[2 * pair + 1]], axis=0).T
            o_ref[:, cols] = (y * out_scale[pair]).astype(o_ref.dtype)

    @pl.when(bounded_ref[0] == 1)
    def _():
        lax.switch(qi, [functools.partial(step, n, True) for n in range(seq // tq)])

    @pl.when(bounded_ref[0] == 0)
    def _():
        step(qi, False)


def _nsa(streams, gate, kc, vc, featk, featw, qfeat, qrel, seg, mmap, place, norm, *, batch, seq, n_cmp):
    tq = NSA_TILE
    nq = seq // tq
    G = NSA_KV_HEADS
    gw = norm.shape[1] // G
    n_half = kc.shape[0] // (batch * G)
    const = lambda a: pl.BlockSpec(a.shape, lambda b, g, i: (0,) * a.ndim)
    stream = lambda name: pl.BlockSpec((seq, LANES),
                                       lambda b, g, i, o=streams[name][1] // LANES: (b, o + g))
    tile = lambda name: pl.BlockSpec((tq, gw), lambda b, g, i, o=streams[name][1] // gw: (b * nq + i, o + g))
    return pl.pallas_call(
        functools.partial(_nsa_kernel, n_cmp=n_cmp),
        grid=(batch, G, nq),
        in_specs=[
            tile("q_a"),
            pl.BlockSpec((seq, gw), lambda b, g, i, o=streams["q_a"][1] // gw: (b, o + g)),
            tile("z_a"),
            pl.BlockSpec((tq, gate.shape[1]), lambda b, g, i: (b * nq + i, 0)),
            stream("slc"), stream("win"),
            pl.BlockSpec((n_half, LANES), lambda b, g, i: (b * G + g, 0)),
            pl.BlockSpec((NSA_HEAD_DIM, n_half), lambda b, g, i: (b * G + g, 0)),
            const(featk), const(featw),
            pl.BlockSpec((1,) + qfeat.shape[1:], lambda b, g, i: (g, 0, 0, 0)),
            pl.BlockSpec((1,) + qrel.shape[1:], lambda b, g, i: (g, 0, 0)),
            const(seg), const(mmap), const(place),
            pl.BlockSpec((1, gw), lambda b, g, i: (0, g)),
        ],
        out_specs=pl.BlockSpec((tq, gw), lambda b, g, i: (b * nq + i, g)),
        out_shape=jax.ShapeDtypeStruct((batch * seq, G * gw), BF16),
        scratch_shapes=[
            pltpu.VMEM((seq, LANES), BF16),
            pltpu.VMEM((seq // tq, NSA_HEAD_DIM, tq), BF16),
            pltpu.VMEM((seq + WINDOW, LANES), BF16),
            pltpu.VMEM(((seq + WINDOW) // tq, NSA_HEAD_DIM, tq), BF16),
            pltpu.VMEM((gate.shape[1], tq), F32),
            pltpu.SMEM((1,), jnp.int32),
        ],
        compiler_params=pltpu.CompilerParams(
            dimension_semantics=("arbitrary", "arbitrary", "arbitrary"),
            vmem_limit_bytes=VMEM_LIMIT),
        name="nsa",
    )(streams["q_a"][0], streams["q_a"][0], streams["z_a"][0], gate, streams["slc"][0], streams["win"][0],
      kc, vc, featk, featw, qfeat, qrel, seg, mmap, place, norm)


def _bf16_terms(x, n):
    terms, rest = [], np.asarray(x, np.float64)
    for _ in range(n):
        t = rest.astype(np.float32).astype(ml_dtypes.bfloat16).astype(np.float64)
        terms.append(t.astype(np.float32))
        rest = rest - t
    return terms


def _nsa_tables(seq, n_half, n_cmp, heads):
    assert seq // SEL_BLOCK <= MAX_SEL_BLOCKS and FLAG_LANE < LANES

    def key_features(pos, onehot_blocks):
        f = np.zeros((len(pos), LANES), np.float32)
        if onehot_blocks:
            f[np.arange(len(pos)), SEL_LANE0 + pos // SEL_BLOCK] = 1.0
        f[:, POS_LANE0:POS_LANE0 + N_SPLIT] = ((pos // 64) * 64)[:, None]
        f[:, POS_LANE0 + N_SPLIT:POS_LANE0 + 2 * N_SPLIT] = (pos % 64)[:, None]
        return f

    assert REL_LANE0 + 3 <= LANES
    featk = key_features(np.arange(seq), True)
    featk[:, SHIFT_LANE0:REL_LANE0 + 3] = 1.0
    featw = np.concatenate([np.zeros((WINDOW, LANES), np.float32), key_features(np.arange(seq), False)])
    featw[:WINDOW, FLAG_LANE] = NEG_INF
    featw[WINDOW:, SHIFT_LANE0:REL_LANE0 + 3] = 1.0
    featc = key_features(CMP_STRIDE * np.arange(n_half) + CMP_BLOCK - 1, False)
    featc[n_cmp:, FLAG_LANE] = NEG_INF

    slopes = (2.0 ** (-8.0 * np.arange(1, heads + 1) / heads)).astype(np.float32).astype(np.float64)
    slopes2 = slopes * LOG2E
    terms = _bf16_terms(slopes2, N_SPLIT)
    rep = heads // NSA_KV_HEADS
    nq = seq // NSA_TILE
    qfeat = np.zeros((NSA_KV_HEADS, nq, 8, LANES), np.float32)
    qrel = np.zeros((NSA_KV_HEADS, rep * NSA_TILE, LANES), np.float32)
    for h in range(heads):
        g, r = divmod(h, rep)
        for i, t in enumerate(terms):
            qfeat[g, :, r, POS_LANE0 + i] = t[h]
            qfeat[g, :, r, POS_LANE0 + N_SPLIT + i] = t[h]
        qfeat[g, :, r, FLAG_LANE] = 1.0
        for i, t in enumerate(_bf16_terms(-slopes2[h] * NSA_TILE * np.arange(nq), N_SPLIT)):
            qfeat[g, :, r, SHIFT_LANE0 + i] = t
        for i, t in enumerate(_bf16_terms(-slopes2[h] * np.arange(NSA_TILE), 3)):
            qrel[g, r * NSA_TILE:(r + 1) * NSA_TILE, REL_LANE0 + i] = t
    seg = np.zeros((2 * LANES, LANES), np.float32)
    seg[np.arange(2 * LANES), np.arange(2 * LANES) // NSA_HEAD_DIM] = 1.0

    cs = CMP_STRIDE * np.arange(n_half)[None, :]
    ss = SEL_BLOCK * np.arange(LANES)[:, None]
    overlap = np.clip(np.minimum(cs + CMP_BLOCK, ss + SEL_BLOCK) - np.maximum(cs, ss), 0, None)
    mmap = (overlap / CMP_BLOCK) * (np.arange(n_half)[None, :] < n_cmp) * (ss < seq)
    place = np.zeros((MAX_SEL_BLOCKS, LANES), np.float32)
    place[np.arange(MAX_SEL_BLOCKS), SEL_LANE0 + np.arange(MAX_SEL_BLOCKS)] = 1.0
    bf = lambda a: jnp.asarray(a, dtype=BF16)
    return bf(featk), bf(featw), bf(featc), jnp.asarray(qfeat), bf(qrel), bf(seg), bf(mmap), bf(place)


def _hgrn_kernel(q_ref, f_ref, v_ref, z_ref, lb_ref, norm_ref, cum_ref, o_ref, *, group):
    C, SUB = HGRN_CHUNK, HGRN_SUB
    n_sub = C // SUB
    seq, dk = q_ref.shape

    lbr = lb_ref[...]
    e = jnp.exp(lbr - jnp.max(lbr, axis=0, keepdims=True))
    lb = e[0:1, :] / jnp.sum(e, axis=0, keepdims=True)
    gain = norm_ref[...]
    cum = cum_ref[...]
    sub = lambda x, i: x[i * SUB:(i + 1) * SUB]
    masks = []
    for i in range(n_sub):
        width = (i + 1) * SUB
        masks.append(lax.broadcasted_iota(jnp.int32, (SUB, width), 1)
                     <= lax.broadcasted_iota(jnp.int32, (SUB, width), 0) + i * SUB)

    def block(cb, st):
        chunks = range(group)
        rows = [pl.ds(pl.multiple_of((cb * group + j) * C, C), C) for j in chunks]
        v16 = [v_ref[rows[j], :].astype(BF16) for j in chunks]
        f = [lb + (1.0 - lb) * jax.nn.sigmoid(f_ref[rows[j], :]) for j in chunks]
        k = [1.0 - f[j] for j in chunks]
        parts = [_split3(jnp.log(f[j])) for j in chunks]
        a = [(jnp.dot(cum, parts[j][0], preferred_element_type=F32)
              + jnp.dot(cum, parts[j][1], preferred_element_type=F32)
              + jnp.dot(cum, parts[j][2], preferred_element_type=F32)) for j in chunks]
        q1, k1, k2, qb, k3, start, dec = [], [], [], [], [], [], []
        for j in chunks:
            q1.append(q_ref[rows[j], :] * jnp.exp(a[j]))
            k1.append(k[j] * jnp.exp(-a[j]))
            tot = [a[j][(i + 1) * SUB - 1:(i + 1) * SUB] for i in range(n_sub)]
            s = [jnp.zeros_like(tot[0])]
            for i in range(n_sub):
                s.append(s[i] + tot[i])
            start.append(s)
            dec.append(jnp.exp(s[n_sub]))
            qb.append(jnp.concatenate([sub(q1[j], i) * jnp.exp(s[i]) for i in range(n_sub)], axis=0)
                      .astype(BF16))
            k2.append([sub(k1[j], i) * jnp.exp(tot[i]) for i in range(n_sub)])
            k3.append(jnp.concatenate([sub(k1[j], i) * jnp.exp(s[n_sub] - s[i]) for i in range(n_sub)],
                                      axis=0).astype(BF16))
        att = []
        for j in chunks:
            row_blocks = []
            for i in range(n_sub):
                rhs = [k2[j][jb] * jnp.exp(start[j][i] - start[j][jb + 1]) if jb < i - 1 else k2[j][jb]
                       for jb in range(i)]
                rhs.append(sub(k1[j], i))
                rhs = jnp.concatenate(rhs, axis=0) if len(rhs) > 1 else rhs[0]
                sc = _nt_dot(sub(q1[j], i).astype(BF16), rhs.astype(BF16))
                row_blocks.append(jnp.where(masks[i], sc, 0.0).astype(BF16))
            att.append(row_blocks)
        intra = [jnp.concatenate([jnp.dot(att[j][i], v16[j][0:(i + 1) * SUB], preferred_element_type=F32)
                                  for i in range(n_sub)], axis=0) for j in chunks]
        incr = [_tn_dot(v16[j], k3[j]) for j in chunks]
        inter = []
        for j in chunks:
            inter.append(_nt_dot(qb[j], st.astype(BF16)))
            st = st * dec[j] + incr[j]
        for j in chunks:
            o = inter[j] + intra[j]
            y = o * lax.rsqrt(jnp.mean(o * o, axis=-1, keepdims=True) + EPS) * gain
            z = z_ref[rows[j], :]
            o_ref[rows[j], :] = (y * (z * jax.nn.sigmoid(z))).astype(o_ref.dtype)
        return st

    lax.fori_loop(0, seq // (C * group), block, jnp.zeros((dk, dk), F32))


def _hgrn_patterns():
    C, SUB = HGRN_CHUNK, HGRN_SUB
    t = np.arange(C)[:, None]
    s = np.arange(C)[None, :]
    return (((t // SUB) == (s // SUB)) & (s <= t)).astype(np.float32)


def _hgrn(streams, lower_bounds, norm, *, batch, seq):
    dk = HGRN_HEAD_DIM
    heads = norm.shape[1] // dk
    cum = jnp.asarray(_hgrn_patterns(), dtype=BF16)
    col = lambda name: (lambda b, h, o=streams[name][1] // dk: (b, o + h))
    return pl.pallas_call(
        functools.partial(_hgrn_kernel, group=16),
        grid=(batch, heads),
        in_specs=[
            pl.BlockSpec((seq, dk), col("q_h")),
            pl.BlockSpec((seq, dk), col("f_h")),
            pl.BlockSpec((seq, dk), col("i_h")),
            pl.BlockSpec((seq, dk), col("z_h")),
            pl.BlockSpec((lower_bounds.shape[0], dk), lambda b, h: (0, h)),
            pl.BlockSpec((1, dk), lambda b, h: (0, h)),
            pl.BlockSpec(cum.shape, lambda b, h: (0, 0)),
        ],
        out_specs=pl.BlockSpec((seq, dk), lambda b, h: (b, h)),
        out_shape=jax.ShapeDtypeStruct((batch * seq, heads * dk), BF16),
        compiler_params=pltpu.CompilerParams(
            dimension_semantics=("arbitrary", "arbitrary"),
            vmem_limit_bytes=VMEM_LIMIT),
        name="hgrn",
    )(streams["q_h"][0], streams["f_h"][0], streams["i_h"][0], streams["z_h"][0], lower_bounds, norm, cum)


def _outproj_kernel(x_ref, oa_ref, oh_ref, wa_ref, wh_ref, g_ref, o_ref):
    y = x_ref[...] + jnp.dot(oa_ref[...], wa_ref[...], preferred_element_type=F32)
    y = y + jnp.dot(oh_ref[...], wh_ref[...], preferred_element_type=F32)
    ms = jnp.mean(y * y, axis=-1, keepdims=True)
    o_ref[...] = y * lax.rsqrt(ms + EPS) * g_ref[...]


def _outproj(x2, o_a, o_h, wa, wh, g, *, tm):
    M, D = x2.shape
    const = lambda a: pl.BlockSpec(a.shape, lambda i: (0,) * a.ndim)
    return pl.pallas_call(
        _outproj_kernel,
        grid=(M // tm,),
        in_specs=[
            pl.BlockSpec((tm, D), lambda i: (i, 0)),
            pl.BlockSpec((tm, o_a.shape[1]), lambda i: (i, 0)),
            pl.BlockSpec((tm, o_h.shape[1]), lambda i: (i, 0)),
            const(wa), const(wh), const(g),
        ],
        out_specs=pl.BlockSpec((tm, D), lambda i: (i, 0)),
        out_shape=jax.ShapeDtypeStruct((M, D), F32),
        compiler_params=pltpu.CompilerParams(
            dimension_semantics=("arbitrary",),
            vmem_limit_bytes=VMEM_LIMIT),
        name="outproj",
    )(x2, o_a, o_h, wa, wh, g)


def kernel(x, norm_in, w_in, cmp_pe_k, cmp_w1_k, cmp_w2_k, cmp_pe_v, cmp_w1_v, cmp_w2_v,
           lower_bounds, nsa_out_norm, hgrn_out_norm, w_out, final_norm):
    B, S, D = x.shape
    assert norm_in.shape[0] == 1, "single-layer problem"
    nsa_w = nsa_out_norm.shape[1]
    hgrn_w = hgrn_out_norm.shape[1]
    dh = NSA_HEAD_DIM
    G = NSA_KV_HEADS
    heads = nsa_w // dh
    kvw = G * dh
    n_gate = 3 * heads
    n_cmp = (S - CMP_BLOCK) // CMP_STRIDE + 1
    n_half = S // CMP_STRIDE
    assert S % NSA_TILE == 0 and WINDOW % NSA_TILE == 0 and n_half <= LANES and 2 * dh == LANES

    names = ["q_a", "k_cmp", "v_cmp", "k_slc", "v_slc", "k_win", "v_win", "gate", "z_a",
             "q_h", "f_h", "i_h", "z_h"]
    widths = [nsa_w] + [kvw] * 6 + [n_gate, nsa_w] + [hgrn_w] * 4
    starts = dict(zip(names, np.cumsum([0] + widths[:-1]).tolist()))
    wd = dict(zip(names, widths))
    w = w_in[0].astype(BF16)
    piece = lambda n: w[:, starts[n]:starts[n] + wd[n]]

    def paired(kname, vname):
        k = piece(kname).reshape(D, G, dh)
        v = piece(vname).reshape(D, G, dh)
        return jnp.concatenate([k, v], axis=-1).reshape(D, G * 2 * dh)

    blocks16 = [("q_a", piece("q_a") * (dh ** -0.5)),
                ("cmp", paired("k_cmp", "v_cmp")), ("slc", paired("k_slc", "v_slc")),
                ("win", paired("k_win", "v_win")), ("q_h", piece("q_h")), ("i_h", piece("i_h"))]
    blocks32 = [("z_a", piece("z_a")), ("f_h", piece("f_h")), ("z_h", piece("z_h"))]
    n16_cols = sum(blk.shape[1] for _, blk in blocks16)
    w_main = jnp.concatenate([blk for _, blk in blocks16 + blocks32], axis=1)
    w_gate = jnp.pad(piece("gate"), ((0, 0), (0, LANES - n_gate)))

    x2 = x.reshape(B * S, D)
    proj16, proj32, gate = _proj(x2, norm_in, w_main, w_gate, tm=min(1024, B * S), tn=1536,
                                 n16_cols=n16_cols)
    streams = {}
    for arr, blocks in ((proj16, blocks16), (proj32, blocks32)):
        off = 0
        for n, blk in blocks:
            streams[n] = (arr, off)
            off += blk.shape[1]

    featk, featw, featc, qfeat, qrel, seg, mmap, place = _nsa_tables(S, n_half, n_cmp, heads)

    def w1_halves(w1k, w1v):
        hk = w1k.shape[1]
        k3 = w1k.reshape(2, CMP_STRIDE, dh, hk)
        v3 = w1v.reshape(2, CMP_STRIDE, dh, hk)
        zk = jnp.zeros_like(k3[0])
        top = lambda a: jnp.concatenate([a, zk], axis=-1)
        bot = lambda a: jnp.concatenate([zk, a], axis=-1)
        half = lambda i: jnp.concatenate([top(k3[i]), bot(v3[i])], axis=1).reshape(CMP_STRIDE * 2 * dh, 2 * hk)
        return half(0).astype(BF16), half(1).astype(BF16)

    wp, wq = w1_halves(cmp_w1_k[0], cmp_w1_v[0])
    zk = jnp.zeros_like(cmp_w2_k[0])
    w2 = jnp.concatenate([jnp.concatenate([cmp_w2_k[0], zk], axis=1),
                          jnp.concatenate([zk, cmp_w2_v[0]], axis=1)], axis=0).astype(BF16)
    pe = jnp.concatenate([cmp_pe_k[0].reshape(2, CMP_STRIDE, dh), cmp_pe_v[0].reshape(2, CMP_STRIDE, dh)],
                         axis=-1).reshape(2, CMP_STRIDE * 2 * dh)
    kc, vc = _compress(streams["cmp"][0], pe, wp, wq, w2, jnp.tile(featc, (G, 1)),
                       batch=B, seq=S, col0=streams["cmp"][1], n_half=n_half)

    o_a = _nsa(streams, gate, kc, vc, featk, featw, qfeat, qrel, seg, mmap, place, nsa_out_norm,
               batch=B, seq=S, n_cmp=n_cmp)
    o_h = _hgrn(streams, lower_bounds, hgrn_out_norm, batch=B, seq=S)

    wo = w_out[0].astype(BF16)
    out = _outproj(x2, o_a, o_h, wo[:nsa_w], wo[nsa_w:], final_norm.reshape(1, D), tm=512)
    return out.reshape(B, S, D)
```

```python
import functools

import ml_dtypes
import numpy as np
import jax
import jax.numpy as jnp
from jax import lax
from jax.experimental import pallas as pl
from jax.experimental.pallas import tpu as pltpu

F32 = jnp.float32
BF16 = jnp.bfloat16

EPS = 1e-6
NEG_INF = -1e30
LOG2E = 1.4426950408889634

NSA_HEAD_DIM = 64
NSA_KV_HEADS = 4
CMP_BLOCK = 32
CMP_STRIDE = 16
SEL_BLOCK = 64
SEL_TOP = 8
SEL_BONUS = 1.0e4
WINDOW = 512
HGRN_HEAD_DIM = 128
HGRN_CHUNK = 64
HGRN_SUB = 16

LANES = 128
VMEM_LIMIT = 56 * 1024 * 1024
NSA_TILE = 256

SEL_LANE0 = NSA_HEAD_DIM
MAX_SEL_BLOCKS = 32
POS_LANE0 = SEL_LANE0 + MAX_SEL_BLOCKS
N_SPLIT = 4
FLAG_LANE = POS_LANE0 + 2 * N_SPLIT
SHIFT_LANE0 = FLAG_LANE + 1
REL_LANE0 = SHIFT_LANE0 + N_SPLIT
SCORE_BOUND = 100.0
SAFETY = 1.02


def _nt_dot(a, b):
    return lax.dot_general(a, b, (((1,), (1,)), ((), ())), preferred_element_type=F32)


def _tn_dot(a, b):
    return lax.dot_general(a, b, (((0,), (0,)), ((), ())), preferred_element_type=F32)


def _split3(x):
    hi = x.astype(BF16)
    r1 = x - hi.astype(F32)
    mid = r1.astype(BF16)
    lo = (r1 - mid.astype(F32)).astype(BF16)
    return hi, mid, lo


def _proj_kernel(x_ref, g_ref, w_ref, wg_ref, o16_ref, o32_ref, og_ref, h_ref, *, row_chunk, n16):
    j = pl.program_id(1)

    @pl.when(j == 0)
    def _():
        n_chunks = x_ref.shape[0] // row_chunk

        def body(c, carry):
            rows = pl.ds(pl.multiple_of(c * row_chunk, row_chunk), row_chunk)
            x = x_ref[rows, :]
            ms = jnp.mean(x * x, axis=-1, keepdims=True)
            h_ref[rows, :] = (x * lax.rsqrt(ms + EPS) * g_ref[...]).astype(BF16)
            return carry

        lax.fori_loop(0, n_chunks, body, 0)
        og_ref[...] = jnp.dot(h_ref[...], wg_ref[...], preferred_element_type=F32)

    @pl.when(j < n16)
    def _():
        o16_ref[...] = jnp.dot(h_ref[...], w_ref[...], preferred_element_type=F32).astype(BF16)

    @pl.when(j >= n16)
    def _():
        o32_ref[...] = jnp.dot(h_ref[...], w_ref[...], preferred_element_type=F32)


def _relayout_kernel(w_ref, o_ref, *, plan):
    o_ref[:, o_ref.shape[1] - LANES:] = jnp.zeros((o_ref.shape[0], LANES), o_ref.dtype)
    for dst, src, width, scale in plan:
        o_ref[:, dst:dst + width] = (w_ref[:, src:src + width] * scale).astype(o_ref.dtype)


def _relayout_weights(w, plan, n_out, *, rows=256):
    D, n_in = w.shape
    return pl.pallas_call(
        functools.partial(_relayout_kernel, plan=plan),
        grid=(D // rows,),
        in_specs=[pl.BlockSpec((rows, n_in), lambda i: (i, 0))],
        out_specs=pl.BlockSpec((rows, n_out), lambda i: (i, 0)),
        out_shape=jax.ShapeDtypeStruct((D, n_out), BF16),
        compiler_params=pltpu.CompilerParams(
            dimension_semantics=("arbitrary",),
            vmem_limit_bytes=VMEM_LIMIT),
        name="relayout",
    )(w)


def _proj(x2, g, w, *, tm, tn, n_main, n16_cols):
    M, D = x2.shape
    N = n_main
    NG = w.shape[1] - n_main
    n16 = n16_cols // tn
    assert n16 * tn == n16_cols and N % tn == 0 and N % NG == 0
    return pl.pallas_call(
        functools.partial(_proj_kernel, row_chunk=128, n16=n16),
        grid=(M // tm, N // tn),
        in_specs=[
            pl.BlockSpec((tm, D), lambda i, j: (i, 0)),
            pl.BlockSpec((1, D), lambda i, j: (0, 0)),
            pl.BlockSpec((D, tn), lambda i, j: (0, j)),
            pl.BlockSpec((D, NG), lambda i, j: (0, N // NG)),
        ],
        out_specs=[
            pl.BlockSpec((tm, tn), lambda i, j: (i, jnp.minimum(j, n16 - 1))),
            pl.BlockSpec((tm, tn), lambda i, j: (i, jnp.maximum(j - n16, 0))),
            pl.BlockSpec((tm, NG), lambda i, j: (i, 0)),
        ],
        out_shape=[
            jax.ShapeDtypeStruct((M, n16_cols), BF16),
            jax.ShapeDtypeStruct((M, N - n16_cols), F32),
            jax.ShapeDtypeStruct((M, NG), F32),
        ],
        scratch_shapes=[pltpu.VMEM((tm, D), BF16)],
        compiler_params=pltpu.CompilerParams(
            dimension_semantics=("arbitrary", "arbitrary"),
            vmem_limit_bytes=VMEM_LIMIT),
        name="proj",
    )(x2, g, w, w)


def _compress_kernel(c0_ref, c1_ref, c2_ref, c3_ref, pe_ref, wp_ref, wq_ref, w2_ref, feat_ref,
                     kc_ref, vc_ref, x_ref, c32_ref, *, n_half):
    for g, c_ref in enumerate((c0_ref, c1_ref, c2_ref, c3_ref)):
        c32_ref[...] = c_ref[...].astype(F32)
        for l in range(CMP_STRIDE):
            x_ref[g * n_half:(g + 1) * n_half, l * LANES:(l + 1) * LANES] = (
                c32_ref[pl.ds(l, n_half, stride=CMP_STRIDE), :])
    x = x_ref[...]
    rows = x.shape[0]
    first = jnp.dot((x + pe_ref[0:1, :]).astype(BF16), wp_ref[...], preferred_element_type=F32)
    second = jnp.dot((x + pe_ref[1:2, :]).astype(BF16), wq_ref[...], preferred_element_type=F32)
    hidden = first + pltpu.roll(second, rows - 1, 0)
    out = jnp.dot(jax.nn.gelu(hidden).astype(BF16), w2_ref[...], preferred_element_type=F32)
    lane = lax.broadcasted_iota(jnp.int32, out.shape, 1)
    dh = NSA_HEAD_DIM
    kc_ref[...] = jnp.where(lane < dh, out.astype(BF16), feat_ref[...])
    for g in range(NSA_KV_HEADS):
        vc_ref[g * dh:(g + 1) * dh, :] = out[g * n_half:(g + 1) * n_half, :].T[dh:, :].astype(BF16)


def _compress(proj, pe, wp, wq, w2, feat, *, batch, seq, col0, n_half):
    rows = NSA_KV_HEADS * n_half
    rows_t = NSA_KV_HEADS * NSA_HEAD_DIM
    const = lambda a: pl.BlockSpec(a.shape, lambda b: (0,) * a.ndim)
    stream = lambda g: pl.BlockSpec((seq, LANES), lambda b, o=col0 // LANES + g: (b, o))
    return pl.pallas_call(
        functools.partial(_compress_kernel, n_half=n_half),
        grid=(batch,),
        in_specs=[stream(0), stream(1), stream(2), stream(3),
                  const(pe), const(wp), const(wq), const(w2), const(feat)],
        out_specs=[
            pl.BlockSpec((rows, LANES), lambda b: (b, 0)),
            pl.BlockSpec((rows_t, n_half), lambda b: (b, 0)),
        ],
        out_shape=[
            jax.ShapeDtypeStruct((batch * rows, LANES), BF16),
            jax.ShapeDtypeStruct((batch * rows_t, n_half), BF16),
        ],
        scratch_shapes=[pltpu.VMEM((rows, CMP_STRIDE * LANES), F32), pltpu.VMEM((seq, LANES), F32)],
        compiler_params=pltpu.CompilerParams(
            dimension_semantics=("arbitrary",),
            vmem_limit_bytes=VMEM_LIMIT),
        name="compress",
    )(proj, proj, proj, proj, pe, wp, wq, w2, feat)


def _nsa_kernel(q_ref, qall_ref, z_ref, gate_ref, slc_ref, win_ref, kc_ref, vct_ref, featk_ref, featw_ref,
                qfeat_ref, qrel_ref, seg_ref, mmap_ref, place_ref, norm_ref, o_ref,
                ksel_ref, vselt_ref, kwin_ref, vwint_ref, gt_ref, bounded_ref, *, n_cmp):
    g = pl.program_id(1)
    qi = pl.program_id(2)
    tq = q_ref.shape[0]
    tk = tq
    seq = slc_ref.shape[0]
    dh = NSA_HEAD_DIM
    rep = q_ref.shape[1] // dh
    n_half = kc_ref.shape[0]
    n_pad = WINDOW // tk

    @pl.when(qi == 0)
    def _():
        lane = lax.broadcasted_iota(jnp.int32, (seq, LANES), 1)

        def max_sq_norm(x, width):
            s = jnp.dot((x * x).astype(BF16), seg_ref[0:width, :], preferred_element_type=F32)
            return jnp.max(s) * SAFETY

        kv = slc_ref[...]
        k2_sel = max_sq_norm(jnp.where(lane < dh, kv.astype(F32), 0.0), LANES)
        ksel_ref[...] = jnp.where(lane < dh, kv, featk_ref[...])
        vt = kv.astype(F32).T[dh:, :].astype(BF16)
        for kt in range(seq // tk):
            vselt_ref[kt] = vt[:, kt * tk:(kt + 1) * tk]
        kv = win_ref[...]
        k2_win = max_sq_norm(jnp.where(lane < dh, kv.astype(F32), 0.0), LANES)
        kwin_ref[0:WINDOW, :] = featw_ref[0:WINDOW, :]
        kwin_ref[WINDOW:, :] = jnp.where(lane < dh, kv, featw_ref[WINDOW:, :])
        vt = kv.astype(F32).T[dh:, :].astype(BF16)
        for kt in range(n_pad):
            vwint_ref[kt] = jnp.zeros((dh, tk), BF16)
        for kt in range(seq // tk):
            vwint_ref[n_pad + kt] = vt[:, kt * tk:(kt + 1) * tk]
        q2 = max_sq_norm(qall_ref[...].astype(F32), rep * dh) * (LOG2E * LOG2E)
        limit = SCORE_BOUND * SCORE_BOUND
        bounded_ref[0] = ((q2 * k2_sel <= limit) & (q2 * k2_win <= limit)).astype(jnp.int32)

    heads = lambda a: jnp.concatenate([a] * rep, axis=1)
    sub8 = lambda a: a.reshape(a.shape[0] // 8, 8, a.shape[1])

    def softmax_pv(lgs, v_tiles, bounded):
        if bounded:
            m = 0.0
        else:
            m8 = functools.reduce(jnp.maximum, [jnp.max(sub8(lg), axis=0) for lg in lgs])
            m = jnp.max(m8, axis=0, keepdims=True)
        l8 = jnp.zeros((8, rep * tq), F32)
        acc = jnp.zeros((dh, rep * tq), F32)
        for lg, vt in zip(lgs, v_tiles):
            pt = jnp.exp2(lg) if bounded else jnp.exp2(lg - m)
            l8 = l8 + jnp.sum(sub8(pt), axis=0)
            acc = acc + jnp.dot(vt, pt.astype(BF16), preferred_element_type=F32)
        return acc / jnp.sum(l8, axis=0, keepdims=True)

    def step(n, bounded):
        static = isinstance(n, int)
        assert static == bounded
        t0 = n * tq
        lane = lax.broadcasted_iota(jnp.int32, (tq, LANES), 1)
        key_i = lax.broadcasted_iota(jnp.int32, (tk, tq), 0)
        qry_i = lax.broadcasted_iota(jnp.int32, (tk, tq), 1)
        causal_1 = jnp.where(key_i <= qry_i, 0.0, NEG_INF)
        causal_bias = heads(causal_1)
        band_bias = heads(jnp.where(key_i > qry_i, 0.0, NEG_INF))

        qfeat = qfeat_ref[0, n]
        rel_lanes = (lane >= REL_LANE0) & (lane < REL_LANE0 + 3)
        qa = []
        for r in range(rep):
            qcol = q_ref[:, (r // 2) * LANES:(r // 2 + 1) * LANES].astype(F32)
            if r % 2:
                qcol = pltpu.roll(qcol, dh, 1)
            feat = jnp.where(rel_lanes, qrel_ref[0, r * tq:(r + 1) * tq, :].astype(F32), qfeat[r:r + 1, :])
            qa.append(jnp.where(lane < dh, qcol * LOG2E, feat))
        q0 = jnp.concatenate(qa, axis=0).astype(BF16)

        gt_ref[...] = jax.nn.sigmoid(gate_ref[...]).T
        out_scale = []
        for pair in range(rep // 2):
            z = z_ref[:, pair * LANES:(pair + 1) * LANES]
            out_scale.append(norm_ref[:, pair * LANES:(pair + 1) * LANES] * (z * jax.nn.sigmoid(z)))

        nc = min(n_half, (t0 + tq) // CMP_STRIDE) if static else n_half
        n_idx = lax.broadcasted_iota(jnp.int32, (nc, tq), 0)
        t_idx = t0 + lax.broadcasted_iota(jnp.int32, (nc, tq), 1)
        valid_c = heads((CMP_STRIDE * n_idx + (CMP_BLOCK - 1) <= t_idx) & (n_idx < n_cmp))
        lg_c = jnp.where(valid_c, _nt_dot(kc_ref[0:nc, :], q0), NEG_INF)

        n_wt = n_pad + 1
        lgs = []
        for w in range(n_wt):
            rows = (slice(t0 + w * tk, t0 + (w + 1) * tk) if static
                    else pl.ds(pl.multiple_of(t0 + w * tk, tk), tk))
            lg = _nt_dot(kwin_ref[rows, :], q0)
            if w == 0:
                lg = lg + band_bias
            if w == n_wt - 1:
                lg = lg + causal_bias
            lgs.append(lg)
        o_win = softmax_pv(lgs, [vwint_ref[n + w] for w in range(n_wt)], bounded)

        e = jnp.exp2(lg_c - jnp.max(lg_c, axis=0, keepdims=True))
        p = jnp.where(valid_c, e / jnp.sum(e, axis=0, keepdims=True), 0.0)
        if nc < n_half:
            p = jnp.concatenate([p, jnp.zeros((n_half - nc, rep * tq), F32)], axis=0)
        o_cmp = jnp.dot(vct_ref[...], p.astype(BF16), preferred_element_type=F32)
        p_sum = p[:, 0:tq]
        for r in range(1, rep):
            p_sum = p_sum + p[:, r * tq:(r + 1) * tq]

        nb = min(MAX_SEL_BLOCKS, (t0 + tq) // SEL_BLOCK) if static else MAX_SEL_BLOCKS
        mm = mmap_ref[...]
        hi, mid, lo = _split3(p_sum)
        p_slc = (jnp.dot(mm, hi, preferred_element_type=F32) + jnp.dot(mm, mid, preferred_element_type=F32)
                 + jnp.dot(mm, lo, preferred_element_type=F32))[0:MAX_SEL_BLOCKS]
        if static:
            raw = [_nt_dot(ksel_ref[kt * tk:(kt + 1) * tk, :], q0) for kt in range(n + 1)]
        jj = lax.broadcasted_iota(jnp.int32, (MAX_SEL_BLOCKS, tq), 0)
        tt = t0 + lax.broadcasted_iota(jnp.int32, (MAX_SEL_BLOCKS, tq), 1)
        cur = lax.shift_right_logical(tt, int(np.log2(SEL_BLOCK)))
        forced = (jj == 0) | (jj == cur) | (jj == cur - 1)
        future = jj > cur
        score = jnp.where(future, -1.0, p_slc + jnp.where(forced, SEL_BONUS, 0.0))
        rank = jnp.zeros((MAX_SEL_BLOCKS, tq), jnp.int32)
        for i in range(nb):
            other = jnp.broadcast_to(score[i:i + 1, :], score.shape)
            beats = (other > score) | ((other == score) & (jj > i))
            rank = rank + beats.astype(jnp.int32)
        keep = (rank < SEL_TOP) & jnp.logical_not(future)

        if static:
            blocks_per_tile = tk // SEL_BLOCK
            block_bias = jnp.where(keep, 0.0, NEG_INF)
            lgs = []
            for kt in range(n + 1):
                b = jnp.concatenate(
                    [jnp.broadcast_to(block_bias[blocks_per_tile * kt + j:blocks_per_tile * kt + j + 1, :],
                                      (SEL_BLOCK, tq)) for j in range(blocks_per_tile)], axis=0)
                lgs.append(raw[kt] + heads(b + causal_1 if kt == n else b))
            o_slc = softmax_pv(lgs, [vselt_ref[kt] for kt in range(n + 1)], bounded)
        else:
            chosen = jnp.where(keep, 1.0, 0.0).astype(BF16)
            placed = _tn_dot(chosen, place_ref[...])
            sel_lanes = (lane >= SEL_LANE0) & (lane < SEL_LANE0 + MAX_SEL_BLOCKS)
            sel_bias = (placed - 1.0) * (-NEG_INF)
            qs = jnp.concatenate([jnp.where(sel_lanes, sel_bias, a) for a in qa], axis=0).astype(BF16)

            def sel_tile(kt, carry, bias=None):
                m, l, acc = carry
                lg = _nt_dot(ksel_ref[pl.ds(pl.multiple_of(kt * tk, tk), tk), :], qs)
                if bias is not None:
                    lg = lg + bias
                m_new = jnp.maximum(m, jnp.max(lg, axis=0, keepdims=True))
                alpha = jnp.exp2(m - m_new)
                pt = jnp.exp2(lg - m_new)
                return (m_new, alpha * l + jnp.sum(pt, axis=0, keepdims=True),
                        alpha * acc + jnp.dot(vselt_ref[kt], pt.astype(BF16), preferred_element_type=F32))

            init = (jnp.full((1, rep * tq), NEG_INF, F32), jnp.zeros((1, rep * tq), F32),
                    jnp.zeros((dh, rep * tq), F32))
            _, l, acc = sel_tile(n, lax.fori_loop(0, n, sel_tile, init), causal_bias)
            o_slc = acc / l

        ys = []
        for r in range(rep):
            cols = slice(r * tq, (r + 1) * tq)
            gate = lambda c: gt_ref[pl.ds(3 * (g * rep + r) + c, 1), :]
            o = gate(0) * o_cmp[:, cols] + gate(1) * o_slc[:, cols] + gate(2) * o_win[:, cols]
            ys.append(o * lax.rsqrt(jnp.mean(o * o, axis=0, keepdims=True) + EPS))
        for pair in range(rep // 2):
            cols = slice(pair * LANES, (pair + 1) * LANES)
            y = jnp.concatenate([ys[2 * pair], ys[2 * pair + 1]], axis=0).T
            o_ref[:, cols] = (y * out_scale[pair]).astype(o_ref.dtype)

    @pl.when(bounded_ref[0] == 1)
    def _():
        lax.switch(qi, [functools.partial(step, n, True) for n in range(seq // tq)])

    @pl.when(bounded_ref[0] == 0)
    def _():
        step(qi, False)


def _nsa(streams, gate, kc, vc, featk, featw, qfeat, qrel, seg, mmap, place, norm, *, batch, seq, n_cmp):
    tq = NSA_TILE
    nq = seq // tq
    G = NSA_KV_HEADS
    gw = norm.shape[1] // G
    n_half = kc.shape[0] // (batch * G)
    const = lambda a: pl.BlockSpec(a.shape, lambda b, g, i: (0,) * a.ndim)
    stream = lambda name: pl.BlockSpec((seq, LANES),
                                       lambda b, g, i, o=streams[name][1] // LANES: (b, o + g))
    tile = lambda name: pl.BlockSpec((tq, gw), lambda b, g, i, o=streams[name][1] // gw: (b * nq + i, o + g))
    return pl.pallas_call(
        functools.partial(_nsa_kernel, n_cmp=n_cmp),
        grid=(batch, G, nq),
        in_specs=[
            tile("q_a"),
            pl.BlockSpec((seq, gw), lambda b, g, i, o=streams["q_a"][1] // gw: (b, o + g)),
            tile("z_a"),
            pl.BlockSpec((tq, gate.shape[1]), lambda b, g, i: (b * nq + i, 0)),
            stream("slc"), stream("win"),
            pl.BlockSpec((n_half, LANES), lambda b, g, i: (b * G + g, 0)),
            pl.BlockSpec((NSA_HEAD_DIM, n_half), lambda b, g, i: (b * G + g, 0)),
            const(featk), const(featw),
            pl.BlockSpec((1,) + qfeat.shape[1:], lambda b, g, i: (g, 0, 0, 0)),
            pl.BlockSpec((1,) + qrel.shape[1:], lambda b, g, i: (g, 0, 0)),
            const(seg), const(mmap), const(place),
            pl.BlockSpec((1, gw), lambda b, g, i: (0, g)),
        ],
        out_specs=pl.BlockSpec((tq, gw), lambda b, g, i: (b * nq + i, g)),
        out_shape=jax.ShapeDtypeStruct((batch * seq, G * gw), BF16),
        scratch_shapes=[
            pltpu.VMEM((seq, LANES), BF16),
            pltpu.VMEM((seq // tq, NSA_HEAD_DIM, tq), BF16),
            pltpu.VMEM((seq + WINDOW, LANES), BF16),
            pltpu.VMEM(((seq + WINDOW) // tq, NSA_HEAD_DIM, tq), BF16),
            pltpu.VMEM((gate.shape[1], tq), F32),
            pltpu.SMEM((1,), jnp.int32),
        ],
        compiler_params=pltpu.CompilerParams(
            dimension_semantics=("arbitrary", "arbitrary", "arbitrary"),
            vmem_limit_bytes=VMEM_LIMIT),
        name="nsa",
    )(streams["q_a"][0], streams["q_a"][0], streams["z_a"][0], gate, streams["slc"][0], streams["win"][0],
      kc, vc, featk, featw, qfeat, qrel, seg, mmap, place, norm)


def _bf16_terms(x, n):
    terms, rest = [], np.asarray(x, np.float64)
    for _ in range(n):
        t = rest.astype(np.float32).astype(ml_dtypes.bfloat16).astype(np.float64)
        terms.append(t.astype(np.float32))
        rest = rest - t
    return terms


def _nsa_tables(seq, n_half, n_cmp, heads):
    assert seq // SEL_BLOCK <= MAX_SEL_BLOCKS and FLAG_LANE < LANES

    def key_features(pos, onehot_blocks):
        f = np.zeros((len(pos), LANES), np.float32)
        if onehot_blocks:
            f[np.arange(len(pos)), SEL_LANE0 + pos // SEL_BLOCK] = 1.0
        f[:, POS_LANE0:POS_LANE0 + N_SPLIT] = ((pos // 64) * 64)[:, None]
        f[:, POS_LANE0 + N_SPLIT:POS_LANE0 + 2 * N_SPLIT] = (pos % 64)[:, None]
        return f

    assert REL_LANE0 + 3 <= LANES
    featk = key_features(np.arange(seq), True)
    featk[:, SHIFT_LANE0:REL_LANE0 + 3] = 1.0
    featw = np.concatenate([np.zeros((WINDOW, LANES), np.float32), key_features(np.arange(seq), False)])
    featw[:WINDOW, FLAG_LANE] = NEG_INF
    featw[WINDOW:, SHIFT_LANE0:REL_LANE0 + 3] = 1.0
    featc = key_features(CMP_STRIDE * np.arange(n_half) + CMP_BLOCK - 1, False)
    featc[n_cmp:, FLAG_LANE] = NEG_INF

    slopes = (2.0 ** (-8.0 * np.arange(1, heads + 1) / heads)).astype(np.float32).astype(np.float64)
    slopes2 = slopes * LOG2E
    terms = _bf16_terms(slopes2, N_SPLIT)
    rep = heads // NSA_KV_HEADS
    nq = seq // NSA_TILE
    qfeat = np.zeros((NSA_KV_HEADS, nq, 8, LANES), np.float32)
    qrel = np.zeros((NSA_KV_HEADS, rep * NSA_TILE, LANES), np.float32)
    for h in range(heads):
        g, r = divmod(h, rep)
        for i, t in enumerate(terms):
            qfeat[g, :, r, POS_LANE0 + i] = t[h]
            qfeat[g, :, r, POS_LANE0 + N_SPLIT + i] = t[h]
        qfeat[g, :, r, FLAG_LANE] = 1.0
        for i, t in enumerate(_bf16_terms(-slopes2[h] * NSA_TILE * np.arange(nq), N_SPLIT)):
            qfeat[g, :, r, SHIFT_LANE0 + i] = t
        for i, t in enumerate(_bf16_terms(-slopes2[h] * np.arange(NSA_TILE), 3)):
            qrel[g, r * NSA_TILE:(r + 1) * NSA_TILE, REL_LANE0 + i] = t
    seg = np.zeros((2 * LANES, LANES), np.float32)
    seg[np.arange(2 * LANES), np.arange(2 * LANES) // NSA_HEAD_DIM] = 1.0

    cs = CMP_STRIDE * np.arange(n_half)[None, :]
    ss = SEL_BLOCK * np.arange(LANES)[:, None]
    overlap = np.clip(np.minimum(cs + CMP_BLOCK, ss + SEL_BLOCK) - np.maximum(cs, ss), 0, None)
    mmap = (overlap / CMP_BLOCK) * (np.arange(n_half)[None, :] < n_cmp) * (ss < seq)
    place = np.zeros((MAX_SEL_BLOCKS, LANES), np.float32)
    place[np.arange(MAX_SEL_BLOCKS), SEL_LANE0 + np.arange(MAX_SEL_BLOCKS)] = 1.0
    bf = lambda a: jnp.asarray(a, dtype=BF16)
    return bf(featk), bf(featw), bf(featc), jnp.asarray(qfeat), bf(qrel), bf(seg), bf(mmap), bf(place)


def _hgrn_stages(q_ref, f_ref, v_ref, z_ref, lb_ref, norm_ref, cum_ref, o_ref, rows):
    C, SUB = HGRN_CHUNK, HGRN_SUB
    n_sub = C // SUB
    chunks = range(len(rows))
    sub = lambda x, i: x[i * SUB:(i + 1) * SUB]
    w = {}

    def gates():
        lbr = lb_ref[...]
        e = jnp.exp(lbr - jnp.max(lbr, axis=0, keepdims=True))
        lb = e[0:1, :] / jnp.sum(e, axis=0, keepdims=True)
        w["v16"] = [v_ref[rows[j], :].astype(BF16) for j in chunks]
        f = [lb + (1.0 - lb) * jax.nn.sigmoid(f_ref[rows[j], :]) for j in chunks]
        w["k"] = [1.0 - f[j] for j in chunks]
        w["parts"] = [_split3(jnp.log(f[j])) for j in chunks]

    def cumsums():
        cum = cum_ref[...]
        parts = w["parts"]
        w["a"] = [(jnp.dot(cum, parts[j][0], preferred_element_type=F32)
                   + jnp.dot(cum, parts[j][1], preferred_element_type=F32)
                   + jnp.dot(cum, parts[j][2], preferred_element_type=F32)) for j in chunks]

    def operands():
        a, k = w["a"], w["k"]
        q1, k1, k2, qb, k3, start, dec = [], [], [], [], [], [], []
        for j in chunks:
            q1.append(q_ref[rows[j], :] * jnp.exp(a[j]))
            k1.append(k[j] * jnp.exp(-a[j]))
            tot = [a[j][(i + 1) * SUB - 1:(i + 1) * SUB] for i in range(n_sub)]
            s = [jnp.zeros_like(tot[0])]
            for i in range(n_sub):
                s.append(s[i] + tot[i])
            start.append(s)
            dec.append(jnp.exp(s[n_sub]))
            qb.append(jnp.concatenate([sub(q1[j], i) * jnp.exp(s[i]) for i in range(n_sub)], axis=0)
                      .astype(BF16))
            k2.append([sub(k1[j], i) * jnp.exp(tot[i]) for i in range(n_sub)])
            k3.append(jnp.concatenate([sub(k1[j], i) * jnp.exp(s[n_sub] - s[i]) for i in range(n_sub)],
                                      axis=0).astype(BF16))
        w.update(q1=q1, k1=k1, k2=k2, qb=qb, k3=k3, start=start, dec=dec)

    def scores():
        q1, k1, k2, start = w["q1"], w["k1"], w["k2"], w["start"]
        att = []
        for j in chunks:
            row_blocks = []
            for i in range(n_sub):
                rhs = [k2[j][jb] * jnp.exp(start[j][i] - start[j][jb + 1]) if jb < i - 1 else k2[j][jb]
                       for jb in range(i)]
                rhs.append(sub(k1[j], i))
                rhs = jnp.concatenate(rhs, axis=0) if len(rhs) > 1 else rhs[0]
                sc = _nt_dot(sub(q1[j], i).astype(BF16), rhs.astype(BF16))
                width = (i + 1) * SUB
                causal = (lax.broadcasted_iota(jnp.int32, (SUB, width), 1)
                          <= lax.broadcasted_iota(jnp.int32, (SUB, width), 0) + i * SUB)
                row_blocks.append(jnp.where(causal, sc, 0.0).astype(BF16))
            att.append(row_blocks)
        w["att"] = att

    def products():
        att, v16, k3 = w["att"], w["v16"], w["k3"]
        w["intra"] = [jnp.concatenate([jnp.dot(att[j][i], v16[j][0:(i + 1) * SUB],
                                               preferred_element_type=F32)
                                       for i in range(n_sub)], axis=0) for j in chunks]
        w["incr"] = [_tn_dot(v16[j], k3[j]) for j in chunks]

    def state_pass(st):
        inter = []
        for j in chunks:
            inter.append(_nt_dot(w["qb"][j], st.astype(BF16)))
            st = st * w["dec"][j] + w["incr"][j]
        w["inter"] = inter
        return st

    def finish():
        gain = norm_ref[...]
        for j in chunks:
            o = w["inter"][j] + w["intra"][j]
            y = o * lax.rsqrt(jnp.mean(o * o, axis=-1, keepdims=True) + EPS) * gain
            z = z_ref[rows[j], :]
            o_ref[rows[j], :] = (y * (z * jax.nn.sigmoid(z))).astype(o_ref.dtype)

    return gates, cumsums, operands, scores, products, state_pass, finish


def _hgrn_kernel(q_ref, f_ref, v_ref, z_ref, lb_ref, norm_ref, cum_ref, o_ref, *, group):
    seq, dk = q_ref.shape

    def block(cb, st):
        rows = [pl.ds(pl.multiple_of((cb * group + j) * HGRN_CHUNK, HGRN_CHUNK), HGRN_CHUNK)
                for j in range(group)]
        gates, cumsums, operands, scores, products, state_pass, finish = _hgrn_stages(
            q_ref, f_ref, v_ref, z_ref, lb_ref, norm_ref, cum_ref, o_ref, rows)
        gates()
        cumsums()
        operands()
        scores()
        products()
        st = state_pass(st)
        finish()
        return st

    lax.fori_loop(0, seq // (HGRN_CHUNK * group), block, jnp.zeros((dk, dk), F32))


def _hgrn_patterns():
    C, SUB = HGRN_CHUNK, HGRN_SUB
    t = np.arange(C)[:, None]
    s = np.arange(C)[None, :]
    return (((t // SUB) == (s // SUB)) & (s <= t)).astype(np.float32)


def _hgrn(streams, lower_bounds, norm, *, batch, seq):
    dk = HGRN_HEAD_DIM
    heads = norm.shape[1] // dk
    cum = jnp.asarray(_hgrn_patterns(), dtype=BF16)
    col = lambda name: (lambda b, h, o=streams[name][1] // dk: (b, o + h))
    return pl.pallas_call(
        functools.partial(_hgrn_kernel, group=16),
        grid=(batch, heads),
        in_specs=[
            pl.BlockSpec((seq, dk), col("q_h")),
            pl.BlockSpec((seq, dk), col("f_h")),
            pl.BlockSpec((seq, dk), col("i_h")),
            pl.BlockSpec((seq, dk), col("z_h")),
            pl.BlockSpec((lower_bounds.shape[0], dk), lambda b, h: (0, h)),
            pl.BlockSpec((1, dk), lambda b, h: (0, h)),
            pl.BlockSpec(cum.shape, lambda b, h: (0, 0)),
        ],
        out_specs=pl.BlockSpec((seq, dk), lambda b, h: (b, h)),
        out_shape=jax.ShapeDtypeStruct((batch * seq, heads * dk), BF16),
        compiler_params=pltpu.CompilerParams(
            dimension_semantics=("arbitrary", "arbitrary"),
            vmem_limit_bytes=VMEM_LIMIT),
        name="hgrn",
    )(streams["q_h"][0], streams["f_h"][0], streams["i_h"][0], streams["z_h"][0], lower_bounds, norm, cum)


def _outproj_kernel(x_ref, oa_ref, oh_ref, wa_ref, wh_ref, g_ref, o_ref):
    y = x_ref[...] + jnp.dot(oa_ref[...], wa_ref[...], preferred_element_type=F32)
    y = y + jnp.dot(oh_ref[...], wh_ref[...], preferred_element_type=F32)
    ms = jnp.mean(y * y, axis=-1, keepdims=True)
    o_ref[...] = y * lax.rsqrt(ms + EPS) * g_ref[...]


def _outproj(x2, o_a, o_h, wa, wh, g, *, tm):
    M, D = x2.shape
    const = lambda a: pl.BlockSpec(a.shape, lambda i: (0,) * a.ndim)
    return pl.pallas_call(
        _outproj_kernel,
        grid=(M // tm,),
        in_specs=[
            pl.BlockSpec((tm, D), lambda i: (i, 0)),
            pl.BlockSpec((tm, o_a.shape[1]), lambda i: (i, 0)),
            pl.BlockSpec((tm, o_h.shape[1]), lambda i: (i, 0)),
            const(wa), const(wh), const(g),
        ],
        out_specs=pl.BlockSpec((tm, D), lambda i: (i, 0)),
        out_shape=jax.ShapeDtypeStruct((M, D), F32),
        compiler_params=pltpu.CompilerParams(
            dimension_semantics=("arbitrary",),
            vmem_limit_bytes=VMEM_LIMIT),
        name="outproj",
    )(x2, o_a, o_h, wa, wh, g)


def kernel(x, norm_in, w_in, cmp_pe_k, cmp_w1_k, cmp_w2_k, cmp_pe_v, cmp_w1_v, cmp_w2_v,
           lower_bounds, nsa_out_norm, hgrn_out_norm, w_out, final_norm):
    B, S, D = x.shape
    assert norm_in.shape[0] == 1, "single-layer problem"
    nsa_w = nsa_out_norm.shape[1]
    hgrn_w = hgrn_out_norm.shape[1]
    dh = NSA_HEAD_DIM
    G = NSA_KV_HEADS
    heads = nsa_w // dh
    kvw = G * dh
    n_gate = 3 * heads
    n_cmp = (S - CMP_BLOCK) // CMP_STRIDE + 1
    n_half = S // CMP_STRIDE
    assert S % NSA_TILE == 0 and WINDOW % NSA_TILE == 0 and n_half <= LANES and 2 * dh == LANES

    names = ["q_a", "k_cmp", "v_cmp", "k_slc", "v_slc", "k_win", "v_win", "gate", "z_a",
             "q_h", "f_h", "i_h", "z_h"]
    widths = [nsa_w] + [kvw] * 6 + [n_gate, nsa_w] + [hgrn_w] * 4
    starts = dict(zip(names, np.cumsum([0] + widths[:-1]).tolist()))
    wd = dict(zip(names, widths))

    paired = lambda kname, vname: [(starts[n] + g * dh, dh) for g in range(G) for n in (kname, vname)]
    whole = lambda n: [(starts[n], wd[n])]
    groups = [[("q_a", whole("q_a")), ("q_h", whole("q_h")), ("i_h", whole("i_h")),
               ("cmp", paired("k_cmp", "v_cmp")), ("slc", paired("k_slc", "v_slc")),
               ("win", paired("k_win", "v_win"))],
              [("z_a", whole("z_a")), ("f_h", whole("f_h")), ("z_h", whole("z_h"))]]
    plan, col_of, off = [], {}, 0
    for group in groups:
        for n, pieces in group:
            col_of[n] = off
            for src, width in pieces:
                plan.append((off, src, width, dh ** -0.5 if n == "q_a" else 1.0))
                off += width
    n_main = off
    n16_cols = col_of["z_a"]
    plan.append((n_main, starts["gate"], n_gate, 1.0))
    w_all = _relayout_weights(w_in[0], tuple(plan), n_main + LANES)

    x2 = x.reshape(B * S, D)
    proj16, proj32, gate = _proj(x2, norm_in, w_all, tm=min(1024, B * S), tn=1536,
                                 n_main=n_main, n16_cols=n16_cols)
    streams = {}
    for arr, group, base in ((proj16, groups[0], 0), (proj32, groups[1], n16_cols)):
        for n, _ in group:
            streams[n] = (arr, col_of[n] - base)

    featk, featw, featc, qfeat, qrel, seg, mmap, place = _nsa_tables(S, n_half, n_cmp, heads)

    def w1_halves(w1k, w1v):
        hk = w1k.shape[1]
        k3 = w1k.reshape(2, CMP_STRIDE, dh, hk)
        v3 = w1v.reshape(2, CMP_STRIDE, dh, hk)
        zk = jnp.zeros_like(k3[0])
        top = lambda a: jnp.concatenate([a, zk], axis=-1)
        bot = lambda a: jnp.concatenate([zk, a], axis=-1)
        half = lambda i: jnp.concatenate([top(k3[i]), bot(v3[i])], axis=1).reshape(CMP_STRIDE * 2 * dh, 2 * hk)
        return half(0).astype(BF16), half(1).astype(BF16)

    wp, wq = w1_halves(cmp_w1_k[0], cmp_w1_v[0])
    zk = jnp.zeros_like(cmp_w2_k[0])
    w2 = jnp.concatenate([jnp.concatenate([cmp_w2_k[0], zk], axis=1),
                          jnp.concatenate([zk, cmp_w2_v[0]], axis=1)], axis=0).astype(BF16)
    pe = jnp.concatenate([cmp_pe_k[0].reshape(2, CMP_STRIDE, dh), cmp_pe_v[0].reshape(2, CMP_STRIDE, dh)],
                         axis=-1).reshape(2, CMP_STRIDE * 2 * dh)
    kc, vc = _compress(streams["cmp"][0], pe, wp, wq, w2, jnp.tile(featc, (G, 1)),
                       batch=B, seq=S, col0=streams["cmp"][1], n_half=n_half)

    o_a = _nsa(streams, gate, kc, vc, featk, featw, qfeat, qrel, seg, mmap, place, nsa_out_norm,
               batch=B, seq=S, n_cmp=n_cmp)
    o_h = _hgrn(streams, lower_bounds, hgrn_out_norm, batch=B, seq=S)

    wo = w_out[0].astype(BF16)
    out = _outproj(x2, o_a, o_h, wo[:nsa_w], wo[nsa_w:], final_norm.reshape(1, D), tm=512)
    return out.reshape(B, S, D)
```

```python
import functools

import ml_dtypes
import numpy as np
import jax
import jax.numpy as jnp
from jax import lax
from jax.experimental import pallas as pl
from jax.experimental.pallas import tpu as pltpu

F32 = jnp.float32
BF16 = jnp.bfloat16

EPS = 1e-6
NEG_INF = -1e30
LOG2E = 1.4426950408889634

NSA_HEAD_DIM = 64
NSA_KV_HEADS = 4
CMP_BLOCK = 32
CMP_STRIDE = 16
SEL_BLOCK = 64
SEL_TOP = 8
SEL_BONUS = 1.0e4
WINDOW = 512
HGRN_HEAD_DIM = 128
HGRN_CHUNK = 64
HGRN_SUB = 16

LANES = 128
VMEM_LIMIT = 56 * 1024 * 1024
NSA_TILE = 256

SEL_LANE0 = NSA_HEAD_DIM
MAX_SEL_BLOCKS = 32
POS_LANE0 = SEL_LANE0 + MAX_SEL_BLOCKS
N_SPLIT = 4
FLAG_LANE = POS_LANE0 + 2 * N_SPLIT
SHIFT_LANE0 = FLAG_LANE + 1
REL_LANE0 = SHIFT_LANE0 + N_SPLIT
SCORE_BOUND = 100.0
SAFETY = 1.02


def _nt_dot(a, b):
    return lax.dot_general(a, b, (((1,), (1,)), ((), ())), preferred_element_type=F32)


def _tn_dot(a, b):
    return lax.dot_general(a, b, (((0,), (0,)), ((), ())), preferred_element_type=F32)


def _split3(x):
    hi = x.astype(BF16)
    r1 = x - hi.astype(F32)
    mid = r1.astype(BF16)
    lo = (r1 - mid.astype(F32)).astype(BF16)
    return hi, mid, lo


def _proj_kernel(x_ref, g_ref, w_ref, wg_ref, o16_ref, o32_ref, og_ref, h_ref, *, row_chunk, n16):
    j = pl.program_id(1)

    @pl.when(j == 0)
    def _():
        n_chunks = x_ref.shape[0] // row_chunk

        def body(c, carry):
            rows = pl.ds(pl.multiple_of(c * row_chunk, row_chunk), row_chunk)
            x = x_ref[rows, :]
            ms = jnp.mean(x * x, axis=-1, keepdims=True)
            h_ref[rows, :] = (x * lax.rsqrt(ms + EPS) * g_ref[...]).astype(BF16)
            return carry

        lax.fori_loop(0, n_chunks, body, 0)
        og_ref[...] = jnp.dot(h_ref[...], wg_ref[...], preferred_element_type=F32)

    @pl.when(j < n16)
    def _():
        o16_ref[...] = jnp.dot(h_ref[...], w_ref[...], preferred_element_type=F32).astype(BF16)

    @pl.when(j >= n16)
    def _():
        o32_ref[...] = jnp.dot(h_ref[...], w_ref[...], preferred_element_type=F32)


def _relayout_kernel(wt_ref, o_ref, *, plan):
    o_ref[:, o_ref.shape[1] - LANES:] = jnp.zeros((o_ref.shape[0], LANES), o_ref.dtype)
    for dst, src, width, scale in plan:
        o_ref[:, dst:dst + width] = (wt_ref[src:src + width, :].T * scale).astype(o_ref.dtype)


def _relayout_weights(wt, plan, n_out, *, rows=256):
    n_in, D = wt.shape
    return pl.pallas_call(
        functools.partial(_relayout_kernel, plan=plan),
        grid=(D // rows,),
        in_specs=[pl.BlockSpec((n_in, rows), lambda i: (0, i))],
        out_specs=pl.BlockSpec((rows, n_out), lambda i: (i, 0)),
        out_shape=jax.ShapeDtypeStruct((D, n_out), BF16),
        compiler_params=pltpu.CompilerParams(
            dimension_semantics=("arbitrary",),
            vmem_limit_bytes=VMEM_LIMIT),
        name="relayout",
    )(wt)


def _proj(x2, g, w, *, tm, tn, n_main, n16_cols):
    M, D = x2.shape
    N = n_main
    NG = w.shape[1] - n_main
    n16 = n16_cols // tn
    assert n16 * tn == n16_cols and N % tn == 0 and N % NG == 0
    return pl.pallas_call(
        functools.partial(_proj_kernel, row_chunk=128, n16=n16),
        grid=(M // tm, N // tn),
        in_specs=[
            pl.BlockSpec((tm, D), lambda i, j: (i, 0)),
            pl.BlockSpec((1, D), lambda i, j: (0, 0)),
            pl.BlockSpec((D, tn), lambda i, j: (0, j)),
            pl.BlockSpec((D, NG), lambda i, j: (0, N // NG)),
        ],
        out_specs=[
            pl.BlockSpec((tm, tn), lambda i, j: (i, jnp.minimum(j, n16 - 1))),
            pl.BlockSpec((tm, tn), lambda i, j: (i, jnp.maximum(j - n16, 0))),
            pl.BlockSpec((tm, NG), lambda i, j: (i, 0)),
        ],
        out_shape=[
            jax.ShapeDtypeStruct((M, n16_cols), BF16),
            jax.ShapeDtypeStruct((M, N - n16_cols), F32),
            jax.ShapeDtypeStruct((M, NG), F32),
        ],
        scratch_shapes=[pltpu.VMEM((tm, D), BF16)],
        compiler_params=pltpu.CompilerParams(
            dimension_semantics=("arbitrary", "arbitrary"),
            vmem_limit_bytes=VMEM_LIMIT),
        name="proj",
    )(x2, g, w, w)


def _compress_kernel(c0_ref, c1_ref, c2_ref, c3_ref, pe_ref, wp_ref, wq_ref, w2_ref, feat_ref,
                     kc_ref, vc_ref, x_ref, c32_ref, *, n_half):
    for g, c_ref in enumerate((c0_ref, c1_ref, c2_ref, c3_ref)):
        c32_ref[...] = c_ref[...].astype(F32)
        for l in range(CMP_STRIDE):
            x_ref[g * n_half:(g + 1) * n_half, l * LANES:(l + 1) * LANES] = (
                c32_ref[pl.ds(l, n_half, stride=CMP_STRIDE), :])
    x = x_ref[...]
    rows = x.shape[0]
    first = jnp.dot((x + pe_ref[0:1, :]).astype(BF16), wp_ref[...], preferred_element_type=F32)
    second = jnp.dot((x + pe_ref[1:2, :]).astype(BF16), wq_ref[...], preferred_element_type=F32)
    hidden = first + pltpu.roll(second, rows - 1, 0)
    out = jnp.dot(jax.nn.gelu(hidden).astype(BF16), w2_ref[...], preferred_element_type=F32)
    lane = lax.broadcasted_iota(jnp.int32, out.shape, 1)
    dh = NSA_HEAD_DIM
    kc_ref[...] = jnp.where(lane < dh, out.astype(BF16), feat_ref[...])
    for g in range(NSA_KV_HEADS):
        vc_ref[g * dh:(g + 1) * dh, :] = out[g * n_half:(g + 1) * n_half, :].T[dh:, :].astype(BF16)


def _compress(proj, pe, wp, wq, w2, feat, *, batch, seq, col0, n_half):
    rows = NSA_KV_HEADS * n_half
    rows_t = NSA_KV_HEADS * NSA_HEAD_DIM
    const = lambda a: pl.BlockSpec(a.shape, lambda b: (0,) * a.ndim)
    stream = lambda g: pl.BlockSpec((seq, LANES), lambda b, o=col0 // LANES + g: (b, o))
    return pl.pallas_call(
        functools.partial(_compress_kernel, n_half=n_half),
        grid=(batch,),
        in_specs=[stream(0), stream(1), stream(2), stream(3),
                  const(pe), const(wp), const(wq), const(w2), const(feat)],
        out_specs=[
            pl.BlockSpec((rows, LANES), lambda b: (b, 0)),
            pl.BlockSpec((rows_t, n_half), lambda b: (b, 0)),
        ],
        out_shape=[
            jax.ShapeDtypeStruct((batch * rows, LANES), BF16),
            jax.ShapeDtypeStruct((batch * rows_t, n_half), BF16),
        ],
        scratch_shapes=[pltpu.VMEM((rows, CMP_STRIDE * LANES), F32), pltpu.VMEM((seq, LANES), F32)],
        compiler_params=pltpu.CompilerParams(
            dimension_semantics=("arbitrary",),
            vmem_limit_bytes=VMEM_LIMIT),
        name="compress",
    )(proj, proj, proj, proj, pe, wp, wq, w2, feat)


def _nsa_kernel(q_ref, qall_ref, z_ref, gate_ref, slc_ref, win_ref, kc_ref, vct_ref, featk_ref, featw_ref,
                qfeat_ref, qrel_ref, seg_ref, mmap_ref, place_ref, norm_ref, o_ref,
                ksel_ref, vselt_ref, kwin_ref, vwint_ref, gt_ref, bounded_ref, *, n_cmp):
    g = pl.program_id(1)
    qi = pl.program_id(2)
    tq = q_ref.shape[0]
    tk = tq
    seq = slc_ref.shape[0]
    dh = NSA_HEAD_DIM
    rep = q_ref.shape[1] // dh
    n_half = kc_ref.shape[0]
    n_pad = WINDOW // tk

    @pl.when(qi == 0)
    def _():
        lane = lax.broadcasted_iota(jnp.int32, (seq, LANES), 1)

        def max_sq_norm(x, width):
            s = jnp.dot((x * x).astype(BF16), seg_ref[0:width, :], preferred_element_type=F32)
            return jnp.max(s) * SAFETY

        kv = slc_ref[...]
        k2_sel = max_sq_norm(jnp.where(lane < dh, kv.astype(F32), 0.0), LANES)
        ksel_ref[...] = jnp.where(lane < dh, kv, featk_ref[...])
        vt = kv.astype(F32).T[dh:, :].astype(BF16)
        for kt in range(seq // tk):
            vselt_ref[kt] = vt[:, kt * tk:(kt + 1) * tk]
        kv = win_ref[...]
        k2_win = max_sq_norm(jnp.where(lane < dh, kv.astype(F32), 0.0), LANES)
        kwin_ref[0:WINDOW, :] = featw_ref[0:WINDOW, :]
        kwin_ref[WINDOW:, :] = jnp.where(lane < dh, kv, featw_ref[WINDOW:, :])
        vt = kv.astype(F32).T[dh:, :].astype(BF16)
        for kt in range(n_pad):
            vwint_ref[kt] = jnp.zeros((dh, tk), BF16)
        for kt in range(seq // tk):
            vwint_ref[n_pad + kt] = vt[:, kt * tk:(kt + 1) * tk]
        q2 = max_sq_norm(qall_ref[...].astype(F32), rep * dh) * (LOG2E * LOG2E)
        limit = SCORE_BOUND * SCORE_BOUND
        bounded_ref[0] = ((q2 * k2_sel <= limit) & (q2 * k2_win <= limit)).astype(jnp.int32)

    heads = lambda a: jnp.concatenate([a] * rep, axis=1)
    sub8 = lambda a: a.reshape(a.shape[0] // 8, 8, a.shape[1])

    def softmax_pv(lgs, v_tiles, bounded):
        if bounded:
            m = 0.0
        else:
            m8 = functools.reduce(jnp.maximum, [jnp.max(sub8(lg), axis=0) for lg in lgs])
            m = jnp.max(m8, axis=0, keepdims=True)
        l8 = jnp.zeros((8, rep * tq), F32)
        acc = jnp.zeros((dh, rep * tq), F32)
        for lg, vt in zip(lgs, v_tiles):
            pt = jnp.exp2(lg) if bounded else jnp.exp2(lg - m)
            l8 = l8 + jnp.sum(sub8(pt), axis=0)
            acc = acc + jnp.dot(vt, pt.astype(BF16), preferred_element_type=F32)
        return acc / jnp.sum(l8, axis=0, keepdims=True)

    def step(n, bounded):
        static = isinstance(n, int)
        assert static == bounded
        t0 = n * tq
        lane = lax.broadcasted_iota(jnp.int32, (tq, LANES), 1)
        key_i = lax.broadcasted_iota(jnp.int32, (tk, tq), 0)
        qry_i = lax.broadcasted_iota(jnp.int32, (tk, tq), 1)
        causal_1 = jnp.where(key_i <= qry_i, 0.0, NEG_INF)
        causal_bias = heads(causal_1)
        band_bias = heads(jnp.where(key_i > qry_i, 0.0, NEG_INF))

        qfeat = qfeat_ref[0, n]
        rel_lanes = (lane >= REL_LANE0) & (lane < REL_LANE0 + 3)
        qa = []
        for r in range(rep):
            qcol = q_ref[:, (r // 2) * LANES:(r // 2 + 1) * LANES].astype(F32)
            if r % 2:
                qcol = pltpu.roll(qcol, dh, 1)
            feat = jnp.where(rel_lanes, qrel_ref[0, r * tq:(r + 1) * tq, :].astype(F32), qfeat[r:r + 1, :])
            qa.append(jnp.where(lane < dh, qcol * LOG2E, feat))
        q0 = jnp.concatenate(qa, axis=0).astype(BF16)

        gt_ref[...] = jax.nn.sigmoid(gate_ref[...]).T
        out_scale = []
        for pair in range(rep // 2):
            z = z_ref[:, pair * LANES:(pair + 1) * LANES]
            out_scale.append(norm_ref[:, pair * LANES:(pair + 1) * LANES] * (z * jax.nn.sigmoid(z)))

        nc = min(n_half, (t0 + tq) // CMP_STRIDE) if static else n_half
        n_idx = lax.broadcasted_iota(jnp.int32, (nc, tq), 0)
        t_idx = t0 + lax.broadcasted_iota(jnp.int32, (nc, tq), 1)
        valid_c = heads((CMP_STRIDE * n_idx + (CMP_BLOCK - 1) <= t_idx) & (n_idx < n_cmp))
        lg_c = jnp.where(valid_c, _nt_dot(kc_ref[0:nc, :], q0), NEG_INF)

        n_wt = n_pad + 1
        lgs = []
        for w in range(n_wt):
            rows = (slice(t0 + w * tk, t0 + (w + 1) * tk) if static
                    else pl.ds(pl.multiple_of(t0 + w * tk, tk), tk))
            lg = _nt_dot(kwin_ref[rows, :], q0)
            if w == 0:
                lg = lg + band_bias
            if w == n_wt - 1:
                lg = lg + causal_bias
            lgs.append(lg)
        o_win = softmax_pv(lgs, [vwint_ref[n + w] for w in range(n_wt)], bounded)

        e = jnp.exp2(lg_c - jnp.max(lg_c, axis=0, keepdims=True))
        p = jnp.where(valid_c, e / jnp.sum(e, axis=0, keepdims=True), 0.0)
        if nc < n_half:
            p = jnp.concatenate([p, jnp.zeros((n_half - nc, rep * tq), F32)], axis=0)
        o_cmp = jnp.dot(vct_ref[...], p.astype(BF16), preferred_element_type=F32)
        p_sum = p[:, 0:tq]
        for r in range(1, rep):
            p_sum = p_sum + p[:, r * tq:(r + 1) * tq]

        nb = min(MAX_SEL_BLOCKS, (t0 + tq) // SEL_BLOCK) if static else MAX_SEL_BLOCKS
        mm = mmap_ref[...]
        hi, mid, lo = _split3(p_sum)
        p_slc = (jnp.dot(mm, hi, preferred_element_type=F32) + jnp.dot(mm, mid, preferred_element_type=F32)
                 + jnp.dot(mm, lo, preferred_element_type=F32))[0:MAX_SEL_BLOCKS]
        if static:
            raw = [_nt_dot(ksel_ref[kt * tk:(kt + 1) * tk, :], q0) for kt in range(n + 1)]
        jj = lax.broadcasted_iota(jnp.int32, (MAX_SEL_BLOCKS, tq), 0)
        tt = t0 + lax.broadcasted_iota(jnp.int32, (MAX_SEL_BLOCKS, tq), 1)
        cur = lax.shift_right_logical(tt, int(np.log2(SEL_BLOCK)))
        forced = (jj == 0) | (jj == cur) | (jj == cur - 1)
        future = jj > cur
        score = jnp.where(future, -1.0, p_slc + jnp.where(forced, SEL_BONUS, 0.0))
        rank = jnp.zeros((MAX_SEL_BLOCKS, tq), jnp.int32)
        for i in range(nb):
            other = jnp.broadcast_to(score[i:i + 1, :], score.shape)
            beats = (other > score) | ((other == score) & (jj > i))
            rank = rank + beats.astype(jnp.int32)
        keep = (rank < SEL_TOP) & jnp.logical_not(future)

        if static:
            blocks_per_tile = tk // SEL_BLOCK
            block_bias = jnp.where(keep, 0.0, NEG_INF)
            lgs = []
            for kt in range(n + 1):
                b = jnp.concatenate(
                    [jnp.broadcast_to(block_bias[blocks_per_tile * kt + j:blocks_per_tile * kt + j + 1, :],
                                      (SEL_BLOCK, tq)) for j in range(blocks_per_tile)], axis=0)
                lgs.append(raw[kt] + heads(b + causal_1 if kt == n else b))
            o_slc = softmax_pv(lgs, [vselt_ref[kt] for kt in range(n + 1)], bounded)
        else:
            chosen = jnp.where(keep, 1.0, 0.0).astype(BF16)
            placed = _tn_dot(chosen, place_ref[...])
            sel_lanes = (lane >= SEL_LANE0) & (lane < SEL_LANE0 + MAX_SEL_BLOCKS)
            sel_bias = (placed - 1.0) * (-NEG_INF)
            qs = jnp.concatenate([jnp.where(sel_lanes, sel_bias, a) for a in qa], axis=0).astype(BF16)

            def sel_tile(kt, carry, bias=None):
                m, l, acc = carry
                lg = _nt_dot(ksel_ref[pl.ds(pl.multiple_of(kt * tk, tk), tk), :], qs)
                if bias is not None:
                    lg = lg + bias
                m_new = jnp.maximum(m, jnp.max(lg, axis=0, keepdims=True))
                alpha = jnp.exp2(m - m_new)
                pt = jnp.exp2(lg - m_new)
                return (m_new, alpha * l + jnp.sum(pt, axis=0, keepdims=True),
                        alpha * acc + jnp.dot(vselt_ref[kt], pt.astype(BF16), preferred_element_type=F32))

            init = (jnp.full((1, rep * tq), NEG_INF, F32), jnp.zeros((1, rep * tq), F32),
                    jnp.zeros((dh, rep * tq), F32))
            _, l, acc = sel_tile(n, lax.fori_loop(0, n, sel_tile, init), causal_bias)
            o_slc = acc / l

        ys = []
        for r in range(rep):
            cols = slice(r * tq, (r + 1) * tq)
            gate = lambda c: gt_ref[pl.ds(3 * (g * rep + r) + c, 1), :]
            o = gate(0) * o_cmp[:, cols] + gate(1) * o_slc[:, cols] + gate(2) * o_win[:, cols]
            ys.append(o * lax.rsqrt(jnp.mean(o * o, axis=0, keepdims=True) + EPS))
        for pair in range(rep // 2):
            cols = slice(pair * LANES, (pair + 1) * LANES)
            y = jnp.concatenate([ys[2 * pair], ys[2 * pair + 1]], axis=0).T
            o_ref[:, cols] = (y * out_scale[pair]).astype(o_ref.dtype)

    @pl.when(bounded_ref[0] == 1)
    def _():
        lax.switch(qi, [functools.partial(step, n, True) for n in range(seq // tq)])

    @pl.when(bounded_ref[0] == 0)
    def _():
        step(qi, False)


def _nsa(streams, gate, kc, vc, featk, featw, qfeat, qrel, seg, mmap, place, norm, *, batch, seq, n_cmp):
    tq = NSA_TILE
    nq = seq // tq
    G = NSA_KV_HEADS
    gw = norm.shape[1] // G
    n_half = kc.shape[0] // (batch * G)
    const = lambda a: pl.BlockSpec(a.shape, lambda b, g, i: (0,) * a.ndim)
    stream = lambda name: pl.BlockSpec((seq, LANES),
                                       lambda b, g, i, o=streams[name][1] // LANES: (b, o + g))
    tile = lambda name: pl.BlockSpec((tq, gw), lambda b, g, i, o=streams[name][1] // gw: (b * nq + i, o + g))
    return pl.pallas_call(
        functools.partial(_nsa_kernel, n_cmp=n_cmp),
        grid=(batch, G, nq),
        in_specs=[
            tile("q_a"),
            pl.BlockSpec((seq, gw), lambda b, g, i, o=streams["q_a"][1] // gw: (b, o + g)),
            tile("z_a"),
            pl.BlockSpec((tq, gate.shape[1]), lambda b, g, i: (b * nq + i, 0)),
            stream("slc"), stream("win"),
            pl.BlockSpec((n_half, LANES), lambda b, g, i: (b * G + g, 0)),
            pl.BlockSpec((NSA_HEAD_DIM, n_half), lambda b, g, i: (b * G + g, 0)),
            const(featk), const(featw),
            pl.BlockSpec((1,) + qfeat.shape[1:], lambda b, g, i: (g, 0, 0, 0)),
            pl.BlockSpec((1,) + qrel.shape[1:], lambda b, g, i: (g, 0, 0)),
            const(seg), const(mmap), const(place),
            pl.BlockSpec((1, gw), lambda b, g, i: (0, g)),
        ],
        out_specs=pl.BlockSpec((tq, gw), lambda b, g, i: (b * nq + i, g)),
        out_shape=jax.ShapeDtypeStruct((batch * seq, G * gw), BF16),
        scratch_shapes=[
            pltpu.VMEM((seq, LANES), BF16),
            pltpu.VMEM((seq // tq, NSA_HEAD_DIM, tq), BF16),
            pltpu.VMEM((seq + WINDOW, LANES), BF16),
            pltpu.VMEM(((seq + WINDOW) // tq, NSA_HEAD_DIM, tq), BF16),
            pltpu.VMEM((gate.shape[1], tq), F32),
            pltpu.SMEM((1,), jnp.int32),
        ],
        compiler_params=pltpu.CompilerParams(
            dimension_semantics=("arbitrary", "arbitrary", "arbitrary"),
            vmem_limit_bytes=VMEM_LIMIT),
        name="nsa",
    )(streams["q_a"][0], streams["q_a"][0], streams["z_a"][0], gate, streams["slc"][0], streams["win"][0],
      kc, vc, featk, featw, qfeat, qrel, seg, mmap, place, norm)


def _bf16_terms(x, n):
    terms, rest = [], np.asarray(x, np.float64)
    for _ in range(n):
        t = rest.astype(np.float32).astype(ml_dtypes.bfloat16).astype(np.float64)
        terms.append(t.astype(np.float32))
        rest = rest - t
    return terms


def _nsa_tables(seq, n_half, n_cmp, heads):
    assert seq // SEL_BLOCK <= MAX_SEL_BLOCKS and FLAG_LANE < LANES

    def key_features(pos, onehot_blocks):
        f = np.zeros((len(pos), LANES), np.float32)
        if onehot_blocks:
            f[np.arange(len(pos)), SEL_LANE0 + pos // SEL_BLOCK] = 1.0
        f[:, POS_LANE0:POS_LANE0 + N_SPLIT] = ((pos // 64) * 64)[:, None]
        f[:, POS_LANE0 + N_SPLIT:POS_LANE0 + 2 * N_SPLIT] = (pos % 64)[:, None]
        return f

    assert REL_LANE0 + 3 <= LANES
    featk = key_features(np.arange(seq), True)
    featk[:, SHIFT_LANE0:REL_LANE0 + 3] = 1.0
    featw = np.concatenate([np.zeros((WINDOW, LANES), np.float32), key_features(np.arange(seq), False)])
    featw[:WINDOW, FLAG_LANE] = NEG_INF
    featw[WINDOW:, SHIFT_LANE0:REL_LANE0 + 3] = 1.0
    featc = key_features(CMP_STRIDE * np.arange(n_half) + CMP_BLOCK - 1, False)
    featc[n_cmp:, FLAG_LANE] = NEG_INF

    slopes = (2.0 ** (-8.0 * np.arange(1, heads + 1) / heads)).astype(np.float32).astype(np.float64)
    slopes2 = slopes * LOG2E
    terms = _bf16_terms(slopes2, N_SPLIT)
    rep = heads // NSA_KV_HEADS
    nq = seq // NSA_TILE
    qfeat = np.zeros((NSA_KV_HEADS, nq, 8, LANES), np.float32)
    qrel = np.zeros((NSA_KV_HEADS, rep * NSA_TILE, LANES), np.float32)
    for h in range(heads):
        g, r = divmod(h, rep)
        for i, t in enumerate(terms):
            qfeat[g, :, r, POS_LANE0 + i] = t[h]
            qfeat[g, :, r, POS_LANE0 + N_SPLIT + i] = t[h]
        qfeat[g, :, r, FLAG_LANE] = 1.0
        for i, t in enumerate(_bf16_terms(-slopes2[h] * NSA_TILE * np.arange(nq), N_SPLIT)):
            qfeat[g, :, r, SHIFT_LANE0 + i] = t
        for i, t in enumerate(_bf16_terms(-slopes2[h] * np.arange(NSA_TILE), 3)):
            qrel[g, r * NSA_TILE:(r + 1) * NSA_TILE, REL_LANE0 + i] = t
    seg = np.zeros((2 * LANES, LANES), np.float32)
    seg[np.arange(2 * LANES), np.arange(2 * LANES) // NSA_HEAD_DIM] = 1.0

    cs = CMP_STRIDE * np.arange(n_half)[None, :]
    ss = SEL_BLOCK * np.arange(LANES)[:, None]
    overlap = np.clip(np.minimum(cs + CMP_BLOCK, ss + SEL_BLOCK) - np.maximum(cs, ss), 0, None)
    mmap = (overlap / CMP_BLOCK) * (np.arange(n_half)[None, :] < n_cmp) * (ss < seq)
    place = np.zeros((MAX_SEL_BLOCKS, LANES), np.float32)
    place[np.arange(MAX_SEL_BLOCKS), SEL_LANE0 + np.arange(MAX_SEL_BLOCKS)] = 1.0
    bf = lambda a: jnp.asarray(a, dtype=BF16)
    return bf(featk), bf(featw), bf(featc), jnp.asarray(qfeat), bf(qrel), bf(seg), bf(mmap), bf(place)


def _hgrn_stages(q_ref, f_ref, v_ref, z_ref, lb_ref, norm_ref, cum_ref, o_ref, rows):
    C, SUB = HGRN_CHUNK, HGRN_SUB
    n_sub = C // SUB
    chunks = range(len(rows))
    sub = lambda x, i: x[i * SUB:(i + 1) * SUB]
    w = {}

    def gates():
        lbr = lb_ref[...]
        e = jnp.exp(lbr - jnp.max(lbr, axis=0, keepdims=True))
        lb = e[0:1, :] / jnp.sum(e, axis=0, keepdims=True)
        w["v16"] = [v_ref[rows[j], :].astype(BF16) for j in chunks]
        f = [lb + (1.0 - lb) * jax.nn.sigmoid(f_ref[rows[j], :]) for j in chunks]
        w["k"] = [1.0 - f[j] for j in chunks]
        w["parts"] = [_split3(jnp.log(f[j])) for j in chunks]

    def cumsums():
        cum = cum_ref[...]
        parts = w["parts"]
        w["a"] = [(jnp.dot(cum, parts[j][0], preferred_element_type=F32)
                   + jnp.dot(cum, parts[j][1], preferred_element_type=F32)
                   + jnp.dot(cum, parts[j][2], preferred_element_type=F32)) for j in chunks]

    def operands():
        a, k = w["a"], w["k"]
        q1, k1, k2, qb, k3, start, dec = [], [], [], [], [], [], []
        for j in chunks:
            q1.append(q_ref[rows[j], :] * jnp.exp(a[j]))
            k1.append(k[j] * jnp.exp(-a[j]))
            tot = [a[j][(i + 1) * SUB - 1:(i + 1) * SUB] for i in range(n_sub)]
            s = [jnp.zeros_like(tot[0])]
            for i in range(n_sub):
                s.append(s[i] + tot[i])
            start.append(s)
            dec.append(jnp.exp(s[n_sub]))
            qb.append(jnp.concatenate([sub(q1[j], i) * jnp.exp(s[i]) for i in range(n_sub)], axis=0)
                      .astype(BF16))
            k2.append([sub(k1[j], i) * jnp.exp(tot[i]) for i in range(n_sub)])
            k3.append(jnp.concatenate([sub(k1[j], i) * jnp.exp(s[n_sub] - s[i]) for i in range(n_sub)],
                                      axis=0).astype(BF16))
        w.update(q1=q1, k1=k1, k2=k2, qb=qb, k3=k3, start=start, dec=dec)

    def scores():
        q1, k1, k2, start = w["q1"], w["k1"], w["k2"], w["start"]
        att = []
        for j in chunks:
            row_blocks = []
            for i in range(n_sub):
                rhs = [k2[j][jb] * jnp.exp(start[j][i] - start[j][jb + 1]) if jb < i - 1 else k2[j][jb]
                       for jb in range(i)]
                rhs.append(sub(k1[j], i))
                rhs = jnp.concatenate(rhs, axis=0) if len(rhs) > 1 else rhs[0]
                sc = _nt_dot(sub(q1[j], i).astype(BF16), rhs.astype(BF16))
                width = (i + 1) * SUB
                causal = (lax.broadcasted_iota(jnp.int32, (SUB, width), 1)
                          <= lax.broadcasted_iota(jnp.int32, (SUB, width), 0) + i * SUB)
                row_blocks.append(jnp.where(causal, sc, 0.0).astype(BF16))
            att.append(row_blocks)
        w["att"] = att

    def products():
        att, v16, k3 = w["att"], w["v16"], w["k3"]
        w["intra"] = [jnp.concatenate([jnp.dot(att[j][i], v16[j][0:(i + 1) * SUB],
                                               preferred_element_type=F32)
                                       for i in range(n_sub)], axis=0) for j in chunks]
        w["incr"] = [_tn_dot(v16[j], k3[j]) for j in chunks]

    def state_pass(st):
        inter = []
        for j in chunks:
            inter.append(_nt_dot(w["qb"][j], st.astype(BF16)))
            st = st * w["dec"][j] + w["incr"][j]
        w["inter"] = inter
        return st

    def finish():
        gain = norm_ref[...]
        for j in chunks:
            o = w["inter"][j] + w["intra"][j]
            y = o * lax.rsqrt(jnp.mean(o * o, axis=-1, keepdims=True) + EPS) * gain
            z = z_ref[rows[j], :]
            o_ref[rows[j], :] = (y * (z * jax.nn.sigmoid(z))).astype(o_ref.dtype)

    return gates, cumsums, operands, scores, products, state_pass, finish


def _hgrn_kernel(q_ref, f_ref, v_ref, z_ref, lb_ref, norm_ref, cum_ref, o_ref, *, group):
    seq, dk = q_ref.shape

    def block(cb, st):
        rows = [pl.ds(pl.multiple_of((cb * group + j) * HGRN_CHUNK, HGRN_CHUNK), HGRN_CHUNK)
                for j in range(group)]
        gates, cumsums, operands, scores, products, state_pass, finish = _hgrn_stages(
            q_ref, f_ref, v_ref, z_ref, lb_ref, norm_ref, cum_ref, o_ref, rows)
        gates()
        cumsums()
        operands()
        scores()
        products()
        st = state_pass(st)
        finish()
        return st

    lax.fori_loop(0, seq // (HGRN_CHUNK * group), block, jnp.zeros((dk, dk), F32))


def _hgrn_patterns():
    C, SUB = HGRN_CHUNK, HGRN_SUB
    t = np.arange(C)[:, None]
    s = np.arange(C)[None, :]
    return (((t // SUB) == (s // SUB)) & (s <= t)).astype(np.float32)


def _hgrn(streams, lower_bounds, norm, *, batch, seq):
    dk = HGRN_HEAD_DIM
    heads = norm.shape[1] // dk
    cum = jnp.asarray(_hgrn_patterns(), dtype=BF16)
    col = lambda name: (lambda b, h, o=streams[name][1] // dk: (b, o + h))
    return pl.pallas_call(
        functools.partial(_hgrn_kernel, group=16),
        grid=(batch, heads),
        in_specs=[
            pl.BlockSpec((seq, dk), col("q_h")),
            pl.BlockSpec((seq, dk), col("f_h")),
            pl.BlockSpec((seq, dk), col("i_h")),
            pl.BlockSpec((seq, dk), col("z_h")),
            pl.BlockSpec((lower_bounds.shape[0], dk), lambda b, h: (0, h)),
            pl.BlockSpec((1, dk), lambda b, h: (0, h)),
            pl.BlockSpec(cum.shape, lambda b, h: (0, 0)),
        ],
        out_specs=pl.BlockSpec((seq, dk), lambda b, h: (b, h)),
        out_shape=jax.ShapeDtypeStruct((batch * seq, heads * dk), BF16),
        compiler_params=pltpu.CompilerParams(
            dimension_semantics=("arbitrary", "arbitrary"),
            vmem_limit_bytes=VMEM_LIMIT),
        name="hgrn",
    )(streams["q_h"][0], streams["f_h"][0], streams["i_h"][0], streams["z_h"][0], lower_bounds, norm, cum)


def _outproj_kernel(x_ref, oa_ref, oh_ref, wa_ref, wh_ref, g_ref, o_ref):
    y = x_ref[...] + jnp.dot(oa_ref[...], wa_ref[...], preferred_element_type=F32)
    y = y + jnp.dot(oh_ref[...], wh_ref[...], preferred_element_type=F32)
    ms = jnp.mean(y * y, axis=-1, keepdims=True)
    o_ref[...] = y * lax.rsqrt(ms + EPS) * g_ref[...]


def _outproj(x2, o_a, o_h, wa, wh, g, *, tm):
    M, D = x2.shape
    const = lambda a: pl.BlockSpec(a.shape, lambda i: (0,) * a.ndim)
    return pl.pallas_call(
        _outproj_kernel,
        grid=(M // tm,),
        in_specs=[
            pl.BlockSpec((tm, D), lambda i: (i, 0)),
            pl.BlockSpec((tm, o_a.shape[1]), lambda i: (i, 0)),
            pl.BlockSpec((tm, o_h.shape[1]), lambda i: (i, 0)),
            const(wa), const(wh), const(g),
        ],
        out_specs=pl.BlockSpec((tm, D), lambda i: (i, 0)),
        out_shape=jax.ShapeDtypeStruct((M, D), F32),
        compiler_params=pltpu.CompilerParams(
            dimension_semantics=("arbitrary",),
            vmem_limit_bytes=VMEM_LIMIT),
        name="outproj",
    )(x2, o_a, o_h, wa, wh, g)


def kernel(x, norm_in, w_in, cmp_pe_k, cmp_w1_k, cmp_w2_k, cmp_pe_v, cmp_w1_v, cmp_w2_v,
           lower_bounds, nsa_out_norm, hgrn_out_norm, w_out, final_norm):
    B, S, D = x.shape
    assert norm_in.shape[0] == 1, "single-layer problem"
    nsa_w = nsa_out_norm.shape[1]
    hgrn_w = hgrn_out_norm.shape[1]
    dh = NSA_HEAD_DIM
    G = NSA_KV_HEADS
    heads = nsa_w // dh
    kvw = G * dh
    n_gate = 3 * heads
    n_cmp = (S - CMP_BLOCK) // CMP_STRIDE + 1
    n_half = S // CMP_STRIDE
    assert S % NSA_TILE == 0 and WINDOW % NSA_TILE == 0 and n_half <= LANES and 2 * dh == LANES

    names = ["q_a", "k_cmp", "v_cmp", "k_slc", "v_slc", "k_win", "v_win", "gate", "z_a",
             "q_h", "f_h", "i_h", "z_h"]
    widths = [nsa_w] + [kvw] * 6 + [n_gate, nsa_w] + [hgrn_w] * 4
    starts = dict(zip(names, np.cumsum([0] + widths[:-1]).tolist()))
    wd = dict(zip(names, widths))

    paired = lambda kname, vname: [(starts[n] + g * dh, dh) for g in range(G) for n in (kname, vname)]
    whole = lambda n: [(starts[n], wd[n])]
    groups = [[("q_a", whole("q_a")), ("q_h", whole("q_h")), ("i_h", whole("i_h")),
               ("cmp", paired("k_cmp", "v_cmp")), ("slc", paired("k_slc", "v_slc")),
               ("win", paired("k_win", "v_win"))],
              [("z_a", whole("z_a")), ("f_h", whole("f_h")), ("z_h", whole("z_h"))]]
    plan, col_of, off = [], {}, 0
    for group in groups:
        for n, pieces in group:
            col_of[n] = off
            for src, width in pieces:
                plan.append((off, src, width, dh ** -0.5 if n == "q_a" else 1.0))
                off += width
    n_main = off
    n16_cols = col_of["z_a"]
    plan.append((n_main, starts["gate"], n_gate, 1.0))
    w_all = _relayout_weights(w_in[0].T, tuple(plan), n_main + LANES)

    x2 = x.reshape(B * S, D)
    proj16, proj32, gate = _proj(x2, norm_in, w_all, tm=min(1024, B * S), tn=1536,
                                 n_main=n_main, n16_cols=n16_cols)
    streams = {}
    for arr, group, base in ((proj16, groups[0], 0), (proj32, groups[1], n16_cols)):
        for n, _ in group:
            streams[n] = (arr, col_of[n] - base)

    featk, featw, featc, qfeat, qrel, seg, mmap, place = _nsa_tables(S, n_half, n_cmp, heads)

    def w1_halves(w1k, w1v):
        hk = w1k.shape[1]
        k3 = w1k.reshape(2, CMP_STRIDE, dh, hk)
        v3 = w1v.reshape(2, CMP_STRIDE, dh, hk)
        zk = jnp.zeros_like(k3[0])
        top = lambda a: jnp.concatenate([a, zk], axis=-1)
        bot = lambda a: jnp.concatenate([zk, a], axis=-1)
        half = lambda i: jnp.concatenate([top(k3[i]), bot(v3[i])], axis=1).reshape(CMP_STRIDE * 2 * dh, 2 * hk)
        return half(0).astype(BF16), half(1).astype(BF16)

    wp, wq = w1_halves(cmp_w1_k[0], cmp_w1_v[0])
    zk = jnp.zeros_like(cmp_w2_k[0])
    w2 = jnp.concatenate([jnp.concatenate([cmp_w2_k[0], zk], axis=1),
                          jnp.concatenate([zk, cmp_w2_v[0]], axis=1)], axis=0).astype(BF16)
    pe = jnp.concatenate([cmp_pe_k[0].reshape(2, CMP_STRIDE, dh), cmp_pe_v[0].reshape(2, CMP_STRIDE, dh)],
                         axis=-1).reshape(2, CMP_STRIDE * 2 * dh)
    kc, vc = _compress(streams["cmp"][0], pe, wp, wq, w2, jnp.tile(featc, (G, 1)),
                       batch=B, seq=S, col0=streams["cmp"][1], n_half=n_half)

    o_a = _nsa(streams, gate, kc, vc, featk, featw, qfeat, qrel, seg, mmap, place, nsa_out_norm,
               batch=B, seq=S, n_cmp=n_cmp)
    o_h = _hgrn(streams, lower_bounds, hgrn_out_norm, batch=B, seq=S)

    wo = w_out[0].astype(BF16)
    out = _outproj(x2, o_a, o_h, wo[:nsa_w], wo[nsa_w:], final_norm.reshape(1, D), tm=512)
    return out.reshape(B, S, D)
```

```python
import functools

import ml_dtypes
import numpy as np
import jax
import jax.numpy as jnp
from jax import lax
from jax.experimental import pallas as pl
from jax.experimental.pallas import tpu as pltpu

F32 = jnp.float32
BF16 = jnp.bfloat16

EPS = 1e-6
NEG_INF = -1e30
LOG2E = 1.4426950408889634

NSA_HEAD_DIM = 64
NSA_KV_HEADS = 4
CMP_BLOCK = 32
CMP_STRIDE = 16
SEL_BLOCK = 64
SEL_TOP = 8
SEL_BONUS = 1.0e4
WINDOW = 512
HGRN_HEAD_DIM = 128
HGRN_CHUNK = 64
HGRN_SUB = 16

LANES = 128
VMEM_LIMIT = 56 * 1024 * 1024
NSA_TILE = 256

SEL_LANE0 = NSA_HEAD_DIM
MAX_SEL_BLOCKS = 32
POS_LANE0 = SEL_LANE0 + MAX_SEL_BLOCKS
N_SPLIT = 4
FLAG_LANE = POS_LANE0 + 2 * N_SPLIT
SHIFT_LANE0 = FLAG_LANE + 1
REL_LANE0 = SHIFT_LANE0 + N_SPLIT
SCORE_BOUND = 100.0
SAFETY = 1.02


def _nt_dot(a, b):
    return lax.dot_general(a, b, (((1,), (1,)), ((), ())), preferred_element_type=F32)


def _tn_dot(a, b):
    return lax.dot_general(a, b, (((0,), (0,)), ((), ())), preferred_element_type=F32)


def _split3(x):
    hi = x.astype(BF16)
    r1 = x - hi.astype(F32)
    mid = r1.astype(BF16)
    lo = (r1 - mid.astype(F32)).astype(BF16)
    return hi, mid, lo


def _proj_kernel(x_ref, g_ref, w_ref, wg_ref, o16_ref, o32_ref, og_ref, h_ref, *, row_chunk, n16):
    j = pl.program_id(1)

    @pl.when(j == 0)
    def _():
        n_chunks = x_ref.shape[0] // row_chunk

        def body(c, carry):
            rows = pl.ds(pl.multiple_of(c * row_chunk, row_chunk), row_chunk)
            x = x_ref[rows, :]
            ms = jnp.mean(x * x, axis=-1, keepdims=True)
            h_ref[rows, :] = (x * lax.rsqrt(ms + EPS) * g_ref[...]).astype(BF16)
            return carry

        lax.fori_loop(0, n_chunks, body, 0)
        og_ref[...] = jnp.dot(h_ref[...], wg_ref[...], preferred_element_type=F32)

    @pl.when(j < n16)
    def _():
        o16_ref[...] = jnp.dot(h_ref[...], w_ref[...], preferred_element_type=F32).astype(BF16)

    @pl.when(j >= n16)
    def _():
        o32_ref[...] = jnp.dot(h_ref[...], w_ref[...], preferred_element_type=F32)


def _relayout_kernel(wt_ref, o_ref, *, plan):
    o_ref[:, o_ref.shape[1] - LANES:] = jnp.zeros((o_ref.shape[0], LANES), o_ref.dtype)
    for dst, src, width, scale in plan:
        o_ref[:, dst:dst + width] = (wt_ref[src:src + width, :].T * scale).astype(o_ref.dtype)


def _relayout_weights(wt, plan, n_out, *, rows=256):
    n_in, D = wt.shape
    return pl.pallas_call(
        functools.partial(_relayout_kernel, plan=plan),
        grid=(D // rows,),
        in_specs=[pl.BlockSpec((n_in, rows), lambda i: (0, i))],
        out_specs=pl.BlockSpec((rows, n_out), lambda i: (i, 0)),
        out_shape=jax.ShapeDtypeStruct((D, n_out), BF16),
        compiler_params=pltpu.CompilerParams(
            dimension_semantics=("arbitrary",),
            vmem_limit_bytes=VMEM_LIMIT),
        name="relayout",
    )(wt)


def _proj(x2, g, w, *, tm, tn, n_main, n16_cols):
    M, D = x2.shape
    N = n_main
    NG = w.shape[1] - n_main
    n16 = n16_cols // tn
    assert n16 * tn == n16_cols and N % tn == 0 and N % NG == 0
    return pl.pallas_call(
        functools.partial(_proj_kernel, row_chunk=128, n16=n16),
        grid=(M // tm, N // tn),
        in_specs=[
            pl.BlockSpec((tm, D), lambda i, j: (i, 0)),
            pl.BlockSpec((1, D), lambda i, j: (0, 0)),
            pl.BlockSpec((D, tn), lambda i, j: (0, j)),
            pl.BlockSpec((D, NG), lambda i, j: (0, N // NG)),
        ],
        out_specs=[
            pl.BlockSpec((tm, tn), lambda i, j: (i, jnp.minimum(j, n16 - 1))),
            pl.BlockSpec((tm, tn), lambda i, j: (i, jnp.maximum(j - n16, 0))),
            pl.BlockSpec((tm, NG), lambda i, j: (i, 0)),
        ],
        out_shape=[
            jax.ShapeDtypeStruct((M, n16_cols), BF16),
            jax.ShapeDtypeStruct((M, N - n16_cols), F32),
            jax.ShapeDtypeStruct((M, NG), F32),
        ],
        scratch_shapes=[pltpu.VMEM((tm, D), BF16)],
        compiler_params=pltpu.CompilerParams(
            dimension_semantics=("arbitrary", "arbitrary"),
            vmem_limit_bytes=VMEM_LIMIT),
        name="proj",
    )(x2, g, w, w)


def _compress_kernel(c0_ref, c1_ref, c2_ref, c3_ref, pe_ref, wp_ref, wq_ref, w2_ref, feat_ref,
                     kc_ref, vc_ref, x_ref, c32_ref, *, n_half):
    for g, c_ref in enumerate((c0_ref, c1_ref, c2_ref, c3_ref)):
        c32_ref[...] = c_ref[...].astype(F32)
        for l in range(CMP_STRIDE):
            x_ref[g * n_half:(g + 1) * n_half, l * LANES:(l + 1) * LANES] = (
                c32_ref[pl.ds(l, n_half, stride=CMP_STRIDE), :])
    x = x_ref[...]
    rows = x.shape[0]
    first = jnp.dot((x + pe_ref[0:1, :]).astype(BF16), wp_ref[...], preferred_element_type=F32)
    second = jnp.dot((x + pe_ref[1:2, :]).astype(BF16), wq_ref[...], preferred_element_type=F32)
    hidden = first + pltpu.roll(second, rows - 1, 0)
    out = jnp.dot(jax.nn.gelu(hidden).astype(BF16), w2_ref[...], preferred_element_type=F32)
    lane = lax.broadcasted_iota(jnp.int32, out.shape, 1)
    dh = NSA_HEAD_DIM
    kc_ref[...] = jnp.where(lane < dh, out.astype(BF16), feat_ref[...])
    for g in range(NSA_KV_HEADS):
        vc_ref[g * dh:(g + 1) * dh, :] = out[g * n_half:(g + 1) * n_half, :].T[dh:, :].astype(BF16)


def _compress(proj, pe, wp, wq, w2, feat, *, batch, seq, col0, n_half):
    rows = NSA_KV_HEADS * n_half
    rows_t = NSA_KV_HEADS * NSA_HEAD_DIM
    const = lambda a: pl.BlockSpec(a.shape, lambda b: (0,) * a.ndim)
    stream = lambda g: pl.BlockSpec((seq, LANES), lambda b, o=col0 // LANES + g: (b, o))
    return pl.pallas_call(
        functools.partial(_compress_kernel, n_half=n_half),
        grid=(batch,),
        in_specs=[stream(0), stream(1), stream(2), stream(3),
                  const(pe), const(wp), const(wq), const(w2), const(feat)],
        out_specs=[
            pl.BlockSpec((rows, LANES), lambda b: (b, 0)),
            pl.BlockSpec((rows_t, n_half), lambda b: (b, 0)),
        ],
        out_shape=[
            jax.ShapeDtypeStruct((batch * rows, LANES), BF16),
            jax.ShapeDtypeStruct((batch * rows_t, n_half), BF16),
        ],
        scratch_shapes=[pltpu.VMEM((rows, CMP_STRIDE * LANES), F32), pltpu.VMEM((seq, LANES), F32)],
        compiler_params=pltpu.CompilerParams(
            dimension_semantics=("arbitrary",),
            vmem_limit_bytes=VMEM_LIMIT),
        name="compress",
    )(proj, proj, proj, proj, pe, wp, wq, w2, feat)


def _nsa_kernel(q_ref, qall_ref, z_ref, gate_ref, slc_ref, win_ref, kc_ref, vct_ref, featk_ref, featw_ref,
                qfeat_ref, qrel_ref, seg_ref, mmap_ref, place_ref, norm_ref, o_ref,
                ksel_ref, vselt_ref, kwin_ref, vwint_ref, gt_ref, bounded_ref, *, n_cmp):
    g = pl.program_id(1)
    qi = pl.program_id(2)
    tq = q_ref.shape[0]
    tk = tq
    seq = slc_ref.shape[0]
    dh = NSA_HEAD_DIM
    rep = q_ref.shape[1] // dh
    n_half = kc_ref.shape[0]
    n_pad = WINDOW // tk

    @pl.when(qi == 0)
    def _():
        lane = lax.broadcasted_iota(jnp.int32, (seq, LANES), 1)

        def max_sq_norm(x, width):
            s = jnp.dot((x * x).astype(BF16), seg_ref[0:width, :], preferred_element_type=F32)
            return jnp.max(s) * SAFETY

        kv = slc_ref[...]
        k2_sel = max_sq_norm(jnp.where(lane < dh, kv.astype(F32), 0.0), LANES)
        ksel_ref[...] = jnp.where(lane < dh, kv, featk_ref[...])
        vt = kv.astype(F32).T[dh:, :].astype(BF16)
        for kt in range(seq // tk):
            vselt_ref[kt] = vt[:, kt * tk:(kt + 1) * tk]
        kv = win_ref[...]
        k2_win = max_sq_norm(jnp.where(lane < dh, kv.astype(F32), 0.0), LANES)
        kwin_ref[0:WINDOW, :] = featw_ref[0:WINDOW, :]
        kwin_ref[WINDOW:, :] = jnp.where(lane < dh, kv, featw_ref[WINDOW:, :])
        vt = kv.astype(F32).T[dh:, :].astype(BF16)
        for kt in range(n_pad):
            vwint_ref[kt] = jnp.zeros((dh, tk), BF16)
        for kt in range(seq // tk):
            vwint_ref[n_pad + kt] = vt[:, kt * tk:(kt + 1) * tk]
        q2 = max_sq_norm(qall_ref[...].astype(F32), rep * dh) * (LOG2E * LOG2E)
        limit = SCORE_BOUND * SCORE_BOUND
        bounded_ref[0] = ((q2 * k2_sel <= limit) & (q2 * k2_win <= limit)).astype(jnp.int32)

    heads = lambda a: jnp.concatenate([a] * rep, axis=1)
    sub8 = lambda a: a.reshape(a.shape[0] // 8, 8, a.shape[1])

    def softmax_pv(lgs, v_tiles, bounded):
        if bounded:
            m = 0.0
        else:
            m8 = functools.reduce(jnp.maximum, [jnp.max(sub8(lg), axis=0) for lg in lgs])
            m = jnp.max(m8, axis=0, keepdims=True)
        l8 = jnp.zeros((8, lgs[0].shape[1]), F32)
        acc = jnp.zeros((dh, lgs[0].shape[1]), F32)
        for lg, vt in zip(lgs, v_tiles):
            pt = jnp.exp2(lg) if bounded else jnp.exp2(lg - m)
            l8 = l8 + jnp.sum(sub8(pt), axis=0)
            acc = acc + jnp.dot(vt, pt.astype(BF16), preferred_element_type=F32)
        return acc / jnp.sum(l8, axis=0, keepdims=True)

    def step(n, bounded):
        static = isinstance(n, int)
        assert static == bounded
        t0 = n * tq
        lane = lax.broadcasted_iota(jnp.int32, (tq, LANES), 1)
        key_i = lax.broadcasted_iota(jnp.int32, (tk, tq), 0)
        qry_i = lax.broadcasted_iota(jnp.int32, (tk, tq), 1)
        causal_1 = jnp.where(key_i <= qry_i, 0.0, NEG_INF)
        causal_bias = heads(causal_1)
        band_bias = heads(jnp.where(key_i > qry_i, 0.0, NEG_INF))

        qfeat = qfeat_ref[0, n]
        rel_lanes = (lane >= REL_LANE0) & (lane < REL_LANE0 + 3)
        qa = []
        for r in range(rep):
            qcol = q_ref[:, (r // 2) * LANES:(r // 2 + 1) * LANES].astype(F32)
            if r % 2:
                qcol = pltpu.roll(qcol, dh, 1)
            feat = jnp.where(rel_lanes, qrel_ref[0, r * tq:(r + 1) * tq, :].astype(F32), qfeat[r:r + 1, :])
            qa.append(jnp.where(lane < dh, qcol * LOG2E, feat))
        q0 = jnp.concatenate(qa, axis=0).astype(BF16)

        gt_ref[...] = jax.nn.sigmoid(gate_ref[...]).T
        out_scale = []
        for pair in range(rep // 2):
            z = z_ref[:, pair * LANES:(pair + 1) * LANES]
            out_scale.append(norm_ref[:, pair * LANES:(pair + 1) * LANES] * (z * jax.nn.sigmoid(z)))

        nc = min(n_half, (t0 + tq) // CMP_STRIDE) if static else n_half
        n_idx = lax.broadcasted_iota(jnp.int32, (nc, tq), 0)
        t_idx = t0 + lax.broadcasted_iota(jnp.int32, (nc, tq), 1)
        valid_c = heads((CMP_STRIDE * n_idx + (CMP_BLOCK - 1) <= t_idx) & (n_idx < n_cmp))
        lg_c = jnp.where(valid_c, _nt_dot(kc_ref[0:nc, :], q0), NEG_INF)

        n_wt = n_pad + 1
        lgs = []
        for w in range(n_wt):
            rows = (slice(t0 + w * tk, t0 + (w + 1) * tk) if static
                    else pl.ds(pl.multiple_of(t0 + w * tk, tk), tk))
            lg = _nt_dot(kwin_ref[rows, :], q0)
            if w == 0:
                lg = lg + band_bias
            if w == n_wt - 1:
                lg = lg + causal_bias
            lgs.append(lg)
        o_win = softmax_pv(lgs, [vwint_ref[n + w] for w in range(n_wt)], bounded)

        e = jnp.exp2(lg_c - jnp.max(lg_c, axis=0, keepdims=True))
        p = jnp.where(valid_c, e / jnp.sum(e, axis=0, keepdims=True), 0.0)
        if nc < n_half:
            p = jnp.concatenate([p, jnp.zeros((n_half - nc, rep * tq), F32)], axis=0)
        o_cmp = jnp.dot(vct_ref[...], p.astype(BF16), preferred_element_type=F32)
        p_sum = p[:, 0:tq]
        for r in range(1, rep):
            p_sum = p_sum + p[:, r * tq:(r + 1) * tq]

        nb = min(MAX_SEL_BLOCKS, (t0 + tq) // SEL_BLOCK) if static else MAX_SEL_BLOCKS
        mm = mmap_ref[...]
        hi, mid, lo = _split3(p_sum)
        p_slc = (jnp.dot(mm, hi, preferred_element_type=F32) + jnp.dot(mm, mid, preferred_element_type=F32)
                 + jnp.dot(mm, lo, preferred_element_type=F32))[0:MAX_SEL_BLOCKS]
        if static:
            raw = [_nt_dot(ksel_ref[kt * tk:(kt + 1) * tk, :], q0) for kt in range(n + 1)]
        jj = lax.broadcasted_iota(jnp.int32, (MAX_SEL_BLOCKS, tq), 0)
        tt = t0 + lax.broadcasted_iota(jnp.int32, (MAX_SEL_BLOCKS, tq), 1)
        cur = lax.shift_right_logical(tt, int(np.log2(SEL_BLOCK)))
        forced = (jj == 0) | (jj == cur) | (jj == cur - 1)
        future = jj > cur
        score = jnp.where(future, -1.0, p_slc + jnp.where(forced, SEL_BONUS, 0.0))
        rank = jnp.zeros((MAX_SEL_BLOCKS, tq), jnp.int32)
        for i in range(nb):
            other = jnp.broadcast_to(score[i:i + 1, :], score.shape)
            beats = (other > score) | ((other == score) & (jj > i))
            rank = rank + beats.astype(jnp.int32)
        keep = (rank < SEL_TOP) & jnp.logical_not(future)

        if static:
            blocks_per_tile = tk // SEL_BLOCK
            block_bias = jnp.where(keep, 0.0, NEG_INF)
            lgs = []
            for kt in range(n + 1):
                b = jnp.concatenate(
                    [jnp.broadcast_to(block_bias[blocks_per_tile * kt + j:blocks_per_tile * kt + j + 1, :],
                                      (SEL_BLOCK, tq)) for j in range(blocks_per_tile)], axis=0)
                lgs.append(raw[kt] + heads(b + causal_1 if kt == n else b))
            o_slc = softmax_pv(lgs, [vselt_ref[kt] for kt in range(n + 1)], bounded)
        else:
            chosen = jnp.where(keep, 1.0, 0.0).astype(BF16)
            placed = _tn_dot(chosen, place_ref[...])
            sel_lanes = (lane >= SEL_LANE0) & (lane < SEL_LANE0 + MAX_SEL_BLOCKS)
            sel_bias = (placed - 1.0) * (-NEG_INF)
            qs = jnp.concatenate([jnp.where(sel_lanes, sel_bias, a) for a in qa], axis=0).astype(BF16)

            def sel_tile(kt, carry, bias=None):
                m, l, acc = carry
                lg = _nt_dot(ksel_ref[pl.ds(pl.multiple_of(kt * tk, tk), tk), :], qs)
                if bias is not None:
                    lg = lg + bias
                m_new = jnp.maximum(m, jnp.max(lg, axis=0, keepdims=True))
                alpha = jnp.exp2(m - m_new)
                pt = jnp.exp2(lg - m_new)
                return (m_new, alpha * l + jnp.sum(pt, axis=0, keepdims=True),
                        alpha * acc + jnp.dot(vselt_ref[kt], pt.astype(BF16), preferred_element_type=F32))

            init = (jnp.full((1, rep * tq), NEG_INF, F32), jnp.zeros((1, rep * tq), F32),
                    jnp.zeros((dh, rep * tq), F32))
            _, l, acc = sel_tile(n, lax.fori_loop(0, n, sel_tile, init), causal_bias)
            o_slc = acc / l

        ys = []
        for r in range(rep):
            cols = slice(r * tq, (r + 1) * tq)
            gate = lambda c: gt_ref[pl.ds(3 * (g * rep + r) + c, 1), :]
            o = gate(0) * o_cmp[:, cols] + gate(1) * o_slc[:, cols] + gate(2) * o_win[:, cols]
            ys.append(o * lax.rsqrt(jnp.mean(o * o, axis=0, keepdims=True) + EPS))
        for pair in range(rep // 2):
            cols = slice(pair * LANES, (pair + 1) * LANES)
            y = jnp.concatenate([ys[2 * pair], ys[2 * pair + 1]], axis=0).T
            o_ref[:, cols] = (y * out_scale[pair]).astype(o_ref.dtype)

    @pl.when(bounded_ref[0] == 1)
    def _():
        lax.switch(qi, [functools.partial(step, n, True) for n in range(seq // tq)])

    @pl.when(bounded_ref[0] == 0)
    def _():
        step(qi, False)


def _nsa(streams, gate, kc, vc, featk, featw, qfeat, qrel, seg, mmap, place, norm, *, batch, seq, n_cmp):
    tq = NSA_TILE
    nq = seq // tq
    G = NSA_KV_HEADS
    gw = norm.shape[1] // G
    n_half = kc.shape[0] // (batch * G)
    const = lambda a: pl.BlockSpec(a.shape, lambda b, g, i: (0,) * a.ndim)
    stream = lambda name: pl.BlockSpec((seq, LANES),
                                       lambda b, g, i, o=streams[name][1] // LANES: (b, o + g))
    tile = lambda name: pl.BlockSpec((tq, gw), lambda b, g, i, o=streams[name][1] // gw: (b * nq + i, o + g))
    return pl.pallas_call(
        functools.partial(_nsa_kernel, n_cmp=n_cmp),
        grid=(batch, G, nq),
        in_specs=[
            tile("q_a"),
            pl.BlockSpec((seq, gw), lambda b, g, i, o=streams["q_a"][1] // gw: (b, o + g)),
            tile("z_a"),
            pl.BlockSpec((tq, gate.shape[1]), lambda b, g, i: (b * nq + i, 0)),
            stream("slc"), stream("win"),
            pl.BlockSpec((n_half, LANES), lambda b, g, i: (b * G + g, 0)),
            pl.BlockSpec((NSA_HEAD_DIM, n_half), lambda b, g, i: (b * G + g, 0)),
            const(featk), const(featw),
            pl.BlockSpec((1,) + qfeat.shape[1:], lambda b, g, i: (g, 0, 0, 0)),
            pl.BlockSpec((1,) + qrel.shape[1:], lambda b, g, i: (g, 0, 0)),
            const(seg), const(mmap), const(place),
            pl.BlockSpec((1, gw), lambda b, g, i: (0, g)),
        ],
        out_specs=pl.BlockSpec((tq, gw), lambda b, g, i: (b * nq + i, g)),
        out_shape=jax.ShapeDtypeStruct((batch * seq, G * gw), BF16),
        scratch_shapes=[
            pltpu.VMEM((seq, LANES), BF16),
            pltpu.VMEM((seq // tq, NSA_HEAD_DIM, tq), BF16),
            pltpu.VMEM((seq + WINDOW, LANES), BF16),
            pltpu.VMEM(((seq + WINDOW) // tq, NSA_HEAD_DIM, tq), BF16),
            pltpu.VMEM((gate.shape[1], tq), F32),
            pltpu.SMEM((1,), jnp.int32),
        ],
        compiler_params=pltpu.CompilerParams(
            dimension_semantics=("arbitrary", "arbitrary", "arbitrary"),
            vmem_limit_bytes=VMEM_LIMIT),
        name="nsa",
    )(streams["q_a"][0], streams["q_a"][0], streams["z_a"][0], gate, streams["slc"][0], streams["win"][0],
      kc, vc, featk, featw, qfeat, qrel, seg, mmap, place, norm)


def _bf16_terms(x, n):
    terms, rest = [], np.asarray(x, np.float64)
    for _ in range(n):
        t = rest.astype(np.float32).astype(ml_dtypes.bfloat16).astype(np.float64)
        terms.append(t.astype(np.float32))
        rest = rest - t
    return terms


def _nsa_tables(seq, n_half, n_cmp, heads):
    assert seq // SEL_BLOCK <= MAX_SEL_BLOCKS and FLAG_LANE < LANES

    def key_features(pos, onehot_blocks):
        f = np.zeros((len(pos), LANES), np.float32)
        if onehot_blocks:
            f[np.arange(len(pos)), SEL_LANE0 + pos // SEL_BLOCK] = 1.0
        f[:, POS_LANE0:POS_LANE0 + N_SPLIT] = ((pos // 64) * 64)[:, None]
        f[:, POS_LANE0 + N_SPLIT:POS_LANE0 + 2 * N_SPLIT] = (pos % 64)[:, None]
        return f

    assert REL_LANE0 + 3 <= LANES
    featk = key_features(np.arange(seq), True)
    featk[:, SHIFT_LANE0:REL_LANE0 + 3] = 1.0
    featw = np.concatenate([np.zeros((WINDOW, LANES), np.float32), key_features(np.arange(seq), False)])
    featw[:WINDOW, FLAG_LANE] = NEG_INF
    featw[WINDOW:, SHIFT_LANE0:REL_LANE0 + 3] = 1.0
    featc = key_features(CMP_STRIDE * np.arange(n_half) + CMP_BLOCK - 1, False)
    featc[n_cmp:, FLAG_LANE] = NEG_INF

    slopes = (2.0 ** (-8.0 * np.arange(1, heads + 1) / heads)).astype(np.float32).astype(np.float64)
    slopes2 = slopes * LOG2E
    terms = _bf16_terms(slopes2, N_SPLIT)
    rep = heads // NSA_KV_HEADS
    nq = seq // NSA_TILE
    qfeat = np.zeros((NSA_KV_HEADS, nq, 8, LANES), np.float32)
    qrel = np.zeros((NSA_KV_HEADS, rep * NSA_TILE, LANES), np.float32)
    for h in range(heads):
        g, r = divmod(h, rep)
        for i, t in enumerate(terms):
            qfeat[g, :, r, POS_LANE0 + i] = t[h]
            qfeat[g, :, r, POS_LANE0 + N_SPLIT + i] = t[h]
        qfeat[g, :, r, FLAG_LANE] = 1.0
        for i, t in enumerate(_bf16_terms(-slopes2[h] * NSA_TILE * np.arange(nq), N_SPLIT)):
            qfeat[g, :, r, SHIFT_LANE0 + i] = t
        for i, t in enumerate(_bf16_terms(-slopes2[h] * np.arange(NSA_TILE), 3)):
            qrel[g, r * NSA_TILE:(r + 1) * NSA_TILE, REL_LANE0 + i] = t
    seg = np.zeros((2 * LANES, LANES), np.float32)
    seg[np.arange(2 * LANES), np.arange(2 * LANES) // NSA_HEAD_DIM] = 1.0

    cs = CMP_STRIDE * np.arange(n_half)[None, :]
    ss = SEL_BLOCK * np.arange(LANES)[:, None]
    overlap = np.clip(np.minimum(cs + CMP_BLOCK, ss + SEL_BLOCK) - np.maximum(cs, ss), 0, None)
    mmap = (overlap / CMP_BLOCK) * (np.arange(n_half)[None, :] < n_cmp) * (ss < seq)
    place = np.zeros((MAX_SEL_BLOCKS, LANES), np.float32)
    place[np.arange(MAX_SEL_BLOCKS), SEL_LANE0 + np.arange(MAX_SEL_BLOCKS)] = 1.0
    bf = lambda a: jnp.asarray(a, dtype=BF16)
    return bf(featk), bf(featw), bf(featc), jnp.asarray(qfeat), bf(qrel), bf(seg), bf(mmap), bf(place)


def _hgrn_stages(q_ref, f_ref, v_ref, z_ref, lb_ref, norm_ref, cum_ref, o_ref, rows):
    C, SUB = HGRN_CHUNK, HGRN_SUB
    n_sub = C // SUB
    chunks = range(len(rows))
    sub = lambda x, i: x[i * SUB:(i + 1) * SUB]
    w = {}

    def gates():
        lbr = lb_ref[...]
        e = jnp.exp(lbr - jnp.max(lbr, axis=0, keepdims=True))
        lb = e[0:1, :] / jnp.sum(e, axis=0, keepdims=True)
        w["v16"] = [v_ref[rows[j], :].astype(BF16) for j in chunks]
        f = [lb + (1.0 - lb) * jax.nn.sigmoid(f_ref[rows[j], :]) for j in chunks]
        w["k"] = [1.0 - f[j] for j in chunks]
        w["parts"] = [_split3(jnp.log(f[j])) for j in chunks]

    def cumsums():
        cum = cum_ref[...]
        parts = w["parts"]
        w["a"] = [(jnp.dot(cum, parts[j][0], preferred_element_type=F32)
                   + jnp.dot(cum, parts[j][1], preferred_element_type=F32)
                   + jnp.dot(cum, parts[j][2], preferred_element_type=F32)) for j in chunks]

    def operands():
        a, k = w["a"], w["k"]
        q1, k1, k2, qb, k3, start, dec = [], [], [], [], [], [], []
        for j in chunks:
            q1.append(q_ref[rows[j], :] * jnp.exp(a[j]))
            k1.append(k[j] * jnp.exp(-a[j]))
            tot = [a[j][(i + 1) * SUB - 1:(i + 1) * SUB] for i in range(n_sub)]
            s = [jnp.zeros_like(tot[0])]
            for i in range(n_sub):
                s.append(s[i] + tot[i])
            start.append(s)
            dec.append(jnp.exp(s[n_sub]))
            qb.append(jnp.concatenate([sub(q1[j], i) * jnp.exp(s[i]) for i in range(n_sub)], axis=0)
                      .astype(BF16))
            k2.append([sub(k1[j], i) * jnp.exp(tot[i]) for i in range(n_sub)])
            k3.append(jnp.concatenate([sub(k1[j], i) * jnp.exp(s[n_sub] - s[i]) for i in range(n_sub)],
                                      axis=0).astype(BF16))
        w.update(q1=q1, k1=k1, k2=k2, qb=qb, k3=k3, start=start, dec=dec)

    def scores():
        q1, k1, k2, start = w["q1"], w["k1"], w["k2"], w["start"]
        att = []
        for j in chunks:
            row_blocks = []
            for i in range(n_sub):
                rhs = [k2[j][jb] * jnp.exp(start[j][i] - start[j][jb + 1]) if jb < i - 1 else k2[j][jb]
                       for jb in range(i)]
                rhs.append(sub(k1[j], i))
                rhs = jnp.concatenate(rhs, axis=0) if len(rhs) > 1 else rhs[0]
                sc = _nt_dot(sub(q1[j], i).astype(BF16), rhs.astype(BF16))
                width = (i + 1) * SUB
                causal = (lax.broadcasted_iota(jnp.int32, (SUB, width), 1)
                          <= lax.broadcasted_iota(jnp.int32, (SUB, width), 0) + i * SUB)
                row_blocks.append(jnp.where(causal, sc, 0.0).astype(BF16))
            att.append(row_blocks)
        w["att"] = att

    def products():
        att, v16, k3 = w["att"], w["v16"], w["k3"]
        w["intra"] = [jnp.concatenate([jnp.dot(att[j][i], v16[j][0:(i + 1) * SUB],
                                               preferred_element_type=F32)
                                       for i in range(n_sub)], axis=0) for j in chunks]
        w["incr"] = [_tn_dot(v16[j], k3[j]) for j in chunks]

    def state_pass(st):
        inter = []
        for j in chunks:
            inter.append(_nt_dot(w["qb"][j], st.astype(BF16)))
            st = st * w["dec"][j] + w["incr"][j]
        w["inter"] = inter
        return st

    def finish():
        gain = norm_ref[...]
        for j in chunks:
            o = w["inter"][j] + w["intra"][j]
            y = o * lax.rsqrt(jnp.mean(o * o, axis=-1, keepdims=True) + EPS) * gain
            z = z_ref[rows[j], :]
            o_ref[rows[j], :] = (y * (z * jax.nn.sigmoid(z))).astype(o_ref.dtype)

    return gates, cumsums, operands, scores, products, state_pass, finish


def _hgrn_kernel(q_ref, f_ref, v_ref, z_ref, lb_ref, norm_ref, cum_ref, o_ref, *, group):
    seq, dk = q_ref.shape

    def block(cb, st):
        rows = [pl.ds(pl.multiple_of((cb * group + j) * HGRN_CHUNK, HGRN_CHUNK), HGRN_CHUNK)
                for j in range(group)]
        gates, cumsums, operands, scores, products, state_pass, finish = _hgrn_stages(
            q_ref, f_ref, v_ref, z_ref, lb_ref, norm_ref, cum_ref, o_ref, rows)
        gates()
        cumsums()
        operands()
        scores()
        products()
        st = state_pass(st)
        finish()
        return st

    lax.fori_loop(0, seq // (HGRN_CHUNK * group), block, jnp.zeros((dk, dk), F32))


def _hgrn_patterns():
    C, SUB = HGRN_CHUNK, HGRN_SUB
    t = np.arange(C)[:, None]
    s = np.arange(C)[None, :]
    return (((t // SUB) == (s // SUB)) & (s <= t)).astype(np.float32)


def _hgrn(streams, lower_bounds, norm, *, batch, seq):
    dk = HGRN_HEAD_DIM
    heads = norm.shape[1] // dk
    cum = jnp.asarray(_hgrn_patterns(), dtype=BF16)
    col = lambda name: (lambda b, h, o=streams[name][1] // dk: (b, o + h))
    return pl.pallas_call(
        functools.partial(_hgrn_kernel, group=16),
        grid=(batch, heads),
        in_specs=[
            pl.BlockSpec((seq, dk), col("q_h")),
            pl.BlockSpec((seq, dk), col("f_h")),
            pl.BlockSpec((seq, dk), col("i_h")),
            pl.BlockSpec((seq, dk), col("z_h")),
            pl.BlockSpec((lower_bounds.shape[0], dk), lambda b, h: (0, h)),
            pl.BlockSpec((1, dk), lambda b, h: (0, h)),
            pl.BlockSpec(cum.shape, lambda b, h: (0, 0)),
        ],
        out_specs=pl.BlockSpec((seq, dk), lambda b, h: (b, h)),
        out_shape=jax.ShapeDtypeStruct((batch * seq, heads * dk), BF16),
        compiler_params=pltpu.CompilerParams(
            dimension_semantics=("arbitrary", "arbitrary"),
            vmem_limit_bytes=VMEM_LIMIT),
        name="hgrn",
    )(streams["q_h"][0], streams["f_h"][0], streams["i_h"][0], streams["z_h"][0], lower_bounds, norm, cum)


def _outproj_kernel(x_ref, oa_ref, oh_ref, wa_ref, wh_ref, g_ref, o_ref):
    y = x_ref[...] + jnp.dot(oa_ref[...], wa_ref[...], preferred_element_type=F32)
    y = y + jnp.dot(oh_ref[...], wh_ref[...], preferred_element_type=F32)
    ms = jnp.mean(y * y, axis=-1, keepdims=True)
    o_ref[...] = y * lax.rsqrt(ms + EPS) * g_ref[...]


def _outproj(x2, o_a, o_h, wa, wh, g, *, tm):
    M, D = x2.shape
    const = lambda a: pl.BlockSpec(a.shape, lambda i: (0,) * a.ndim)
    return pl.pallas_call(
        _outproj_kernel,
        grid=(M // tm,),
        in_specs=[
            pl.BlockSpec((tm, D), lambda i: (i, 0)),
            pl.BlockSpec((tm, o_a.shape[1]), lambda i: (i, 0)),
            pl.BlockSpec((tm, o_h.shape[1]), lambda i: (i, 0)),
            const(wa), const(wh), const(g),
        ],
        out_specs=pl.BlockSpec((tm, D), lambda i: (i, 0)),
        out_shape=jax.ShapeDtypeStruct((M, D), F32),
        compiler_params=pltpu.CompilerParams(
            dimension_semantics=("arbitrary",),
            vmem_limit_bytes=VMEM_LIMIT),
        name="outproj",
    )(x2, o_a, o_h, wa, wh, g)


def kernel(x, norm_in, w_in, cmp_pe_k, cmp_w1_k, cmp_w2_k, cmp_pe_v, cmp_w1_v, cmp_w2_v,
           lower_bounds, nsa_out_norm, hgrn_out_norm, w_out, final_norm):
    B, S, D = x.shape
    assert norm_in.shape[0] == 1, "single-layer problem"
    nsa_w = nsa_out_norm.shape[1]
    hgrn_w = hgrn_out_norm.shape[1]
    dh = NSA_HEAD_DIM
    G = NSA_KV_HEADS
    heads = nsa_w // dh
    kvw = G * dh
    n_gate = 3 * heads
    n_cmp = (S - CMP_BLOCK) // CMP_STRIDE + 1
    n_half = S // CMP_STRIDE
    assert S % NSA_TILE == 0 and WINDOW % NSA_TILE == 0 and n_half <= LANES and 2 * dh == LANES

    names = ["q_a", "k_cmp", "v_cmp", "k_slc", "v_slc", "k_win", "v_win", "gate", "z_a",
             "q_h", "f_h", "i_h", "z_h"]
    widths = [nsa_w] + [kvw] * 6 + [n_gate, nsa_w] + [hgrn_w] * 4
    starts = dict(zip(names, np.cumsum([0] + widths[:-1]).tolist()))
    wd = dict(zip(names, widths))

    paired = lambda kname, vname: [(starts[n] + g * dh, dh) for g in range(G) for n in (kname, vname)]
    whole = lambda n: [(starts[n], wd[n])]
    groups = [[("q_a", whole("q_a")),
               ("cmp", paired("k_cmp", "v_cmp")), ("slc", paired("k_slc", "v_slc")),
               ("win", paired("k_win", "v_win")), ("q_h", whole("q_h")), ("i_h", whole("i_h"))],
              [("z_a", whole("z_a")), ("f_h", whole("f_h")), ("z_h", whole("z_h"))]]
    plan, col_of, off = [], {}, 0
    for group in groups:
        for n, pieces in group:
            col_of[n] = off
            for src, width in pieces:
                plan.append((off, src, width, dh ** -0.5 if n == "q_a" else 1.0))
                off += width
    n_main = off
    n16_cols = col_of["z_a"]
    plan.append((n_main, starts["gate"], n_gate, 1.0))
    w_all = _relayout_weights(w_in[0].T, tuple(plan), n_main + LANES)

    x2 = x.reshape(B * S, D)
    proj16, proj32, gate = _proj(x2, norm_in, w_all, tm=min(1024, B * S), tn=1536,
                                 n_main=n_main, n16_cols=n16_cols)
    streams = {}
    for arr, group, base in ((proj16, groups[0], 0), (proj32, groups[1], n16_cols)):
        for n, _ in group:
            streams[n] = (arr, col_of[n] - base)

    featk, featw, featc, qfeat, qrel, seg, mmap, place = _nsa_tables(S, n_half, n_cmp, heads)

    def w1_halves(w1k, w1v):
        hk = w1k.shape[1]
        k3 = w1k.reshape(2, CMP_STRIDE, dh, hk)
        v3 = w1v.reshape(2, CMP_STRIDE, dh, hk)
        zk = jnp.zeros_like(k3[0])
        top = lambda a: jnp.concatenate([a, zk], axis=-1)
        bot = lambda a: jnp.concatenate([zk, a], axis=-1)
        half = lambda i: jnp.concatenate([top(k3[i]), bot(v3[i])], axis=1).reshape(CMP_STRIDE * 2 * dh, 2 * hk)
        return half(0).astype(BF16), half(1).astype(BF16)

    wp, wq = w1_halves(cmp_w1_k[0], cmp_w1_v[0])
    zk = jnp.zeros_like(cmp_w2_k[0])
    w2 = jnp.concatenate([jnp.concatenate([cmp_w2_k[0], zk], axis=1),
                          jnp.concatenate([zk, cmp_w2_v[0]], axis=1)], axis=0).astype(BF16)
    pe = jnp.concatenate([cmp_pe_k[0].reshape(2, CMP_STRIDE, dh), cmp_pe_v[0].reshape(2, CMP_STRIDE, dh)],
                         axis=-1).reshape(2, CMP_STRIDE * 2 * dh)
    kc, vc = _compress(streams["cmp"][0], pe, wp, wq, w2, jnp.tile(featc, (G, 1)),
                       batch=B, seq=S, col0=streams["cmp"][1], n_half=n_half)

    o_a = _nsa(streams, gate, kc, vc, featk, featw, qfeat, qrel, seg, mmap, place, nsa_out_norm,
               batch=B, seq=S, n_cmp=n_cmp)
    o_h = _hgrn(streams, lower_bounds, hgrn_out_norm, batch=B, seq=S)

    wo = w_out[0].astype(BF16)
    out = _outproj(x2, o_a, o_h, wo[:nsa_w], wo[nsa_w:], final_norm.reshape(1, D), tm=512)
    return out.reshape(B, S, D)
```

```python
import functools

import ml_dtypes
import numpy as np
import jax
import jax.numpy as jnp
from jax import lax
from jax.experimental import pallas as pl
from jax.experimental.pallas import tpu as pltpu

F32 = jnp.float32
BF16 = jnp.bfloat16

EPS = 1e-6
NEG_INF = -1e30
LOG2E = 1.4426950408889634

NSA_HEAD_DIM = 64
NSA_KV_HEADS = 4
CMP_BLOCK = 32
CMP_STRIDE = 16
SEL_BLOCK = 64
SEL_TOP = 8
SEL_BONUS = 1.0e4
WINDOW = 512
HGRN_HEAD_DIM = 128
HGRN_CHUNK = 64
HGRN_SUB = 16

LANES = 128
VMEM_LIMIT = 56 * 1024 * 1024
NSA_TILE = 256
SEL_ALWAYS = 2

SEL_LANE0 = NSA_HEAD_DIM
MAX_SEL_BLOCKS = 32
POS_LANE0 = SEL_LANE0 + MAX_SEL_BLOCKS
N_SPLIT = 4
FLAG_LANE = POS_LANE0 + 2 * N_SPLIT
SHIFT_LANE0 = FLAG_LANE + 1
REL_LANE0 = SHIFT_LANE0 + N_SPLIT
SCORE_BOUND = 100.0
SAFETY = 1.02


def _nt_dot(a, b):
    return lax.dot_general(a, b, (((1,), (1,)), ((), ())), preferred_element_type=F32)


def _tn_dot(a, b):
    return lax.dot_general(a, b, (((0,), (0,)), ((), ())), preferred_element_type=F32)


def _split3(x):
    hi = x.astype(BF16)
    r1 = x - hi.astype(F32)
    mid = r1.astype(BF16)
    lo = (r1 - mid.astype(F32)).astype(BF16)
    return hi, mid, lo


def _proj_kernel(x_ref, g_ref, w_ref, wg_ref, o16_ref, o32_ref, og_ref, h_ref, *, row_chunk, n16):
    j = pl.program_id(1)

    @pl.when(j == 0)
    def _():
        n_chunks = x_ref.shape[0] // row_chunk

        def body(c, carry):
            rows = pl.ds(pl.multiple_of(c * row_chunk, row_chunk), row_chunk)
            x = x_ref[rows, :]
            ms = jnp.mean(x * x, axis=-1, keepdims=True)
            h_ref[rows, :] = (x * lax.rsqrt(ms + EPS) * g_ref[...]).astype(BF16)
            return carry

        lax.fori_loop(0, n_chunks, body, 0)
        og_ref[...] = jnp.dot(h_ref[...], wg_ref[...], preferred_element_type=F32)

    @pl.when(j < n16)
    def _():
        o16_ref[...] = jnp.dot(h_ref[...], w_ref[...], preferred_element_type=F32).astype(BF16)

    @pl.when(j >= n16)
    def _():
        o32_ref[...] = jnp.dot(h_ref[...], w_ref[...], preferred_element_type=F32)


def _relayout_kernel(wt_ref, o_ref, *, plan):
    o_ref[:, o_ref.shape[1] - LANES:] = jnp.zeros((o_ref.shape[0], LANES), o_ref.dtype)
    for dst, src, width, scale in plan:
        o_ref[:, dst:dst + width] = (wt_ref[src:src + width, :].T * scale).astype(o_ref.dtype)


def _relayout_weights(wt, plan, n_out, *, rows=256):
    n_in, D = wt.shape
    return pl.pallas_call(
        functools.partial(_relayout_kernel, plan=plan),
        grid=(D // rows,),
        in_specs=[pl.BlockSpec((n_in, rows), lambda i: (0, i))],
        out_specs=pl.BlockSpec((rows, n_out), lambda i: (i, 0)),
        out_shape=jax.ShapeDtypeStruct((D, n_out), BF16),
        compiler_params=pltpu.CompilerParams(
            dimension_semantics=("arbitrary",),
            vmem_limit_bytes=VMEM_LIMIT),
        name="relayout",
    )(wt)


def _proj(x2, g, w, *, tm, tn, n_main, n16_cols):
    M, D = x2.shape
    N = n_main
    NG = w.shape[1] - n_main
    n16 = n16_cols // tn
    assert n16 * tn == n16_cols and N % tn == 0 and N % NG == 0
    return pl.pallas_call(
        functools.partial(_proj_kernel, row_chunk=128, n16=n16),
        grid=(M // tm, N // tn),
        in_specs=[
            pl.BlockSpec((tm, D), lambda i, j: (i, 0)),
            pl.BlockSpec((1, D), lambda i, j: (0, 0)),
            pl.BlockSpec((D, tn), lambda i, j: (0, j)),
            pl.BlockSpec((D, NG), lambda i, j: (0, N // NG)),
        ],
        out_specs=[
            pl.BlockSpec((tm, tn), lambda i, j: (i, jnp.minimum(j, n16 - 1))),
            pl.BlockSpec((tm, tn), lambda i, j: (i, jnp.maximum(j - n16, 0))),
            pl.BlockSpec((tm, NG), lambda i, j: (i, 0)),
        ],
        out_shape=[
            jax.ShapeDtypeStruct((M, n16_cols), BF16),
            jax.ShapeDtypeStruct((M, N - n16_cols), F32),
            jax.ShapeDtypeStruct((M, NG), F32),
        ],
        scratch_shapes=[pltpu.VMEM((tm, D), BF16)],
        compiler_params=pltpu.CompilerParams(
            dimension_semantics=("arbitrary", "arbitrary"),
            vmem_limit_bytes=VMEM_LIMIT),
        name="proj",
    )(x2, g, w, w)


def _compress_kernel(c0_ref, c1_ref, c2_ref, c3_ref, pe_ref, wp_ref, wq_ref, w2_ref, feat_ref,
                     kc_ref, vc_ref, x_ref, c32_ref, *, n_half):
    for g, c_ref in enumerate((c0_ref, c1_ref, c2_ref, c3_ref)):
        c32_ref[...] = c_ref[...].astype(F32)
        for l in range(CMP_STRIDE):
            x_ref[g * n_half:(g + 1) * n_half, l * LANES:(l + 1) * LANES] = (
                c32_ref[pl.ds(l, n_half, stride=CMP_STRIDE), :])
    x = x_ref[...]
    rows = x.shape[0]
    first = jnp.dot((x + pe_ref[0:1, :]).astype(BF16), wp_ref[...], preferred_element_type=F32)
    second = jnp.dot((x + pe_ref[1:2, :]).astype(BF16), wq_ref[...], preferred_element_type=F32)
    hidden = first + pltpu.roll(second, rows - 1, 0)
    out = jnp.dot(jax.nn.gelu(hidden).astype(BF16), w2_ref[...], preferred_element_type=F32)
    lane = lax.broadcasted_iota(jnp.int32, out.shape, 1)
    dh = NSA_HEAD_DIM
    kc_ref[...] = jnp.where(lane < dh, out.astype(BF16), feat_ref[...])
    for g in range(NSA_KV_HEADS):
        vc_ref[g * dh:(g + 1) * dh, :] = out[g * n_half:(g + 1) * n_half, :].T[dh:, :].astype(BF16)


def _compress(proj, pe, wp, wq, w2, feat, *, batch, seq, col0, n_half):
    rows = NSA_KV_HEADS * n_half
    rows_t = NSA_KV_HEADS * NSA_HEAD_DIM
    const = lambda a: pl.BlockSpec(a.shape, lambda b: (0,) * a.ndim)
    stream = lambda g: pl.BlockSpec((seq, LANES), lambda b, o=col0 // LANES + g: (b, o))
    return pl.pallas_call(
        functools.partial(_compress_kernel, n_half=n_half),
        grid=(batch,),
        in_specs=[stream(0), stream(1), stream(2), stream(3),
                  const(pe), const(wp), const(wq), const(w2), const(feat)],
        out_specs=[
            pl.BlockSpec((rows, LANES), lambda b: (b, 0)),
            pl.BlockSpec((rows_t, n_half), lambda b: (b, 0)),
        ],
        out_shape=[
            jax.ShapeDtypeStruct((batch * rows, LANES), BF16),
            jax.ShapeDtypeStruct((batch * rows_t, n_half), BF16),
        ],
        scratch_shapes=[pltpu.VMEM((rows, CMP_STRIDE * LANES), F32), pltpu.VMEM((seq, LANES), F32)],
        compiler_params=pltpu.CompilerParams(
            dimension_semantics=("arbitrary",),
            vmem_limit_bytes=VMEM_LIMIT),
        name="compress",
    )(proj, proj, proj, proj, pe, wp, wq, w2, feat)


def _nsa_kernel(q_ref, qall_ref, z_ref, gate_ref, slc_ref, win_ref, kc_ref, vct_ref, featk_ref, featw_ref,
                qfeat_ref, qrel_ref, seg_ref, mmap_ref, place_ref, norm_ref, o_ref,
                ksel_ref, vselt_ref, kwin_ref, vwint_ref, gt_ref, bounded_ref, *, n_cmp):
    g = pl.program_id(1)
    qi = pl.program_id(2)
    tq = q_ref.shape[0]
    tk = tq
    seq = slc_ref.shape[0]
    dh = NSA_HEAD_DIM
    rep = q_ref.shape[1] // dh
    n_half = kc_ref.shape[0]
    n_pad = WINDOW // tk

    @pl.when(qi == 0)
    def _():
        lane = lax.broadcasted_iota(jnp.int32, (seq, LANES), 1)

        def max_sq_norm(x, width):
            s = jnp.dot((x * x).astype(BF16), seg_ref[0:width, :], preferred_element_type=F32)
            return jnp.max(s) * SAFETY

        kv = slc_ref[...]
        k2_sel = max_sq_norm(jnp.where(lane < dh, kv.astype(F32), 0.0), LANES)
        ksel_ref[...] = jnp.where(lane < dh, kv, featk_ref[...])
        vt = kv.astype(F32).T[dh:, :].astype(BF16)
        for kt in range(seq // tk):
            vselt_ref[kt] = vt[:, kt * tk:(kt + 1) * tk]
        kv = win_ref[...]
        k2_win = max_sq_norm(jnp.where(lane < dh, kv.astype(F32), 0.0), LANES)
        kwin_ref[0:WINDOW, :] = featw_ref[0:WINDOW, :]
        kwin_ref[WINDOW:, :] = jnp.where(lane < dh, kv, featw_ref[WINDOW:, :])
        vt = kv.astype(F32).T[dh:, :].astype(BF16)
        for kt in range(n_pad):
            vwint_ref[kt] = jnp.zeros((dh, tk), BF16)
        for kt in range(seq // tk):
            vwint_ref[n_pad + kt] = vt[:, kt * tk:(kt + 1) * tk]
        q2 = max_sq_norm(qall_ref[...].astype(F32), rep * dh) * (LOG2E * LOG2E)
        limit = SCORE_BOUND * SCORE_BOUND
        bounded_ref[0] = ((q2 * k2_sel <= limit) & (q2 * k2_win <= limit)).astype(jnp.int32)

    heads = lambda a: jnp.concatenate([a] * rep, axis=1)
    sub8 = lambda a: a.reshape(a.shape[0] // 8, 8, a.shape[1])

    def softmax_pv(lgs, v_tiles, bounded, extra=None):
        if bounded:
            m = 0.0
        else:
            m8 = functools.reduce(jnp.maximum, [jnp.max(sub8(lg), axis=0) for lg in lgs])
            m = jnp.max(m8, axis=0, keepdims=True)
        l8 = jnp.zeros((8, lgs[0].shape[1]), F32)
        acc = jnp.zeros((dh, lgs[0].shape[1]), F32)
        for lg, vt in zip(lgs, v_tiles):
            pt = jnp.exp2(lg) if bounded else jnp.exp2(lg - m)
            l8 = l8 + jnp.sum(sub8(pt), axis=0)
            acc = acc + jnp.dot(vt, pt.astype(BF16), preferred_element_type=F32)
        if extra is not None:
            l8, acc = extra(l8, acc)
        return acc / jnp.sum(l8, axis=0, keepdims=True)

    def step(n, bounded):
        static = isinstance(n, int)
        assert static == bounded
        t0 = n * tq
        lane = lax.broadcasted_iota(jnp.int32, (tq, LANES), 1)
        key_i = lax.broadcasted_iota(jnp.int32, (tk, tq), 0)
        qry_i = lax.broadcasted_iota(jnp.int32, (tk, tq), 1)
        causal_1 = jnp.where(key_i <= qry_i, 0.0, NEG_INF)
        causal_bias = heads(causal_1)
        band_bias = heads(jnp.where(key_i > qry_i, 0.0, NEG_INF))

        qfeat = qfeat_ref[0, n]
        rel_lanes = (lane >= REL_LANE0) & (lane < REL_LANE0 + 3)
        qa = []
        for r in range(rep):
            qcol = q_ref[:, (r // 2) * LANES:(r // 2 + 1) * LANES].astype(F32)
            if r % 2:
                qcol = pltpu.roll(qcol, dh, 1)
            feat = jnp.where(rel_lanes, qrel_ref[0, r * tq:(r + 1) * tq, :].astype(F32), qfeat[r:r + 1, :])
            qa.append(jnp.where(lane < dh, qcol * LOG2E, feat))
        q0 = jnp.concatenate(qa, axis=0).astype(BF16)

        gt_ref[...] = jax.nn.sigmoid(gate_ref[...]).T
        out_scale = []
        for pair in range(rep // 2):
            z = z_ref[:, pair * LANES:(pair + 1) * LANES]
            out_scale.append(norm_ref[:, pair * LANES:(pair + 1) * LANES] * (z * jax.nn.sigmoid(z)))

        nc = min(n_half, (t0 + tq) // CMP_STRIDE) if static else n_half
        n_idx = lax.broadcasted_iota(jnp.int32, (nc, tq), 0)
        t_idx = t0 + lax.broadcasted_iota(jnp.int32, (nc, tq), 1)
        valid_c = heads((CMP_STRIDE * n_idx + (CMP_BLOCK - 1) <= t_idx) & (n_idx < n_cmp))
        lg_c = jnp.where(valid_c, _nt_dot(kc_ref[0:nc, :], q0), NEG_INF)

        n_wt = n_pad + 1
        lgs = []
        for w in range(n_wt):
            rows = (slice(t0 + w * tk, t0 + (w + 1) * tk) if static
                    else pl.ds(pl.multiple_of(t0 + w * tk, tk), tk))
            lg = _nt_dot(kwin_ref[rows, :], q0)
            if w == 0:
                lg = lg + band_bias
            if w == n_wt - 1:
                lg = lg + causal_bias
            lgs.append(lg)
        o_win = softmax_pv(lgs, [vwint_ref[n + w] for w in range(n_wt)], bounded)

        e = jnp.exp2(lg_c - jnp.max(lg_c, axis=0, keepdims=True))
        p = jnp.where(valid_c, e / jnp.sum(e, axis=0, keepdims=True), 0.0)
        if nc < n_half:
            p = jnp.concatenate([p, jnp.zeros((n_half - nc, rep * tq), F32)], axis=0)
        o_cmp = jnp.dot(vct_ref[...], p.astype(BF16), preferred_element_type=F32)
        p_sum = p[:, 0:tq]
        for r in range(1, rep):
            p_sum = p_sum + p[:, r * tq:(r + 1) * tq]

        nb = min(MAX_SEL_BLOCKS, (t0 + tq) // SEL_BLOCK) if static else MAX_SEL_BLOCKS
        mm = mmap_ref[...]
        hi, mid, lo = _split3(p_sum)
        p_slc = (jnp.dot(mm, hi, preferred_element_type=F32) + jnp.dot(mm, mid, preferred_element_type=F32)
                 + jnp.dot(mm, lo, preferred_element_type=F32))[0:MAX_SEL_BLOCKS]
        if static:
            sel_scores = lambda kt: _nt_dot(ksel_ref[kt * tk:(kt + 1) * tk, :], q0)
            maybe_idle = [kt for kt in range(n + 1) if SEL_ALWAYS <= kt <= n - SEL_ALWAYS]
            raw = {kt: sel_scores(kt) for kt in range(n + 1) if kt not in maybe_idle}
        jj = lax.broadcasted_iota(jnp.int32, (MAX_SEL_BLOCKS, tq), 0)
        tt = t0 + lax.broadcasted_iota(jnp.int32, (MAX_SEL_BLOCKS, tq), 1)
        cur = lax.shift_right_logical(tt, int(np.log2(SEL_BLOCK)))
        forced = (jj == 0) | (jj == cur) | (jj == cur - 1)
        future = jj > cur
        score = jnp.where(future, -1.0, p_slc + jnp.where(forced, SEL_BONUS, 0.0))
        rank = jnp.zeros((MAX_SEL_BLOCKS, tq), jnp.int32)
        for i in range(nb):
            other = jnp.broadcast_to(score[i:i + 1, :], score.shape)
            beats = (other > score) | ((other == score) & (jj > i))
            rank = rank + beats.astype(jnp.int32)
        keep = (rank < SEL_TOP) & jnp.logical_not(future)

        if static:
            blocks_per_tile = tk // SEL_BLOCK
            block_bias = jnp.where(keep, 0.0, NEG_INF)

            def tile_bias(kt):
                b = jnp.concatenate(
                    [jnp.broadcast_to(block_bias[blocks_per_tile * kt + j:blocks_per_tile * kt + j + 1, :],
                                      (SEL_BLOCK, tq)) for j in range(blocks_per_tile)], axis=0)
                return heads(b + causal_1 if kt == n else b)

            wanted = [jnp.max(block_bias[blocks_per_tile * kt:blocks_per_tile * (kt + 1)]) == 0.0
                      for kt in maybe_idle]

            def idle_tiles(l8, acc):
                for kt, some in zip(maybe_idle, wanted):
                    def live(l8, acc, kt=kt):
                        pt = jnp.exp2(sel_scores(kt) + tile_bias(kt))
                        return (l8 + jnp.sum(sub8(pt), axis=0),
                                acc + jnp.dot(vselt_ref[kt], pt.astype(BF16), preferred_element_type=F32))

                    l8, acc = lax.cond(some, live, lambda l8, acc: (l8, acc), l8, acc)
                return l8, acc

            o_slc = softmax_pv([raw[kt] + tile_bias(kt) for kt in raw], [vselt_ref[kt] for kt in raw],
                               bounded, extra=idle_tiles if maybe_idle else None)
        else:
            chosen = jnp.where(keep, 1.0, 0.0).astype(BF16)
            placed = _tn_dot(chosen, place_ref[...])
            sel_lanes = (lane >= SEL_LANE0) & (lane < SEL_LANE0 + MAX_SEL_BLOCKS)
            sel_bias = (placed - 1.0) * (-NEG_INF)
            qs = jnp.concatenate([jnp.where(sel_lanes, sel_bias, a) for a in qa], axis=0).astype(BF16)

            def sel_tile(kt, carry, bias=None):
                m, l, acc = carry
                lg = _nt_dot(ksel_ref[pl.ds(pl.multiple_of(kt * tk, tk), tk), :], qs)
                if bias is not None:
                    lg = lg + bias
                m_new = jnp.maximum(m, jnp.max(lg, axis=0, keepdims=True))
                alpha = jnp.exp2(m - m_new)
                pt = jnp.exp2(lg - m_new)
                return (m_new, alpha * l + jnp.sum(pt, axis=0, keepdims=True),
                        alpha * acc + jnp.dot(vselt_ref[kt], pt.astype(BF16), preferred_element_type=F32))

            init = (jnp.full((1, rep * tq), NEG_INF, F32), jnp.zeros((1, rep * tq), F32),
                    jnp.zeros((dh, rep * tq), F32))
            _, l, acc = sel_tile(n, lax.fori_loop(0, n, sel_tile, init), causal_bias)
            o_slc = acc / l

        ys = []
        for r in range(rep):
            cols = slice(r * tq, (r + 1) * tq)
            gate = lambda c: gt_ref[pl.ds(3 * (g * rep + r) + c, 1), :]
            o = gate(0) * o_cmp[:, cols] + gate(1) * o_slc[:, cols] + gate(2) * o_win[:, cols]
            ys.append(o * lax.rsqrt(jnp.mean(o * o, axis=0, keepdims=True) + EPS))
        for pair in range(rep // 2):
            cols = slice(pair * LANES, (pair + 1) * LANES)
            y = jnp.concatenate([ys[2 * pair], ys[2 * pair + 1]], axis=0).T
            o_ref[:, cols] = (y * out_scale[pair]).astype(o_ref.dtype)

    @pl.when(bounded_ref[0] == 1)
    def _():
        lax.switch(qi, [functools.partial(step, n, True) for n in range(seq // tq)])

    @pl.when(bounded_ref[0] == 0)
    def _():
        step(qi, False)


def _nsa(streams, gate, kc, vc, featk, featw, qfeat, qrel, seg, mmap, place, norm, *, batch, seq, n_cmp):
    tq = NSA_TILE
    nq = seq // tq
    G = NSA_KV_HEADS
    gw = norm.shape[1] // G
    n_half = kc.shape[0] // (batch * G)
    const = lambda a: pl.BlockSpec(a.shape, lambda b, g, i: (0,) * a.ndim)
    stream = lambda name: pl.BlockSpec((seq, LANES),
                                       lambda b, g, i, o=streams[name][1] // LANES: (b, o + g))
    tile = lambda name: pl.BlockSpec((tq, gw), lambda b, g, i, o=streams[name][1] // gw: (b * nq + i, o + g))
    return pl.pallas_call(
        functools.partial(_nsa_kernel, n_cmp=n_cmp),
        grid=(batch, G, nq),
        in_specs=[
            tile("q_a"),
            pl.BlockSpec((seq, gw), lambda b, g, i, o=streams["q_a"][1] // gw: (b, o + g)),
            tile("z_a"),
            pl.BlockSpec((tq, gate.shape[1]), lambda b, g, i: (b * nq + i, 0)),
            stream("slc"), stream("win"),
            pl.BlockSpec((n_half, LANES), lambda b, g, i: (b * G + g, 0)),
            pl.BlockSpec((NSA_HEAD_DIM, n_half), lambda b, g, i: (b * G + g, 0)),
            const(featk), const(featw),
            pl.BlockSpec((1,) + qfeat.shape[1:], lambda b, g, i: (g, 0, 0, 0)),
            pl.BlockSpec((1,) + qrel.shape[1:], lambda b, g, i: (g, 0, 0)),
            const(seg), const(mmap), const(place),
            pl.BlockSpec((1, gw), lambda b, g, i: (0, g)),
        ],
        out_specs=pl.BlockSpec((tq, gw), lambda b, g, i: (b * nq + i, g)),
        out_shape=jax.ShapeDtypeStruct((batch * seq, G * gw), BF16),
        scratch_shapes=[
            pltpu.VMEM((seq, LANES), BF16),
            pltpu.VMEM((seq // tq, NSA_HEAD_DIM, tq), BF16),
            pltpu.VMEM((seq + WINDOW, LANES), BF16),
            pltpu.VMEM(((seq + WINDOW) // tq, NSA_HEAD_DIM, tq), BF16),
            pltpu.VMEM((gate.shape[1], tq), F32),
            pltpu.SMEM((1,), jnp.int32),
        ],
        compiler_params=pltpu.CompilerParams(
            dimension_semantics=("arbitrary", "arbitrary", "arbitrary"),
            vmem_limit_bytes=VMEM_LIMIT),
        name="nsa",
    )(streams["q_a"][0], streams["q_a"][0], streams["z_a"][0], gate, streams["slc"][0], streams["win"][0],
      kc, vc, featk, featw, qfeat, qrel, seg, mmap, place, norm)


def _bf16_terms(x, n):
    terms, rest = [], np.asarray(x, np.float64)
    for _ in range(n):
        t = rest.astype(np.float32).astype(ml_dtypes.bfloat16).astype(np.float64)
        terms.append(t.astype(np.float32))
        rest = rest - t
    return terms


def _nsa_tables(seq, n_half, n_cmp, heads):
    assert seq // SEL_BLOCK <= MAX_SEL_BLOCKS and FLAG_LANE < LANES

    def key_features(pos, onehot_blocks):
        f = np.zeros((len(pos), LANES), np.float32)
        if onehot_blocks:
            f[np.arange(len(pos)), SEL_LANE0 + pos // SEL_BLOCK] = 1.0
        f[:, POS_LANE0:POS_LANE0 + N_SPLIT] = ((pos // 64) * 64)[:, None]
        f[:, POS_LANE0 + N_SPLIT:POS_LANE0 + 2 * N_SPLIT] = (pos % 64)[:, None]
        return f

    assert REL_LANE0 + 3 <= LANES
    featk = key_features(np.arange(seq), True)
    featk[:, SHIFT_LANE0:REL_LANE0 + 3] = 1.0
    featw = np.concatenate([np.zeros((WINDOW, LANES), np.float32), key_features(np.arange(seq), False)])
    featw[:WINDOW, FLAG_LANE] = NEG_INF
    featw[WINDOW:, SHIFT_LANE0:REL_LANE0 + 3] = 1.0
    featc = key_features(CMP_STRIDE * np.arange(n_half) + CMP_BLOCK - 1, False)
    featc[n_cmp:, FLAG_LANE] = NEG_INF

    slopes = (2.0 ** (-8.0 * np.arange(1, heads + 1) / heads)).astype(np.float32).astype(np.float64)
    slopes2 = slopes * LOG2E
    terms = _bf16_terms(slopes2, N_SPLIT)
    rep = heads // NSA_KV_HEADS
    nq = seq // NSA_TILE
    qfeat = np.zeros((NSA_KV_HEADS, nq, 8, LANES), np.float32)
    qrel = np.zeros((NSA_KV_HEADS, rep * NSA_TILE, LANES), np.float32)
    for h in range(heads):
        g, r = divmod(h, rep)
        for i, t in enumerate(terms):
            qfeat[g, :, r, POS_LANE0 + i] = t[h]
            qfeat[g, :, r, POS_LANE0 + N_SPLIT + i] = t[h]
        qfeat[g, :, r, FLAG_LANE] = 1.0
        for i, t in enumerate(_bf16_terms(-slopes2[h] * NSA_TILE * np.arange(nq), N_SPLIT)):
            qfeat[g, :, r, SHIFT_LANE0 + i] = t
        for i, t in enumerate(_bf16_terms(-slopes2[h] * np.arange(NSA_TILE), 3)):
            qrel[g, r * NSA_TILE:(r + 1) * NSA_TILE, REL_LANE0 + i] = t
    seg = np.zeros((2 * LANES, LANES), np.float32)
    seg[np.arange(2 * LANES), np.arange(2 * LANES) // NSA_HEAD_DIM] = 1.0

    cs = CMP_STRIDE * np.arange(n_half)[None, :]
    ss = SEL_BLOCK * np.arange(LANES)[:, None]
    overlap = np.clip(np.minimum(cs + CMP_BLOCK, ss + SEL_BLOCK) - np.maximum(cs, ss), 0, None)
    mmap = (overlap / CMP_BLOCK) * (np.arange(n_half)[None, :] < n_cmp) * (ss < seq)
    place = np.zeros((MAX_SEL_BLOCKS, LANES), np.float32)
    place[np.arange(MAX_SEL_BLOCKS), SEL_LANE0 + np.arange(MAX_SEL_BLOCKS)] = 1.0
    bf = lambda a: jnp.asarray(a, dtype=BF16)
    return bf(featk), bf(featw), bf(featc), jnp.asarray(qfeat), bf(qrel), bf(seg), bf(mmap), bf(place)


def _hgrn_stages(q_ref, f_ref, v_ref, z_ref, lb_ref, norm_ref, cum_ref, o_ref, rows):
    C, SUB = HGRN_CHUNK, HGRN_SUB
    n_sub = C // SUB
    chunks = range(len(rows))
    sub = lambda x, i: x[i * SUB:(i + 1) * SUB]
    w = {}

    def gates():
        lbr = lb_ref[...]
        e = jnp.exp(lbr - jnp.max(lbr, axis=0, keepdims=True))
        lb = e[0:1, :] / jnp.sum(e, axis=0, keepdims=True)
        w["v16"] = [v_ref[rows[j], :].astype(BF16) for j in chunks]
        f = [lb + (1.0 - lb) * jax.nn.sigmoid(f_ref[rows[j], :]) for j in chunks]
        w["k"] = [1.0 - f[j] for j in chunks]
        w["parts"] = [_split3(jnp.log(f[j])) for j in chunks]

    def cumsums():
        cum = cum_ref[...]
        parts = w["parts"]
        w["a"] = [(jnp.dot(cum, parts[j][0], preferred_element_type=F32)
                   + jnp.dot(cum, parts[j][1], preferred_element_type=F32)
                   + jnp.dot(cum, parts[j][2], preferred_element_type=F32)) for j in chunks]

    def operands():
        a, k = w["a"], w["k"]
        q1, k1, k2, qb, k3, start, dec = [], [], [], [], [], [], []
        for j in chunks:
            q1.append(q_ref[rows[j], :] * jnp.exp(a[j]))
            k1.append(k[j] * jnp.exp(-a[j]))
            tot = [a[j][(i + 1) * SUB - 1:(i + 1) * SUB] for i in range(n_sub)]
            s = [jnp.zeros_like(tot[0])]
            for i in range(n_sub):
                s.append(s[i] + tot[i])
            start.append(s)
            dec.append(jnp.exp(s[n_sub]))
            qb.append(jnp.concatenate([sub(q1[j], i) * jnp.exp(s[i]) for i in range(n_sub)], axis=0)
                      .astype(BF16))
            k2.append([sub(k1[j], i) * jnp.exp(tot[i]) for i in range(n_sub)])
            k3.append(jnp.concatenate([sub(k1[j], i) * jnp.exp(s[n_sub] - s[i]) for i in range(n_sub)],
                                      axis=0).astype(BF16))
        w.update(q1=q1, k1=k1, k2=k2, qb=qb, k3=k3, start=start, dec=dec)

    def scores():
        q1, k1, k2, start = w["q1"], w["k1"], w["k2"], w["start"]
        att = []
        for j in chunks:
            row_blocks = []
            for i in range(n_sub):
                rhs = [k2[j][jb] * jnp.exp(start[j][i] - start[j][jb + 1]) if jb < i - 1 else k2[j][jb]
                       for jb in range(i)]
                rhs.append(sub(k1[j], i))
                rhs = jnp.concatenate(rhs, axis=0) if len(rhs) > 1 else rhs[0]
                sc = _nt_dot(sub(q1[j], i).astype(BF16), rhs.astype(BF16))
                width = (i + 1) * SUB
                causal = (lax.broadcasted_iota(jnp.int32, (SUB, width), 1)
                          <= lax.broadcasted_iota(jnp.int32, (SUB, width), 0) + i * SUB)
                row_blocks.append(jnp.where(causal, sc, 0.0).astype(BF16))
            att.append(row_blocks)
        w["att"] = att

    def products():
        att, v16, k3 = w["att"], w["v16"], w["k3"]
        w["intra"] = [jnp.concatenate([jnp.dot(att[j][i], v16[j][0:(i + 1) * SUB],
                                               preferred_element_type=F32)
                                       for i in range(n_sub)], axis=0) for j in chunks]
        w["incr"] = [_tn_dot(v16[j], k3[j]) for j in chunks]

    def state_pass(st):
        inter = []
        for j in chunks:
            inter.append(_nt_dot(w["qb"][j], st.astype(BF16)))
            st = st * w["dec"][j] + w["incr"][j]
        w["inter"] = inter
        return st

    def finish():
        gain = norm_ref[...]
        for j in chunks:
            o = w["inter"][j] + w["intra"][j]
            y = o * lax.rsqrt(jnp.mean(o * o, axis=-1, keepdims=True) + EPS) * gain
            z = z_ref[rows[j], :]
            o_ref[rows[j], :] = (y * (z * jax.nn.sigmoid(z))).astype(o_ref.dtype)

    return gates, cumsums, operands, scores, products, state_pass, finish


def _hgrn_kernel(q_ref, f_ref, v_ref, z_ref, lb_ref, norm_ref, cum_ref, o_ref, *, group):
    seq, dk = q_ref.shape

    def block(cb, st):
        rows = [pl.ds(pl.multiple_of((cb * group + j) * HGRN_CHUNK, HGRN_CHUNK), HGRN_CHUNK)
                for j in range(group)]
        gates, cumsums, operands, scores, products, state_pass, finish = _hgrn_stages(
            q_ref, f_ref, v_ref, z_ref, lb_ref, norm_ref, cum_ref, o_ref, rows)
        gates()
        cumsums()
        operands()
        scores()
        products()
        st = state_pass(st)
        finish()
        return st

    lax.fori_loop(0, seq // (HGRN_CHUNK * group), block, jnp.zeros((dk, dk), F32))


def _hgrn_patterns():
    C, SUB = HGRN_CHUNK, HGRN_SUB
    t = np.arange(C)[:, None]
    s = np.arange(C)[None, :]
    return (((t // SUB) == (s // SUB)) & (s <= t)).astype(np.float32)


def _hgrn(streams, lower_bounds, norm, *, batch, seq):
    dk = HGRN_HEAD_DIM
    heads = norm.shape[1] // dk
    cum = jnp.asarray(_hgrn_patterns(), dtype=BF16)
    col = lambda name: (lambda b, h, o=streams[name][1] // dk: (b, o + h))
    return pl.pallas_call(
        functools.partial(_hgrn_kernel, group=16),
        grid=(batch, heads),
        in_specs=[
            pl.BlockSpec((seq, dk), col("q_h")),
            pl.BlockSpec((seq, dk), col("f_h")),
            pl.BlockSpec((seq, dk), col("i_h")),
            pl.BlockSpec((seq, dk), col("z_h")),
            pl.BlockSpec((lower_bounds.shape[0], dk), lambda b, h: (0, h)),
            pl.BlockSpec((1, dk), lambda b, h: (0, h)),
            pl.BlockSpec(cum.shape, lambda b, h: (0, 0)),
        ],
        out_specs=pl.BlockSpec((seq, dk), lambda b, h: (b, h)),
        out_shape=jax.ShapeDtypeStruct((batch * seq, heads * dk), BF16),
        compiler_params=pltpu.CompilerParams(
            dimension_semantics=("arbitrary", "arbitrary"),
            vmem_limit_bytes=VMEM_LIMIT),
        name="hgrn",
    )(streams["q_h"][0], streams["f_h"][0], streams["i_h"][0], streams["z_h"][0], lower_bounds, norm, cum)


def _outproj_kernel(x_ref, oa_ref, oh_ref, wa_ref, wh_ref, g_ref, o_ref):
    y = x_ref[...] + jnp.dot(oa_ref[...], wa_ref[...], preferred_element_type=F32)
    y = y + jnp.dot(oh_ref[...], wh_ref[...], preferred_element_type=F32)
    ms = jnp.mean(y * y, axis=-1, keepdims=True)
    o_ref[...] = y * lax.rsqrt(ms + EPS) * g_ref[...]


def _outproj(x2, o_a, o_h, wa, wh, g, *, tm):
    M, D = x2.shape
    const = lambda a: pl.BlockSpec(a.shape, lambda i: (0,) * a.ndim)
    return pl.pallas_call(
        _outproj_kernel,
        grid=(M // tm,),
        in_specs=[
            pl.BlockSpec((tm, D), lambda i: (i, 0)),
            pl.BlockSpec((tm, o_a.shape[1]), lambda i: (i, 0)),
            pl.BlockSpec((tm, o_h.shape[1]), lambda i: (i, 0)),
            const(wa), const(wh), const(g),
        ],
        out_specs=pl.BlockSpec((tm, D), lambda i: (i, 0)),
        out_shape=jax.ShapeDtypeStruct((M, D), F32),
        compiler_params=pltpu.CompilerParams(
            dimension_semantics=("arbitrary",),
            vmem_limit_bytes=VMEM_LIMIT),
        name="outproj",
    )(x2, o_a, o_h, wa, wh, g)


def kernel(x, norm_in, w_in, cmp_pe_k, cmp_w1_k, cmp_w2_k, cmp_pe_v, cmp_w1_v, cmp_w2_v,
           lower_bounds, nsa_out_norm, hgrn_out_norm, w_out, final_norm):
    B, S, D = x.shape
    assert norm_in.shape[0] == 1, "single-layer problem"
    nsa_w = nsa_out_norm.shape[1]
    hgrn_w = hgrn_out_norm.shape[1]
    dh = NSA_HEAD_DIM
    G = NSA_KV_HEADS
    heads = nsa_w // dh
    kvw = G * dh
    n_gate = 3 * heads
    n_cmp = (S - CMP_BLOCK) // CMP_STRIDE + 1
    n_half = S // CMP_STRIDE
    assert S % NSA_TILE == 0 and WINDOW % NSA_TILE == 0 and n_half <= LANES and 2 * dh == LANES

    names = ["q_a", "k_cmp", "v_cmp", "k_slc", "v_slc", "k_win", "v_win", "gate", "z_a",
             "q_h", "f_h", "i_h", "z_h"]
    widths = [nsa_w] + [kvw] * 6 + [n_gate, nsa_w] + [hgrn_w] * 4
    starts = dict(zip(names, np.cumsum([0] + widths[:-1]).tolist()))
    wd = dict(zip(names, widths))

    paired = lambda kname, vname: [(starts[n] + g * dh, dh) for g in range(G) for n in (kname, vname)]
    whole = lambda n: [(starts[n], wd[n])]
    groups = [[("q_a", whole("q_a")),
               ("cmp", paired("k_cmp", "v_cmp")), ("slc", paired("k_slc", "v_slc")),
               ("win", paired("k_win", "v_win")), ("q_h", whole("q_h")), ("i_h", whole("i_h"))],
              [("z_a", whole("z_a")), ("f_h", whole("f_h")), ("z_h", whole("z_h"))]]
    plan, col_of, off = [], {}, 0
    for group in groups:
        for n, pieces in group:
            col_of[n] = off
            for src, width in pieces:
                plan.append((off, src, width, dh ** -0.5 if n == "q_a" else 1.0))
                off += width
    n_main = off
    n16_cols = col_of["z_a"]
    plan.append((n_main, starts["gate"], n_gate, 1.0))
    w_all = _relayout_weights(w_in[0].T, tuple(plan), n_main + LANES)

    x2 = x.reshape(B * S, D)
    proj16, proj32, gate = _proj(x2, norm_in, w_all, tm=min(1024, B * S), tn=1536,
                                 n_main=n_main, n16_cols=n16_cols)
    streams = {}
    for arr, group, base in ((proj16, groups[0], 0), (proj32, groups[1], n16_cols)):
        for n, _ in group:
            streams[n] = (arr, col_of[n] - base)

    featk, featw, featc, qfeat, qrel, seg, mmap, place = _nsa_tables(S, n_half, n_cmp, heads)

    def w1_halves(w1k, w1v):
        hk = w1k.shape[1]
        k3 = w1k.reshape(2, CMP_STRIDE, dh, hk)
        v3 = w1v.reshape(2, CMP_STRIDE, dh, hk)
        zk = jnp.zeros_like(k3[0])
        top = lambda a: jnp.concatenate([a, zk], axis=-1)
        bot = lambda a: jnp.concatenate([zk, a], axis=-1)
        half = lambda i: jnp.concatenate([top(k3[i]), bot(v3[i])], axis=1).reshape(CMP_STRIDE * 2 * dh, 2 * hk)
        return half(0).astype(BF16), half(1).astype(BF16)

    wp, wq = w1_halves(cmp_w1_k[0], cmp_w1_v[0])
    zk = jnp.zeros_like(cmp_w2_k[0])
    w2 = jnp.concatenate([jnp.concatenate([cmp_w2_k[0], zk], axis=1),
                          jnp.concatenate([zk, cmp_w2_v[0]], axis=1)], axis=0).astype(BF16)
    pe = jnp.concatenate([cmp_pe_k[0].reshape(2, CMP_STRIDE, dh), cmp_pe_v[0].reshape(2, CMP_STRIDE, dh)],
                         axis=-1).reshape(2, CMP_STRIDE * 2 * dh)
    kc, vc = _compress(streams["cmp"][0], pe, wp, wq, w2, jnp.tile(featc, (G, 1)),
                       batch=B, seq=S, col0=streams["cmp"][1], n_half=n_half)

    o_a = _nsa(streams, gate, kc, vc, featk, featw, qfeat, qrel, seg, mmap, place, nsa_out_norm,
               batch=B, seq=S, n_cmp=n_cmp)
    o_h = _hgrn(streams, lower_bounds, hgrn_out_norm, batch=B, seq=S)

    wo = w_out[0].astype(BF16)
    out = _outproj(x2, o_a, o_h, wo[:nsa_w], wo[nsa_w:], final_norm.reshape(1, D), tm=512)
    return out.reshape(B, S, D)
```

```python
import functools

import ml_dtypes
import numpy as np
import jax
import jax.numpy as jnp
from jax import lax
from jax.experimental import pallas as pl
from jax.experimental.pallas import tpu as pltpu

F32 = jnp.float32
BF16 = jnp.bfloat16

EPS = 1e-6
NEG_INF = -1e30
LOG2E = 1.4426950408889634

NSA_HEAD_DIM = 64
NSA_KV_HEADS = 4
CMP_BLOCK = 32
CMP_STRIDE = 16
SEL_BLOCK = 64
SEL_TOP = 8
SEL_BONUS = 1.0e4
WINDOW = 512
HGRN_HEAD_DIM = 128
HGRN_CHUNK = 64
HGRN_SUB = 16

LANES = 128
VMEM_LIMIT = 56 * 1024 * 1024
NSA_TILE = 256
SEL_RECENT = 3

SEL_LANE0 = NSA_HEAD_DIM
MAX_SEL_BLOCKS = 32
POS_LANE0 = SEL_LANE0 + MAX_SEL_BLOCKS
N_SPLIT = 4
FLAG_LANE = POS_LANE0 + 2 * N_SPLIT
SHIFT_LANE0 = FLAG_LANE + 1
REL_LANE0 = SHIFT_LANE0 + N_SPLIT
SCORE_BOUND = 100.0
SAFETY = 1.02


def _nt_dot(a, b):
    return lax.dot_general(a, b, (((1,), (1,)), ((), ())), preferred_element_type=F32)


def _tn_dot(a, b):
    return lax.dot_general(a, b, (((0,), (0,)), ((), ())), preferred_element_type=F32)


def _split3(x):
    hi = x.astype(BF16)
    r1 = x - hi.astype(F32)
    mid = r1.astype(BF16)
    lo = (r1 - mid.astype(F32)).astype(BF16)
    return hi, mid, lo


def _proj_kernel(x_ref, g_ref, w_ref, wg_ref, o16_ref, o32_ref, og_ref, h_ref, *, row_chunk, n16):
    j = pl.program_id(1)

    @pl.when(j == 0)
    def _():
        n_chunks = x_ref.shape[0] // row_chunk

        def body(c, carry):
            rows = pl.ds(pl.multiple_of(c * row_chunk, row_chunk), row_chunk)
            x = x_ref[rows, :]
            ms = jnp.mean(x * x, axis=-1, keepdims=True)
            h_ref[rows, :] = (x * lax.rsqrt(ms + EPS) * g_ref[...]).astype(BF16)
            return carry

        lax.fori_loop(0, n_chunks, body, 0)
        og_ref[...] = jnp.dot(h_ref[...], wg_ref[...], preferred_element_type=F32)

    @pl.when(j < n16)
    def _():
        o16_ref[...] = jnp.dot(h_ref[...], w_ref[...], preferred_element_type=F32).astype(BF16)

    @pl.when(j >= n16)
    def _():
        o32_ref[...] = jnp.dot(h_ref[...], w_ref[...], preferred_element_type=F32)


def _relayout_kernel(wt_ref, o_ref, *, plan):
    o_ref[:, o_ref.shape[1] - LANES:] = jnp.zeros((o_ref.shape[0], LANES), o_ref.dtype)
    for dst, src, width, scale in plan:
        o_ref[:, dst:dst + width] = (wt_ref[src:src + width, :].T * scale).astype(o_ref.dtype)


def _relayout_weights(wt, plan, n_out, *, rows=256):
    n_in, D = wt.shape
    return pl.pallas_call(
        functools.partial(_relayout_kernel, plan=plan),
        grid=(D // rows,),
        in_specs=[pl.BlockSpec((n_in, rows), lambda i: (0, i))],
        out_specs=pl.BlockSpec((rows, n_out), lambda i: (i, 0)),
        out_shape=jax.ShapeDtypeStruct((D, n_out), BF16),
        compiler_params=pltpu.CompilerParams(
            dimension_semantics=("arbitrary",),
            vmem_limit_bytes=VMEM_LIMIT),
        name="relayout",
    )(wt)


def _proj(x2, g, w, *, tm, tn, n_main, n16_cols):
    M, D = x2.shape
    N = n_main
    NG = w.shape[1] - n_main
    n16 = n16_cols // tn
    assert n16 * tn == n16_cols and N % tn == 0 and N % NG == 0
    return pl.pallas_call(
        functools.partial(_proj_kernel, row_chunk=128, n16=n16),
        grid=(M // tm, N // tn),
        in_specs=[
            pl.BlockSpec((tm, D), lambda i, j: (i, 0)),
            pl.BlockSpec((1, D), lambda i, j: (0, 0)),
            pl.BlockSpec((D, tn), lambda i, j: (0, j)),
            pl.BlockSpec((D, NG), lambda i, j: (0, N // NG)),
        ],
        out_specs=[
            pl.BlockSpec((tm, tn), lambda i, j: (i, jnp.minimum(j, n16 - 1))),
            pl.BlockSpec((tm, tn), lambda i, j: (i, jnp.maximum(j - n16, 0))),
            pl.BlockSpec((tm, NG), lambda i, j: (i, 0)),
        ],
        out_shape=[
            jax.ShapeDtypeStruct((M, n16_cols), BF16),
            jax.ShapeDtypeStruct((M, N - n16_cols), F32),
            jax.ShapeDtypeStruct((M, NG), F32),
        ],
        scratch_shapes=[pltpu.VMEM((tm, D), BF16)],
        compiler_params=pltpu.CompilerParams(
            dimension_semantics=("arbitrary", "arbitrary"),
            vmem_limit_bytes=VMEM_LIMIT),
        name="proj",
    )(x2, g, w, w)


def _compress_kernel(c0_ref, c1_ref, c2_ref, c3_ref, pe_ref, wp_ref, wq_ref, w2_ref, feat_ref,
                     kc_ref, vc_ref, x_ref, c32_ref, *, n_half):
    for g, c_ref in enumerate((c0_ref, c1_ref, c2_ref, c3_ref)):
        c32_ref[...] = c_ref[...].astype(F32)
        for l in range(CMP_STRIDE):
            x_ref[g * n_half:(g + 1) * n_half, l * LANES:(l + 1) * LANES] = (
                c32_ref[pl.ds(l, n_half, stride=CMP_STRIDE), :])
    x = x_ref[...]
    rows = x.shape[0]
    first = jnp.dot((x + pe_ref[0:1, :]).astype(BF16), wp_ref[...], preferred_element_type=F32)
    second = jnp.dot((x + pe_ref[1:2, :]).astype(BF16), wq_ref[...], preferred_element_type=F32)
    hidden = first + pltpu.roll(second, rows - 1, 0)
    out = jnp.dot(jax.nn.gelu(hidden).astype(BF16), w2_ref[...], preferred_element_type=F32)
    lane = lax.broadcasted_iota(jnp.int32, out.shape, 1)
    dh = NSA_HEAD_DIM
    kc_ref[...] = jnp.where(lane < dh, out.astype(BF16), feat_ref[...])
    for g in range(NSA_KV_HEADS):
        vc_ref[g * dh:(g + 1) * dh, :] = out[g * n_half:(g + 1) * n_half, :].T[dh:, :].astype(BF16)


def _compress(proj, pe, wp, wq, w2, feat, *, batch, seq, col0, n_half):
    rows = NSA_KV_HEADS * n_half
    rows_t = NSA_KV_HEADS * NSA_HEAD_DIM
    const = lambda a: pl.BlockSpec(a.shape, lambda b: (0,) * a.ndim)
    stream = lambda g: pl.BlockSpec((seq, LANES), lambda b, o=col0 // LANES + g: (b, o))
    return pl.pallas_call(
        functools.partial(_compress_kernel, n_half=n_half),
        grid=(batch,),
        in_specs=[stream(0), stream(1), stream(2), stream(3),
                  const(pe), const(wp), const(wq), const(w2), const(feat)],
        out_specs=[
            pl.BlockSpec((rows, LANES), lambda b: (b, 0)),
            pl.BlockSpec((rows_t, n_half), lambda b: (b, 0)),
        ],
        out_shape=[
            jax.ShapeDtypeStruct((batch * rows, LANES), BF16),
            jax.ShapeDtypeStruct((batch * rows_t, n_half), BF16),
        ],
        scratch_shapes=[pltpu.VMEM((rows, CMP_STRIDE * LANES), F32), pltpu.VMEM((seq, LANES), F32)],
        compiler_params=pltpu.CompilerParams(
            dimension_semantics=("arbitrary",),
            vmem_limit_bytes=VMEM_LIMIT),
        name="compress",
    )(proj, proj, proj, proj, pe, wp, wq, w2, feat)


def _nsa_kernel(q_ref, qall_ref, z_ref, gate_ref, slc_ref, win_ref, kc_ref, vct_ref, featk_ref, featw_ref,
                qfeat_ref, qrel_ref, seg_ref, mmap_ref, place_ref, norm_ref, o_ref,
                ksel_ref, vselt_ref, kwin_ref, vwint_ref, gt_ref, bounded_ref, *, n_cmp):
    g = pl.program_id(1)
    qi = pl.program_id(2)
    tq = q_ref.shape[0]
    tk = tq
    seq = slc_ref.shape[0]
    dh = NSA_HEAD_DIM
    rep = q_ref.shape[1] // dh
    n_half = kc_ref.shape[0]
    n_pad = WINDOW // tk

    @pl.when(qi == 0)
    def _():
        lane = lax.broadcasted_iota(jnp.int32, (seq, LANES), 1)

        def max_sq_norm(x, width):
            s = jnp.dot((x * x).astype(BF16), seg_ref[0:width, :], preferred_element_type=F32)
            return jnp.max(s) * SAFETY

        kv = slc_ref[...]
        k2_sel = max_sq_norm(jnp.where(lane < dh, kv.astype(F32), 0.0), LANES)
        ksel_ref[...] = jnp.where(lane < dh, kv, featk_ref[...])
        vt = kv.astype(F32).T[dh:, :].astype(BF16)
        for kt in range(seq // tk):
            vselt_ref[kt] = vt[:, kt * tk:(kt + 1) * tk]
        kv = win_ref[...]
        k2_win = max_sq_norm(jnp.where(lane < dh, kv.astype(F32), 0.0), LANES)
        kwin_ref[0:WINDOW, :] = featw_ref[0:WINDOW, :]
        kwin_ref[WINDOW:, :] = jnp.where(lane < dh, kv, featw_ref[WINDOW:, :])
        vt = kv.astype(F32).T[dh:, :].astype(BF16)
        for kt in range(n_pad):
            vwint_ref[kt] = jnp.zeros((dh, tk), BF16)
        for kt in range(seq // tk):
            vwint_ref[n_pad + kt] = vt[:, kt * tk:(kt + 1) * tk]
        q2 = max_sq_norm(qall_ref[...].astype(F32), rep * dh) * (LOG2E * LOG2E)
        limit = SCORE_BOUND * SCORE_BOUND
        bounded_ref[0] = ((q2 * k2_sel <= limit) & (q2 * k2_win <= limit)).astype(jnp.int32)

    heads = lambda a: jnp.concatenate([a] * rep, axis=1)
    sub8 = lambda a: a.reshape(a.shape[0] // 8, 8, a.shape[1])

    def softmax_pv(lgs, v_tiles, bounded, extra=None):
        if bounded:
            m = 0.0
        else:
            m8 = functools.reduce(jnp.maximum, [jnp.max(sub8(lg), axis=0) for lg in lgs])
            m = jnp.max(m8, axis=0, keepdims=True)
        l8 = jnp.zeros((8, lgs[0].shape[1]), F32)
        acc = jnp.zeros((dh, lgs[0].shape[1]), F32)
        for lg, vt in zip(lgs, v_tiles):
            pt = jnp.exp2(lg) if bounded else jnp.exp2(lg - m)
            l8 = l8 + jnp.sum(sub8(pt), axis=0)
            acc = acc + jnp.dot(vt, pt.astype(BF16), preferred_element_type=F32)
        if extra is not None:
            l8, acc = extra(l8, acc)
        return acc / jnp.sum(l8, axis=0, keepdims=True)

    def step(n, bounded):
        static = isinstance(n, int)
        assert static == bounded
        t0 = n * tq
        lane = lax.broadcasted_iota(jnp.int32, (tq, LANES), 1)
        key_i = lax.broadcasted_iota(jnp.int32, (tk, tq), 0)
        qry_i = lax.broadcasted_iota(jnp.int32, (tk, tq), 1)
        causal_1 = jnp.where(key_i <= qry_i, 0.0, NEG_INF)
        causal_bias = heads(causal_1)
        band_bias = heads(jnp.where(key_i > qry_i, 0.0, NEG_INF))

        qfeat = qfeat_ref[0, n]
        rel_lanes = (lane >= REL_LANE0) & (lane < REL_LANE0 + 3)
        qa = []
        for r in range(rep):
            qcol = q_ref[:, (r // 2) * LANES:(r // 2 + 1) * LANES].astype(F32)
            if r % 2:
                qcol = pltpu.roll(qcol, dh, 1)
            feat = jnp.where(rel_lanes, qrel_ref[0, r * tq:(r + 1) * tq, :].astype(F32), qfeat[r:r + 1, :])
            qa.append(jnp.where(lane < dh, qcol * LOG2E, feat))
        q0 = jnp.concatenate(qa, axis=0).astype(BF16)

        gt_ref[...] = jax.nn.sigmoid(gate_ref[...]).T
        out_scale = []
        for pair in range(rep // 2):
            z = z_ref[:, pair * LANES:(pair + 1) * LANES]
            out_scale.append(norm_ref[:, pair * LANES:(pair + 1) * LANES] * (z * jax.nn.sigmoid(z)))

        nc = min(n_half, (t0 + tq) // CMP_STRIDE) if static else n_half
        n_idx = lax.broadcasted_iota(jnp.int32, (nc, tq), 0)
        t_idx = t0 + lax.broadcasted_iota(jnp.int32, (nc, tq), 1)
        valid_c = heads((CMP_STRIDE * n_idx + (CMP_BLOCK - 1) <= t_idx) & (n_idx < n_cmp))
        lg_c = jnp.where(valid_c, _nt_dot(kc_ref[0:nc, :], q0), NEG_INF)

        n_wt = n_pad + 1
        lgs = []
        for w in range(n_wt):
            rows = (slice(t0 + w * tk, t0 + (w + 1) * tk) if static
                    else pl.ds(pl.multiple_of(t0 + w * tk, tk), tk))
            lg = _nt_dot(kwin_ref[rows, :], q0)
            if w == 0:
                lg = lg + band_bias
            if w == n_wt - 1:
                lg = lg + causal_bias
            lgs.append(lg)
        o_win = softmax_pv(lgs, [vwint_ref[n + w] for w in range(n_wt)], bounded)

        e = jnp.exp2(lg_c - jnp.max(lg_c, axis=0, keepdims=True))
        p = jnp.where(valid_c, e / jnp.sum(e, axis=0, keepdims=True), 0.0)
        if nc < n_half:
            p = jnp.concatenate([p, jnp.zeros((n_half - nc, rep * tq), F32)], axis=0)
        o_cmp = jnp.dot(vct_ref[...], p.astype(BF16), preferred_element_type=F32)
        p_sum = p[:, 0:tq]
        for r in range(1, rep):
            p_sum = p_sum + p[:, r * tq:(r + 1) * tq]

        nb = min(MAX_SEL_BLOCKS, (t0 + tq) // SEL_BLOCK) if static else MAX_SEL_BLOCKS
        mm = mmap_ref[...]
        hi, mid, lo = _split3(p_sum)
        p_slc = (jnp.dot(mm, hi, preferred_element_type=F32) + jnp.dot(mm, mid, preferred_element_type=F32)
                 + jnp.dot(mm, lo, preferred_element_type=F32))[0:MAX_SEL_BLOCKS]
        if static:
            sel_scores = lambda kt: _nt_dot(ksel_ref[kt * tk:(kt + 1) * tk, :], q0)
            maybe_idle = [kt for kt in range(1, n + 1 - SEL_RECENT)]
            raw = {kt: sel_scores(kt) for kt in range(n + 1) if kt not in maybe_idle}
        jj = lax.broadcasted_iota(jnp.int32, (MAX_SEL_BLOCKS, tq), 0)
        tt = t0 + lax.broadcasted_iota(jnp.int32, (MAX_SEL_BLOCKS, tq), 1)
        cur = lax.shift_right_logical(tt, int(np.log2(SEL_BLOCK)))
        forced = (jj == 0) | (jj == cur) | (jj == cur - 1)
        future = jj > cur
        score = jnp.where(future, -1.0, p_slc + jnp.where(forced, SEL_BONUS, 0.0))
        rank = jnp.zeros((MAX_SEL_BLOCKS, tq), jnp.int32)
        for i in range(nb):
            other = jnp.broadcast_to(score[i:i + 1, :], score.shape)
            beats = (other > score) | ((other == score) & (jj > i))
            rank = rank + beats.astype(jnp.int32)
        keep = (rank < SEL_TOP) & jnp.logical_not(future)

        if static:
            blocks_per_tile = tk // SEL_BLOCK
            block_bias = jnp.where(keep, 0.0, NEG_INF)

            def tile_bias(kt):
                b = jnp.concatenate(
                    [jnp.broadcast_to(block_bias[blocks_per_tile * kt + j:blocks_per_tile * kt + j + 1, :],
                                      (SEL_BLOCK, tq)) for j in range(blocks_per_tile)], axis=0)
                return heads(b + causal_1 if kt == n else b)

            wanted = [jnp.max(block_bias[blocks_per_tile * kt:blocks_per_tile * (kt + 1)]) == 0.0
                      for kt in maybe_idle]

            def idle_tiles(l8, acc):
                for kt, some in zip(maybe_idle, wanted):
                    def live(l8, acc, kt=kt):
                        pt = jnp.exp2(sel_scores(kt) + tile_bias(kt))
                        return (l8 + jnp.sum(sub8(pt), axis=0),
                                acc + jnp.dot(vselt_ref[kt], pt.astype(BF16), preferred_element_type=F32))

                    l8, acc = lax.cond(some, live, lambda l8, acc: (l8, acc), l8, acc)
                return l8, acc

            o_slc = softmax_pv([raw[kt] + tile_bias(kt) for kt in raw], [vselt_ref[kt] for kt in raw],
                               bounded, extra=idle_tiles if maybe_idle else None)
        else:
            chosen = jnp.where(keep, 1.0, 0.0).astype(BF16)
            placed = _tn_dot(chosen, place_ref[...])
            sel_lanes = (lane >= SEL_LANE0) & (lane < SEL_LANE0 + MAX_SEL_BLOCKS)
            sel_bias = (placed - 1.0) * (-NEG_INF)
            qs = jnp.concatenate([jnp.where(sel_lanes, sel_bias, a) for a in qa], axis=0).astype(BF16)

            def sel_tile(kt, carry, bias=None):
                m, l, acc = carry
                lg = _nt_dot(ksel_ref[pl.ds(pl.multiple_of(kt * tk, tk), tk), :], qs)
                if bias is not None:
                    lg = lg + bias
                m_new = jnp.maximum(m, jnp.max(lg, axis=0, keepdims=True))
                alpha = jnp.exp2(m - m_new)
                pt = jnp.exp2(lg - m_new)
                return (m_new, alpha * l + jnp.sum(pt, axis=0, keepdims=True),
                        alpha * acc + jnp.dot(vselt_ref[kt], pt.astype(BF16), preferred_element_type=F32))

            init = (jnp.full((1, rep * tq), NEG_INF, F32), jnp.zeros((1, rep * tq), F32),
                    jnp.zeros((dh, rep * tq), F32))
            _, l, acc = sel_tile(n, lax.fori_loop(0, n, sel_tile, init), causal_bias)
            o_slc = acc / l

        ys = []
        for r in range(rep):
            cols = slice(r * tq, (r + 1) * tq)
            gate = lambda c: gt_ref[pl.ds(3 * (g * rep + r) + c, 1), :]
            o = gate(0) * o_cmp[:, cols] + gate(1) * o_slc[:, cols] + gate(2) * o_win[:, cols]
            ys.append(o * lax.rsqrt(jnp.mean(o * o, axis=0, keepdims=True) + EPS))
        for pair in range(rep // 2):
            cols = slice(pair * LANES, (pair + 1) * LANES)
            y = jnp.concatenate([ys[2 * pair], ys[2 * pair + 1]], axis=0).T
            o_ref[:, cols] = (y * out_scale[pair]).astype(o_ref.dtype)

    @pl.when(bounded_ref[0] == 1)
    def _():
        lax.switch(qi, [functools.partial(step, n, True) for n in range(seq // tq)])

    @pl.when(bounded_ref[0] == 0)
    def _():
        step(qi, False)


def _nsa(streams, gate, kc, vc, featk, featw, qfeat, qrel, seg, mmap, place, norm, *, batch, seq, n_cmp):
    tq = NSA_TILE
    nq = seq // tq
    G = NSA_KV_HEADS
    gw = norm.shape[1] // G
    n_half = kc.shape[0] // (batch * G)
    const = lambda a: pl.BlockSpec(a.shape, lambda b, g, i: (0,) * a.ndim)
    stream = lambda name: pl.BlockSpec((seq, LANES),
                                       lambda b, g, i, o=streams[name][1] // LANES: (b, o + g))
    tile = lambda name: pl.BlockSpec((tq, gw), lambda b, g, i, o=streams[name][1] // gw: (b * nq + i, o + g))
    return pl.pallas_call(
        functools.partial(_nsa_kernel, n_cmp=n_cmp),
        grid=(batch, G, nq),
        in_specs=[
            tile("q_a"),
            pl.BlockSpec((seq, gw), lambda b, g, i, o=streams["q_a"][1] // gw: (b, o + g)),
            tile("z_a"),
            pl.BlockSpec((tq, gate.shape[1]), lambda b, g, i: (b * nq + i, 0)),
            stream("slc"), stream("win"),
            pl.BlockSpec((n_half, LANES), lambda b, g, i: (b * G + g, 0)),
            pl.BlockSpec((NSA_HEAD_DIM, n_half), lambda b, g, i: (b * G + g, 0)),
            const(featk), const(featw),
            pl.BlockSpec((1,) + qfeat.shape[1:], lambda b, g, i: (g, 0, 0, 0)),
            pl.BlockSpec((1,) + qrel.shape[1:], lambda b, g, i: (g, 0, 0)),
            const(seg), const(mmap), const(place),
            pl.BlockSpec((1, gw), lambda b, g, i: (0, g)),
        ],
        out_specs=pl.BlockSpec((tq, gw), lambda b, g, i: (b * nq + i, g)),
        out_shape=jax.ShapeDtypeStruct((batch * seq, G * gw), BF16),
        scratch_shapes=[
            pltpu.VMEM((seq, LANES), BF16),
            pltpu.VMEM((seq // tq, NSA_HEAD_DIM, tq), BF16),
            pltpu.VMEM((seq + WINDOW, LANES), BF16),
            pltpu.VMEM(((seq + WINDOW) // tq, NSA_HEAD_DIM, tq), BF16),
            pltpu.VMEM((gate.shape[1], tq), F32),
            pltpu.SMEM((1,), jnp.int32),
        ],
        compiler_params=pltpu.CompilerParams(
            dimension_semantics=("arbitrary", "arbitrary", "arbitrary"),
            vmem_limit_bytes=VMEM_LIMIT),
        name="nsa",
    )(streams["q_a"][0], streams["q_a"][0], streams["z_a"][0], gate, streams["slc"][0], streams["win"][0],
      kc, vc, featk, featw, qfeat, qrel, seg, mmap, place, norm)


def _bf16_terms(x, n):
    terms, rest = [], np.asarray(x, np.float64)
    for _ in range(n):
        t = rest.astype(np.float32).astype(ml_dtypes.bfloat16).astype(np.float64)
        terms.append(t.astype(np.float32))
        rest = rest - t
    return terms


def _nsa_tables(seq, n_half, n_cmp, heads):
    assert seq // SEL_BLOCK <= MAX_SEL_BLOCKS and FLAG_LANE < LANES

    def key_features(pos, onehot_blocks):
        f = np.zeros((len(pos), LANES), np.float32)
        if onehot_blocks:
            f[np.arange(len(pos)), SEL_LANE0 + pos // SEL_BLOCK] = 1.0
        f[:, POS_LANE0:POS_LANE0 + N_SPLIT] = ((pos // 64) * 64)[:, None]
        f[:, POS_LANE0 + N_SPLIT:POS_LANE0 + 2 * N_SPLIT] = (pos % 64)[:, None]
        return f

    assert REL_LANE0 + 3 <= LANES
    featk = key_features(np.arange(seq), True)
    featk[:, SHIFT_LANE0:REL_LANE0 + 3] = 1.0
    featw = np.concatenate([np.zeros((WINDOW, LANES), np.float32), key_features(np.arange(seq), False)])
    featw[:WINDOW, FLAG_LANE] = NEG_INF
    featw[WINDOW:, SHIFT_LANE0:REL_LANE0 + 3] = 1.0
    featc = key_features(CMP_STRIDE * np.arange(n_half) + CMP_BLOCK - 1, False)
    featc[n_cmp:, FLAG_LANE] = NEG_INF

    slopes = (2.0 ** (-8.0 * np.arange(1, heads + 1) / heads)).astype(np.float32).astype(np.float64)
    slopes2 = slopes * LOG2E
    terms = _bf16_terms(slopes2, N_SPLIT)
    rep = heads // NSA_KV_HEADS
    nq = seq // NSA_TILE
    qfeat = np.zeros((NSA_KV_HEADS, nq, 8, LANES), np.float32)
    qrel = np.zeros((NSA_KV_HEADS, rep * NSA_TILE, LANES), np.float32)
    for h in range(heads):
        g, r = divmod(h, rep)
        for i, t in enumerate(terms):
            qfeat[g, :, r, POS_LANE0 + i] = t[h]
            qfeat[g, :, r, POS_LANE0 + N_SPLIT + i] = t[h]
        qfeat[g, :, r, FLAG_LANE] = 1.0
        for i, t in enumerate(_bf16_terms(-slopes2[h] * NSA_TILE * np.arange(nq), N_SPLIT)):
            qfeat[g, :, r, SHIFT_LANE0 + i] = t
        for i, t in enumerate(_bf16_terms(-slopes2[h] * np.arange(NSA_TILE), 3)):
            qrel[g, r * NSA_TILE:(r + 1) * NSA_TILE, REL_LANE0 + i] = t
    seg = np.zeros((2 * LANES, LANES), np.float32)
    seg[np.arange(2 * LANES), np.arange(2 * LANES) // NSA_HEAD_DIM] = 1.0

    cs = CMP_STRIDE * np.arange(n_half)[None, :]
    ss = SEL_BLOCK * np.arange(LANES)[:, None]
    overlap = np.clip(np.minimum(cs + CMP_BLOCK, ss + SEL_BLOCK) - np.maximum(cs, ss), 0, None)
    mmap = (overlap / CMP_BLOCK) * (np.arange(n_half)[None, :] < n_cmp) * (ss < seq)
    place = np.zeros((MAX_SEL_BLOCKS, LANES), np.float32)
    place[np.arange(MAX_SEL_BLOCKS), SEL_LANE0 + np.arange(MAX_SEL_BLOCKS)] = 1.0
    bf = lambda a: jnp.asarray(a, dtype=BF16)
    return bf(featk), bf(featw), bf(featc), jnp.asarray(qfeat), bf(qrel), bf(seg), bf(mmap), bf(place)


def _hgrn_stages(q_ref, f_ref, v_ref, z_ref, lb_ref, norm_ref, cum_ref, o_ref, rows):
    C, SUB = HGRN_CHUNK, HGRN_SUB
    n_sub = C // SUB
    chunks = range(len(rows))
    sub = lambda x, i: x[i * SUB:(i + 1) * SUB]
    w = {}

    def gates():
        lbr = lb_ref[...]
        e = jnp.exp(lbr - jnp.max(lbr, axis=0, keepdims=True))
        lb = e[0:1, :] / jnp.sum(e, axis=0, keepdims=True)
        w["v16"] = [v_ref[rows[j], :].astype(BF16) for j in chunks]
        f = [lb + (1.0 - lb) * jax.nn.sigmoid(f_ref[rows[j], :]) for j in chunks]
        w["k"] = [1.0 - f[j] for j in chunks]
        w["parts"] = [_split3(jnp.log(f[j])) for j in chunks]

    def cumsums():
        cum = cum_ref[...]
        parts = w["parts"]
        w["a"] = [(jnp.dot(cum, parts[j][0], preferred_element_type=F32)
                   + jnp.dot(cum, parts[j][1], preferred_element_type=F32)
                   + jnp.dot(cum, parts[j][2], preferred_element_type=F32)) for j in chunks]

    def operands():
        a, k = w["a"], w["k"]
        q1, k1, k2, qb, k3, start, dec = [], [], [], [], [], [], []
        for j in chunks:
            q1.append(q_ref[rows[j], :] * jnp.exp(a[j]))
            k1.append(k[j] * jnp.exp(-a[j]))
            tot = [a[j][(i + 1) * SUB - 1:(i + 1) * SUB] for i in range(n_sub)]
            s = [jnp.zeros_like(tot[0])]
            for i in range(n_sub):
                s.append(s[i] + tot[i])
            start.append(s)
            dec.append(jnp.exp(s[n_sub]))
            qb.append(jnp.concatenate([sub(q1[j], i) * jnp.exp(s[i]) for i in range(n_sub)], axis=0)
                      .astype(BF16))
            k2.append([sub(k1[j], i) * jnp.exp(tot[i]) for i in range(n_sub)])
            k3.append(jnp.concatenate([sub(k1[j], i) * jnp.exp(s[n_sub] - s[i]) for i in range(n_sub)],
                                      axis=0).astype(BF16))
        w.update(q1=q1, k1=k1, k2=k2, qb=qb, k3=k3, start=start, dec=dec)

    def scores():
        q1, k1, k2, start = w["q1"], w["k1"], w["k2"], w["start"]
        att = []
        for j in chunks:
            row_blocks = []
            for i in range(n_sub):
                rhs = [k2[j][jb] * jnp.exp(start[j][i] - start[j][jb + 1]) if jb < i - 1 else k2[j][jb]
                       for jb in range(i)]
                rhs.append(sub(k1[j], i))
                rhs = jnp.concatenate(rhs, axis=0) if len(rhs) > 1 else rhs[0]
                sc = _nt_dot(sub(q1[j], i).astype(BF16), rhs.astype(BF16))
                width = (i + 1) * SUB
                causal = (lax.broadcasted_iota(jnp.int32, (SUB, width), 1)
                          <= lax.broadcasted_iota(jnp.int32, (SUB, width), 0) + i * SUB)
                row_blocks.append(jnp.where(causal, sc, 0.0).astype(BF16))
            att.append(row_blocks)
        w["att"] = att

    def products():
        att, v16, k3 = w["att"], w["v16"], w["k3"]
        w["intra"] = [jnp.concatenate([jnp.dot(att[j][i], v16[j][0:(i + 1) * SUB],
                                               preferred_element_type=F32)
                                       for i in range(n_sub)], axis=0) for j in chunks]
        w["incr"] = [_tn_dot(v16[j], k3[j]) for j in chunks]

    def state_pass(st):
        inter = []
        for j in chunks:
            inter.append(_nt_dot(w["qb"][j], st.astype(BF16)))
            st = st * w["dec"][j] + w["incr"][j]
        w["inter"] = inter
        return st

    def finish():
        gain = norm_ref[...]
        for j in chunks:
            o = w["inter"][j] + w["intra"][j]
            y = o * lax.rsqrt(jnp.mean(o * o, axis=-1, keepdims=True) + EPS) * gain
            z = z_ref[rows[j], :]
            o_ref[rows[j], :] = (y * (z * jax.nn.sigmoid(z))).astype(o_ref.dtype)

    return gates, cumsums, operands, scores, products, state_pass, finish


def _hgrn_kernel(q_ref, f_ref, v_ref, z_ref, lb_ref, norm_ref, cum_ref, o_ref, *, group):
    seq, dk = q_ref.shape

    def block(cb, st):
        rows = [pl.ds(pl.multiple_of((cb * group + j) * HGRN_CHUNK, HGRN_CHUNK), HGRN_CHUNK)
                for j in range(group)]
        gates, cumsums, operands, scores, products, state_pass, finish = _hgrn_stages(
            q_ref, f_ref, v_ref, z_ref, lb_ref, norm_ref, cum_ref, o_ref, rows)
        gates()
        cumsums()
        operands()
        scores()
        products()
        st = state_pass(st)
        finish()
        return st

    lax.fori_loop(0, seq // (HGRN_CHUNK * group), block, jnp.zeros((dk, dk), F32))


def _hgrn_patterns():
    C, SUB = HGRN_CHUNK, HGRN_SUB
    t = np.arange(C)[:, None]
    s = np.arange(C)[None, :]
    return (((t // SUB) == (s // SUB)) & (s <= t)).astype(np.float32)


def _hgrn(streams, lower_bounds, norm, *, batch, seq):
    dk = HGRN_HEAD_DIM
    heads = norm.shape[1] // dk
    cum = jnp.asarray(_hgrn_patterns(), dtype=BF16)
    col = lambda name: (lambda b, h, o=streams[name][1] // dk: (b, o + h))
    return pl.pallas_call(
        functools.partial(_hgrn_kernel, group=16),
        grid=(batch, heads),
        in_specs=[
            pl.BlockSpec((seq, dk), col("q_h")),
            pl.BlockSpec((seq, dk), col("f_h")),
            pl.BlockSpec((seq, dk), col("i_h")),
            pl.BlockSpec((seq, dk), col("z_h")),
            pl.BlockSpec((lower_bounds.shape[0], dk), lambda b, h: (0, h)),
            pl.BlockSpec((1, dk), lambda b, h: (0, h)),
            pl.BlockSpec(cum.shape, lambda b, h: (0, 0)),
        ],
        out_specs=pl.BlockSpec((seq, dk), lambda b, h: (b, h)),
        out_shape=jax.ShapeDtypeStruct((batch * seq, heads * dk), BF16),
        compiler_params=pltpu.CompilerParams(
            dimension_semantics=("arbitrary", "arbitrary"),
            vmem_limit_bytes=VMEM_LIMIT),
        name="hgrn",
    )(streams["q_h"][0], streams["f_h"][0], streams["i_h"][0], streams["z_h"][0], lower_bounds, norm, cum)


def _outproj_kernel(x_ref, oa_ref, oh_ref, wa_ref, wh_ref, g_ref, o_ref):
    y = x_ref[...] + jnp.dot(oa_ref[...], wa_ref[...], preferred_element_type=F32)
    y = y + jnp.dot(oh_ref[...], wh_ref[...], preferred_element_type=F32)
    ms = jnp.mean(y * y, axis=-1, keepdims=True)
    o_ref[...] = y * lax.rsqrt(ms + EPS) * g_ref[...]


def _outproj(x2, o_a, o_h, wa, wh, g, *, tm):
    M, D = x2.shape
    const = lambda a: pl.BlockSpec(a.shape, lambda i: (0,) * a.ndim)
    return pl.pallas_call(
        _outproj_kernel,
        grid=(M // tm,),
        in_specs=[
            pl.BlockSpec((tm, D), lambda i: (i, 0)),
            pl.BlockSpec((tm, o_a.shape[1]), lambda i: (i, 0)),
            pl.BlockSpec((tm, o_h.shape[1]), lambda i: (i, 0)),
            const(wa), const(wh), const(g),
        ],
        out_specs=pl.BlockSpec((tm, D), lambda i: (i, 0)),
        out_shape=jax.ShapeDtypeStruct((M, D), F32),
        compiler_params=pltpu.CompilerParams(
            dimension_semantics=("arbitrary",),
            vmem_limit_bytes=VMEM_LIMIT),
        name="outproj",
    )(x2, o_a, o_h, wa, wh, g)


def kernel(x, norm_in, w_in, cmp_pe_k, cmp_w1_k, cmp_w2_k, cmp_pe_v, cmp_w1_v, cmp_w2_v,
           lower_bounds, nsa_out_norm, hgrn_out_norm, w_out, final_norm):
    B, S, D = x.shape
    assert norm_in.shape[0] == 1, "single-layer problem"
    nsa_w = nsa_out_norm.shape[1]
    hgrn_w = hgrn_out_norm.shape[1]
    dh = NSA_HEAD_DIM
    G = NSA_KV_HEADS
    heads = nsa_w // dh
    kvw = G * dh
    n_gate = 3 * heads
    n_cmp = (S - CMP_BLOCK) // CMP_STRIDE + 1
    n_half = S // CMP_STRIDE
    assert S % NSA_TILE == 0 and WINDOW % NSA_TILE == 0 and n_half <= LANES and 2 * dh == LANES

    names = ["q_a", "k_cmp", "v_cmp", "k_slc", "v_slc", "k_win", "v_win", "gate", "z_a",
             "q_h", "f_h", "i_h", "z_h"]
    widths = [nsa_w] + [kvw] * 6 + [n_gate, nsa_w] + [hgrn_w] * 4
    starts = dict(zip(names, np.cumsum([0] + widths[:-1]).tolist()))
    wd = dict(zip(names, widths))

    paired = lambda kname, vname: [(starts[n] + g * dh, dh) for g in range(G) for n in (kname, vname)]
    whole = lambda n: [(starts[n], wd[n])]
    groups = [[("q_a", whole("q_a")),
               ("cmp", paired("k_cmp", "v_cmp")), ("slc", paired("k_slc", "v_slc")),
               ("win", paired("k_win", "v_win")), ("q_h", whole("q_h")), ("i_h", whole("i_h"))],
              [("z_a", whole("z_a")), ("f_h", whole("f_h")), ("z_h", whole("z_h"))]]
    plan, col_of, off = [], {}, 0
    for group in groups:
        for n, pieces in group:
            col_of[n] = off
            for src, width in pieces:
                plan.append((off, src, width, dh ** -0.5 if n == "q_a" else 1.0))
                off += width
    n_main = off
    n16_cols = col_of["z_a"]
    plan.append((n_main, starts["gate"], n_gate, 1.0))
    w_all = _relayout_weights(w_in[0].T, tuple(plan), n_main + LANES)

    x2 = x.reshape(B * S, D)
    proj16, proj32, gate = _proj(x2, norm_in, w_all, tm=min(1024, B * S), tn=1536,
                                 n_main=n_main, n16_cols=n16_cols)
    streams = {}
    for arr, group, base in ((proj16, groups[0], 0), (proj32, groups[1], n16_cols)):
        for n, _ in group:
            streams[n] = (arr, col_of[n] - base)

    featk, featw, featc, qfeat, qrel, seg, mmap, place = _nsa_tables(S, n_half, n_cmp, heads)

    def w1_halves(w1k, w1v):
        hk = w1k.shape[1]
        k3 = w1k.reshape(2, CMP_STRIDE, dh, hk)
        v3 = w1v.reshape(2, CMP_STRIDE, dh, hk)
        zk = jnp.zeros_like(k3[0])
        top = lambda a: jnp.concatenate([a, zk], axis=-1)
        bot = lambda a: jnp.concatenate([zk, a], axis=-1)
        half = lambda i: jnp.concatenate([top(k3[i]), bot(v3[i])], axis=1).reshape(CMP_STRIDE * 2 * dh, 2 * hk)
        return half(0).astype(BF16), half(1).astype(BF16)

    wp, wq = w1_halves(cmp_w1_k[0], cmp_w1_v[0])
    zk = jnp.zeros_like(cmp_w2_k[0])
    w2 = jnp.concatenate([jnp.concatenate([cmp_w2_k[0], zk], axis=1),
                          jnp.concatenate([zk, cmp_w2_v[0]], axis=1)], axis=0).astype(BF16)
    pe = jnp.concatenate([cmp_pe_k[0].reshape(2, CMP_STRIDE, dh), cmp_pe_v[0].reshape(2, CMP_STRIDE, dh)],
                         axis=-1).reshape(2, CMP_STRIDE * 2 * dh)
    kc, vc = _compress(streams["cmp"][0], pe, wp, wq, w2, jnp.tile(featc, (G, 1)),
                       batch=B, seq=S, col0=streams["cmp"][1], n_half=n_half)

    o_a = _nsa(streams, gate, kc, vc, featk, featw, qfeat, qrel, seg, mmap, place, nsa_out_norm,
               batch=B, seq=S, n_cmp=n_cmp)
    o_h = _hgrn(streams, lower_bounds, hgrn_out_norm, batch=B, seq=S)

    wo = w_out[0].astype(BF16)
    out = _outproj(x2, o_a, o_h, wo[:nsa_w], wo[nsa_w:], final_norm.reshape(1, D), tm=512)
    return out.reshape(B, S, D)
```

```python
import functools

import ml_dtypes
import numpy as np
import jax
import jax.numpy as jnp
from jax import lax
from jax.experimental import pallas as pl
from jax.experimental.pallas import tpu as pltpu

F32 = jnp.float32
BF16 = jnp.bfloat16

EPS = 1e-6
NEG_INF = -1e30
LOG2E = 1.4426950408889634

NSA_HEAD_DIM = 64
NSA_KV_HEADS = 4
CMP_BLOCK = 32
CMP_STRIDE = 16
SEL_BLOCK = 64
SEL_TOP = 8
SEL_BONUS = 1.0e4
WINDOW = 512
HGRN_HEAD_DIM = 128
HGRN_CHUNK = 64
HGRN_SUB = 16
HGRN_MAX_LOG_DECAY = 60.0

LANES = 128
VMEM_LIMIT = 56 * 1024 * 1024
NSA_TILE = 256
SEL_RECENT = 3

SEL_LANE0 = NSA_HEAD_DIM
MAX_SEL_BLOCKS = 32
POS_LANE0 = SEL_LANE0 + MAX_SEL_BLOCKS
N_SPLIT = 4
FLAG_LANE = POS_LANE0 + 2 * N_SPLIT
SHIFT_LANE0 = FLAG_LANE + 1
REL_LANE0 = SHIFT_LANE0 + N_SPLIT
SCORE_BOUND = 100.0
SAFETY = 1.02


def _nt_dot(a, b):
    return lax.dot_general(a, b, (((1,), (1,)), ((), ())), preferred_element_type=F32)


def _tn_dot(a, b):
    return lax.dot_general(a, b, (((0,), (0,)), ((), ())), preferred_element_type=F32)


def _split3(x):
    hi = x.astype(BF16)
    r1 = x - hi.astype(F32)
    mid = r1.astype(BF16)
    lo = (r1 - mid.astype(F32)).astype(BF16)
    return hi, mid, lo


def _proj_kernel(x_ref, g_ref, w_ref, wg_ref, o16_ref, o32_ref, og_ref, h_ref, *, row_chunk, n16):
    j = pl.program_id(1)

    @pl.when(j == 0)
    def _():
        n_chunks = x_ref.shape[0] // row_chunk

        def body(c, carry):
            rows = pl.ds(pl.multiple_of(c * row_chunk, row_chunk), row_chunk)
            x = x_ref[rows, :]
            ms = jnp.mean(x * x, axis=-1, keepdims=True)
            h_ref[rows, :] = (x * lax.rsqrt(ms + EPS) * g_ref[...]).astype(BF16)
            return carry

        lax.fori_loop(0, n_chunks, body, 0)
        og_ref[...] = jnp.dot(h_ref[...], wg_ref[...], preferred_element_type=F32)

    @pl.when(j < n16)
    def _():
        o16_ref[...] = jnp.dot(h_ref[...], w_ref[...], preferred_element_type=F32).astype(BF16)

    @pl.when(j >= n16)
    def _():
        o32_ref[...] = jnp.dot(h_ref[...], w_ref[...], preferred_element_type=F32)


def _relayout_kernel(wt_ref, o_ref, *, plan):
    o_ref[:, o_ref.shape[1] - LANES:] = jnp.zeros((o_ref.shape[0], LANES), o_ref.dtype)
    for dst, src, width, scale in plan:
        o_ref[:, dst:dst + width] = (wt_ref[src:src + width, :].T * scale).astype(o_ref.dtype)


def _relayout_weights(wt, plan, n_out, *, rows=256):
    n_in, D = wt.shape
    return pl.pallas_call(
        functools.partial(_relayout_kernel, plan=plan),
        grid=(D // rows,),
        in_specs=[pl.BlockSpec((n_in, rows), lambda i: (0, i))],
        out_specs=pl.BlockSpec((rows, n_out), lambda i: (i, 0)),
        out_shape=jax.ShapeDtypeStruct((D, n_out), BF16),
        compiler_params=pltpu.CompilerParams(
            dimension_semantics=("arbitrary",),
            vmem_limit_bytes=VMEM_LIMIT),
        name="relayout",
    )(wt)


def _proj(x2, g, w, *, tm, tn, n_main, n16_cols):
    M, D = x2.shape
    N = n_main
    NG = w.shape[1] - n_main
    n16 = n16_cols // tn
    assert n16 * tn == n16_cols and N % tn == 0 and N % NG == 0
    return pl.pallas_call(
        functools.partial(_proj_kernel, row_chunk=128, n16=n16),
        grid=(M // tm, N // tn),
        in_specs=[
            pl.BlockSpec((tm, D), lambda i, j: (i, 0)),
            pl.BlockSpec((1, D), lambda i, j: (0, 0)),
            pl.BlockSpec((D, tn), lambda i, j: (0, j)),
            pl.BlockSpec((D, NG), lambda i, j: (0, N // NG)),
        ],
        out_specs=[
            pl.BlockSpec((tm, tn), lambda i, j: (i, jnp.minimum(j, n16 - 1))),
            pl.BlockSpec((tm, tn), lambda i, j: (i, jnp.maximum(j - n16, 0))),
            pl.BlockSpec((tm, NG), lambda i, j: (i, 0)),
        ],
        out_shape=[
            jax.ShapeDtypeStruct((M, n16_cols), BF16),
            jax.ShapeDtypeStruct((M, N - n16_cols), F32),
            jax.ShapeDtypeStruct((M, NG), F32),
        ],
        scratch_shapes=[pltpu.VMEM((tm, D), BF16)],
        compiler_params=pltpu.CompilerParams(
            dimension_semantics=("arbitrary", "arbitrary"),
            vmem_limit_bytes=VMEM_LIMIT),
        name="proj",
    )(x2, g, w, w)


def _compress_kernel(c0_ref, c1_ref, c2_ref, c3_ref, pe_ref, wp_ref, wq_ref, w2_ref, feat_ref,
                     kc_ref, vc_ref, x_ref, c32_ref, *, n_half):
    for g, c_ref in enumerate((c0_ref, c1_ref, c2_ref, c3_ref)):
        c32_ref[...] = c_ref[...].astype(F32)
        for l in range(CMP_STRIDE):
            x_ref[g * n_half:(g + 1) * n_half, l * LANES:(l + 1) * LANES] = (
                c32_ref[pl.ds(l, n_half, stride=CMP_STRIDE), :])
    x = x_ref[...]
    rows = x.shape[0]
    first = jnp.dot((x + pe_ref[0:1, :]).astype(BF16), wp_ref[...], preferred_element_type=F32)
    second = jnp.dot((x + pe_ref[1:2, :]).astype(BF16), wq_ref[...], preferred_element_type=F32)
    hidden = first + pltpu.roll(second, rows - 1, 0)
    out = jnp.dot(jax.nn.gelu(hidden).astype(BF16), w2_ref[...], preferred_element_type=F32)
    lane = lax.broadcasted_iota(jnp.int32, out.shape, 1)
    dh = NSA_HEAD_DIM
    kc_ref[...] = jnp.where(lane < dh, out.astype(BF16), feat_ref[...])
    for g in range(NSA_KV_HEADS):
        vc_ref[g * dh:(g + 1) * dh, :] = out[g * n_half:(g + 1) * n_half, :].T[dh:, :].astype(BF16)


def _compress(proj, pe, wp, wq, w2, feat, *, batch, seq, col0, n_half):
    rows = NSA_KV_HEADS * n_half
    rows_t = NSA_KV_HEADS * NSA_HEAD_DIM
    const = lambda a: pl.BlockSpec(a.shape, lambda b: (0,) * a.ndim)
    stream = lambda g: pl.BlockSpec((seq, LANES), lambda b, o=col0 // LANES + g: (b, o))
    return pl.pallas_call(
        functools.partial(_compress_kernel, n_half=n_half),
        grid=(batch,),
        in_specs=[stream(0), stream(1), stream(2), stream(3),
                  const(pe), const(wp), const(wq), const(w2), const(feat)],
        out_specs=[
            pl.BlockSpec((rows, LANES), lambda b: (b, 0)),
            pl.BlockSpec((rows_t, n_half), lambda b: (b, 0)),
        ],
        out_shape=[
            jax.ShapeDtypeStruct((batch * rows, LANES), BF16),
            jax.ShapeDtypeStruct((batch * rows_t, n_half), BF16),
        ],
        scratch_shapes=[pltpu.VMEM((rows, CMP_STRIDE * LANES), F32), pltpu.VMEM((seq, LANES), F32)],
        compiler_params=pltpu.CompilerParams(
            dimension_semantics=("arbitrary",),
            vmem_limit_bytes=VMEM_LIMIT),
        name="compress",
    )(proj, proj, proj, proj, pe, wp, wq, w2, feat)


def _nsa_kernel(q_ref, qall_ref, z_ref, gate_ref, slc_ref, win_ref, kc_ref, vct_ref, featk_ref, featw_ref,
                qfeat_ref, qrel_ref, seg_ref, mmap_ref, place_ref, norm_ref, o_ref,
                ksel_ref, vselt_ref, kwin_ref, vwint_ref, gt_ref, bounded_ref, *, n_cmp):
    g = pl.program_id(1)
    qi = pl.program_id(2)
    tq = q_ref.shape[0]
    tk = tq
    seq = slc_ref.shape[0]
    dh = NSA_HEAD_DIM
    rep = q_ref.shape[1] // dh
    n_half = kc_ref.shape[0]
    n_pad = WINDOW // tk

    @pl.when(qi == 0)
    def _():
        lane = lax.broadcasted_iota(jnp.int32, (seq, LANES), 1)

        def max_sq_norm(x, width):
            s = jnp.dot((x * x).astype(BF16), seg_ref[0:width, :], preferred_element_type=F32)
            return jnp.max(s) * SAFETY

        kv = slc_ref[...]
        k2_sel = max_sq_norm(jnp.where(lane < dh, kv.astype(F32), 0.0), LANES)
        ksel_ref[...] = jnp.where(lane < dh, kv, featk_ref[...])
        vt = kv.astype(F32).T[dh:, :].astype(BF16)
        for kt in range(seq // tk):
            vselt_ref[kt] = vt[:, kt * tk:(kt + 1) * tk]
        kv = win_ref[...]
        k2_win = max_sq_norm(jnp.where(lane < dh, kv.astype(F32), 0.0), LANES)
        kwin_ref[0:WINDOW, :] = featw_ref[0:WINDOW, :]
        kwin_ref[WINDOW:, :] = jnp.where(lane < dh, kv, featw_ref[WINDOW:, :])
        vt = kv.astype(F32).T[dh:, :].astype(BF16)
        for kt in range(n_pad):
            vwint_ref[kt] = jnp.zeros((dh, tk), BF16)
        for kt in range(seq // tk):
            vwint_ref[n_pad + kt] = vt[:, kt * tk:(kt + 1) * tk]
        q2 = max_sq_norm(qall_ref[...].astype(F32), rep * dh) * (LOG2E * LOG2E)
        limit = SCORE_BOUND * SCORE_BOUND
        bounded_ref[0] = ((q2 * k2_sel <= limit) & (q2 * k2_win <= limit)).astype(jnp.int32)

    heads = lambda a: jnp.concatenate([a] * rep, axis=1)
    sub8 = lambda a: a.reshape(a.shape[0] // 8, 8, a.shape[1])

    def softmax_pv(lgs, v_tiles, bounded, extra=None):
        if bounded:
            m = 0.0
        else:
            m8 = functools.reduce(jnp.maximum, [jnp.max(sub8(lg), axis=0) for lg in lgs])
            m = jnp.max(m8, axis=0, keepdims=True)
        l8 = jnp.zeros((8, lgs[0].shape[1]), F32)
        acc = jnp.zeros((dh, lgs[0].shape[1]), F32)
        for lg, vt in zip(lgs, v_tiles):
            pt = jnp.exp2(lg) if bounded else jnp.exp2(lg - m)
            l8 = l8 + jnp.sum(sub8(pt), axis=0)
            acc = acc + jnp.dot(vt, pt.astype(BF16), preferred_element_type=F32)
        if extra is not None:
            l8, acc = extra(l8, acc)
        return acc / jnp.sum(l8, axis=0, keepdims=True)

    def step(n, bounded):
        static = isinstance(n, int)
        assert static == bounded
        t0 = n * tq
        lane = lax.broadcasted_iota(jnp.int32, (tq, LANES), 1)
        key_i = lax.broadcasted_iota(jnp.int32, (tk, tq), 0)
        qry_i = lax.broadcasted_iota(jnp.int32, (tk, tq), 1)
        causal_1 = jnp.where(key_i <= qry_i, 0.0, NEG_INF)
        causal_bias = heads(causal_1)
        band_bias = heads(jnp.where(key_i > qry_i, 0.0, NEG_INF))

        qfeat = qfeat_ref[0, n]
        rel_lanes = (lane >= REL_LANE0) & (lane < REL_LANE0 + 3)
        qa = []
        for r in range(rep):
            qcol = q_ref[:, (r // 2) * LANES:(r // 2 + 1) * LANES].astype(F32)
            if r % 2:
                qcol = pltpu.roll(qcol, dh, 1)
            feat = jnp.where(rel_lanes, qrel_ref[0, r * tq:(r + 1) * tq, :].astype(F32), qfeat[r:r + 1, :])
            qa.append(jnp.where(lane < dh, qcol * LOG2E, feat))
        q0 = jnp.concatenate(qa, axis=0).astype(BF16)

        gt_ref[...] = jax.nn.sigmoid(gate_ref[...]).T
        out_scale = []
        for pair in range(rep // 2):
            z = z_ref[:, pair * LANES:(pair + 1) * LANES]
            out_scale.append(norm_ref[:, pair * LANES:(pair + 1) * LANES] * (z * jax.nn.sigmoid(z)))

        nc = min(n_half, (t0 + tq) // CMP_STRIDE) if static else n_half
        n_idx = lax.broadcasted_iota(jnp.int32, (nc, tq), 0)
        t_idx = t0 + lax.broadcasted_iota(jnp.int32, (nc, tq), 1)
        valid_c = heads((CMP_STRIDE * n_idx + (CMP_BLOCK - 1) <= t_idx) & (n_idx < n_cmp))
        lg_c = jnp.where(valid_c, _nt_dot(kc_ref[0:nc, :], q0), NEG_INF)

        n_wt = n_pad + 1
        lgs = []
        for w in range(n_wt):
            rows = (slice(t0 + w * tk, t0 + (w + 1) * tk) if static
                    else pl.ds(pl.multiple_of(t0 + w * tk, tk), tk))
            lg = _nt_dot(kwin_ref[rows, :], q0)
            if w == 0:
                lg = lg + band_bias
            if w == n_wt - 1:
                lg = lg + causal_bias
            lgs.append(lg)
        o_win = softmax_pv(lgs, [vwint_ref[n + w] for w in range(n_wt)], bounded)

        e = jnp.exp2(lg_c - jnp.max(lg_c, axis=0, keepdims=True))
        p = jnp.where(valid_c, e / jnp.sum(e, axis=0, keepdims=True), 0.0)
        if nc < n_half:
            p = jnp.concatenate([p, jnp.zeros((n_half - nc, rep * tq), F32)], axis=0)
        o_cmp = jnp.dot(vct_ref[...], p.astype(BF16), preferred_element_type=F32)
        p_sum = p[:, 0:tq]
        for r in range(1, rep):
            p_sum = p_sum + p[:, r * tq:(r + 1) * tq]

        nb = min(MAX_SEL_BLOCKS, (t0 + tq) // SEL_BLOCK) if static else MAX_SEL_BLOCKS
        mm = mmap_ref[...]
        hi, mid, lo = _split3(p_sum)
        p_slc = (jnp.dot(mm, hi, preferred_element_type=F32) + jnp.dot(mm, mid, preferred_element_type=F32)
                 + jnp.dot(mm, lo, preferred_element_type=F32))[0:MAX_SEL_BLOCKS]
        if static:
            sel_scores = lambda kt: _nt_dot(ksel_ref[kt * tk:(kt + 1) * tk, :], q0)
            maybe_idle = [kt for kt in range(1, n + 1 - SEL_RECENT)]
            raw = {kt: sel_scores(kt) for kt in range(n + 1) if kt not in maybe_idle}
        jj = lax.broadcasted_iota(jnp.int32, (MAX_SEL_BLOCKS, tq), 0)
        tt = t0 + lax.broadcasted_iota(jnp.int32, (MAX_SEL_BLOCKS, tq), 1)
        cur = lax.shift_right_logical(tt, int(np.log2(SEL_BLOCK)))
        forced = (jj == 0) | (jj == cur) | (jj == cur - 1)
        future = jj > cur
        score = jnp.where(future, -1.0, p_slc + jnp.where(forced, SEL_BONUS, 0.0))
        rank = jnp.zeros((MAX_SEL_BLOCKS, tq), jnp.int32)
        for i in range(nb):
            other = jnp.broadcast_to(score[i:i + 1, :], score.shape)
            beats = (other > score) | ((other == score) & (jj > i))
            rank = rank + beats.astype(jnp.int32)
        keep = (rank < SEL_TOP) & jnp.logical_not(future)

        if static:
            blocks_per_tile = tk // SEL_BLOCK
            block_bias = jnp.where(keep, 0.0, NEG_INF)

            def tile_bias(kt):
                b = jnp.concatenate(
                    [jnp.broadcast_to(block_bias[blocks_per_tile * kt + j:blocks_per_tile * kt + j + 1, :],
                                      (SEL_BLOCK, tq)) for j in range(blocks_per_tile)], axis=0)
                return heads(b + causal_1 if kt == n else b)

            wanted = [jnp.max(block_bias[blocks_per_tile * kt:blocks_per_tile * (kt + 1)]) == 0.0
                      for kt in maybe_idle]

            def idle_tiles(l8, acc):
                for kt, some in zip(maybe_idle, wanted):
                    def live(l8, acc, kt=kt):
                        pt = jnp.exp2(sel_scores(kt) + tile_bias(kt))
                        return (l8 + jnp.sum(sub8(pt), axis=0),
                                acc + jnp.dot(vselt_ref[kt], pt.astype(BF16), preferred_element_type=F32))

                    l8, acc = lax.cond(some, live, lambda l8, acc: (l8, acc), l8, acc)
                return l8, acc

            o_slc = softmax_pv([raw[kt] + tile_bias(kt) for kt in raw], [vselt_ref[kt] for kt in raw],
                               bounded, extra=idle_tiles if maybe_idle else None)
        else:
            chosen = jnp.where(keep, 1.0, 0.0).astype(BF16)
            placed = _tn_dot(chosen, place_ref[...])
            sel_lanes = (lane >= SEL_LANE0) & (lane < SEL_LANE0 + MAX_SEL_BLOCKS)
            sel_bias = (placed - 1.0) * (-NEG_INF)
            qs = jnp.concatenate([jnp.where(sel_lanes, sel_bias, a) for a in qa], axis=0).astype(BF16)

            def sel_tile(kt, carry, bias=None):
                m, l, acc = carry
                lg = _nt_dot(ksel_ref[pl.ds(pl.multiple_of(kt * tk, tk), tk), :], qs)
                if bias is not None:
                    lg = lg + bias
                m_new = jnp.maximum(m, jnp.max(lg, axis=0, keepdims=True))
                alpha = jnp.exp2(m - m_new)
                pt = jnp.exp2(lg - m_new)
                return (m_new, alpha * l + jnp.sum(pt, axis=0, keepdims=True),
                        alpha * acc + jnp.dot(vselt_ref[kt], pt.astype(BF16), preferred_element_type=F32))

            init = (jnp.full((1, rep * tq), NEG_INF, F32), jnp.zeros((1, rep * tq), F32),
                    jnp.zeros((dh, rep * tq), F32))
            _, l, acc = sel_tile(n, lax.fori_loop(0, n, sel_tile, init), causal_bias)
            o_slc = acc / l

        ys = []
        for r in range(rep):
            cols = slice(r * tq, (r + 1) * tq)
            gate = lambda c: gt_ref[pl.ds(3 * (g * rep + r) + c, 1), :]
            o = gate(0) * o_cmp[:, cols] + gate(1) * o_slc[:, cols] + gate(2) * o_win[:, cols]
            ys.append(o * lax.rsqrt(jnp.mean(o * o, axis=0, keepdims=True) + EPS))
        for pair in range(rep // 2):
            cols = slice(pair * LANES, (pair + 1) * LANES)
            y = jnp.concatenate([ys[2 * pair], ys[2 * pair + 1]], axis=0).T
            o_ref[:, cols] = (y * out_scale[pair]).astype(o_ref.dtype)

    @pl.when(bounded_ref[0] == 1)
    def _():
        lax.switch(qi, [functools.partial(step, n, True) for n in range(seq // tq)])

    @pl.when(bounded_ref[0] == 0)
    def _():
        step(qi, False)


def _nsa(streams, gate, kc, vc, featk, featw, qfeat, qrel, seg, mmap, place, norm, *, batch, seq, n_cmp):
    tq = NSA_TILE
    nq = seq // tq
    G = NSA_KV_HEADS
    gw = norm.shape[1] // G
    n_half = kc.shape[0] // (batch * G)
    const = lambda a: pl.BlockSpec(a.shape, lambda b, g, i: (0,) * a.ndim)
    stream = lambda name: pl.BlockSpec((seq, LANES),
                                       lambda b, g, i, o=streams[name][1] // LANES: (b, o + g))
    tile = lambda name: pl.BlockSpec((tq, gw), lambda b, g, i, o=streams[name][1] // gw: (b * nq + i, o + g))
    return pl.pallas_call(
        functools.partial(_nsa_kernel, n_cmp=n_cmp),
        grid=(batch, G, nq),
        in_specs=[
            tile("q_a"),
            pl.BlockSpec((seq, gw), lambda b, g, i, o=streams["q_a"][1] // gw: (b, o + g)),
            tile("z_a"),
            pl.BlockSpec((tq, gate.shape[1]), lambda b, g, i: (b * nq + i, 0)),
            stream("slc"), stream("win"),
            pl.BlockSpec((n_half, LANES), lambda b, g, i: (b * G + g, 0)),
            pl.BlockSpec((NSA_HEAD_DIM, n_half), lambda b, g, i: (b * G + g, 0)),
            const(featk), const(featw),
            pl.BlockSpec((1,) + qfeat.shape[1:], lambda b, g, i: (g, 0, 0, 0)),
            pl.BlockSpec((1,) + qrel.shape[1:], lambda b, g, i: (g, 0, 0)),
            const(seg), const(mmap), const(place),
            pl.BlockSpec((1, gw), lambda b, g, i: (0, g)),
        ],
        out_specs=pl.BlockSpec((tq, gw), lambda b, g, i: (b * nq + i, g)),
        out_shape=jax.ShapeDtypeStruct((batch * seq, G * gw), BF16),
        scratch_shapes=[
            pltpu.VMEM((seq, LANES), BF16),
            pltpu.VMEM((seq // tq, NSA_HEAD_DIM, tq), BF16),
            pltpu.VMEM((seq + WINDOW, LANES), BF16),
            pltpu.VMEM(((seq + WINDOW) // tq, NSA_HEAD_DIM, tq), BF16),
            pltpu.VMEM((gate.shape[1], tq), F32),
            pltpu.SMEM((1,), jnp.int32),
        ],
        compiler_params=pltpu.CompilerParams(
            dimension_semantics=("arbitrary", "arbitrary", "arbitrary"),
            vmem_limit_bytes=VMEM_LIMIT),
        name="nsa",
    )(streams["q_a"][0], streams["q_a"][0], streams["z_a"][0], gate, streams["slc"][0], streams["win"][0],
      kc, vc, featk, featw, qfeat, qrel, seg, mmap, place, norm)


def _bf16_terms(x, n):
    terms, rest = [], np.asarray(x, np.float64)
    for _ in range(n):
        t = rest.astype(np.float32).astype(ml_dtypes.bfloat16).astype(np.float64)
        terms.append(t.astype(np.float32))
        rest = rest - t
    return terms


def _nsa_tables(seq, n_half, n_cmp, heads):
    assert seq // SEL_BLOCK <= MAX_SEL_BLOCKS and FLAG_LANE < LANES

    def key_features(pos, onehot_blocks):
        f = np.zeros((len(pos), LANES), np.float32)
        if onehot_blocks:
            f[np.arange(len(pos)), SEL_LANE0 + pos // SEL_BLOCK] = 1.0
        f[:, POS_LANE0:POS_LANE0 + N_SPLIT] = ((pos // 64) * 64)[:, None]
        f[:, POS_LANE0 + N_SPLIT:POS_LANE0 + 2 * N_SPLIT] = (pos % 64)[:, None]
        return f

    assert REL_LANE0 + 3 <= LANES
    featk = key_features(np.arange(seq), True)
    featk[:, SHIFT_LANE0:REL_LANE0 + 3] = 1.0
    featw = np.concatenate([np.zeros((WINDOW, LANES), np.float32), key_features(np.arange(seq), False)])
    featw[:WINDOW, FLAG_LANE] = NEG_INF
    featw[WINDOW:, SHIFT_LANE0:REL_LANE0 + 3] = 1.0
    featc = key_features(CMP_STRIDE * np.arange(n_half) + CMP_BLOCK - 1, False)
    featc[n_cmp:, FLAG_LANE] = NEG_INF

    slopes = (2.0 ** (-8.0 * np.arange(1, heads + 1) / heads)).astype(np.float32).astype(np.float64)
    slopes2 = slopes * LOG2E
    terms = _bf16_terms(slopes2, N_SPLIT)
    rep = heads // NSA_KV_HEADS
    nq = seq // NSA_TILE
    qfeat = np.zeros((NSA_KV_HEADS, nq, 8, LANES), np.float32)
    qrel = np.zeros((NSA_KV_HEADS, rep * NSA_TILE, LANES), np.float32)
    for h in range(heads):
        g, r = divmod(h, rep)
        for i, t in enumerate(terms):
            qfeat[g, :, r, POS_LANE0 + i] = t[h]
            qfeat[g, :, r, POS_LANE0 + N_SPLIT + i] = t[h]
        qfeat[g, :, r, FLAG_LANE] = 1.0
        for i, t in enumerate(_bf16_terms(-slopes2[h] * NSA_TILE * np.arange(nq), N_SPLIT)):
            qfeat[g, :, r, SHIFT_LANE0 + i] = t
        for i, t in enumerate(_bf16_terms(-slopes2[h] * np.arange(NSA_TILE), 3)):
            qrel[g, r * NSA_TILE:(r + 1) * NSA_TILE, REL_LANE0 + i] = t
    seg = np.zeros((2 * LANES, LANES), np.float32)
    seg[np.arange(2 * LANES), np.arange(2 * LANES) // NSA_HEAD_DIM] = 1.0

    cs = CMP_STRIDE * np.arange(n_half)[None, :]
    ss = SEL_BLOCK * np.arange(LANES)[:, None]
    overlap = np.clip(np.minimum(cs + CMP_BLOCK, ss + SEL_BLOCK) - np.maximum(cs, ss), 0, None)
    mmap = (overlap / CMP_BLOCK) * (np.arange(n_half)[None, :] < n_cmp) * (ss < seq)
    place = np.zeros((MAX_SEL_BLOCKS, LANES), np.float32)
    place[np.arange(MAX_SEL_BLOCKS), SEL_LANE0 + np.arange(MAX_SEL_BLOCKS)] = 1.0
    bf = lambda a: jnp.asarray(a, dtype=BF16)
    return bf(featk), bf(featw), bf(featc), jnp.asarray(qfeat), bf(qrel), bf(seg), bf(mmap), bf(place)


def _hgrn_stages(q_ref, f_ref, v_ref, z_ref, lb_ref, norm_ref, cum_ref, o_ref, rows):
    C, SUB = HGRN_CHUNK, HGRN_SUB
    n_sub = C // SUB
    chunks = range(len(rows))
    sub = lambda x, i: x[i * SUB:(i + 1) * SUB]
    w = {}

    def gates():
        lbr = lb_ref[...]
        e = jnp.exp(lbr - jnp.max(lbr, axis=0, keepdims=True))
        lb = e[0:1, :] / jnp.sum(e, axis=0, keepdims=True)
        w["v16"] = [v_ref[rows[j], :].astype(BF16) for j in chunks]
        f = [lb + (1.0 - lb) * jax.nn.sigmoid(f_ref[rows[j], :]) for j in chunks]
        w["k"] = [1.0 - f[j] for j in chunks]
        w["parts"] = [_split3(jnp.log(f[j])) for j in chunks]

    def cumsums():
        cum = cum_ref[...]
        parts = w["parts"]
        w["a"] = [(jnp.dot(cum, parts[j][0], preferred_element_type=F32)
                   + jnp.dot(cum, parts[j][1], preferred_element_type=F32)
                   + jnp.dot(cum, parts[j][2], preferred_element_type=F32)) for j in chunks]

    def operands():
        a, k = w["a"], w["k"]
        q1, k1, k2, qb, k3, start, dec = [], [], [], [], [], [], []
        for j in chunks:
            q1.append(q_ref[rows[j], :] * jnp.exp(a[j]))
            k1.append(k[j] * jnp.exp(-a[j]))
            tot = [a[j][(i + 1) * SUB - 1:(i + 1) * SUB] for i in range(n_sub)]
            s = [jnp.zeros_like(tot[0])]
            for i in range(n_sub):
                s.append(s[i] + tot[i])
            start.append(s)
            dec.append(jnp.exp(s[n_sub]))
            qb.append(jnp.concatenate([sub(q1[j], i) * jnp.exp(s[i]) for i in range(n_sub)], axis=0)
                      .astype(BF16))
            k2.append([sub(k1[j], i) * jnp.exp(tot[i]) for i in range(n_sub)])
            k3.append(jnp.concatenate([sub(k1[j], i) * jnp.exp(s[n_sub] - s[i]) for i in range(n_sub)],
                                      axis=0).astype(BF16))
        w.update(q1=q1, k1=k1, k2=k2, qb=qb, k3=k3, start=start, dec=dec)

    def scores():
        q1, k1, k2, start = w["q1"], w["k1"], w["k2"], w["start"]
        att = []
        for j in chunks:
            row_blocks = []
            for i in range(n_sub):
                rhs = [k2[j][jb] * jnp.exp(start[j][i] - start[j][jb + 1]) if jb < i - 1 else k2[j][jb]
                       for jb in range(i)]
                rhs.append(sub(k1[j], i))
                rhs = jnp.concatenate(rhs, axis=0) if len(rhs) > 1 else rhs[0]
                sc = _nt_dot(sub(q1[j], i).astype(BF16), rhs.astype(BF16))
                width = (i + 1) * SUB
                causal = (lax.broadcasted_iota(jnp.int32, (SUB, width), 1)
                          <= lax.broadcasted_iota(jnp.int32, (SUB, width), 0) + i * SUB)
                row_blocks.append(jnp.where(causal, sc, 0.0).astype(BF16))
            att.append(row_blocks)
        w["att"] = att

    def products():
        att, v16, k3 = w["att"], w["v16"], w["k3"]
        w["intra"] = [jnp.concatenate([jnp.dot(att[j][i], v16[j][0:(i + 1) * SUB],
                                               preferred_element_type=F32)
                                       for i in range(n_sub)], axis=0) for j in chunks]
        w["incr"] = [_tn_dot(v16[j], k3[j]) for j in chunks]

    def state_pass(st):
        inter = []
        for j in chunks:
            inter.append(_nt_dot(w["qb"][j], st.astype(BF16)))
            st = st * w["dec"][j] + w["incr"][j]
        w["inter"] = inter
        return st

    def finish():
        for j in chunks:
            _hgrn_head_out(w["inter"][j] + w["intra"][j], z_ref, norm_ref, o_ref, rows[j])

    return gates, cumsums, operands, scores, products, state_pass, finish


def _hgrn_head_out(o, z_ref, norm_ref, o_ref, rows):
    y = o * lax.rsqrt(jnp.mean(o * o, axis=-1, keepdims=True) + EPS) * norm_ref[...]
    z = z_ref[rows, :]
    o_ref[rows, :] = (y * (z * jax.nn.sigmoid(z))).astype(o_ref.dtype)


def _hgrn_pairwise_chunk(q_ref, f_ref, v_ref, z_ref, lb, norm_ref, cum_ref, o_ref, col_ref, c, st):
    C, SUB = HGRN_CHUNK, HGRN_SUB
    dk = q_ref.shape[1]
    rows = pl.ds(pl.multiple_of(c * C, C), C)
    f = lb + (1.0 - lb) * jax.nn.sigmoid(f_ref[rows, :])
    k = 1.0 - f
    cum = cum_ref[...]
    hi, mid, lo = _split3(jnp.log(f))
    a = (jnp.dot(cum, hi, preferred_element_type=F32) + jnp.dot(cum, mid, preferred_element_type=F32)
         + jnp.dot(cum, lo, preferred_element_type=F32))
    before, pieces = jnp.zeros((1, dk), F32), []
    for i in range(C // SUB):
        pieces.append(a[i * SUB:(i + 1) * SUB] + before)
        before = before + a[(i + 1) * SUB - 1:(i + 1) * SUB]
    b, b_end = jnp.concatenate(pieces, axis=0), before
    q = q_ref[rows, :].astype(F32)
    v16 = v_ref[rows, :].astype(BF16)
    col_ref[0] = b
    col_ref[1] = k
    key_i = lax.broadcasted_iota(jnp.int32, (C, C), 1)
    qry_i = lax.broadcasted_iota(jnp.int32, (C, C), 0)

    def column(s, sc):
        bs = col_ref[0, pl.ds(s, 1), :]
        ks = col_ref[1, pl.ds(s, 1), :]
        terms = jnp.exp(jnp.minimum(b - bs, 0.0)) * (q * ks)
        return jnp.where(key_i == s, jnp.sum(terms, axis=1, keepdims=True), sc)

    sc = lax.fori_loop(0, C, column, jnp.zeros((C, C), F32))
    att = jnp.where(key_i <= qry_i, sc, 0.0).astype(BF16)
    intra = jnp.dot(att, v16, preferred_element_type=F32)
    inter = _nt_dot((q * jnp.exp(b)).astype(BF16), st.astype(BF16))
    incr = _tn_dot(v16, (k * jnp.exp(b_end - b)).astype(BF16))
    _hgrn_head_out(inter + intra, z_ref, norm_ref, o_ref, rows)
    return st * jnp.exp(b_end) + incr


def _hgrn_kernel(q_ref, f_ref, v_ref, z_ref, lb_ref, norm_ref, cum_ref, o_ref, col_ref, factored_ref, *, group):
    seq, dk = q_ref.shape
    head = pl.program_id(1)

    def lower_bound():
        lbr = lb_ref[...]
        e = jnp.exp(lbr - jnp.max(lbr, axis=0, keepdims=True))
        return e[0:1, :] / jnp.sum(e, axis=0, keepdims=True)

    @pl.when(pl.program_id(0) == 0)
    def _():
        ok = jnp.min(jnp.log(lower_bound())) * HGRN_SUB >= -HGRN_MAX_LOG_DECAY
        factored_ref[head] = ok.astype(jnp.int32)

    def block(cb, st):
        rows = [pl.ds(pl.multiple_of((cb * group + j) * HGRN_CHUNK, HGRN_CHUNK), HGRN_CHUNK)
                for j in range(group)]
        gates, cumsums, operands, scores, products, state_pass, finish = _hgrn_stages(
            q_ref, f_ref, v_ref, z_ref, lb_ref, norm_ref, cum_ref, o_ref, rows)
        gates()
        cumsums()
        operands()
        scores()
        products()
        st = state_pass(st)
        finish()
        return st

    @pl.when(factored_ref[head] == 1)
    def _():
        lax.fori_loop(0, seq // (HGRN_CHUNK * group), block, jnp.zeros((dk, dk), F32))

    @pl.when(factored_ref[head] == 0)
    def _():
        chunk = functools.partial(_hgrn_pairwise_chunk, q_ref, f_ref, v_ref, z_ref, lower_bound(), norm_ref,
                                  cum_ref, o_ref, col_ref)
        lax.fori_loop(0, seq // HGRN_CHUNK, chunk, jnp.zeros((dk, dk), F32))


def _hgrn_patterns():
    C, SUB = HGRN_CHUNK, HGRN_SUB
    t = np.arange(C)[:, None]
    s = np.arange(C)[None, :]
    return (((t // SUB) == (s // SUB)) & (s <= t)).astype(np.float32)


def _hgrn(streams, lower_bounds, norm, *, batch, seq):
    dk = HGRN_HEAD_DIM
    heads = norm.shape[1] // dk
    cum = jnp.asarray(_hgrn_patterns(), dtype=BF16)
    col = lambda name: (lambda b, h, o=streams[name][1] // dk: (b, o + h))
    return pl.pallas_call(
        functools.partial(_hgrn_kernel, group=16),
        grid=(batch, heads),
        in_specs=[
            pl.BlockSpec((seq, dk), col("q_h")),
            pl.BlockSpec((seq, dk), col("f_h")),
            pl.BlockSpec((seq, dk), col("i_h")),
            pl.BlockSpec((seq, dk), col("z_h")),
            pl.BlockSpec((lower_bounds.shape[0], dk), lambda b, h: (0, h)),
            pl.BlockSpec((1, dk), lambda b, h: (0, h)),
            pl.BlockSpec(cum.shape, lambda b, h: (0, 0)),
        ],
        out_specs=pl.BlockSpec((seq, dk), lambda b, h: (b, h)),
        out_shape=jax.ShapeDtypeStruct((batch * seq, heads * dk), BF16),
        scratch_shapes=[pltpu.VMEM((2, HGRN_CHUNK, dk), F32), pltpu.SMEM((heads,), jnp.int32)],
        compiler_params=pltpu.CompilerParams(
            dimension_semantics=("arbitrary", "arbitrary"),
            vmem_limit_bytes=VMEM_LIMIT),
        name="hgrn",
    )(streams["q_h"][0], streams["f_h"][0], streams["i_h"][0], streams["z_h"][0], lower_bounds, norm, cum)


def _outproj_kernel(x_ref, oa_ref, oh_ref, wa_ref, wh_ref, g_ref, o_ref):
    y = x_ref[...] + jnp.dot(oa_ref[...], wa_ref[...], preferred_element_type=F32)
    y = y + jnp.dot(oh_ref[...], wh_ref[...], preferred_element_type=F32)
    ms = jnp.mean(y * y, axis=-1, keepdims=True)
    o_ref[...] = y * lax.rsqrt(ms + EPS) * g_ref[...]


def _outproj(x2, o_a, o_h, wa, wh, g, *, tm):
    M, D = x2.shape
    const = lambda a: pl.BlockSpec(a.shape, lambda i: (0,) * a.ndim)
    return pl.pallas_call(
        _outproj_kernel,
        grid=(M // tm,),
        in_specs=[
            pl.BlockSpec((tm, D), lambda i: (i, 0)),
            pl.BlockSpec((tm, o_a.shape[1]), lambda i: (i, 0)),
            pl.BlockSpec((tm, o_h.shape[1]), lambda i: (i, 0)),
            const(wa), const(wh), const(g),
        ],
        out_specs=pl.BlockSpec((tm, D), lambda i: (i, 0)),
        out_shape=jax.ShapeDtypeStruct((M, D), F32),
        compiler_params=pltpu.CompilerParams(
            dimension_semantics=("arbitrary",),
            vmem_limit_bytes=VMEM_LIMIT),
        name="outproj",
    )(x2, o_a, o_h, wa, wh, g)


def kernel(x, norm_in, w_in, cmp_pe_k, cmp_w1_k, cmp_w2_k, cmp_pe_v, cmp_w1_v, cmp_w2_v,
           lower_bounds, nsa_out_norm, hgrn_out_norm, w_out, final_norm):
    B, S, D = x.shape
    assert norm_in.shape[0] == 1, "single-layer problem"
    nsa_w = nsa_out_norm.shape[1]
    hgrn_w = hgrn_out_norm.shape[1]
    dh = NSA_HEAD_DIM
    G = NSA_KV_HEADS
    heads = nsa_w // dh
    kvw = G * dh
    n_gate = 3 * heads
    n_cmp = (S - CMP_BLOCK) // CMP_STRIDE + 1
    n_half = S // CMP_STRIDE
    assert S % NSA_TILE == 0 and WINDOW % NSA_TILE == 0 and n_half <= LANES and 2 * dh == LANES

    names = ["q_a", "k_cmp", "v_cmp", "k_slc", "v_slc", "k_win", "v_win", "gate", "z_a",
             "q_h", "f_h", "i_h", "z_h"]
    widths = [nsa_w] + [kvw] * 6 + [n_gate, nsa_w] + [hgrn_w] * 4
    starts = dict(zip(names, np.cumsum([0] + widths[:-1]).tolist()))
    wd = dict(zip(names, widths))

    paired = lambda kname, vname: [(starts[n] + g * dh, dh) for g in range(G) for n in (kname, vname)]
    whole = lambda n: [(starts[n], wd[n])]
    groups = [[("q_a", whole("q_a")),
               ("cmp", paired("k_cmp", "v_cmp")), ("slc", paired("k_slc", "v_slc")),
               ("win", paired("k_win", "v_win")), ("q_h", whole("q_h")), ("i_h", whole("i_h"))],
              [("z_a", whole("z_a")), ("f_h", whole("f_h")), ("z_h", whole("z_h"))]]
    plan, col_of, off = [], {}, 0
    for group in groups:
        for n, pieces in group:
            col_of[n] = off
            for src, width in pieces:
                plan.append((off, src, width, dh ** -0.5 if n == "q_a" else 1.0))
                off += width
    n_main = off
    n16_cols = col_of["z_a"]
    plan.append((n_main, starts["gate"], n_gate, 1.0))
    w_all = _relayout_weights(w_in[0].T, tuple(plan), n_main + LANES)

    x2 = x.reshape(B * S, D)
    proj16, proj32, gate = _proj(x2, norm_in, w_all, tm=min(1024, B * S), tn=1536,
                                 n_main=n_main, n16_cols=n16_cols)
    streams = {}
    for arr, group, base in ((proj16, groups[0], 0), (proj32, groups[1], n16_cols)):
        for n, _ in group:
            streams[n] = (arr, col_of[n] - base)

    featk, featw, featc, qfeat, qrel, seg, mmap, place = _nsa_tables(S, n_half, n_cmp, heads)

    def w1_halves(w1k, w1v):
        hk = w1k.shape[1]
        k3 = w1k.reshape(2, CMP_STRIDE, dh, hk)
        v3 = w1v.reshape(2, CMP_STRIDE, dh, hk)
        zk = jnp.zeros_like(k3[0])
        top = lambda a: jnp.concatenate([a, zk], axis=-1)
        bot = lambda a: jnp.concatenate([zk, a], axis=-1)
        half = lambda i: jnp.concatenate([top(k3[i]), bot(v3[i])], axis=1).reshape(CMP_STRIDE * 2 * dh, 2 * hk)
        return half(0).astype(BF16), half(1).astype(BF16)

    wp, wq = w1_halves(cmp_w1_k[0], cmp_w1_v[0])
    zk = jnp.zeros_like(cmp_w2_k[0])
    w2 = jnp.concatenate([jnp.concatenate([cmp_w2_k[0], zk], axis=1),
                          jnp.concatenate([zk, cmp_w2_v[0]], axis=1)], axis=0).astype(BF16)
    pe = jnp.concatenate([cmp_pe_k[0].reshape(2, CMP_STRIDE, dh), cmp_pe_v[0].reshape(2, CMP_STRIDE, dh)],
                         axis=-1).reshape(2, CMP_STRIDE * 2 * dh)
    kc, vc = _compress(streams["cmp"][0], pe, wp, wq, w2, jnp.tile(featc, (G, 1)),
                       batch=B, seq=S, col0=streams["cmp"][1], n_half=n_half)

    o_a = _nsa(streams, gate, kc, vc, featk, featw, qfeat, qrel, seg, mmap, place, nsa_out_norm,
               batch=B, seq=S, n_cmp=n_cmp)
    o_h = _hgrn(streams, lower_bounds, hgrn_out_norm, batch=B, seq=S)

    wo = w_out[0].astype(BF16)
    out = _outproj(x2, o_a, o_h, wo[:nsa_w], wo[nsa_w:], final_norm.reshape(1, D), tm=512)
    return out.reshape(B, S, D)
```

```python
import functools

import ml_dtypes
import numpy as np
import jax
import jax.numpy as jnp
from jax import lax
from jax.experimental import pallas as pl
from jax.experimental.pallas import tpu as pltpu

F32 = jnp.float32
BF16 = jnp.bfloat16

EPS = 1e-6
NEG_INF = -1e30
LOG2E = 1.4426950408889634

NSA_HEAD_DIM = 64
NSA_KV_HEADS = 4
CMP_BLOCK = 32
CMP_STRIDE = 16
SEL_BLOCK = 64
SEL_TOP = 8
SEL_BONUS = 1.0e4
WINDOW = 512
HGRN_HEAD_DIM = 128
HGRN_CHUNK = 64
HGRN_SUB = 16
HGRN_MAX_LOG_DECAY = 60.0

LANES = 128
VMEM_LIMIT = 56 * 1024 * 1024
NSA_TILE = 256
SEL_RECENT = 3

SEL_LANE0 = NSA_HEAD_DIM
MAX_SEL_BLOCKS = 32
POS_LANE0 = SEL_LANE0 + MAX_SEL_BLOCKS
N_SPLIT = 4
FLAG_LANE = POS_LANE0 + 2 * N_SPLIT
SHIFT_LANE0 = FLAG_LANE + 1
REL_LANE0 = SHIFT_LANE0 + N_SPLIT
SCORE_BOUND = 100.0
SAFETY = 1.02


def _nt_dot(a, b):
    return lax.dot_general(a, b, (((1,), (1,)), ((), ())), preferred_element_type=F32)


def _tn_dot(a, b):
    return lax.dot_general(a, b, (((0,), (0,)), ((), ())), preferred_element_type=F32)


def _split3(x):
    hi = x.astype(BF16)
    r1 = x - hi.astype(F32)
    mid = r1.astype(BF16)
    lo = (r1 - mid.astype(F32)).astype(BF16)
    return hi, mid, lo


def _proj_kernel(x_ref, g_ref, w_ref, wg_ref, o16_ref, o32_ref, og_ref, h_ref, *, row_chunk, n16):
    j = pl.program_id(1)

    @pl.when(j == 0)
    def _():
        n_chunks = x_ref.shape[0] // row_chunk

        def body(c, carry):
            rows = pl.ds(pl.multiple_of(c * row_chunk, row_chunk), row_chunk)
            x = x_ref[rows, :]
            ms = jnp.mean(x * x, axis=-1, keepdims=True)
            h_ref[rows, :] = (x * lax.rsqrt(ms + EPS) * g_ref[...]).astype(BF16)
            return carry

        lax.fori_loop(0, n_chunks, body, 0)
        og_ref[...] = jnp.dot(h_ref[...], wg_ref[...], preferred_element_type=F32)

    @pl.when(j < n16)
    def _():
        o16_ref[...] = jnp.dot(h_ref[...], w_ref[...], preferred_element_type=F32).astype(BF16)

    @pl.when(j >= n16)
    def _():
        o32_ref[...] = jnp.dot(h_ref[...], w_ref[...], preferred_element_type=F32)


def _relayout_kernel(wt_ref, o_ref, *, plan):
    o_ref[:, o_ref.shape[1] - LANES:] = jnp.zeros((o_ref.shape[0], LANES), o_ref.dtype)
    for dst, src, width, scale in plan:
        o_ref[:, dst:dst + width] = (wt_ref[src:src + width, :].T * scale).astype(o_ref.dtype)


def _relayout_weights(wt, plan, n_out, *, rows=256):
    n_in, D = wt.shape
    return pl.pallas_call(
        functools.partial(_relayout_kernel, plan=plan),
        grid=(D // rows,),
        in_specs=[pl.BlockSpec((n_in, rows), lambda i: (0, i))],
        out_specs=pl.BlockSpec((rows, n_out), lambda i: (i, 0)),
        out_shape=jax.ShapeDtypeStruct((D, n_out), BF16),
        compiler_params=pltpu.CompilerParams(
            dimension_semantics=("arbitrary",),
            vmem_limit_bytes=VMEM_LIMIT),
        name="relayout",
    )(wt)


def _proj(x2, g, w, *, tm, tn, n_main, n16_cols):
    M, D = x2.shape
    N = n_main
    NG = w.shape[1] - n_main
    n16 = n16_cols // tn
    assert n16 * tn == n16_cols and N % tn == 0 and N % NG == 0
    return pl.pallas_call(
        functools.partial(_proj_kernel, row_chunk=128, n16=n16),
        grid=(M // tm, N // tn),
        in_specs=[
            pl.BlockSpec((tm, D), lambda i, j: (i, 0)),
            pl.BlockSpec((1, D), lambda i, j: (0, 0)),
            pl.BlockSpec((D, tn), lambda i, j: (0, j)),
            pl.BlockSpec((D, NG), lambda i, j: (0, N // NG)),
        ],
        out_specs=[
            pl.BlockSpec((tm, tn), lambda i, j: (i, jnp.minimum(j, n16 - 1))),
            pl.BlockSpec((tm, tn), lambda i, j: (i, jnp.maximum(j - n16, 0))),
            pl.BlockSpec((tm, NG), lambda i, j: (i, 0)),
        ],
        out_shape=[
            jax.ShapeDtypeStruct((M, n16_cols), BF16),
            jax.ShapeDtypeStruct((M, N - n16_cols), F32),
            jax.ShapeDtypeStruct((M, NG), F32),
        ],
        scratch_shapes=[pltpu.VMEM((tm, D), BF16)],
        compiler_params=pltpu.CompilerParams(
            dimension_semantics=("arbitrary", "arbitrary"),
            vmem_limit_bytes=VMEM_LIMIT),
        name="proj",
    )(x2, g, w, w)


def _compress_kernel(c0_ref, c1_ref, c2_ref, c3_ref, pe_ref, wp_ref, wq_ref, w2_ref, feat_ref,
                     kc_ref, vc_ref, x_ref, c32_ref, *, n_half):
    for g, c_ref in enumerate((c0_ref, c1_ref, c2_ref, c3_ref)):
        c32_ref[...] = c_ref[...].astype(F32)
        for l in range(CMP_STRIDE):
            x_ref[g * n_half:(g + 1) * n_half, l * LANES:(l + 1) * LANES] = (
                c32_ref[pl.ds(l, n_half, stride=CMP_STRIDE), :])
    x = x_ref[...]
    rows = x.shape[0]
    first = jnp.dot((x + pe_ref[0:1, :]).astype(BF16), wp_ref[...], preferred_element_type=F32)
    second = jnp.dot((x + pe_ref[1:2, :]).astype(BF16), wq_ref[...], preferred_element_type=F32)
    hidden = first + pltpu.roll(second, rows - 1, 0)
    out = jnp.dot(jax.nn.gelu(hidden).astype(BF16), w2_ref[...], preferred_element_type=F32)
    lane = lax.broadcasted_iota(jnp.int32, out.shape, 1)
    dh = NSA_HEAD_DIM
    kc_ref[...] = jnp.where(lane < dh, out.astype(BF16), feat_ref[...])
    for g in range(NSA_KV_HEADS):
        vc_ref[g * dh:(g + 1) * dh, :] = out[g * n_half:(g + 1) * n_half, :].T[dh:, :].astype(BF16)


def _compress(proj, pe, wp, wq, w2, feat, *, batch, seq, col0, n_half):
    rows = NSA_KV_HEADS * n_half
    rows_t = NSA_KV_HEADS * NSA_HEAD_DIM
    const = lambda a: pl.BlockSpec(a.shape, lambda b: (0,) * a.ndim)
    stream = lambda g: pl.BlockSpec((seq, LANES), lambda b, o=col0 // LANES + g: (b, o))
    return pl.pallas_call(
        functools.partial(_compress_kernel, n_half=n_half),
        grid=(batch,),
        in_specs=[stream(0), stream(1), stream(2), stream(3),
                  const(pe), const(wp), const(wq), const(w2), const(feat)],
        out_specs=[
            pl.BlockSpec((rows, LANES), lambda b: (b, 0)),
            pl.BlockSpec((rows_t, n_half), lambda b: (b, 0)),
        ],
        out_shape=[
            jax.ShapeDtypeStruct((batch * rows, LANES), BF16),
            jax.ShapeDtypeStruct((batch * rows_t, n_half), BF16),
        ],
        scratch_shapes=[pltpu.VMEM((rows, CMP_STRIDE * LANES), F32), pltpu.VMEM((seq, LANES), F32)],
        compiler_params=pltpu.CompilerParams(
            dimension_semantics=("arbitrary",),
            vmem_limit_bytes=VMEM_LIMIT),
        name="compress",
    )(proj, proj, proj, proj, pe, wp, wq, w2, feat)


def _nsa_kernel(q_ref, qall_ref, z_ref, gate_ref, slc_ref, win_ref, kc_ref, vct_ref, featk_ref, featw_ref,
                qfeat_ref, qrel_ref, seg_ref, mmap_ref, place_ref, norm_ref, o_ref,
                ksel_ref, vselt_ref, kwin_ref, vwint_ref, gt_ref, bounded_ref, *, n_cmp):
    g = pl.program_id(1)
    qi = pl.program_id(2)
    tq = q_ref.shape[0]
    tk = tq
    seq = slc_ref.shape[0]
    dh = NSA_HEAD_DIM
    rep = q_ref.shape[1] // dh
    n_half = kc_ref.shape[0]
    n_pad = WINDOW // tk

    @pl.when(qi == 0)
    def _():
        lane = lax.broadcasted_iota(jnp.int32, (seq, LANES), 1)

        def max_sq_norm(x, width):
            s = jnp.dot((x * x).astype(BF16), seg_ref[0:width, :], preferred_element_type=F32)
            return jnp.max(s) * SAFETY

        kv = slc_ref[...]
        k2_sel = max_sq_norm(jnp.where(lane < dh, kv.astype(F32), 0.0), LANES)
        ksel_ref[...] = jnp.where(lane < dh, kv, featk_ref[...])
        vt = kv.astype(F32).T[dh:, :].astype(BF16)
        for kt in range(seq // tk):
            vselt_ref[kt] = vt[:, kt * tk:(kt + 1) * tk]
        kv = win_ref[...]
        k2_win = max_sq_norm(jnp.where(lane < dh, kv.astype(F32), 0.0), LANES)
        kwin_ref[0:WINDOW, :] = featw_ref[0:WINDOW, :]
        kwin_ref[WINDOW:, :] = jnp.where(lane < dh, kv, featw_ref[WINDOW:, :])
        vt = kv.astype(F32).T[dh:, :].astype(BF16)
        for kt in range(n_pad):
            vwint_ref[kt] = jnp.zeros((dh, tk), BF16)
        for kt in range(seq // tk):
            vwint_ref[n_pad + kt] = vt[:, kt * tk:(kt + 1) * tk]
        q2 = max_sq_norm(qall_ref[...].astype(F32), rep * dh) * (LOG2E * LOG2E)
        limit = SCORE_BOUND * SCORE_BOUND
        bounded_ref[0] = ((q2 * k2_sel <= limit) & (q2 * k2_win <= limit)).astype(jnp.int32)

    heads = lambda a: jnp.concatenate([a] * rep, axis=1)
    sub8 = lambda a: a.reshape(a.shape[0] // 8, 8, a.shape[1])

    def softmax_pv(lgs, v_tiles, bounded, extra=None):
        if bounded:
            m = 0.0
        else:
            m8 = functools.reduce(jnp.maximum, [jnp.max(sub8(lg), axis=0) for lg in lgs])
            m = jnp.max(m8, axis=0, keepdims=True)
        l8 = jnp.zeros((8, lgs[0].shape[1]), F32)
        acc = jnp.zeros((dh, lgs[0].shape[1]), F32)
        for lg, vt in zip(lgs, v_tiles):
            pt = jnp.exp2(lg) if bounded else jnp.exp2(lg - m)
            l8 = l8 + jnp.sum(sub8(pt), axis=0)
            acc = acc + jnp.dot(vt, pt.astype(BF16), preferred_element_type=F32)
        if extra is not None:
            l8, acc = extra(l8, acc)
        return acc / jnp.sum(l8, axis=0, keepdims=True)

    def step(n, bounded):
        static = isinstance(n, int)
        assert static == bounded
        t0 = n * tq
        lane = lax.broadcasted_iota(jnp.int32, (tq, LANES), 1)
        key_i = lax.broadcasted_iota(jnp.int32, (tk, tq), 0)
        qry_i = lax.broadcasted_iota(jnp.int32, (tk, tq), 1)
        causal_1 = jnp.where(key_i <= qry_i, 0.0, NEG_INF)
        causal_bias = heads(causal_1)
        band_bias = heads(jnp.where(key_i > qry_i, 0.0, NEG_INF))

        qfeat = qfeat_ref[0, n]
        rel_lanes = (lane >= REL_LANE0) & (lane < REL_LANE0 + 3)
        qa = []
        for r in range(rep):
            qcol = q_ref[:, (r // 2) * LANES:(r // 2 + 1) * LANES].astype(F32)
            if r % 2:
                qcol = pltpu.roll(qcol, dh, 1)
            feat = jnp.where(rel_lanes, qrel_ref[0, r * tq:(r + 1) * tq, :].astype(F32), qfeat[r:r + 1, :])
            qa.append(jnp.where(lane < dh, qcol * LOG2E, feat))
        q0 = jnp.concatenate(qa, axis=0).astype(BF16)

        gt_ref[...] = jax.nn.sigmoid(gate_ref[...]).T
        out_scale = []
        for pair in range(rep // 2):
            z = z_ref[:, pair * LANES:(pair + 1) * LANES]
            out_scale.append(norm_ref[:, pair * LANES:(pair + 1) * LANES] * (z * jax.nn.sigmoid(z)))

        nc = min(n_half, (t0 + tq) // CMP_STRIDE) if static else n_half
        n_idx = lax.broadcasted_iota(jnp.int32, (nc, tq), 0)
        t_idx = t0 + lax.broadcasted_iota(jnp.int32, (nc, tq), 1)
        valid_c = heads((CMP_STRIDE * n_idx + (CMP_BLOCK - 1) <= t_idx) & (n_idx < n_cmp))
        lg_c = jnp.where(valid_c, _nt_dot(kc_ref[0:nc, :], q0), NEG_INF)

        n_wt = n_pad + 1
        win_tiles = [w for w in range(n_wt) if not (static and n + w < n_pad)]
        lgs = []
        for w in win_tiles:
            rows = (slice(t0 + w * tk, t0 + (w + 1) * tk) if static
                    else pl.ds(pl.multiple_of(t0 + w * tk, tk), tk))
            lg = _nt_dot(kwin_ref[rows, :], q0)
            if w == 0:
                lg = lg + band_bias
            if w == n_wt - 1:
                lg = lg + causal_bias
            lgs.append(lg)
        o_win = softmax_pv(lgs, [vwint_ref[n + w] for w in win_tiles], bounded)

        e = jnp.exp2(lg_c - jnp.max(lg_c, axis=0, keepdims=True))
        p = jnp.where(valid_c, e / jnp.sum(e, axis=0, keepdims=True), 0.0)
        if nc < n_half:
            p = jnp.concatenate([p, jnp.zeros((n_half - nc, rep * tq), F32)], axis=0)
        o_cmp = jnp.dot(vct_ref[...], p.astype(BF16), preferred_element_type=F32)
        p_sum = p[:, 0:tq]
        for r in range(1, rep):
            p_sum = p_sum + p[:, r * tq:(r + 1) * tq]

        nb = min(MAX_SEL_BLOCKS, (t0 + tq) // SEL_BLOCK) if static else MAX_SEL_BLOCKS
        mm = mmap_ref[...]
        hi, mid, lo = _split3(p_sum)
        p_slc = (jnp.dot(mm, hi, preferred_element_type=F32) + jnp.dot(mm, mid, preferred_element_type=F32)
                 + jnp.dot(mm, lo, preferred_element_type=F32))[0:MAX_SEL_BLOCKS]
        if static:
            sel_scores = lambda kt: _nt_dot(ksel_ref[kt * tk:(kt + 1) * tk, :], q0)
            maybe_idle = [kt for kt in range(1, n + 1 - SEL_RECENT)]
            raw = {kt: sel_scores(kt) for kt in range(n + 1) if kt not in maybe_idle}
        jj = lax.broadcasted_iota(jnp.int32, (MAX_SEL_BLOCKS, tq), 0)
        tt = t0 + lax.broadcasted_iota(jnp.int32, (MAX_SEL_BLOCKS, tq), 1)
        cur = lax.shift_right_logical(tt, int(np.log2(SEL_BLOCK)))
        forced = (jj == 0) | (jj == cur) | (jj == cur - 1)
        future = jj > cur
        score = jnp.where(future, -1.0, p_slc + jnp.where(forced, SEL_BONUS, 0.0))
        rank = jnp.zeros((MAX_SEL_BLOCKS, tq), jnp.int32)
        for i in range(nb):
            other = jnp.broadcast_to(score[i:i + 1, :], score.shape)
            beats = (other > score) | ((other == score) & (jj > i))
            rank = rank + beats.astype(jnp.int32)
        keep = (rank < SEL_TOP) & jnp.logical_not(future)

        if static:
            blocks_per_tile = tk // SEL_BLOCK
            block_bias = jnp.where(keep, 0.0, NEG_INF)

            def tile_bias(kt):
                b = jnp.concatenate(
                    [jnp.broadcast_to(block_bias[blocks_per_tile * kt + j:blocks_per_tile * kt + j + 1, :],
                                      (SEL_BLOCK, tq)) for j in range(blocks_per_tile)], axis=0)
                return heads(b + causal_1 if kt == n else b)

            wanted = [jnp.max(block_bias[blocks_per_tile * kt:blocks_per_tile * (kt + 1)]) == 0.0
                      for kt in maybe_idle]

            def idle_tiles(l8, acc):
                for kt, some in zip(maybe_idle, wanted):
                    def live(l8, acc, kt=kt):
                        pt = jnp.exp2(sel_scores(kt) + tile_bias(kt))
                        return (l8 + jnp.sum(sub8(pt), axis=0),
                                acc + jnp.dot(vselt_ref[kt], pt.astype(BF16), preferred_element_type=F32))

                    l8, acc = lax.cond(some, live, lambda l8, acc: (l8, acc), l8, acc)
                return l8, acc

            o_slc = softmax_pv([raw[kt] + tile_bias(kt) for kt in raw], [vselt_ref[kt] for kt in raw],
                               bounded, extra=idle_tiles if maybe_idle else None)
        else:
            chosen = jnp.where(keep, 1.0, 0.0).astype(BF16)
            placed = _tn_dot(chosen, place_ref[...])
            sel_lanes = (lane >= SEL_LANE0) & (lane < SEL_LANE0 + MAX_SEL_BLOCKS)
            sel_bias = (placed - 1.0) * (-NEG_INF)
            qs = jnp.concatenate([jnp.where(sel_lanes, sel_bias, a) for a in qa], axis=0).astype(BF16)

            def sel_tile(kt, carry, bias=None):
                m, l, acc = carry
                lg = _nt_dot(ksel_ref[pl.ds(pl.multiple_of(kt * tk, tk), tk), :], qs)
                if bias is not None:
                    lg = lg + bias
                m_new = jnp.maximum(m, jnp.max(lg, axis=0, keepdims=True))
                alpha = jnp.exp2(m - m_new)
                pt = jnp.exp2(lg - m_new)
                return (m_new, alpha * l + jnp.sum(pt, axis=0, keepdims=True),
                        alpha * acc + jnp.dot(vselt_ref[kt], pt.astype(BF16), preferred_element_type=F32))

            init = (jnp.full((1, rep * tq), NEG_INF, F32), jnp.zeros((1, rep * tq), F32),
                    jnp.zeros((dh, rep * tq), F32))
            _, l, acc = sel_tile(n, lax.fori_loop(0, n, sel_tile, init), causal_bias)
            o_slc = acc / l

        ys = []
        for r in range(rep):
            cols = slice(r * tq, (r + 1) * tq)
            gate = lambda c: gt_ref[pl.ds(3 * (g * rep + r) + c, 1), :]
            o = gate(0) * o_cmp[:, cols] + gate(1) * o_slc[:, cols] + gate(2) * o_win[:, cols]
            ys.append(o * lax.rsqrt(jnp.mean(o * o, axis=0, keepdims=True) + EPS))
        for pair in range(rep // 2):
            cols = slice(pair * LANES, (pair + 1) * LANES)
            y = jnp.concatenate([ys[2 * pair], ys[2 * pair + 1]], axis=0).T
            o_ref[:, cols] = (y * out_scale[pair]).astype(o_ref.dtype)

    @pl.when(bounded_ref[0] == 1)
    def _():
        lax.switch(qi, [functools.partial(step, n, True) for n in range(seq // tq)])

    @pl.when(bounded_ref[0] == 0)
    def _():
        step(qi, False)


def _nsa(streams, gate, kc, vc, featk, featw, qfeat, qrel, seg, mmap, place, norm, *, batch, seq, n_cmp):
    tq = NSA_TILE
    nq = seq // tq
    G = NSA_KV_HEADS
    gw = norm.shape[1] // G
    n_half = kc.shape[0] // (batch * G)
    const = lambda a: pl.BlockSpec(a.shape, lambda b, g, i: (0,) * a.ndim)
    stream = lambda name: pl.BlockSpec((seq, LANES),
                                       lambda b, g, i, o=streams[name][1] // LANES: (b, o + g))
    tile = lambda name: pl.BlockSpec((tq, gw), lambda b, g, i, o=streams[name][1] // gw: (b * nq + i, o + g))
    return pl.pallas_call(
        functools.partial(_nsa_kernel, n_cmp=n_cmp),
        grid=(batch, G, nq),
        in_specs=[
            tile("q_a"),
            pl.BlockSpec((seq, gw), lambda b, g, i, o=streams["q_a"][1] // gw: (b, o + g)),
            tile("z_a"),
            pl.BlockSpec((tq, gate.shape[1]), lambda b, g, i: (b * nq + i, 0)),
            stream("slc"), stream("win"),
            pl.BlockSpec((n_half, LANES), lambda b, g, i: (b * G + g, 0)),
            pl.BlockSpec((NSA_HEAD_DIM, n_half), lambda b, g, i: (b * G + g, 0)),
            const(featk), const(featw),
            pl.BlockSpec((1,) + qfeat.shape[1:], lambda b, g, i: (g, 0, 0, 0)),
            pl.BlockSpec((1,) + qrel.shape[1:], lambda b, g, i: (g, 0, 0)),
            const(seg), const(mmap), const(place),
            pl.BlockSpec((1, gw), lambda b, g, i: (0, g)),
        ],
        out_specs=pl.BlockSpec((tq, gw), lambda b, g, i: (b * nq + i, g)),
        out_shape=jax.ShapeDtypeStruct((batch * seq, G * gw), BF16),
        scratch_shapes=[
            pltpu.VMEM((seq, LANES), BF16),
            pltpu.VMEM((seq // tq, NSA_HEAD_DIM, tq), BF16),
            pltpu.VMEM((seq + WINDOW, LANES), BF16),
            pltpu.VMEM(((seq + WINDOW) // tq, NSA_HEAD_DIM, tq), BF16),
            pltpu.VMEM((gate.shape[1], tq), F32),
            pltpu.SMEM((1,), jnp.int32),
        ],
        compiler_params=pltpu.CompilerParams(
            dimension_semantics=("arbitrary", "arbitrary", "arbitrary"),
            vmem_limit_bytes=VMEM_LIMIT),
        name="nsa",
    )(streams["q_a"][0], streams["q_a"][0], streams["z_a"][0], gate, streams["slc"][0], streams["win"][0],
      kc, vc, featk, featw, qfeat, qrel, seg, mmap, place, norm)


def _bf16_terms(x, n):
    terms, rest = [], np.asarray(x, np.float64)
    for _ in range(n):
        t = rest.astype(np.float32).astype(ml_dtypes.bfloat16).astype(np.float64)
        terms.append(t.astype(np.float32))
        rest = rest - t
    return terms


def _nsa_tables(seq, n_half, n_cmp, heads):
    assert seq // SEL_BLOCK <= MAX_SEL_BLOCKS and FLAG_LANE < LANES

    def key_features(pos, onehot_blocks):
        f = np.zeros((len(pos), LANES), np.float32)
        if onehot_blocks:
            f[np.arange(len(pos)), SEL_LANE0 + pos // SEL_BLOCK] = 1.0
        f[:, POS_LANE0:POS_LANE0 + N_SPLIT] = ((pos // 64) * 64)[:, None]
        f[:, POS_LANE0 + N_SPLIT:POS_LANE0 + 2 * N_SPLIT] = (pos % 64)[:, None]
        return f

    assert REL_LANE0 + 3 <= LANES
    featk = key_features(np.arange(seq), True)
    featk[:, SHIFT_LANE0:REL_LANE0 + 3] = 1.0
    featw = np.concatenate([np.zeros((WINDOW, LANES), np.float32), key_features(np.arange(seq), False)])
    featw[:WINDOW, FLAG_LANE] = NEG_INF
    featw[WINDOW:, SHIFT_LANE0:REL_LANE0 + 3] = 1.0
    featc = key_features(CMP_STRIDE * np.arange(n_half) + CMP_BLOCK - 1, False)
    featc[n_cmp:, FLAG_LANE] = NEG_INF

    slopes = (2.0 ** (-8.0 * np.arange(1, heads + 1) / heads)).astype(np.float32).astype(np.float64)
    slopes2 = slopes * LOG2E
    terms = _bf16_terms(slopes2, N_SPLIT)
    rep = heads // NSA_KV_HEADS
    nq = seq // NSA_TILE
    qfeat = np.zeros((NSA_KV_HEADS, nq, 8, LANES), np.float32)
    qrel = np.zeros((NSA_KV_HEADS, rep * NSA_TILE, LANES), np.float32)
    for h in range(heads):
        g, r = divmod(h, rep)
        for i, t in enumerate(terms):
            qfeat[g, :, r, POS_LANE0 + i] = t[h]
            qfeat[g, :, r, POS_LANE0 + N_SPLIT + i] = t[h]
        qfeat[g, :, r, FLAG_LANE] = 1.0
        for i, t in enumerate(_bf16_terms(-slopes2[h] * NSA_TILE * np.arange(nq), N_SPLIT)):
            qfeat[g, :, r, SHIFT_LANE0 + i] = t
        for i, t in enumerate(_bf16_terms(-slopes2[h] * np.arange(NSA_TILE), 3)):
            qrel[g, r * NSA_TILE:(r + 1) * NSA_TILE, REL_LANE0 + i] = t
    seg = np.zeros((2 * LANES, LANES), np.float32)
    seg[np.arange(2 * LANES), np.arange(2 * LANES) // NSA_HEAD_DIM] = 1.0

    cs = CMP_STRIDE * np.arange(n_half)[None, :]
    ss = SEL_BLOCK * np.arange(LANES)[:, None]
    overlap = np.clip(np.minimum(cs + CMP_BLOCK, ss + SEL_BLOCK) - np.maximum(cs, ss), 0, None)
    mmap = (overlap / CMP_BLOCK) * (np.arange(n_half)[None, :] < n_cmp) * (ss < seq)
    place = np.zeros((MAX_SEL_BLOCKS, LANES), np.float32)
    place[np.arange(MAX_SEL_BLOCKS), SEL_LANE0 + np.arange(MAX_SEL_BLOCKS)] = 1.0
    bf = lambda a: jnp.asarray(a, dtype=BF16)
    return bf(featk), bf(featw), bf(featc), jnp.asarray(qfeat), bf(qrel), bf(seg), bf(mmap), bf(place)


def _hgrn_stages(q_ref, f_ref, v_ref, z_ref, lb_ref, norm_ref, cum_ref, o_ref, rows):
    C, SUB = HGRN_CHUNK, HGRN_SUB
    n_sub = C // SUB
    chunks = range(len(rows))
    sub = lambda x, i: x[i * SUB:(i + 1) * SUB]
    w = {}

    def gates():
        lbr = lb_ref[...]
        e = jnp.exp(lbr - jnp.max(lbr, axis=0, keepdims=True))
        lb = e[0:1, :] / jnp.sum(e, axis=0, keepdims=True)
        w["v16"] = [v_ref[rows[j], :].astype(BF16) for j in chunks]
        f = [lb + (1.0 - lb) * jax.nn.sigmoid(f_ref[rows[j], :]) for j in chunks]
        w["k"] = [1.0 - f[j] for j in chunks]
        w["parts"] = [_split3(jnp.log(f[j])) for j in chunks]

    def cumsums():
        cum = cum_ref[...]
        parts = w["parts"]
        w["a"] = [(jnp.dot(cum, parts[j][0], preferred_element_type=F32)
                   + jnp.dot(cum, parts[j][1], preferred_element_type=F32)
                   + jnp.dot(cum, parts[j][2], preferred_element_type=F32)) for j in chunks]

    def operands():
        a, k = w["a"], w["k"]
        q1, k1, k2, qb, k3, start, dec = [], [], [], [], [], [], []
        for j in chunks:
            q1.append(q_ref[rows[j], :] * jnp.exp(a[j]))
            k1.append(k[j] * jnp.exp(-a[j]))
            tot = [a[j][(i + 1) * SUB - 1:(i + 1) * SUB] for i in range(n_sub)]
            s = [jnp.zeros_like(tot[0])]
            for i in range(n_sub):
                s.append(s[i] + tot[i])
            start.append(s)
            dec.append(jnp.exp(s[n_sub]))
            qb.append(jnp.concatenate([sub(q1[j], i) * jnp.exp(s[i]) for i in range(n_sub)], axis=0)
                      .astype(BF16))
            k2.append([sub(k1[j], i) * jnp.exp(tot[i]) for i in range(n_sub)])
            k3.append(jnp.concatenate([sub(k1[j], i) * jnp.exp(s[n_sub] - s[i]) for i in range(n_sub)],
                                      axis=0).astype(BF16))
        w.update(q1=q1, k1=k1, k2=k2, qb=qb, k3=k3, start=start, dec=dec)

    def scores():
        q1, k1, k2, start = w["q1"], w["k1"], w["k2"], w["start"]
        att = []
        for j in chunks:
            row_blocks = []
            for i in range(n_sub):
                rhs = [k2[j][jb] * jnp.exp(start[j][i] - start[j][jb + 1]) if jb < i - 1 else k2[j][jb]
                       for jb in range(i)]
                rhs.append(sub(k1[j], i))
                rhs = jnp.concatenate(rhs, axis=0) if len(rhs) > 1 else rhs[0]
                sc = _nt_dot(sub(q1[j], i).astype(BF16), rhs.astype(BF16))
                width = (i + 1) * SUB
                causal = (lax.broadcasted_iota(jnp.int32, (SUB, width), 1)
                          <= lax.broadcasted_iota(jnp.int32, (SUB, width), 0) + i * SUB)
                row_blocks.append(jnp.where(causal, sc, 0.0).astype(BF16))
            att.append(row_blocks)
        w["att"] = att

    def products():
        att, v16, k3 = w["att"], w["v16"], w["k3"]
        w["intra"] = [jnp.concatenate([jnp.dot(att[j][i], v16[j][0:(i + 1) * SUB],
                                               preferred_element_type=F32)
                                       for i in range(n_sub)], axis=0) for j in chunks]
        w["incr"] = [_tn_dot(v16[j], k3[j]) for j in chunks]

    def state_pass(st):
        inter = []
        for j in chunks:
            inter.append(_nt_dot(w["qb"][j], st.astype(BF16)))
            st = st * w["dec"][j] + w["incr"][j]
        w["inter"] = inter
        return st

    def finish():
        for j in chunks:
            _hgrn_head_out(w["inter"][j] + w["intra"][j], z_ref, norm_ref, o_ref, rows[j])

    return gates, cumsums, operands, scores, products, state_pass, finish


def _hgrn_head_out(o, z_ref, norm_ref, o_ref, rows):
    y = o * lax.rsqrt(jnp.mean(o * o, axis=-1, keepdims=True) + EPS) * norm_ref[...]
    z = z_ref[rows, :]
    o_ref[rows, :] = (y * (z * jax.nn.sigmoid(z))).astype(o_ref.dtype)


def _hgrn_pairwise_chunk(q_ref, f_ref, v_ref, z_ref, lb, norm_ref, cum_ref, o_ref, col_ref, c, st):
    C, SUB = HGRN_CHUNK, HGRN_SUB
    dk = q_ref.shape[1]
    rows = pl.ds(pl.multiple_of(c * C, C), C)
    f = lb + (1.0 - lb) * jax.nn.sigmoid(f_ref[rows, :])
    k = 1.0 - f
    cum = cum_ref[...]
    hi, mid, lo = _split3(jnp.log(f))
    a = (jnp.dot(cum, hi, preferred_element_type=F32) + jnp.dot(cum, mid, preferred_element_type=F32)
         + jnp.dot(cum, lo, preferred_element_type=F32))
    before, pieces = jnp.zeros((1, dk), F32), []
    for i in range(C // SUB):
        pieces.append(a[i * SUB:(i + 1) * SUB] + before)
        before = before + a[(i + 1) * SUB - 1:(i + 1) * SUB]
    b, b_end = jnp.concatenate(pieces, axis=0), before
    q = q_ref[rows, :].astype(F32)
    v16 = v_ref[rows, :].astype(BF16)
    col_ref[0] = b
    col_ref[1] = k
    key_i = lax.broadcasted_iota(jnp.int32, (C, C), 1)
    qry_i = lax.broadcasted_iota(jnp.int32, (C, C), 0)

    def column(s, sc):
        bs = col_ref[0, pl.ds(s, 1), :]
        ks = col_ref[1, pl.ds(s, 1), :]
        terms = jnp.exp(jnp.minimum(b - bs, 0.0)) * (q * ks)
        return jnp.where(key_i == s, jnp.sum(terms, axis=1, keepdims=True), sc)

    sc = lax.fori_loop(0, C, column, jnp.zeros((C, C), F32))
    att = jnp.where(key_i <= qry_i, sc, 0.0).astype(BF16)
    intra = jnp.dot(att, v16, preferred_element_type=F32)
    inter = _nt_dot((q * jnp.exp(b)).astype(BF16), st.astype(BF16))
    incr = _tn_dot(v16, (k * jnp.exp(b_end - b)).astype(BF16))
    _hgrn_head_out(inter + intra, z_ref, norm_ref, o_ref, rows)
    return st * jnp.exp(b_end) + incr


def _hgrn_kernel(q_ref, f_ref, v_ref, z_ref, lb_ref, norm_ref, cum_ref, o_ref, col_ref, factored_ref, *, group):
    seq, dk = q_ref.shape
    head = pl.program_id(1)

    def lower_bound():
        lbr = lb_ref[...]
        e = jnp.exp(lbr - jnp.max(lbr, axis=0, keepdims=True))
        return e[0:1, :] / jnp.sum(e, axis=0, keepdims=True)

    @pl.when(pl.program_id(0) == 0)
    def _():
        ok = jnp.min(jnp.log(lower_bound())) * HGRN_SUB >= -HGRN_MAX_LOG_DECAY
        factored_ref[head] = ok.astype(jnp.int32)

    def block(cb, st):
        rows = [pl.ds(pl.multiple_of((cb * group + j) * HGRN_CHUNK, HGRN_CHUNK), HGRN_CHUNK)
                for j in range(group)]
        gates, cumsums, operands, scores, products, state_pass, finish = _hgrn_stages(
            q_ref, f_ref, v_ref, z_ref, lb_ref, norm_ref, cum_ref, o_ref, rows)
        gates()
        cumsums()
        operands()
        scores()
        products()
        st = state_pass(st)
        finish()
        return st

    @pl.when(factored_ref[head] == 1)
    def _():
        lax.fori_loop(0, seq // (HGRN_CHUNK * group), block, jnp.zeros((dk, dk), F32))

    @pl.when(factored_ref[head] == 0)
    def _():
        chunk = functools.partial(_hgrn_pairwise_chunk, q_ref, f_ref, v_ref, z_ref, lower_bound(), norm_ref,
                                  cum_ref, o_ref, col_ref)
        lax.fori_loop(0, seq // HGRN_CHUNK, chunk, jnp.zeros((dk, dk), F32))


def _hgrn_patterns():
    C, SUB = HGRN_CHUNK, HGRN_SUB
    t = np.arange(C)[:, None]
    s = np.arange(C)[None, :]
    return (((t // SUB) == (s // SUB)) & (s <= t)).astype(np.float32)


def _hgrn(streams, lower_bounds, norm, *, batch, seq):
    dk = HGRN_HEAD_DIM
    heads = norm.shape[1] // dk
    cum = jnp.asarray(_hgrn_patterns(), dtype=BF16)
    col = lambda name: (lambda b, h, o=streams[name][1] // dk: (b, o + h))
    return pl.pallas_call(
        functools.partial(_hgrn_kernel, group=16),
        grid=(batch, heads),
        in_specs=[
            pl.BlockSpec((seq, dk), col("q_h")),
            pl.BlockSpec((seq, dk), col("f_h")),
            pl.BlockSpec((seq, dk), col("i_h")),
            pl.BlockSpec((seq, dk), col("z_h")),
            pl.BlockSpec((lower_bounds.shape[0], dk), lambda b, h: (0, h)),
            pl.BlockSpec((1, dk), lambda b, h: (0, h)),
            pl.BlockSpec(cum.shape, lambda b, h: (0, 0)),
        ],
        out_specs=pl.BlockSpec((seq, dk), lambda b, h: (b, h)),
        out_shape=jax.ShapeDtypeStruct((batch * seq, heads * dk), BF16),
        scratch_shapes=[pltpu.VMEM((2, HGRN_CHUNK, dk), F32), pltpu.SMEM((heads,), jnp.int32)],
        compiler_params=pltpu.CompilerParams(
            dimension_semantics=("arbitrary", "arbitrary"),
            vmem_limit_bytes=VMEM_LIMIT),
        name="hgrn",
    )(streams["q_h"][0], streams["f_h"][0], streams["i_h"][0], streams["z_h"][0], lower_bounds, norm, cum)


def _outproj_kernel(x_ref, oa_ref, oh_ref, wa_ref, wh_ref, g_ref, o_ref):
    y = x_ref[...] + jnp.dot(oa_ref[...], wa_ref[...], preferred_element_type=F32)
    y = y + jnp.dot(oh_ref[...], wh_ref[...], preferred_element_type=F32)
    ms = jnp.mean(y * y, axis=-1, keepdims=True)
    o_ref[...] = y * lax.rsqrt(ms + EPS) * g_ref[...]


def _outproj(x2, o_a, o_h, wa, wh, g, *, tm):
    M, D = x2.shape
    const = lambda a: pl.BlockSpec(a.shape, lambda i: (0,) * a.ndim)
    return pl.pallas_call(
        _outproj_kernel,
        grid=(M // tm,),
        in_specs=[
            pl.BlockSpec((tm, D), lambda i: (i, 0)),
            pl.BlockSpec((tm, o_a.shape[1]), lambda i: (i, 0)),
            pl.BlockSpec((tm, o_h.shape[1]), lambda i: (i, 0)),
            const(wa), const(wh), const(g),
        ],
        out_specs=pl.BlockSpec((tm, D), lambda i: (i, 0)),
        out_shape=jax.ShapeDtypeStruct((M, D), F32),
        compiler_params=pltpu.CompilerParams(
            dimension_semantics=("arbitrary",),
            vmem_limit_bytes=VMEM_LIMIT),
        name="outproj",
    )(x2, o_a, o_h, wa, wh, g)


def kernel(x, norm_in, w_in, cmp_pe_k, cmp_w1_k, cmp_w2_k, cmp_pe_v, cmp_w1_v, cmp_w2_v,
           lower_bounds, nsa_out_norm, hgrn_out_norm, w_out, final_norm):
    B, S, D = x.shape
    assert norm_in.shape[0] == 1, "single-layer problem"
    nsa_w = nsa_out_norm.shape[1]
    hgrn_w = hgrn_out_norm.shape[1]
    dh = NSA_HEAD_DIM
    G = NSA_KV_HEADS
    heads = nsa_w // dh
    kvw = G * dh
    n_gate = 3 * heads
    n_cmp = (S - CMP_BLOCK) // CMP_STRIDE + 1
    n_half = S // CMP_STRIDE
    assert S % NSA_TILE == 0 and WINDOW % NSA_TILE == 0 and n_half <= LANES and 2 * dh == LANES

    names = ["q_a", "k_cmp", "v_cmp", "k_slc", "v_slc", "k_win", "v_win", "gate", "z_a",
             "q_h", "f_h", "i_h", "z_h"]
    widths = [nsa_w] + [kvw] * 6 + [n_gate, nsa_w] + [hgrn_w] * 4
    starts = dict(zip(names, np.cumsum([0] + widths[:-1]).tolist()))
    wd = dict(zip(names, widths))

    paired = lambda kname, vname: [(starts[n] + g * dh, dh) for g in range(G) for n in (kname, vname)]
    whole = lambda n: [(starts[n], wd[n])]
    groups = [[("q_a", whole("q_a")),
               ("cmp", paired("k_cmp", "v_cmp")), ("slc", paired("k_slc", "v_slc")),
               ("win", paired("k_win", "v_win")), ("q_h", whole("q_h")), ("i_h", whole("i_h"))],
              [("z_a", whole("z_a")), ("f_h", whole("f_h")), ("z_h", whole("z_h"))]]
    plan, col_of, off = [], {}, 0
    for group in groups:
        for n, pieces in group:
            col_of[n] = off
            for src, width in pieces:
                plan.append((off, src, width, dh ** -0.5 if n == "q_a" else 1.0))
                off += width
    n_main = off
    n16_cols = col_of["z_a"]
    plan.append((n_main, starts["gate"], n_gate, 1.0))
    w_all = _relayout_weights(w_in[0].T, tuple(plan), n_main + LANES)

    x2 = x.reshape(B * S, D)
    proj16, proj32, gate = _proj(x2, norm_in, w_all, tm=min(1024, B * S), tn=1536,
                                 n_main=n_main, n16_cols=n16_cols)
    streams = {}
    for arr, group, base in ((proj16, groups[0], 0), (proj32, groups[1], n16_cols)):
        for n, _ in group:
            streams[n] = (arr, col_of[n] - base)

    featk, featw, featc, qfeat, qrel, seg, mmap, place = _nsa_tables(S, n_half, n_cmp, heads)

    def w1_halves(w1k, w1v):
        hk = w1k.shape[1]
        k3 = w1k.reshape(2, CMP_STRIDE, dh, hk)
        v3 = w1v.reshape(2, CMP_STRIDE, dh, hk)
        zk = jnp.zeros_like(k3[0])
        top = lambda a: jnp.concatenate([a, zk], axis=-1)
        bot = lambda a: jnp.concatenate([zk, a], axis=-1)
        half = lambda i: jnp.concatenate([top(k3[i]), bot(v3[i])], axis=1).reshape(CMP_STRIDE * 2 * dh, 2 * hk)
        return half(0).astype(BF16), half(1).astype(BF16)

    wp, wq = w1_halves(cmp_w1_k[0], cmp_w1_v[0])
    zk = jnp.zeros_like(cmp_w2_k[0])
    w2 = jnp.concatenate([jnp.concatenate([cmp_w2_k[0], zk], axis=1),
                          jnp.concatenate([zk, cmp_w2_v[0]], axis=1)], axis=0).astype(BF16)
    pe = jnp.concatenate([cmp_pe_k[0].reshape(2, CMP_STRIDE, dh), cmp_pe_v[0].reshape(2, CMP_STRIDE, dh)],
                         axis=-1).reshape(2, CMP_STRIDE * 2 * dh)
    kc, vc = _compress(streams["cmp"][0], pe, wp, wq, w2, jnp.tile(featc, (G, 1)),
                       batch=B, seq=S, col0=streams["cmp"][1], n_half=n_half)

    o_a = _nsa(streams, gate, kc, vc, featk, featw, qfeat, qrel, seg, mmap, place, nsa_out_norm,
               batch=B, seq=S, n_cmp=n_cmp)
    o_h = _hgrn(streams, lower_bounds, hgrn_out_norm, batch=B, seq=S)

    wo = w_out[0].astype(BF16)
    out = _outproj(x2, o_a, o_h, wo[:nsa_w], wo[nsa_w:], final_norm.reshape(1, D), tm=512)
    return out.reshape(B, S, D)
```

```python
import functools

import ml_dtypes
import numpy as np
import jax
import jax.numpy as jnp
from jax import lax
from jax.experimental import pallas as pl
from jax.experimental.pallas import tpu as pltpu

F32 = jnp.float32
BF16 = jnp.bfloat16

EPS = 1e-6
NEG_INF = -1e30
LOG2E = 1.4426950408889634

NSA_HEAD_DIM = 64
NSA_KV_HEADS = 4
CMP_BLOCK = 32
CMP_STRIDE = 16
SEL_BLOCK = 64
SEL_TOP = 8
SEL_BONUS = 1.0e4
WINDOW = 512
HGRN_HEAD_DIM = 128
HGRN_CHUNK = 64
HGRN_SUB = 16
HGRN_MAX_LOG_DECAY = 60.0

LANES = 128
VMEM_LIMIT = 56 * 1024 * 1024
NSA_TILE = 256
NSA_TILES_PER_STEP = 2
SEL_RECENT = 3

SEL_LANE0 = NSA_HEAD_DIM
MAX_SEL_BLOCKS = 32
POS_LANE0 = SEL_LANE0 + MAX_SEL_BLOCKS
N_SPLIT = 4
FLAG_LANE = POS_LANE0 + 2 * N_SPLIT
SHIFT_LANE0 = FLAG_LANE + 1
REL_LANE0 = SHIFT_LANE0 + N_SPLIT
SCORE_BOUND = 100.0
SAFETY = 1.02


def _nt_dot(a, b):
    return lax.dot_general(a, b, (((1,), (1,)), ((), ())), preferred_element_type=F32)


def _tn_dot(a, b):
    return lax.dot_general(a, b, (((0,), (0,)), ((), ())), preferred_element_type=F32)


def _split3(x):
    hi = x.astype(BF16)
    r1 = x - hi.astype(F32)
    mid = r1.astype(BF16)
    lo = (r1 - mid.astype(F32)).astype(BF16)
    return hi, mid, lo


def _proj_kernel(x_ref, g_ref, w_ref, wg_ref, o16_ref, o32_ref, og_ref, h_ref, *, row_chunk, n16):
    j = pl.program_id(1)

    @pl.when(j == 0)
    def _():
        n_chunks = x_ref.shape[0] // row_chunk

        def body(c, carry):
            rows = pl.ds(pl.multiple_of(c * row_chunk, row_chunk), row_chunk)
            x = x_ref[rows, :]
            ms = jnp.mean(x * x, axis=-1, keepdims=True)
            h_ref[rows, :] = (x * lax.rsqrt(ms + EPS) * g_ref[...]).astype(BF16)
            return carry

        lax.fori_loop(0, n_chunks, body, 0)
        og_ref[...] = jnp.dot(h_ref[...], wg_ref[...], preferred_element_type=F32)

    @pl.when(j < n16)
    def _():
        o16_ref[...] = jnp.dot(h_ref[...], w_ref[...], preferred_element_type=F32).astype(BF16)

    @pl.when(j >= n16)
    def _():
        o32_ref[...] = jnp.dot(h_ref[...], w_ref[...], preferred_element_type=F32)


def _relayout_kernel(wt_ref, o_ref, *, plan):
    o_ref[:, o_ref.shape[1] - LANES:] = jnp.zeros((o_ref.shape[0], LANES), o_ref.dtype)
    for dst, src, width, scale in plan:
        o_ref[:, dst:dst + width] = (wt_ref[src:src + width, :].T * scale).astype(o_ref.dtype)


def _relayout_weights(wt, plan, n_out, *, rows=256):
    n_in, D = wt.shape
    return pl.pallas_call(
        functools.partial(_relayout_kernel, plan=plan),
        grid=(D // rows,),
        in_specs=[pl.BlockSpec((n_in, rows), lambda i: (0, i))],
        out_specs=pl.BlockSpec((rows, n_out), lambda i: (i, 0)),
        out_shape=jax.ShapeDtypeStruct((D, n_out), BF16),
        compiler_params=pltpu.CompilerParams(
            dimension_semantics=("arbitrary",),
            vmem_limit_bytes=VMEM_LIMIT),
        name="relayout",
    )(wt)


def _proj(x2, g, w, *, tm, tn, n_main, n16_cols):
    M, D = x2.shape
    N = n_main
    NG = w.shape[1] - n_main
    n16 = n16_cols // tn
    assert n16 * tn == n16_cols and N % tn == 0 and N % NG == 0
    return pl.pallas_call(
        functools.partial(_proj_kernel, row_chunk=128, n16=n16),
        grid=(M // tm, N // tn),
        in_specs=[
            pl.BlockSpec((tm, D), lambda i, j: (i, 0)),
            pl.BlockSpec((1, D), lambda i, j: (0, 0)),
            pl.BlockSpec((D, tn), lambda i, j: (0, j)),
            pl.BlockSpec((D, NG), lambda i, j: (0, N // NG)),
        ],
        out_specs=[
            pl.BlockSpec((tm, tn), lambda i, j: (i, jnp.minimum(j, n16 - 1))),
            pl.BlockSpec((tm, tn), lambda i, j: (i, jnp.maximum(j - n16, 0))),
            pl.BlockSpec((tm, NG), lambda i, j: (i, 0)),
        ],
        out_shape=[
            jax.ShapeDtypeStruct((M, n16_cols), BF16),
            jax.ShapeDtypeStruct((M, N - n16_cols), F32),
            jax.ShapeDtypeStruct((M, NG), F32),
        ],
        scratch_shapes=[pltpu.VMEM((tm, D), BF16)],
        compiler_params=pltpu.CompilerParams(
            dimension_semantics=("arbitrary", "arbitrary"),
            vmem_limit_bytes=VMEM_LIMIT),
        name="proj",
    )(x2, g, w, w)


def _compress_kernel(c0_ref, c1_ref, c2_ref, c3_ref, pe_ref, wp_ref, wq_ref, w2_ref, feat_ref,
                     kc_ref, vc_ref, x_ref, c32_ref, *, n_half):
    for g, c_ref in enumerate((c0_ref, c1_ref, c2_ref, c3_ref)):
        c32_ref[...] = c_ref[...].astype(F32)
        for l in range(CMP_STRIDE):
            x_ref[g * n_half:(g + 1) * n_half, l * LANES:(l + 1) * LANES] = (
                c32_ref[pl.ds(l, n_half, stride=CMP_STRIDE), :])
    x = x_ref[...]
    rows = x.shape[0]
    first = jnp.dot((x + pe_ref[0:1, :]).astype(BF16), wp_ref[...], preferred_element_type=F32)
    second = jnp.dot((x + pe_ref[1:2, :]).astype(BF16), wq_ref[...], preferred_element_type=F32)
    hidden = first + pltpu.roll(second, rows - 1, 0)
    out = jnp.dot(jax.nn.gelu(hidden).astype(BF16), w2_ref[...], preferred_element_type=F32)
    lane = lax.broadcasted_iota(jnp.int32, out.shape, 1)
    dh = NSA_HEAD_DIM
    kc_ref[...] = jnp.where(lane < dh, out.astype(BF16), feat_ref[...])
    for g in range(NSA_KV_HEADS):
        vc_ref[g * dh:(g + 1) * dh, :] = out[g * n_half:(g + 1) * n_half, :].T[dh:, :].astype(BF16)


def _compress(proj, pe, wp, wq, w2, feat, *, batch, seq, col0, n_half):
    rows = NSA_KV_HEADS * n_half
    rows_t = NSA_KV_HEADS * NSA_HEAD_DIM
    const = lambda a: pl.BlockSpec(a.shape, lambda b: (0,) * a.ndim)
    stream = lambda g: pl.BlockSpec((seq, LANES), lambda b, o=col0 // LANES + g: (b, o))
    return pl.pallas_call(
        functools.partial(_compress_kernel, n_half=n_half),
        grid=(batch,),
        in_specs=[stream(0), stream(1), stream(2), stream(3),
                  const(pe), const(wp), const(wq), const(w2), const(feat)],
        out_specs=[
            pl.BlockSpec((rows, LANES), lambda b: (b, 0)),
            pl.BlockSpec((rows_t, n_half), lambda b: (b, 0)),
        ],
        out_shape=[
            jax.ShapeDtypeStruct((batch * rows, LANES), BF16),
            jax.ShapeDtypeStruct((batch * rows_t, n_half), BF16),
        ],
        scratch_shapes=[pltpu.VMEM((rows, CMP_STRIDE * LANES), F32), pltpu.VMEM((seq, LANES), F32)],
        compiler_params=pltpu.CompilerParams(
            dimension_semantics=("arbitrary",),
            vmem_limit_bytes=VMEM_LIMIT),
        name="compress",
    )(proj, proj, proj, proj, pe, wp, wq, w2, feat)


def _nsa_kernel(q_ref, qall_ref, z_ref, gate_ref, slc_ref, win_ref, kc_ref, vct_ref, featk_ref, featw_ref,
                qfeat_ref, qrel_ref, seg_ref, mmap_ref, place_ref, norm_ref, o_ref,
                ksel_ref, vselt_ref, kwin_ref, vwint_ref, gt_ref, bounded_ref, *, n_cmp):
    g = pl.program_id(1)
    qi = pl.program_id(2)
    tq = NSA_TILE
    per_step = q_ref.shape[0] // tq
    tk = tq
    seq = slc_ref.shape[0]
    dh = NSA_HEAD_DIM
    rep = q_ref.shape[1] // dh
    n_half = kc_ref.shape[0]
    n_pad = WINDOW // tk

    @pl.when(qi == 0)
    def _():
        lane = lax.broadcasted_iota(jnp.int32, (seq, LANES), 1)

        def max_sq_norm(x, width):
            s = jnp.dot((x * x).astype(BF16), seg_ref[0:width, :], preferred_element_type=F32)
            return jnp.max(s) * SAFETY

        kv = slc_ref[...]
        k2_sel = max_sq_norm(jnp.where(lane < dh, kv.astype(F32), 0.0), LANES)
        ksel_ref[...] = jnp.where(lane < dh, kv, featk_ref[...])
        vt = kv.astype(F32).T[dh:, :].astype(BF16)
        for kt in range(seq // tk):
            vselt_ref[kt] = vt[:, kt * tk:(kt + 1) * tk]
        kv = win_ref[...]
        k2_win = max_sq_norm(jnp.where(lane < dh, kv.astype(F32), 0.0), LANES)
        kwin_ref[0:WINDOW, :] = featw_ref[0:WINDOW, :]
        kwin_ref[WINDOW:, :] = jnp.where(lane < dh, kv, featw_ref[WINDOW:, :])
        vt = kv.astype(F32).T[dh:, :].astype(BF16)
        for kt in range(n_pad):
            vwint_ref[kt] = jnp.zeros((dh, tk), BF16)
        for kt in range(seq // tk):
            vwint_ref[n_pad + kt] = vt[:, kt * tk:(kt + 1) * tk]
        q2 = max_sq_norm(qall_ref[...].astype(F32), rep * dh) * (LOG2E * LOG2E)
        limit = SCORE_BOUND * SCORE_BOUND
        bounded_ref[0] = ((q2 * k2_sel <= limit) & (q2 * k2_win <= limit)).astype(jnp.int32)

    heads = lambda a: jnp.concatenate([a] * rep, axis=1)
    sub8 = lambda a: a.reshape(a.shape[0] // 8, 8, a.shape[1])

    def softmax_pv(lgs, v_tiles, bounded, extra=None):
        if bounded:
            m = 0.0
        else:
            m8 = functools.reduce(jnp.maximum, [jnp.max(sub8(lg), axis=0) for lg in lgs])
            m = jnp.max(m8, axis=0, keepdims=True)
        l8 = jnp.zeros((8, lgs[0].shape[1]), F32)
        acc = jnp.zeros((dh, lgs[0].shape[1]), F32)
        for lg, vt in zip(lgs, v_tiles):
            pt = jnp.exp2(lg) if bounded else jnp.exp2(lg - m)
            l8 = l8 + jnp.sum(sub8(pt), axis=0)
            acc = acc + jnp.dot(vt, pt.astype(BF16), preferred_element_type=F32)
        if extra is not None:
            l8, acc = extra(l8, acc)
        return acc / jnp.sum(l8, axis=0, keepdims=True)

    def step(n, bounded, slot):
        static = isinstance(n, int)
        assert static == bounded
        own = slice(slot * tq, (slot + 1) * tq)
        t0 = n * tq
        lane = lax.broadcasted_iota(jnp.int32, (tq, LANES), 1)
        key_i = lax.broadcasted_iota(jnp.int32, (tk, tq), 0)
        qry_i = lax.broadcasted_iota(jnp.int32, (tk, tq), 1)
        causal_1 = jnp.where(key_i <= qry_i, 0.0, NEG_INF)
        causal_bias = heads(causal_1)
        band_bias = heads(jnp.where(key_i > qry_i, 0.0, NEG_INF))

        qfeat = qfeat_ref[0, n]
        rel_lanes = (lane >= REL_LANE0) & (lane < REL_LANE0 + 3)
        qa = []
        for r in range(rep):
            qcol = q_ref[own, (r // 2) * LANES:(r // 2 + 1) * LANES].astype(F32)
            if r % 2:
                qcol = pltpu.roll(qcol, dh, 1)
            feat = jnp.where(rel_lanes, qrel_ref[0, r * tq:(r + 1) * tq, :].astype(F32), qfeat[r:r + 1, :])
            qa.append(jnp.where(lane < dh, qcol * LOG2E, feat))
        q0 = jnp.concatenate(qa, axis=0).astype(BF16)

        gt_ref[slot] = jax.nn.sigmoid(gate_ref[own, :]).T
        out_scale = []
        for pair in range(rep // 2):
            z = z_ref[own, pair * LANES:(pair + 1) * LANES]
            out_scale.append(norm_ref[:, pair * LANES:(pair + 1) * LANES] * (z * jax.nn.sigmoid(z)))

        nc = min(n_half, (t0 + tq) // CMP_STRIDE) if static else n_half
        n_idx = lax.broadcasted_iota(jnp.int32, (nc, tq), 0)
        t_idx = t0 + lax.broadcasted_iota(jnp.int32, (nc, tq), 1)
        valid_c = heads((CMP_STRIDE * n_idx + (CMP_BLOCK - 1) <= t_idx) & (n_idx < n_cmp))
        lg_c = jnp.where(valid_c, _nt_dot(kc_ref[0:nc, :], q0), NEG_INF)

        n_wt = n_pad + 1
        win_tiles = [w for w in range(n_wt) if not (static and n + w < n_pad)]
        lgs = []
        for w in win_tiles:
            rows = (slice(t0 + w * tk, t0 + (w + 1) * tk) if static
                    else pl.ds(pl.multiple_of(t0 + w * tk, tk), tk))
            lg = _nt_dot(kwin_ref[rows, :], q0)
            if w == 0:
                lg = lg + band_bias
            if w == n_wt - 1:
                lg = lg + causal_bias
            lgs.append(lg)
        o_win = softmax_pv(lgs, [vwint_ref[n + w] for w in win_tiles], bounded)

        e = jnp.exp2(lg_c - jnp.max(lg_c, axis=0, keepdims=True))
        p = jnp.where(valid_c, e / jnp.sum(e, axis=0, keepdims=True), 0.0)
        if nc < n_half:
            p = jnp.concatenate([p, jnp.zeros((n_half - nc, rep * tq), F32)], axis=0)
        o_cmp = jnp.dot(vct_ref[...], p.astype(BF16), preferred_element_type=F32)
        p_sum = p[:, 0:tq]
        for r in range(1, rep):
            p_sum = p_sum + p[:, r * tq:(r + 1) * tq]

        nb = min(MAX_SEL_BLOCKS, (t0 + tq) // SEL_BLOCK) if static else MAX_SEL_BLOCKS
        mm = mmap_ref[...]
        hi, mid, lo = _split3(p_sum)
        p_slc = (jnp.dot(mm, hi, preferred_element_type=F32) + jnp.dot(mm, mid, preferred_element_type=F32)
                 + jnp.dot(mm, lo, preferred_element_type=F32))[0:MAX_SEL_BLOCKS]
        if static:
            sel_scores = lambda kt: _nt_dot(ksel_ref[kt * tk:(kt + 1) * tk, :], q0)
            maybe_idle = [kt for kt in range(1, n + 1 - SEL_RECENT)]
            raw = {kt: sel_scores(kt) for kt in range(n + 1) if kt not in maybe_idle}
        jj = lax.broadcasted_iota(jnp.int32, (MAX_SEL_BLOCKS, tq), 0)
        tt = t0 + lax.broadcasted_iota(jnp.int32, (MAX_SEL_BLOCKS, tq), 1)
        cur = lax.shift_right_logical(tt, int(np.log2(SEL_BLOCK)))
        forced = (jj == 0) | (jj == cur) | (jj == cur - 1)
        future = jj > cur
        score = jnp.where(future, -1.0, p_slc + jnp.where(forced, SEL_BONUS, 0.0))
        rank = jnp.zeros((MAX_SEL_BLOCKS, tq), jnp.int32)
        for i in range(nb):
            other = jnp.broadcast_to(score[i:i + 1, :], score.shape)
            beats = (other > score) | ((other == score) & (jj > i))
            rank = rank + beats.astype(jnp.int32)
        keep = (rank < SEL_TOP) & jnp.logical_not(future)

        if static:
            blocks_per_tile = tk // SEL_BLOCK
            block_bias = jnp.where(keep, 0.0, NEG_INF)

            def tile_bias(kt):
                b = jnp.concatenate(
                    [jnp.broadcast_to(block_bias[blocks_per_tile * kt + j:blocks_per_tile * kt + j + 1, :],
                                      (SEL_BLOCK, tq)) for j in range(blocks_per_tile)], axis=0)
                return heads(b + causal_1 if kt == n else b)

            wanted = [jnp.max(block_bias[blocks_per_tile * kt:blocks_per_tile * (kt + 1)]) == 0.0
                      for kt in maybe_idle]

            def idle_tiles(l8, acc):
                for kt, some in zip(maybe_idle, wanted):
                    def live(l8, acc, kt=kt):
                        pt = jnp.exp2(sel_scores(kt) + tile_bias(kt))
                        return (l8 + jnp.sum(sub8(pt), axis=0),
                                acc + jnp.dot(vselt_ref[kt], pt.astype(BF16), preferred_element_type=F32))

                    l8, acc = lax.cond(some, live, lambda l8, acc: (l8, acc), l8, acc)
                return l8, acc

            o_slc = softmax_pv([raw[kt] + tile_bias(kt) for kt in raw], [vselt_ref[kt] for kt in raw],
                               bounded, extra=idle_tiles if maybe_idle else None)
        else:
            chosen = jnp.where(keep, 1.0, 0.0).astype(BF16)
            placed = _tn_dot(chosen, place_ref[...])
            sel_lanes = (lane >= SEL_LANE0) & (lane < SEL_LANE0 + MAX_SEL_BLOCKS)
            sel_bias = (placed - 1.0) * (-NEG_INF)
            qs = jnp.concatenate([jnp.where(sel_lanes, sel_bias, a) for a in qa], axis=0).astype(BF16)

            def sel_tile(kt, carry, bias=None):
                m, l, acc = carry
                lg = _nt_dot(ksel_ref[pl.ds(pl.multiple_of(kt * tk, tk), tk), :], qs)
                if bias is not None:
                    lg = lg + bias
                m_new = jnp.maximum(m, jnp.max(lg, axis=0, keepdims=True))
                alpha = jnp.exp2(m - m_new)
                pt = jnp.exp2(lg - m_new)
                return (m_new, alpha * l + jnp.sum(pt, axis=0, keepdims=True),
                        alpha * acc + jnp.dot(vselt_ref[kt], pt.astype(BF16), preferred_element_type=F32))

            init = (jnp.full((1, rep * tq), NEG_INF, F32), jnp.zeros((1, rep * tq), F32),
                    jnp.zeros((dh, rep * tq), F32))
            _, l, acc = sel_tile(n, lax.fori_loop(0, n, sel_tile, init), causal_bias)
            o_slc = acc / l

        ys = []
        for r in range(rep):
            cols = slice(r * tq, (r + 1) * tq)
            gate = lambda c: gt_ref[slot, pl.ds(3 * (g * rep + r) + c, 1), :]
            o = gate(0) * o_cmp[:, cols] + gate(1) * o_slc[:, cols] + gate(2) * o_win[:, cols]
            ys.append(o * lax.rsqrt(jnp.mean(o * o, axis=0, keepdims=True) + EPS))
        for pair in range(rep // 2):
            cols = slice(pair * LANES, (pair + 1) * LANES)
            y = jnp.concatenate([ys[2 * pair], ys[2 * pair + 1]], axis=0).T
            o_ref[own, cols] = (y * out_scale[pair]).astype(o_ref.dtype)

    def tiles(first, bounded):
        for slot in range(per_step):
            step(first + slot, bounded, slot)

    @pl.when(bounded_ref[0] == 1)
    def _():
        lax.switch(qi, [functools.partial(tiles, first, True) for first in range(0, seq // tq, per_step)])

    @pl.when(bounded_ref[0] == 0)
    def _():
        tiles(qi * per_step, False)


def _nsa(streams, gate, kc, vc, featk, featw, qfeat, qrel, seg, mmap, place, norm, *, batch, seq, n_cmp):
    tq = NSA_TILE
    rows = NSA_TILES_PER_STEP * tq
    nq = seq // rows
    G = NSA_KV_HEADS
    gw = norm.shape[1] // G
    n_half = kc.shape[0] // (batch * G)
    const = lambda a: pl.BlockSpec(a.shape, lambda b, g, i: (0,) * a.ndim)
    stream = lambda name: pl.BlockSpec((seq, LANES),
                                       lambda b, g, i, o=streams[name][1] // LANES: (b, o + g))
    tile = lambda name: pl.BlockSpec((rows, gw), lambda b, g, i, o=streams[name][1] // gw: (b * nq + i, o + g))
    return pl.pallas_call(
        functools.partial(_nsa_kernel, n_cmp=n_cmp),
        grid=(batch, G, nq),
        in_specs=[
            tile("q_a"),
            pl.BlockSpec((seq, gw), lambda b, g, i, o=streams["q_a"][1] // gw: (b, o + g)),
            tile("z_a"),
            pl.BlockSpec((rows, gate.shape[1]), lambda b, g, i: (b * nq + i, 0)),
            stream("slc"), stream("win"),
            pl.BlockSpec((n_half, LANES), lambda b, g, i: (b * G + g, 0)),
            pl.BlockSpec((NSA_HEAD_DIM, n_half), lambda b, g, i: (b * G + g, 0)),
            const(featk), const(featw),
            pl.BlockSpec((1,) + qfeat.shape[1:], lambda b, g, i: (g, 0, 0, 0)),
            pl.BlockSpec((1,) + qrel.shape[1:], lambda b, g, i: (g, 0, 0)),
            const(seg), const(mmap), const(place),
            pl.BlockSpec((1, gw), lambda b, g, i: (0, g)),
        ],
        out_specs=pl.BlockSpec((rows, gw), lambda b, g, i: (b * nq + i, g)),
        out_shape=jax.ShapeDtypeStruct((batch * seq, G * gw), BF16),
        scratch_shapes=[
            pltpu.VMEM((seq, LANES), BF16),
            pltpu.VMEM((seq // tq, NSA_HEAD_DIM, tq), BF16),
            pltpu.VMEM((seq + WINDOW, LANES), BF16),
            pltpu.VMEM(((seq + WINDOW) // tq, NSA_HEAD_DIM, tq), BF16),
            pltpu.VMEM((NSA_TILES_PER_STEP, gate.shape[1], tq), F32),
            pltpu.SMEM((1,), jnp.int32),
        ],
        compiler_params=pltpu.CompilerParams(
            dimension_semantics=("arbitrary", "arbitrary", "arbitrary"),
            vmem_limit_bytes=VMEM_LIMIT),
        name="nsa",
    )(streams["q_a"][0], streams["q_a"][0], streams["z_a"][0], gate, streams["slc"][0], streams["win"][0],
      kc, vc, featk, featw, qfeat, qrel, seg, mmap, place, norm)


def _bf16_terms(x, n):
    terms, rest = [], np.asarray(x, np.float64)
    for _ in range(n):
        t = rest.astype(np.float32).astype(ml_dtypes.bfloat16).astype(np.float64)
        terms.append(t.astype(np.float32))
        rest = rest - t
    return terms


def _nsa_tables(seq, n_half, n_cmp, heads):
    assert seq // SEL_BLOCK <= MAX_SEL_BLOCKS and FLAG_LANE < LANES

    def key_features(pos, onehot_blocks):
        f = np.zeros((len(pos), LANES), np.float32)
        if onehot_blocks:
            f[np.arange(len(pos)), SEL_LANE0 + pos // SEL_BLOCK] = 1.0
        f[:, POS_LANE0:POS_LANE0 + N_SPLIT] = ((pos // 64) * 64)[:, None]
        f[:, POS_LANE0 + N_SPLIT:POS_LANE0 + 2 * N_SPLIT] = (pos % 64)[:, None]
        return f

    assert REL_LANE0 + 3 <= LANES
    featk = key_features(np.arange(seq), True)
    featk[:, SHIFT_LANE0:REL_LANE0 + 3] = 1.0
    featw = np.concatenate([np.zeros((WINDOW, LANES), np.float32), key_features(np.arange(seq), False)])
    featw[:WINDOW, FLAG_LANE] = NEG_INF
    featw[WINDOW:, SHIFT_LANE0:REL_LANE0 + 3] = 1.0
    featc = key_features(CMP_STRIDE * np.arange(n_half) + CMP_BLOCK - 1, False)
    featc[n_cmp:, FLAG_LANE] = NEG_INF

    slopes = (2.0 ** (-8.0 * np.arange(1, heads + 1) / heads)).astype(np.float32).astype(np.float64)
    slopes2 = slopes * LOG2E
    terms = _bf16_terms(slopes2, N_SPLIT)
    rep = heads // NSA_KV_HEADS
    nq = seq // NSA_TILE
    qfeat = np.zeros((NSA_KV_HEADS, nq, 8, LANES), np.float32)
    qrel = np.zeros((NSA_KV_HEADS, rep * NSA_TILE, LANES), np.float32)
    for h in range(heads):
        g, r = divmod(h, rep)
        for i, t in enumerate(terms):
            qfeat[g, :, r, POS_LANE0 + i] = t[h]
            qfeat[g, :, r, POS_LANE0 + N_SPLIT + i] = t[h]
        qfeat[g, :, r, FLAG_LANE] = 1.0
        for i, t in enumerate(_bf16_terms(-slopes2[h] * NSA_TILE * np.arange(nq), N_SPLIT)):
            qfeat[g, :, r, SHIFT_LANE0 + i] = t
        for i, t in enumerate(_bf16_terms(-slopes2[h] * np.arange(NSA_TILE), 3)):
            qrel[g, r * NSA_TILE:(r + 1) * NSA_TILE, REL_LANE0 + i] = t
    seg = np.zeros((2 * LANES, LANES), np.float32)
    seg[np.arange(2 * LANES), np.arange(2 * LANES) // NSA_HEAD_DIM] = 1.0

    cs = CMP_STRIDE * np.arange(n_half)[None, :]
    ss = SEL_BLOCK * np.arange(LANES)[:, None]
    overlap = np.clip(np.minimum(cs + CMP_BLOCK, ss + SEL_BLOCK) - np.maximum(cs, ss), 0, None)
    mmap = (overlap / CMP_BLOCK) * (np.arange(n_half)[None, :] < n_cmp) * (ss < seq)
    place = np.zeros((MAX_SEL_BLOCKS, LANES), np.float32)
    place[np.arange(MAX_SEL_BLOCKS), SEL_LANE0 + np.arange(MAX_SEL_BLOCKS)] = 1.0
    bf = lambda a: jnp.asarray(a, dtype=BF16)
    return bf(featk), bf(featw), bf(featc), jnp.asarray(qfeat), bf(qrel), bf(seg), bf(mmap), bf(place)


def _hgrn_stages(q_ref, f_ref, v_ref, z_ref, lb_ref, norm_ref, cum_ref, o_ref, rows):
    C, SUB = HGRN_CHUNK, HGRN_SUB
    n_sub = C // SUB
    chunks = range(len(rows))
    sub = lambda x, i: x[i * SUB:(i + 1) * SUB]
    w = {}

    def gates():
        lbr = lb_ref[...]
        e = jnp.exp(lbr - jnp.max(lbr, axis=0, keepdims=True))
        lb = e[0:1, :] / jnp.sum(e, axis=0, keepdims=True)
        w["v16"] = [v_ref[rows[j], :].astype(BF16) for j in chunks]
        f = [lb + (1.0 - lb) * jax.nn.sigmoid(f_ref[rows[j], :]) for j in chunks]
        w["k"] = [1.0 - f[j] for j in chunks]
        w["parts"] = [_split3(jnp.log(f[j])) for j in chunks]

    def cumsums():
        cum = cum_ref[...]
        parts = w["parts"]
        w["a"] = [(jnp.dot(cum, parts[j][0], preferred_element_type=F32)
                   + jnp.dot(cum, parts[j][1], preferred_element_type=F32)
                   + jnp.dot(cum, parts[j][2], preferred_element_type=F32)) for j in chunks]

    def operands():
        a, k = w["a"], w["k"]
        q1, k1, k2, qb, k3, start, dec = [], [], [], [], [], [], []
        for j in chunks:
            q1.append(q_ref[rows[j], :] * jnp.exp(a[j]))
            k1.append(k[j] * jnp.exp(-a[j]))
            tot = [a[j][(i + 1) * SUB - 1:(i + 1) * SUB] for i in range(n_sub)]
            s = [jnp.zeros_like(tot[0])]
            for i in range(n_sub):
                s.append(s[i] + tot[i])
            start.append(s)
            dec.append(jnp.exp(s[n_sub]))
            qb.append(jnp.concatenate([sub(q1[j], i) * jnp.exp(s[i]) for i in range(n_sub)], axis=0)
                      .astype(BF16))
            k2.append([sub(k1[j], i) * jnp.exp(tot[i]) for i in range(n_sub)])
            k3.append(jnp.concatenate([sub(k1[j], i) * jnp.exp(s[n_sub] - s[i]) for i in range(n_sub)],
                                      axis=0).astype(BF16))
        w.update(q1=q1, k1=k1, k2=k2, qb=qb, k3=k3, start=start, dec=dec)

    def scores():
        q1, k1, k2, start = w["q1"], w["k1"], w["k2"], w["start"]
        att = []
        for j in chunks:
            row_blocks = []
            for i in range(n_sub):
                rhs = [k2[j][jb] * jnp.exp(start[j][i] - start[j][jb + 1]) if jb < i - 1 else k2[j][jb]
                       for jb in range(i)]
                rhs.append(sub(k1[j], i))
                rhs = jnp.concatenate(rhs, axis=0) if len(rhs) > 1 else rhs[0]
                sc = _nt_dot(sub(q1[j], i).astype(BF16), rhs.astype(BF16))
                width = (i + 1) * SUB
                causal = (lax.broadcasted_iota(jnp.int32, (SUB, width), 1)
                          <= lax.broadcasted_iota(jnp.int32, (SUB, width), 0) + i * SUB)
                row_blocks.append(jnp.where(causal, sc, 0.0).astype(BF16))
            att.append(row_blocks)
        w["att"] = att

    def products():
        att, v16, k3 = w["att"], w["v16"], w["k3"]
        w["intra"] = [jnp.concatenate([jnp.dot(att[j][i], v16[j][0:(i + 1) * SUB],
                                               preferred_element_type=F32)
                                       for i in range(n_sub)], axis=0) for j in chunks]
        w["incr"] = [_tn_dot(v16[j], k3[j]) for j in chunks]

    def state_pass(st):
        inter = []
        for j in chunks:
            inter.append(_nt_dot(w["qb"][j], st.astype(BF16)))
            st = st * w["dec"][j] + w["incr"][j]
        w["inter"] = inter
        return st

    def finish():
        for j in chunks:
            _hgrn_head_out(w["inter"][j] + w["intra"][j], z_ref, norm_ref, o_ref, rows[j])

    return gates, cumsums, operands, scores, products, state_pass, finish


def _hgrn_head_out(o, z_ref, norm_ref, o_ref, rows):
    y = o * lax.rsqrt(jnp.mean(o * o, axis=-1, keepdims=True) + EPS) * norm_ref[...]
    z = z_ref[rows, :]
    o_ref[rows, :] = (y * (z * jax.nn.sigmoid(z))).astype(o_ref.dtype)


def _hgrn_pairwise_chunk(q_ref, f_ref, v_ref, z_ref, lb, norm_ref, cum_ref, o_ref, col_ref, c, st):
    C, SUB = HGRN_CHUNK, HGRN_SUB
    dk = q_ref.shape[1]
    rows = pl.ds(pl.multiple_of(c * C, C), C)
    f = lb + (1.0 - lb) * jax.nn.sigmoid(f_ref[rows, :])
    k = 1.0 - f
    cum = cum_ref[...]
    hi, mid, lo = _split3(jnp.log(f))
    a = (jnp.dot(cum, hi, preferred_element_type=F32) + jnp.dot(cum, mid, preferred_element_type=F32)
         + jnp.dot(cum, lo, preferred_element_type=F32))
    before, pieces = jnp.zeros((1, dk), F32), []
    for i in range(C // SUB):
        pieces.append(a[i * SUB:(i + 1) * SUB] + before)
        before = before + a[(i + 1) * SUB - 1:(i + 1) * SUB]
    b, b_end = jnp.concatenate(pieces, axis=0), before
    q = q_ref[rows, :].astype(F32)
    v16 = v_ref[rows, :].astype(BF16)
    col_ref[0] = b
    col_ref[1] = k
    key_i = lax.broadcasted_iota(jnp.int32, (C, C), 1)
    qry_i = lax.broadcasted_iota(jnp.int32, (C, C), 0)

    def column(s, sc):
        bs = col_ref[0, pl.ds(s, 1), :]
        ks = col_ref[1, pl.ds(s, 1), :]
        terms = jnp.exp(jnp.minimum(b - bs, 0.0)) * (q * ks)
        return jnp.where(key_i == s, jnp.sum(terms, axis=1, keepdims=True), sc)

    sc = lax.fori_loop(0, C, column, jnp.zeros((C, C), F32))
    att = jnp.where(key_i <= qry_i, sc, 0.0).astype(BF16)
    intra = jnp.dot(att, v16, preferred_element_type=F32)
    inter = _nt_dot((q * jnp.exp(b)).astype(BF16), st.astype(BF16))
    incr = _tn_dot(v16, (k * jnp.exp(b_end - b)).astype(BF16))
    _hgrn_head_out(inter + intra, z_ref, norm_ref, o_ref, rows)
    return st * jnp.exp(b_end) + incr


def _hgrn_kernel(q_ref, f_ref, v_ref, z_ref, lb_ref, norm_ref, cum_ref, o_ref, col_ref, factored_ref, *, group):
    seq, dk = q_ref.shape
    head = pl.program_id(1)

    def lower_bound():
        lbr = lb_ref[...]
        e = jnp.exp(lbr - jnp.max(lbr, axis=0, keepdims=True))
        return e[0:1, :] / jnp.sum(e, axis=0, keepdims=True)

    @pl.when(pl.program_id(0) == 0)
    def _():
        ok = jnp.min(jnp.log(lower_bound())) * HGRN_SUB >= -HGRN_MAX_LOG_DECAY
        factored_ref[head] = ok.astype(jnp.int32)

    def block(cb, st):
        rows = [pl.ds(pl.multiple_of((cb * group + j) * HGRN_CHUNK, HGRN_CHUNK), HGRN_CHUNK)
                for j in range(group)]
        gates, cumsums, operands, scores, products, state_pass, finish = _hgrn_stages(
            q_ref, f_ref, v_ref, z_ref, lb_ref, norm_ref, cum_ref, o_ref, rows)
        gates()
        cumsums()
        operands()
        scores()
        products()
        st = state_pass(st)
        finish()
        return st

    @pl.when(factored_ref[head] == 1)
    def _():
        lax.fori_loop(0, seq // (HGRN_CHUNK * group), block, jnp.zeros((dk, dk), F32))

    @pl.when(factored_ref[head] == 0)
    def _():
        chunk = functools.partial(_hgrn_pairwise_chunk, q_ref, f_ref, v_ref, z_ref, lower_bound(), norm_ref,
                                  cum_ref, o_ref, col_ref)
        lax.fori_loop(0, seq // HGRN_CHUNK, chunk, jnp.zeros((dk, dk), F32))


def _hgrn_patterns():
    C, SUB = HGRN_CHUNK, HGRN_SUB
    t = np.arange(C)[:, None]
    s = np.arange(C)[None, :]
    return (((t // SUB) == (s // SUB)) & (s <= t)).astype(np.float32)


def _hgrn(streams, lower_bounds, norm, *, batch, seq):
    dk = HGRN_HEAD_DIM
    heads = norm.shape[1] // dk
    cum = jnp.asarray(_hgrn_patterns(), dtype=BF16)
    col = lambda name: (lambda b, h, o=streams[name][1] // dk: (b, o + h))
    return pl.pallas_call(
        functools.partial(_hgrn_kernel, group=16),
        grid=(batch, heads),
        in_specs=[
            pl.BlockSpec((seq, dk), col("q_h")),
            pl.BlockSpec((seq, dk), col("f_h")),
            pl.BlockSpec((seq, dk), col("i_h")),
            pl.BlockSpec((seq, dk), col("z_h")),
            pl.BlockSpec((lower_bounds.shape[0], dk), lambda b, h: (0, h)),
            pl.BlockSpec((1, dk), lambda b, h: (0, h)),
            pl.BlockSpec(cum.shape, lambda b, h: (0, 0)),
        ],
        out_specs=pl.BlockSpec((seq, dk), lambda b, h: (b, h)),
        out_shape=jax.ShapeDtypeStruct((batch * seq, heads * dk), BF16),
        scratch_shapes=[pltpu.VMEM((2, HGRN_CHUNK, dk), F32), pltpu.SMEM((heads,), jnp.int32)],
        compiler_params=pltpu.CompilerParams(
            dimension_semantics=("arbitrary", "arbitrary"),
            vmem_limit_bytes=VMEM_LIMIT),
        name="hgrn",
    )(streams["q_h"][0], streams["f_h"][0], streams["i_h"][0], streams["z_h"][0], lower_bounds, norm, cum)


def _outproj_kernel(x_ref, oa_ref, oh_ref, wa_ref, wh_ref, g_ref, o_ref):
    y = x_ref[...] + jnp.dot(oa_ref[...], wa_ref[...], preferred_element_type=F32)
    y = y + jnp.dot(oh_ref[...], wh_ref[...], preferred_element_type=F32)
    ms = jnp.mean(y * y, axis=-1, keepdims=True)
    o_ref[...] = y * lax.rsqrt(ms + EPS) * g_ref[...]


def _outproj(x2, o_a, o_h, wa, wh, g, *, tm):
    M, D = x2.shape
    const = lambda a: pl.BlockSpec(a.shape, lambda i: (0,) * a.ndim)
    return pl.pallas_call(
        _outproj_kernel,
        grid=(M // tm,),
        in_specs=[
            pl.BlockSpec((tm, D), lambda i: (i, 0)),
            pl.BlockSpec((tm, o_a.shape[1]), lambda i: (i, 0)),
            pl.BlockSpec((tm, o_h.shape[1]), lambda i: (i, 0)),
            const(wa), const(wh), const(g),
        ],
        out_specs=pl.BlockSpec((tm, D), lambda i: (i, 0)),
        out_shape=jax.ShapeDtypeStruct((M, D), F32),
        compiler_params=pltpu.CompilerParams(
            dimension_semantics=("arbitrary",),
            vmem_limit_bytes=VMEM_LIMIT),
        name="outproj",
    )(x2, o_a, o_h, wa, wh, g)


def kernel(x, norm_in, w_in, cmp_pe_k, cmp_w1_k, cmp_w2_k, cmp_pe_v, cmp_w1_v, cmp_w2_v,
           lower_bounds, nsa_out_norm, hgrn_out_norm, w_out, final_norm):
    B, S, D = x.shape
    assert norm_in.shape[0] == 1, "single-layer problem"
    nsa_w = nsa_out_norm.shape[1]
    hgrn_w = hgrn_out_norm.shape[1]
    dh = NSA_HEAD_DIM
    G = NSA_KV_HEADS
    heads = nsa_w // dh
    kvw = G * dh
    n_gate = 3 * heads
    n_cmp = (S - CMP_BLOCK) // CMP_STRIDE + 1
    n_half = S // CMP_STRIDE
    assert S % NSA_TILE == 0 and WINDOW % NSA_TILE == 0 and n_half <= LANES and 2 * dh == LANES

    names = ["q_a", "k_cmp", "v_cmp", "k_slc", "v_slc", "k_win", "v_win", "gate", "z_a",
             "q_h", "f_h", "i_h", "z_h"]
    widths = [nsa_w] + [kvw] * 6 + [n_gate, nsa_w] + [hgrn_w] * 4
    starts = dict(zip(names, np.cumsum([0] + widths[:-1]).tolist()))
    wd = dict(zip(names, widths))

    paired = lambda kname, vname: [(starts[n] + g * dh, dh) for g in range(G) for n in (kname, vname)]
    whole = lambda n: [(starts[n], wd[n])]
    groups = [[("q_a", whole("q_a")),
               ("cmp", paired("k_cmp", "v_cmp")), ("slc", paired("k_slc", "v_slc")),
               ("win", paired("k_win", "v_win")), ("q_h", whole("q_h")), ("i_h", whole("i_h"))],
              [("z_a", whole("z_a")), ("f_h", whole("f_h")), ("z_h", whole("z_h"))]]
    plan, col_of, off = [], {}, 0
    for group in groups:
        for n, pieces in group:
            col_of[n] = off
            for src, width in pieces:
                plan.append((off, src, width, dh ** -0.5 if n == "q_a" else 1.0))
                off += width
    n_main = off
    n16_cols = col_of["z_a"]
    plan.append((n_main, starts["gate"], n_gate, 1.0))
    w_all = _relayout_weights(w_in[0].T, tuple(plan), n_main + LANES)

    x2 = x.reshape(B * S, D)
    proj16, proj32, gate = _proj(x2, norm_in, w_all, tm=min(1024, B * S), tn=1536,
                                 n_main=n_main, n16_cols=n16_cols)
    streams = {}
    for arr, group, base in ((proj16, groups[0], 0), (proj32, groups[1], n16_cols)):
        for n, _ in group:
            streams[n] = (arr, col_of[n] - base)

    featk, featw, featc, qfeat, qrel, seg, mmap, place = _nsa_tables(S, n_half, n_cmp, heads)

    def w1_halves(w1k, w1v):
        hk = w1k.shape[1]
        k3 = w1k.reshape(2, CMP_STRIDE, dh, hk)
        v3 = w1v.reshape(2, CMP_STRIDE, dh, hk)
        zk = jnp.zeros_like(k3[0])
        top = lambda a: jnp.concatenate([a, zk], axis=-1)
        bot = lambda a: jnp.concatenate([zk, a], axis=-1)
        half = lambda i: jnp.concatenate([top(k3[i]), bot(v3[i])], axis=1).reshape(CMP_STRIDE * 2 * dh, 2 * hk)
        return half(0).astype(BF16), half(1).astype(BF16)

    wp, wq = w1_halves(cmp_w1_k[0], cmp_w1_v[0])
    zk = jnp.zeros_like(cmp_w2_k[0])
    w2 = jnp.concatenate([jnp.concatenate([cmp_w2_k[0], zk], axis=1),
                          jnp.concatenate([zk, cmp_w2_v[0]], axis=1)], axis=0).astype(BF16)
    pe = jnp.concatenate([cmp_pe_k[0].reshape(2, CMP_STRIDE, dh), cmp_pe_v[0].reshape(2, CMP_STRIDE, dh)],
                         axis=-1).reshape(2, CMP_STRIDE * 2 * dh)
    kc, vc = _compress(streams["cmp"][0], pe, wp, wq, w2, jnp.tile(featc, (G, 1)),
                       batch=B, seq=S, col0=streams["cmp"][1], n_half=n_half)

    o_a = _nsa(streams, gate, kc, vc, featk, featw, qfeat, qrel, seg, mmap, place, nsa_out_norm,
               batch=B, seq=S, n_cmp=n_cmp)
    o_h = _hgrn(streams, lower_bounds, hgrn_out_norm, batch=B, seq=S)

    wo = w_out[0].astype(BF16)
    out = _outproj(x2, o_a, o_h, wo[:nsa_w], wo[nsa_w:], final_norm.reshape(1, D), tm=512)
    return out.reshape(B, S, D)
```

```python
import functools

import ml_dtypes
import numpy as np
import jax
import jax.numpy as jnp
from jax import lax
from jax.experimental import pallas as pl
from jax.experimental.pallas import tpu as pltpu

F32 = jnp.float32
BF16 = jnp.bfloat16

EPS = 1e-6
NEG_INF = -1e30
LOG2E = 1.4426950408889634

NSA_HEAD_DIM = 64
NSA_KV_HEADS = 4
CMP_BLOCK = 32
CMP_STRIDE = 16
SEL_BLOCK = 64
SEL_TOP = 8
SEL_BONUS = 1.0e4
WINDOW = 512
HGRN_HEAD_DIM = 128
HGRN_CHUNK = 64
HGRN_SUB = 16
HGRN_MAX_LOG_DECAY = 60.0

LANES = 128
VMEM_LIMIT = 56 * 1024 * 1024
NSA_TILE = 256
NSA_TILES_PER_STEP = 8
SEL_RECENT = 3

SEL_LANE0 = NSA_HEAD_DIM
MAX_SEL_BLOCKS = 32
POS_LANE0 = SEL_LANE0 + MAX_SEL_BLOCKS
N_SPLIT = 4
FLAG_LANE = POS_LANE0 + 2 * N_SPLIT
SHIFT_LANE0 = FLAG_LANE + 1
REL_LANE0 = SHIFT_LANE0 + N_SPLIT
SCORE_BOUND = 100.0
SAFETY = 1.02


def _nt_dot(a, b):
    return lax.dot_general(a, b, (((1,), (1,)), ((), ())), preferred_element_type=F32)


def _tn_dot(a, b):
    return lax.dot_general(a, b, (((0,), (0,)), ((), ())), preferred_element_type=F32)


def _split3(x):
    hi = x.astype(BF16)
    r1 = x - hi.astype(F32)
    mid = r1.astype(BF16)
    lo = (r1 - mid.astype(F32)).astype(BF16)
    return hi, mid, lo


def _proj_kernel(x_ref, g_ref, w_ref, wg_ref, o16_ref, o32_ref, og_ref, h_ref, *, row_chunk, n16):
    j = pl.program_id(1)

    @pl.when(j == 0)
    def _():
        n_chunks = x_ref.shape[0] // row_chunk

        def body(c, carry):
            rows = pl.ds(pl.multiple_of(c * row_chunk, row_chunk), row_chunk)
            x = x_ref[rows, :]
            ms = jnp.mean(x * x, axis=-1, keepdims=True)
            h_ref[rows, :] = (x * lax.rsqrt(ms + EPS) * g_ref[...]).astype(BF16)
            return carry

        lax.fori_loop(0, n_chunks, body, 0)
        og_ref[...] = jnp.dot(h_ref[...], wg_ref[...], preferred_element_type=F32)

    @pl.when(j < n16)
    def _():
        o16_ref[...] = jnp.dot(h_ref[...], w_ref[...], preferred_element_type=F32).astype(BF16)

    @pl.when(j >= n16)
    def _():
        o32_ref[...] = jnp.dot(h_ref[...], w_ref[...], preferred_element_type=F32)


def _relayout_kernel(wt_ref, o_ref, *, plan):
    o_ref[:, o_ref.shape[1] - LANES:] = jnp.zeros((o_ref.shape[0], LANES), o_ref.dtype)
    for dst, src, width, scale in plan:
        o_ref[:, dst:dst + width] = (wt_ref[src:src + width, :].T * scale).astype(o_ref.dtype)


def _relayout_weights(wt, plan, n_out, *, rows=256):
    n_in, D = wt.shape
    return pl.pallas_call(
        functools.partial(_relayout_kernel, plan=plan),
        grid=(D // rows,),
        in_specs=[pl.BlockSpec((n_in, rows), lambda i: (0, i))],
        out_specs=pl.BlockSpec((rows, n_out), lambda i: (i, 0)),
        out_shape=jax.ShapeDtypeStruct((D, n_out), BF16),
        compiler_params=pltpu.CompilerParams(
            dimension_semantics=("arbitrary",),
            vmem_limit_bytes=VMEM_LIMIT),
        name="relayout",
    )(wt)


def _proj(x2, g, w, *, tm, tn, n_main, n16_cols):
    M, D = x2.shape
    N = n_main
    NG = w.shape[1] - n_main
    n16 = n16_cols // tn
    assert n16 * tn == n16_cols and N % tn == 0 and N % NG == 0
    return pl.pallas_call(
        functools.partial(_proj_kernel, row_chunk=128, n16=n16),
        grid=(M // tm, N // tn),
        in_specs=[
            pl.BlockSpec((tm, D), lambda i, j: (i, 0)),
            pl.BlockSpec((1, D), lambda i, j: (0, 0)),
            pl.BlockSpec((D, tn), lambda i, j: (0, j)),
            pl.BlockSpec((D, NG), lambda i, j: (0, N // NG)),
        ],
        out_specs=[
            pl.BlockSpec((tm, tn), lambda i, j: (i, jnp.minimum(j, n16 - 1))),
            pl.BlockSpec((tm, tn), lambda i, j: (i, jnp.maximum(j - n16, 0))),
            pl.BlockSpec((tm, NG), lambda i, j: (i, 0)),
        ],
        out_shape=[
            jax.ShapeDtypeStruct((M, n16_cols), BF16),
            jax.ShapeDtypeStruct((M, N - n16_cols), F32),
            jax.ShapeDtypeStruct((M, NG), F32),
        ],
        scratch_shapes=[pltpu.VMEM((tm, D), BF16)],
        compiler_params=pltpu.CompilerParams(
            dimension_semantics=("arbitrary", "arbitrary"),
            vmem_limit_bytes=VMEM_LIMIT),
        name="proj",
    )(x2, g, w, w)


def _compress_kernel(c0_ref, c1_ref, c2_ref, c3_ref, pe_ref, wp_ref, wq_ref, w2_ref, feat_ref,
                     kc_ref, vc_ref, x_ref, c32_ref, *, n_half):
    for g, c_ref in enumerate((c0_ref, c1_ref, c2_ref, c3_ref)):
        c32_ref[...] = c_ref[...].astype(F32)
        for l in range(CMP_STRIDE):
            x_ref[g * n_half:(g + 1) * n_half, l * LANES:(l + 1) * LANES] = (
                c32_ref[pl.ds(l, n_half, stride=CMP_STRIDE), :])
    x = x_ref[...]
    rows = x.shape[0]
    first = jnp.dot((x + pe_ref[0:1, :]).astype(BF16), wp_ref[...], preferred_element_type=F32)
    second = jnp.dot((x + pe_ref[1:2, :]).astype(BF16), wq_ref[...], preferred_element_type=F32)
    hidden = first + pltpu.roll(second, rows - 1, 0)
    out = jnp.dot(jax.nn.gelu(hidden).astype(BF16), w2_ref[...], preferred_element_type=F32)
    lane = lax.broadcasted_iota(jnp.int32, out.shape, 1)
    dh = NSA_HEAD_DIM
    kc_ref[...] = jnp.where(lane < dh, out.astype(BF16), feat_ref[...])
    for g in range(NSA_KV_HEADS):
        vc_ref[g * dh:(g + 1) * dh, :] = out[g * n_half:(g + 1) * n_half, :].T[dh:, :].astype(BF16)


def _compress(proj, pe, wp, wq, w2, feat, *, batch, seq, col0, n_half):
    rows = NSA_KV_HEADS * n_half
    rows_t = NSA_KV_HEADS * NSA_HEAD_DIM
    const = lambda a: pl.BlockSpec(a.shape, lambda b: (0,) * a.ndim)
    stream = lambda g: pl.BlockSpec((seq, LANES), lambda b, o=col0 // LANES + g: (b, o))
    return pl.pallas_call(
        functools.partial(_compress_kernel, n_half=n_half),
        grid=(batch,),
        in_specs=[stream(0), stream(1), stream(2), stream(3),
                  const(pe), const(wp), const(wq), const(w2), const(feat)],
        out_specs=[
            pl.BlockSpec((rows, LANES), lambda b: (b, 0)),
            pl.BlockSpec((rows_t, n_half), lambda b: (b, 0)),
        ],
        out_shape=[
            jax.ShapeDtypeStruct((batch * rows, LANES), BF16),
            jax.ShapeDtypeStruct((batch * rows_t, n_half), BF16),
        ],
        scratch_shapes=[pltpu.VMEM((rows, CMP_STRIDE * LANES), F32), pltpu.VMEM((seq, LANES), F32)],
        compiler_params=pltpu.CompilerParams(
            dimension_semantics=("arbitrary",),
            vmem_limit_bytes=VMEM_LIMIT),
        name="compress",
    )(proj, proj, proj, proj, pe, wp, wq, w2, feat)


def _nsa_kernel(q_ref, qall_ref, z_ref, gate_ref, slc_ref, win_ref, kc_ref, vct_ref, featk_ref, featw_ref,
                qfeat_ref, qrel_ref, seg_ref, mmap_ref, place_ref, norm_ref, o_ref,
                ksel_ref, vselt_ref, kwin_ref, vwint_ref, gt_ref, bounded_ref, *, n_cmp):
    g = pl.program_id(1)
    qi = pl.program_id(2)
    tq = NSA_TILE
    per_step = q_ref.shape[0] // tq
    tk = tq
    seq = slc_ref.shape[0]
    dh = NSA_HEAD_DIM
    rep = q_ref.shape[1] // dh
    n_half = kc_ref.shape[0]
    n_pad = WINDOW // tk

    @pl.when(qi == 0)
    def _():
        lane = lax.broadcasted_iota(jnp.int32, (seq, LANES), 1)

        def max_sq_norm(x, width):
            s = jnp.dot((x * x).astype(BF16), seg_ref[0:width, :], preferred_element_type=F32)
            return jnp.max(s) * SAFETY

        kv = slc_ref[...]
        k2_sel = max_sq_norm(jnp.where(lane < dh, kv.astype(F32), 0.0), LANES)
        ksel_ref[...] = jnp.where(lane < dh, kv, featk_ref[...])
        vt = kv.astype(F32).T[dh:, :].astype(BF16)
        for kt in range(seq // tk):
            vselt_ref[kt] = vt[:, kt * tk:(kt + 1) * tk]
        kv = win_ref[...]
        k2_win = max_sq_norm(jnp.where(lane < dh, kv.astype(F32), 0.0), LANES)
        kwin_ref[0:WINDOW, :] = featw_ref[0:WINDOW, :]
        kwin_ref[WINDOW:, :] = jnp.where(lane < dh, kv, featw_ref[WINDOW:, :])
        vt = kv.astype(F32).T[dh:, :].astype(BF16)
        for kt in range(n_pad):
            vwint_ref[kt] = jnp.zeros((dh, tk), BF16)
        for kt in range(seq // tk):
            vwint_ref[n_pad + kt] = vt[:, kt * tk:(kt + 1) * tk]
        q2 = max_sq_norm(qall_ref[...].astype(F32), rep * dh) * (LOG2E * LOG2E)
        limit = SCORE_BOUND * SCORE_BOUND
        bounded_ref[0] = ((q2 * k2_sel <= limit) & (q2 * k2_win <= limit)).astype(jnp.int32)

    heads = lambda a: jnp.concatenate([a] * rep, axis=1)
    sub8 = lambda a: a.reshape(a.shape[0] // 8, 8, a.shape[1])

    def softmax_pv(lgs, v_tiles, bounded, extra=None):
        if bounded:
            m = 0.0
        else:
            m8 = functools.reduce(jnp.maximum, [jnp.max(sub8(lg), axis=0) for lg in lgs])
            m = jnp.max(m8, axis=0, keepdims=True)
        l8 = jnp.zeros((8, lgs[0].shape[1]), F32)
        acc = jnp.zeros((dh, lgs[0].shape[1]), F32)
        for lg, vt in zip(lgs, v_tiles):
            pt = jnp.exp2(lg) if bounded else jnp.exp2(lg - m)
            l8 = l8 + jnp.sum(sub8(pt), axis=0)
            acc = acc + jnp.dot(vt, pt.astype(BF16), preferred_element_type=F32)
        if extra is not None:
            l8, acc = extra(l8, acc)
        return acc / jnp.sum(l8, axis=0, keepdims=True)

    def step(n, bounded, slot):
        static = isinstance(n, int)
        assert static == bounded
        own = (slice(slot * tq, (slot + 1) * tq) if static
               else pl.ds(pl.multiple_of(slot * tq, tq), tq))
        t0 = n * tq
        lane = lax.broadcasted_iota(jnp.int32, (tq, LANES), 1)
        key_i = lax.broadcasted_iota(jnp.int32, (tk, tq), 0)
        qry_i = lax.broadcasted_iota(jnp.int32, (tk, tq), 1)
        causal_1 = jnp.where(key_i <= qry_i, 0.0, NEG_INF)
        causal_bias = heads(causal_1)
        band_bias = heads(jnp.where(key_i > qry_i, 0.0, NEG_INF))

        qfeat = qfeat_ref[0, n]
        rel_lanes = (lane >= REL_LANE0) & (lane < REL_LANE0 + 3)
        qa = []
        for r in range(rep):
            qcol = q_ref[own, (r // 2) * LANES:(r // 2 + 1) * LANES].astype(F32)
            if r % 2:
                qcol = pltpu.roll(qcol, dh, 1)
            feat = jnp.where(rel_lanes, qrel_ref[0, r * tq:(r + 1) * tq, :].astype(F32), qfeat[r:r + 1, :])
            qa.append(jnp.where(lane < dh, qcol * LOG2E, feat))
        q0 = jnp.concatenate(qa, axis=0).astype(BF16)

        gt_ref[slot] = jax.nn.sigmoid(gate_ref[own, :]).T
        out_scale = []
        for pair in range(rep // 2):
            z = z_ref[own, pair * LANES:(pair + 1) * LANES]
            out_scale.append(norm_ref[:, pair * LANES:(pair + 1) * LANES] * (z * jax.nn.sigmoid(z)))

        nc = min(n_half, (t0 + tq) // CMP_STRIDE) if static else n_half
        n_idx = lax.broadcasted_iota(jnp.int32, (nc, tq), 0)
        t_idx = t0 + lax.broadcasted_iota(jnp.int32, (nc, tq), 1)
        valid_c = heads((CMP_STRIDE * n_idx + (CMP_BLOCK - 1) <= t_idx) & (n_idx < n_cmp))
        lg_c = jnp.where(valid_c, _nt_dot(kc_ref[0:nc, :], q0), NEG_INF)

        n_wt = n_pad + 1
        win_tiles = [w for w in range(n_wt) if not (static and n + w < n_pad)]
        lgs = []
        for w in win_tiles:
            rows = (slice(t0 + w * tk, t0 + (w + 1) * tk) if static
                    else pl.ds(pl.multiple_of(t0 + w * tk, tk), tk))
            lg = _nt_dot(kwin_ref[rows, :], q0)
            if w == 0:
                lg = lg + band_bias
            if w == n_wt - 1:
                lg = lg + causal_bias
            lgs.append(lg)
        o_win = softmax_pv(lgs, [vwint_ref[n + w] for w in win_tiles], bounded)

        e = jnp.exp2(lg_c - jnp.max(lg_c, axis=0, keepdims=True))
        p = jnp.where(valid_c, e / jnp.sum(e, axis=0, keepdims=True), 0.0)
        if nc < n_half:
            p = jnp.concatenate([p, jnp.zeros((n_half - nc, rep * tq), F32)], axis=0)
        o_cmp = jnp.dot(vct_ref[...], p.astype(BF16), preferred_element_type=F32)
        p_sum = p[:, 0:tq]
        for r in range(1, rep):
            p_sum = p_sum + p[:, r * tq:(r + 1) * tq]

        nb = min(MAX_SEL_BLOCKS, (t0 + tq) // SEL_BLOCK) if static else MAX_SEL_BLOCKS
        mm = mmap_ref[...]
        hi, mid, lo = _split3(p_sum)
        p_slc = (jnp.dot(mm, hi, preferred_element_type=F32) + jnp.dot(mm, mid, preferred_element_type=F32)
                 + jnp.dot(mm, lo, preferred_element_type=F32))[0:MAX_SEL_BLOCKS]
        if static:
            sel_scores = lambda kt: _nt_dot(ksel_ref[kt * tk:(kt + 1) * tk, :], q0)
            maybe_idle = [kt for kt in range(1, n + 1 - SEL_RECENT)]
            raw = {kt: sel_scores(kt) for kt in range(n + 1) if kt not in maybe_idle}
        jj = lax.broadcasted_iota(jnp.int32, (MAX_SEL_BLOCKS, tq), 0)
        tt = t0 + lax.broadcasted_iota(jnp.int32, (MAX_SEL_BLOCKS, tq), 1)
        cur = lax.shift_right_logical(tt, int(np.log2(SEL_BLOCK)))
        forced = (jj == 0) | (jj == cur) | (jj == cur - 1)
        future = jj > cur
        score = jnp.where(future, -1.0, p_slc + jnp.where(forced, SEL_BONUS, 0.0))
        rank = jnp.zeros((MAX_SEL_BLOCKS, tq), jnp.int32)
        for i in range(nb):
            other = jnp.broadcast_to(score[i:i + 1, :], score.shape)
            beats = (other > score) | ((other == score) & (jj > i))
            rank = rank + beats.astype(jnp.int32)
        keep = (rank < SEL_TOP) & jnp.logical_not(future)

        if static:
            blocks_per_tile = tk // SEL_BLOCK
            block_bias = jnp.where(keep, 0.0, NEG_INF)

            def tile_bias(kt):
                b = jnp.concatenate(
                    [jnp.broadcast_to(block_bias[blocks_per_tile * kt + j:blocks_per_tile * kt + j + 1, :],
                                      (SEL_BLOCK, tq)) for j in range(blocks_per_tile)], axis=0)
                return heads(b + causal_1 if kt == n else b)

            wanted = [jnp.max(block_bias[blocks_per_tile * kt:blocks_per_tile * (kt + 1)]) == 0.0
                      for kt in maybe_idle]

            def idle_tiles(l8, acc):
                for kt, some in zip(maybe_idle, wanted):
                    def live(l8, acc, kt=kt):
                        pt = jnp.exp2(sel_scores(kt) + tile_bias(kt))
                        return (l8 + jnp.sum(sub8(pt), axis=0),
                                acc + jnp.dot(vselt_ref[kt], pt.astype(BF16), preferred_element_type=F32))

                    l8, acc = lax.cond(some, live, lambda l8, acc: (l8, acc), l8, acc)
                return l8, acc

            o_slc = softmax_pv([raw[kt] + tile_bias(kt) for kt in raw], [vselt_ref[kt] for kt in raw],
                               bounded, extra=idle_tiles if maybe_idle else None)
        else:
            chosen = jnp.where(keep, 1.0, 0.0).astype(BF16)
            placed = _tn_dot(chosen, place_ref[...])
            sel_lanes = (lane >= SEL_LANE0) & (lane < SEL_LANE0 + MAX_SEL_BLOCKS)
            sel_bias = (placed - 1.0) * (-NEG_INF)
            qs = jnp.concatenate([jnp.where(sel_lanes, sel_bias, a) for a in qa], axis=0).astype(BF16)

            def sel_tile(kt, carry, bias=None):
                m, l, acc = carry
                lg = _nt_dot(ksel_ref[pl.ds(pl.multiple_of(kt * tk, tk), tk), :], qs)
                if bias is not None:
                    lg = lg + bias
                m_new = jnp.maximum(m, jnp.max(lg, axis=0, keepdims=True))
                alpha = jnp.exp2(m - m_new)
                pt = jnp.exp2(lg - m_new)
                return (m_new, alpha * l + jnp.sum(pt, axis=0, keepdims=True),
                        alpha * acc + jnp.dot(vselt_ref[kt], pt.astype(BF16), preferred_element_type=F32))

            init = (jnp.full((1, rep * tq), NEG_INF, F32), jnp.zeros((1, rep * tq), F32),
                    jnp.zeros((dh, rep * tq), F32))
            _, l, acc = sel_tile(n, lax.fori_loop(0, n, sel_tile, init), causal_bias)
            o_slc = acc / l

        ys = []
        for r in range(rep):
            cols = slice(r * tq, (r + 1) * tq)
            gate = lambda c: gt_ref[slot, pl.ds(3 * (g * rep + r) + c, 1), :]
            o = gate(0) * o_cmp[:, cols] + gate(1) * o_slc[:, cols] + gate(2) * o_win[:, cols]
            ys.append(o * lax.rsqrt(jnp.mean(o * o, axis=0, keepdims=True) + EPS))
        for pair in range(rep // 2):
            cols = slice(pair * LANES, (pair + 1) * LANES)
            y = jnp.concatenate([ys[2 * pair], ys[2 * pair + 1]], axis=0).T
            o_ref[own, cols] = (y * out_scale[pair]).astype(o_ref.dtype)

    def tiles(first):
        for slot in range(per_step):
            step(first + slot, True, slot)

    @pl.when(bounded_ref[0] == 1)
    def _():
        lax.switch(qi, [functools.partial(tiles, first) for first in range(0, seq // tq, per_step)])

    @pl.when(bounded_ref[0] == 0)
    def _():
        @pl.loop(0, per_step)
        def _(slot):
            step(qi * per_step + slot, False, slot)


def _nsa(streams, gate, kc, vc, featk, featw, qfeat, qrel, seg, mmap, place, norm, *, batch, seq, n_cmp):
    tq = NSA_TILE
    rows = NSA_TILES_PER_STEP * tq
    nq = seq // rows
    G = NSA_KV_HEADS
    gw = norm.shape[1] // G
    n_half = kc.shape[0] // (batch * G)
    const = lambda a: pl.BlockSpec(a.shape, lambda b, g, i: (0,) * a.ndim)
    stream = lambda name: pl.BlockSpec((seq, LANES),
                                       lambda b, g, i, o=streams[name][1] // LANES: (b, o + g))
    tile = lambda name: pl.BlockSpec((rows, gw), lambda b, g, i, o=streams[name][1] // gw: (b * nq + i, o + g))
    return pl.pallas_call(
        functools.partial(_nsa_kernel, n_cmp=n_cmp),
        grid=(batch, G, nq),
        in_specs=[
            tile("q_a"),
            pl.BlockSpec((seq, gw), lambda b, g, i, o=streams["q_a"][1] // gw: (b, o + g)),
            tile("z_a"),
            pl.BlockSpec((rows, gate.shape[1]), lambda b, g, i: (b * nq + i, 0)),
            stream("slc"), stream("win"),
            pl.BlockSpec((n_half, LANES), lambda b, g, i: (b * G + g, 0)),
            pl.BlockSpec((NSA_HEAD_DIM, n_half), lambda b, g, i: (b * G + g, 0)),
            const(featk), const(featw),
            pl.BlockSpec((1,) + qfeat.shape[1:], lambda b, g, i: (g, 0, 0, 0)),
            pl.BlockSpec((1,) + qrel.shape[1:], lambda b, g, i: (g, 0, 0)),
            const(seg), const(mmap), const(place),
            pl.BlockSpec((1, gw), lambda b, g, i: (0, g)),
        ],
        out_specs=pl.BlockSpec((rows, gw), lambda b, g, i: (b * nq + i, g)),
        out_shape=jax.ShapeDtypeStruct((batch * seq, G * gw), BF16),
        scratch_shapes=[
            pltpu.VMEM((seq, LANES), BF16),
            pltpu.VMEM((seq // tq, NSA_HEAD_DIM, tq), BF16),
            pltpu.VMEM((seq + WINDOW, LANES), BF16),
            pltpu.VMEM(((seq + WINDOW) // tq, NSA_HEAD_DIM, tq), BF16),
            pltpu.VMEM((NSA_TILES_PER_STEP, gate.shape[1], tq), F32),
            pltpu.SMEM((1,), jnp.int32),
        ],
        compiler_params=pltpu.CompilerParams(
            dimension_semantics=("arbitrary", "arbitrary", "arbitrary"),
            vmem_limit_bytes=VMEM_LIMIT),
        name="nsa",
    )(streams["q_a"][0], streams["q_a"][0], streams["z_a"][0], gate, streams["slc"][0], streams["win"][0],
      kc, vc, featk, featw, qfeat, qrel, seg, mmap, place, norm)


def _bf16_terms(x, n):
    terms, rest = [], np.asarray(x, np.float64)
    for _ in range(n):
        t = rest.astype(np.float32).astype(ml_dtypes.bfloat16).astype(np.float64)
        terms.append(t.astype(np.float32))
        rest = rest - t
    return terms


def _nsa_tables(seq, n_half, n_cmp, heads):
    assert seq // SEL_BLOCK <= MAX_SEL_BLOCKS and FLAG_LANE < LANES

    def key_features(pos, onehot_blocks):
        f = np.zeros((len(pos), LANES), np.float32)
        if onehot_blocks:
            f[np.arange(len(pos)), SEL_LANE0 + pos // SEL_BLOCK] = 1.0
        f[:, POS_LANE0:POS_LANE0 + N_SPLIT] = ((pos // 64) * 64)[:, None]
        f[:, POS_LANE0 + N_SPLIT:POS_LANE0 + 2 * N_SPLIT] = (pos % 64)[:, None]
        return f

    assert REL_LANE0 + 3 <= LANES
    featk = key_features(np.arange(seq), True)
    featk[:, SHIFT_LANE0:REL_LANE0 + 3] = 1.0
    featw = np.concatenate([np.zeros((WINDOW, LANES), np.float32), key_features(np.arange(seq), False)])
    featw[:WINDOW, FLAG_LANE] = NEG_INF
    featw[WINDOW:, SHIFT_LANE0:REL_LANE0 + 3] = 1.0
    featc = key_features(CMP_STRIDE * np.arange(n_half) + CMP_BLOCK - 1, False)
    featc[n_cmp:, FLAG_LANE] = NEG_INF

    slopes = (2.0 ** (-8.0 * np.arange(1, heads + 1) / heads)).astype(np.float32).astype(np.float64)
    slopes2 = slopes * LOG2E
    terms = _bf16_terms(slopes2, N_SPLIT)
    rep = heads // NSA_KV_HEADS
    nq = seq // NSA_TILE
    qfeat = np.zeros((NSA_KV_HEADS, nq, 8, LANES), np.float32)
    qrel = np.zeros((NSA_KV_HEADS, rep * NSA_TILE, LANES), np.float32)
    for h in range(heads):
        g, r = divmod(h, rep)
        for i, t in enumerate(terms):
            qfeat[g, :, r, POS_LANE0 + i] = t[h]
            qfeat[g, :, r, POS_LANE0 + N_SPLIT + i] = t[h]
        qfeat[g, :, r, FLAG_LANE] = 1.0
        for i, t in enumerate(_bf16_terms(-slopes2[h] * NSA_TILE * np.arange(nq), N_SPLIT)):
            qfeat[g, :, r, SHIFT_LANE0 + i] = t
        for i, t in enumerate(_bf16_terms(-slopes2[h] * np.arange(NSA_TILE), 3)):
            qrel[g, r * NSA_TILE:(r + 1) * NSA_TILE, REL_LANE0 + i] = t
    seg = np.zeros((2 * LANES, LANES), np.float32)
    seg[np.arange(2 * LANES), np.arange(2 * LANES) // NSA_HEAD_DIM] = 1.0

    cs = CMP_STRIDE * np.arange(n_half)[None, :]
    ss = SEL_BLOCK * np.arange(LANES)[:, None]
    overlap = np.clip(np.minimum(cs + CMP_BLOCK, ss + SEL_BLOCK) - np.maximum(cs, ss), 0, None)
    mmap = (overlap / CMP_BLOCK) * (np.arange(n_half)[None, :] < n_cmp) * (ss < seq)
    place = np.zeros((MAX_SEL_BLOCKS, LANES), np.float32)
    place[np.arange(MAX_SEL_BLOCKS), SEL_LANE0 + np.arange(MAX_SEL_BLOCKS)] = 1.0
    bf = lambda a: jnp.asarray(a, dtype=BF16)
    return bf(featk), bf(featw), bf(featc), jnp.asarray(qfeat), bf(qrel), bf(seg), bf(mmap), bf(place)


def _hgrn_stages(q_ref, f_ref, v_ref, z_ref, lb_ref, norm_ref, cum_ref, o_ref, rows):
    C, SUB = HGRN_CHUNK, HGRN_SUB
    n_sub = C // SUB
    chunks = range(len(rows))
    sub = lambda x, i: x[i * SUB:(i + 1) * SUB]
    w = {}

    def gates():
        lbr = lb_ref[...]
        e = jnp.exp(lbr - jnp.max(lbr, axis=0, keepdims=True))
        lb = e[0:1, :] / jnp.sum(e, axis=0, keepdims=True)
        w["v16"] = [v_ref[rows[j], :].astype(BF16) for j in chunks]
        f = [lb + (1.0 - lb) * jax.nn.sigmoid(f_ref[rows[j], :]) for j in chunks]
        w["k"] = [1.0 - f[j] for j in chunks]
        w["parts"] = [_split3(jnp.log(f[j])) for j in chunks]

    def cumsums():
        cum = cum_ref[...]
        parts = w["parts"]
        w["a"] = [(jnp.dot(cum, parts[j][0], preferred_element_type=F32)
                   + jnp.dot(cum, parts[j][1], preferred_element_type=F32)
                   + jnp.dot(cum, parts[j][2], preferred_element_type=F32)) for j in chunks]

    def operands():
        a, k = w["a"], w["k"]
        q1, k1, k2, qb, k3, start, dec = [], [], [], [], [], [], []
        for j in chunks:
            q1.append(q_ref[rows[j], :] * jnp.exp(a[j]))
            k1.append(k[j] * jnp.exp(-a[j]))
            tot = [a[j][(i + 1) * SUB - 1:(i + 1) * SUB] for i in range(n_sub)]
            s = [jnp.zeros_like(tot[0])]
            for i in range(n_sub):
                s.append(s[i] + tot[i])
            start.append(s)
            dec.append(jnp.exp(s[n_sub]))
            qb.append(jnp.concatenate([sub(q1[j], i) * jnp.exp(s[i]) for i in range(n_sub)], axis=0)
                      .astype(BF16))
            k2.append([sub(k1[j], i) * jnp.exp(tot[i]) for i in range(n_sub)])
            k3.append(jnp.concatenate([sub(k1[j], i) * jnp.exp(s[n_sub] - s[i]) for i in range(n_sub)],
                                      axis=0).astype(BF16))
        w.update(q1=q1, k1=k1, k2=k2, qb=qb, k3=k3, start=start, dec=dec)

    def scores():
        q1, k1, k2, start = w["q1"], w["k1"], w["k2"], w["start"]
        att = []
        for j in chunks:
            row_blocks = []
            for i in range(n_sub):
                rhs = [k2[j][jb] * jnp.exp(start[j][i] - start[j][jb + 1]) if jb < i - 1 else k2[j][jb]
                       for jb in range(i)]
                rhs.append(sub(k1[j], i))
                rhs = jnp.concatenate(rhs, axis=0) if len(rhs) > 1 else rhs[0]
                sc = _nt_dot(sub(q1[j], i).astype(BF16), rhs.astype(BF16))
                width = (i + 1) * SUB
                causal = (lax.broadcasted_iota(jnp.int32, (SUB, width), 1)
                          <= lax.broadcasted_iota(jnp.int32, (SUB, width), 0) + i * SUB)
                row_blocks.append(jnp.where(causal, sc, 0.0).astype(BF16))
            att.append(row_blocks)
        w["att"] = att

    def products():
        att, v16, k3 = w["att"], w["v16"], w["k3"]
        w["intra"] = [jnp.concatenate([jnp.dot(att[j][i], v16[j][0:(i + 1) * SUB],
                                               preferred_element_type=F32)
                                       for i in range(n_sub)], axis=0) for j in chunks]
        w["incr"] = [_tn_dot(v16[j], k3[j]) for j in chunks]

    def state_pass(st):
        inter = []
        for j in chunks:
            inter.append(_nt_dot(w["qb"][j], st.astype(BF16)))
            st = st * w["dec"][j] + w["incr"][j]
        w["inter"] = inter
        return st

    def finish():
        for j in chunks:
            _hgrn_head_out(w["inter"][j] + w["intra"][j], z_ref, norm_ref, o_ref, rows[j])

    return gates, cumsums, operands, scores, products, state_pass, finish


def _hgrn_head_out(o, z_ref, norm_ref, o_ref, rows):
    y = o * lax.rsqrt(jnp.mean(o * o, axis=-1, keepdims=True) + EPS) * norm_ref[...]
    z = z_ref[rows, :]
    o_ref[rows, :] = (y * (z * jax.nn.sigmoid(z))).astype(o_ref.dtype)


def _hgrn_pairwise_chunk(q_ref, f_ref, v_ref, z_ref, lb, norm_ref, cum_ref, o_ref, col_ref, c, st):
    C, SUB = HGRN_CHUNK, HGRN_SUB
    dk = q_ref.shape[1]
    rows = pl.ds(pl.multiple_of(c * C, C), C)
    f = lb + (1.0 - lb) * jax.nn.sigmoid(f_ref[rows, :])
    k = 1.0 - f
    cum = cum_ref[...]
    hi, mid, lo = _split3(jnp.log(f))
    a = (jnp.dot(cum, hi, preferred_element_type=F32) + jnp.dot(cum, mid, preferred_element_type=F32)
         + jnp.dot(cum, lo, preferred_element_type=F32))
    before, pieces = jnp.zeros((1, dk), F32), []
    for i in range(C // SUB):
        pieces.append(a[i * SUB:(i + 1) * SUB] + before)
        before = before + a[(i + 1) * SUB - 1:(i + 1) * SUB]
    b, b_end = jnp.concatenate(pieces, axis=0), before
    q = q_ref[rows, :].astype(F32)
    v16 = v_ref[rows, :].astype(BF16)
    col_ref[0] = b
    col_ref[1] = k
    key_i = lax.broadcasted_iota(jnp.int32, (C, C), 1)
    qry_i = lax.broadcasted_iota(jnp.int32, (C, C), 0)

    def column(s, sc):
        bs = col_ref[0, pl.ds(s, 1), :]
        ks = col_ref[1, pl.ds(s, 1), :]
        terms = jnp.exp(jnp.minimum(b - bs, 0.0)) * (q * ks)
        return jnp.where(key_i == s, jnp.sum(terms, axis=1, keepdims=True), sc)

    sc = lax.fori_loop(0, C, column, jnp.zeros((C, C), F32))
    att = jnp.where(key_i <= qry_i, sc, 0.0).astype(BF16)
    intra = jnp.dot(att, v16, preferred_element_type=F32)
    inter = _nt_dot((q * jnp.exp(b)).astype(BF16), st.astype(BF16))
    incr = _tn_dot(v16, (k * jnp.exp(b_end - b)).astype(BF16))
    _hgrn_head_out(inter + intra, z_ref, norm_ref, o_ref, rows)
    return st * jnp.exp(b_end) + incr


def _hgrn_kernel(q_ref, f_ref, v_ref, z_ref, lb_ref, norm_ref, cum_ref, o_ref, col_ref, factored_ref, *, group):
    seq, dk = q_ref.shape
    head = pl.program_id(1)

    def lower_bound():
        lbr = lb_ref[...]
        e = jnp.exp(lbr - jnp.max(lbr, axis=0, keepdims=True))
        return e[0:1, :] / jnp.sum(e, axis=0, keepdims=True)

    @pl.when(pl.program_id(0) == 0)
    def _():
        ok = jnp.min(jnp.log(lower_bound())) * HGRN_SUB >= -HGRN_MAX_LOG_DECAY
        factored_ref[head] = ok.astype(jnp.int32)

    def block(cb, st):
        rows = [pl.ds(pl.multiple_of((cb * group + j) * HGRN_CHUNK, HGRN_CHUNK), HGRN_CHUNK)
                for j in range(group)]
        gates, cumsums, operands, scores, products, state_pass, finish = _hgrn_stages(
            q_ref, f_ref, v_ref, z_ref, lb_ref, norm_ref, cum_ref, o_ref, rows)
        gates()
        cumsums()
        operands()
        scores()
        products()
        st = state_pass(st)
        finish()
        return st

    @pl.when(factored_ref[head] == 1)
    def _():
        lax.fori_loop(0, seq // (HGRN_CHUNK * group), block, jnp.zeros((dk, dk), F32))

    @pl.when(factored_ref[head] == 0)
    def _():
        chunk = functools.partial(_hgrn_pairwise_chunk, q_ref, f_ref, v_ref, z_ref, lower_bound(), norm_ref,
                                  cum_ref, o_ref, col_ref)
        lax.fori_loop(0, seq // HGRN_CHUNK, chunk, jnp.zeros((dk, dk), F32))


def _hgrn_patterns():
    C, SUB = HGRN_CHUNK, HGRN_SUB
    t = np.arange(C)[:, None]
    s = np.arange(C)[None, :]
    return (((t // SUB) == (s // SUB)) & (s <= t)).astype(np.float32)


def _hgrn(streams, lower_bounds, norm, *, batch, seq):
    dk = HGRN_HEAD_DIM
    heads = norm.shape[1] // dk
    cum = jnp.asarray(_hgrn_patterns(), dtype=BF16)
    col = lambda name: (lambda b, h, o=streams[name][1] // dk: (b, o + h))
    return pl.pallas_call(
        functools.partial(_hgrn_kernel, group=16),
        grid=(batch, heads),
        in_specs=[
            pl.BlockSpec((seq, dk), col("q_h")),
            pl.BlockSpec((seq, dk), col("f_h")),
            pl.BlockSpec((seq, dk), col("i_h")),
            pl.BlockSpec((seq, dk), col("z_h")),
            pl.BlockSpec((lower_bounds.shape[0], dk), lambda b, h: (0, h)),
            pl.BlockSpec((1, dk), lambda b, h: (0, h)),
            pl.BlockSpec(cum.shape, lambda b, h: (0, 0)),
        ],
        out_specs=pl.BlockSpec((seq, dk), lambda b, h: (b, h)),
        out_shape=jax.ShapeDtypeStruct((batch * seq, heads * dk), BF16),
        scratch_shapes=[pltpu.VMEM((2, HGRN_CHUNK, dk), F32), pltpu.SMEM((heads,), jnp.int32)],
        compiler_params=pltpu.CompilerParams(
            dimension_semantics=("arbitrary", "arbitrary"),
            vmem_limit_bytes=VMEM_LIMIT),
        name="hgrn",
    )(streams["q_h"][0], streams["f_h"][0], streams["i_h"][0], streams["z_h"][0], lower_bounds, norm, cum)


def _outproj_kernel(x_ref, oa_ref, oh_ref, wa_ref, wh_ref, g_ref, o_ref):
    y = x_ref[...] + jnp.dot(oa_ref[...], wa_ref[...], preferred_element_type=F32)
    y = y + jnp.dot(oh_ref[...], wh_ref[...], preferred_element_type=F32)
    ms = jnp.mean(y * y, axis=-1, keepdims=True)
    o_ref[...] = y * lax.rsqrt(ms + EPS) * g_ref[...]


def _outproj(x2, o_a, o_h, wa, wh, g, *, tm):
    M, D = x2.shape
    const = lambda a: pl.BlockSpec(a.shape, lambda i: (0,) * a.ndim)
    return pl.pallas_call(
        _outproj_kernel,
        grid=(M // tm,),
        in_specs=[
            pl.BlockSpec((tm, D), lambda i: (i, 0)),
            pl.BlockSpec((tm, o_a.shape[1]), lambda i: (i, 0)),
            pl.BlockSpec((tm, o_h.shape[1]), lambda i: (i, 0)),
            const(wa), const(wh), const(g),
        ],
        out_specs=pl.BlockSpec((tm, D), lambda i: (i, 0)),
        out_shape=jax.ShapeDtypeStruct((M, D), F32),
        compiler_params=pltpu.CompilerParams(
            dimension_semantics=("arbitrary",),
            vmem_limit_bytes=VMEM_LIMIT),
        name="outproj",
    )(x2, o_a, o_h, wa, wh, g)


def kernel(x, norm_in, w_in, cmp_pe_k, cmp_w1_k, cmp_w2_k, cmp_pe_v, cmp_w1_v, cmp_w2_v,
           lower_bounds, nsa_out_norm, hgrn_out_norm, w_out, final_norm):
    B, S, D = x.shape
    assert norm_in.shape[0] == 1, "single-layer problem"
    nsa_w = nsa_out_norm.shape[1]
    hgrn_w = hgrn_out_norm.shape[1]
    dh = NSA_HEAD_DIM
    G = NSA_KV_HEADS
    heads = nsa_w // dh
    kvw = G * dh
    n_gate = 3 * heads
    n_cmp = (S - CMP_BLOCK) // CMP_STRIDE + 1
    n_half = S // CMP_STRIDE
    assert S % NSA_TILE == 0 and WINDOW % NSA_TILE == 0 and n_half <= LANES and 2 * dh == LANES

    names = ["q_a", "k_cmp", "v_cmp", "k_slc", "v_slc", "k_win", "v_win", "gate", "z_a",
             "q_h", "f_h", "i_h", "z_h"]
    widths = [nsa_w] + [kvw] * 6 + [n_gate, nsa_w] + [hgrn_w] * 4
    starts = dict(zip(names, np.cumsum([0] + widths[:-1]).tolist()))
    wd = dict(zip(names, widths))

    paired = lambda kname, vname: [(starts[n] + g * dh, dh) for g in range(G) for n in (kname, vname)]
    whole = lambda n: [(starts[n], wd[n])]
    groups = [[("q_a", whole("q_a")),
               ("cmp", paired("k_cmp", "v_cmp")), ("slc", paired("k_slc", "v_slc")),
               ("win", paired("k_win", "v_win")), ("q_h", whole("q_h")), ("i_h", whole("i_h"))],
              [("z_a", whole("z_a")), ("f_h", whole("f_h")), ("z_h", whole("z_h"))]]
    plan, col_of, off = [], {}, 0
    for group in groups:
        for n, pieces in group:
            col_of[n] = off
            for src, width in pieces:
                plan.append((off, src, width, dh ** -0.5 if n == "q_a" else 1.0))
                off += width
    n_main = off
    n16_cols = col_of["z_a"]
    plan.append((n_main, starts["gate"], n_gate, 1.0))
    w_all = _relayout_weights(w_in[0].T, tuple(plan), n_main + LANES)

    x2 = x.reshape(B * S, D)
    proj16, proj32, gate = _proj(x2, norm_in, w_all, tm=min(1024, B * S), tn=1536,
                                 n_main=n_main, n16_cols=n16_cols)
    streams = {}
    for arr, group, base in ((proj16, groups[0], 0), (proj32, groups[1], n16_cols)):
        for n, _ in group:
            streams[n] = (arr, col_of[n] - base)

    featk, featw, featc, qfeat, qrel, seg, mmap, place = _nsa_tables(S, n_half, n_cmp, heads)

    def w1_halves(w1k, w1v):
        hk = w1k.shape[1]
        k3 = w1k.reshape(2, CMP_STRIDE, dh, hk)
        v3 = w1v.reshape(2, CMP_STRIDE, dh, hk)
        zk = jnp.zeros_like(k3[0])
        top = lambda a: jnp.concatenate([a, zk], axis=-1)
        bot = lambda a: jnp.concatenate([zk, a], axis=-1)
        half = lambda i: jnp.concatenate([top(k3[i]), bot(v3[i])], axis=1).reshape(CMP_STRIDE * 2 * dh, 2 * hk)
        return half(0).astype(BF16), half(1).astype(BF16)

    wp, wq = w1_halves(cmp_w1_k[0], cmp_w1_v[0])
    zk = jnp.zeros_like(cmp_w2_k[0])
    w2 = jnp.concatenate([jnp.concatenate([cmp_w2_k[0], zk], axis=1),
                          jnp.concatenate([zk, cmp_w2_v[0]], axis=1)], axis=0).astype(BF16)
    pe = jnp.concatenate([cmp_pe_k[0].reshape(2, CMP_STRIDE, dh), cmp_pe_v[0].reshape(2, CMP_STRIDE, dh)],
                         axis=-1).reshape(2, CMP_STRIDE * 2 * dh)
    kc, vc = _compress(streams["cmp"][0], pe, wp, wq, w2, jnp.tile(featc, (G, 1)),
                       batch=B, seq=S, col0=streams["cmp"][1], n_half=n_half)

    o_a = _nsa(streams, gate, kc, vc, featk, featw, qfeat, qrel, seg, mmap, place, nsa_out_norm,
               batch=B, seq=S, n_cmp=n_cmp)
    o_h = _hgrn(streams, lower_bounds, hgrn_out_norm, batch=B, seq=S)

    wo = w_out[0].astype(BF16)
    out = _outproj(x2, o_a, o_h, wo[:nsa_w], wo[nsa_w:], final_norm.reshape(1, D), tm=512)
    return out.reshape(B, S, D)
```

```python
import functools

import ml_dtypes
import numpy as np
import jax
import jax.numpy as jnp
from jax import lax
from jax.experimental import pallas as pl
from jax.experimental.pallas import tpu as pltpu

F32 = jnp.float32
BF16 = jnp.bfloat16

EPS = 1e-6
NEG_INF = -1e30
LOG2E = 1.4426950408889634

NSA_HEAD_DIM = 64
NSA_KV_HEADS = 4
CMP_BLOCK = 32
CMP_STRIDE = 16
SEL_BLOCK = 64
SEL_TOP = 8
SEL_BONUS = 1.0e4
WINDOW = 512
HGRN_HEAD_DIM = 128
HGRN_CHUNK = 64
HGRN_SUB = 16
HGRN_MAX_LOG_DECAY = 60.0

LANES = 128
VMEM_LIMIT = 56 * 1024 * 1024
NSA_TILE = 256
NSA_TILES_PER_STEP = 8
SEL_RECENT = 3

SEL_LANE0 = NSA_HEAD_DIM
MAX_SEL_BLOCKS = 32
POS_LANE0 = SEL_LANE0 + MAX_SEL_BLOCKS
N_SPLIT = 4
FLAG_LANE = POS_LANE0 + 2 * N_SPLIT
SHIFT_LANE0 = FLAG_LANE + 1
REL_LANE0 = SHIFT_LANE0 + N_SPLIT
SCORE_BOUND = 100.0
SAFETY = 1.02


def _nt_dot(a, b):
    return lax.dot_general(a, b, (((1,), (1,)), ((), ())), preferred_element_type=F32)


def _tn_dot(a, b):
    return lax.dot_general(a, b, (((0,), (0,)), ((), ())), preferred_element_type=F32)


def _split3(x):
    hi = x.astype(BF16)
    r1 = x - hi.astype(F32)
    mid = r1.astype(BF16)
    lo = (r1 - mid.astype(F32)).astype(BF16)
    return hi, mid, lo


def _proj_kernel(x_ref, g_ref, w_ref, wg_ref, o16_ref, o32_ref, og_ref, h_ref, *, row_chunk, n16):
    j = pl.program_id(1)

    @pl.when(j == 0)
    def _():
        n_chunks = x_ref.shape[0] // row_chunk

        def body(c, carry):
            rows = pl.ds(pl.multiple_of(c * row_chunk, row_chunk), row_chunk)
            x = x_ref[rows, :]
            ms = jnp.mean(x * x, axis=-1, keepdims=True)
            h_ref[rows, :] = (x * lax.rsqrt(ms + EPS) * g_ref[...]).astype(BF16)
            return carry

        lax.fori_loop(0, n_chunks, body, 0)
        og_ref[...] = jnp.dot(h_ref[...], wg_ref[...], preferred_element_type=F32)

    @pl.when(j < n16)
    def _():
        o16_ref[...] = jnp.dot(h_ref[...], w_ref[...], preferred_element_type=F32).astype(BF16)

    @pl.when(j >= n16)
    def _():
        o32_ref[...] = jnp.dot(h_ref[...], w_ref[...], preferred_element_type=F32)


def _relayout_kernel(wt_ref, o_ref, *, plan):
    o_ref[:, o_ref.shape[1] - LANES:] = jnp.zeros((o_ref.shape[0], LANES), o_ref.dtype)
    for dst, src, width, scale in plan:
        o_ref[:, dst:dst + width] = (wt_ref[src:src + width, :].T * scale).astype(o_ref.dtype)


def _relayout_weights(wt, plan, n_out, *, rows=256):
    n_in, D = wt.shape
    return pl.pallas_call(
        functools.partial(_relayout_kernel, plan=plan),
        grid=(D // rows,),
        in_specs=[pl.BlockSpec((n_in, rows), lambda i: (0, i))],
        out_specs=pl.BlockSpec((rows, n_out), lambda i: (i, 0)),
        out_shape=jax.ShapeDtypeStruct((D, n_out), BF16),
        compiler_params=pltpu.CompilerParams(
            dimension_semantics=("arbitrary",),
            vmem_limit_bytes=VMEM_LIMIT),
        name="relayout",
    )(wt)


def _proj(x2, g, w, *, tm, tn, n_main, n16_cols):
    M, D = x2.shape
    N = n_main
    NG = w.shape[1] - n_main
    n16 = n16_cols // tn
    assert n16 * tn == n16_cols and N % tn == 0 and N % NG == 0
    return pl.pallas_call(
        functools.partial(_proj_kernel, row_chunk=128, n16=n16),
        grid=(M // tm, N // tn),
        in_specs=[
            pl.BlockSpec((tm, D), lambda i, j: (i, 0)),
            pl.BlockSpec((1, D), lambda i, j: (0, 0)),
            pl.BlockSpec((D, tn), lambda i, j: (0, j)),
            pl.BlockSpec((D, NG), lambda i, j: (0, N // NG)),
        ],
        out_specs=[
            pl.BlockSpec((tm, tn), lambda i, j: (i, jnp.minimum(j, n16 - 1))),
            pl.BlockSpec((tm, tn), lambda i, j: (i, jnp.maximum(j - n16, 0))),
            pl.BlockSpec((tm, NG), lambda i, j: (i, 0)),
        ],
        out_shape=[
            jax.ShapeDtypeStruct((M, n16_cols), BF16),
            jax.ShapeDtypeStruct((M, N - n16_cols), F32),
            jax.ShapeDtypeStruct((M, NG), F32),
        ],
        scratch_shapes=[pltpu.VMEM((tm, D), BF16)],
        compiler_params=pltpu.CompilerParams(
            dimension_semantics=("arbitrary", "arbitrary"),
            vmem_limit_bytes=VMEM_LIMIT),
        name="proj",
    )(x2, g, w, w)


def _compress_kernel(c0_ref, c1_ref, c2_ref, c3_ref, pe_ref, wp_ref, wq_ref, w2_ref, feat_ref,
                     kc_ref, vc_ref, x_ref, c32_ref, *, n_half):
    for g, c_ref in enumerate((c0_ref, c1_ref, c2_ref, c3_ref)):
        c32_ref[...] = c_ref[...].astype(F32)
        for l in range(CMP_STRIDE):
            x_ref[g * n_half:(g + 1) * n_half, l * LANES:(l + 1) * LANES] = (
                c32_ref[pl.ds(l, n_half, stride=CMP_STRIDE), :])
    x = x_ref[...]
    rows = x.shape[0]
    first = jnp.dot((x + pe_ref[0:1, :]).astype(BF16), wp_ref[...], preferred_element_type=F32)
    second = jnp.dot((x + pe_ref[1:2, :]).astype(BF16), wq_ref[...], preferred_element_type=F32)
    hidden = first + pltpu.roll(second, rows - 1, 0)
    out = jnp.dot(jax.nn.gelu(hidden).astype(BF16), w2_ref[...], preferred_element_type=F32)
    lane = lax.broadcasted_iota(jnp.int32, out.shape, 1)
    dh = NSA_HEAD_DIM
    kc_ref[...] = jnp.where(lane < dh, out.astype(BF16), feat_ref[...])
    for g in range(NSA_KV_HEADS):
        vc_ref[g * dh:(g + 1) * dh, :] = out[g * n_half:(g + 1) * n_half, :].T[dh:, :].astype(BF16)


def _compress(proj, pe, wp, wq, w2, feat, *, batch, seq, col0, n_half):
    rows = NSA_KV_HEADS * n_half
    rows_t = NSA_KV_HEADS * NSA_HEAD_DIM
    const = lambda a: pl.BlockSpec(a.shape, lambda b: (0,) * a.ndim)
    stream = lambda g: pl.BlockSpec((seq, LANES), lambda b, o=col0 // LANES + g: (b, o))
    return pl.pallas_call(
        functools.partial(_compress_kernel, n_half=n_half),
        grid=(batch,),
        in_specs=[stream(0), stream(1), stream(2), stream(3),
                  const(pe), const(wp), const(wq), const(w2), const(feat)],
        out_specs=[
            pl.BlockSpec((rows, LANES), lambda b: (b, 0)),
            pl.BlockSpec((rows_t, n_half), lambda b: (b, 0)),
        ],
        out_shape=[
            jax.ShapeDtypeStruct((batch * rows, LANES), BF16),
            jax.ShapeDtypeStruct((batch * rows_t, n_half), BF16),
        ],
        scratch_shapes=[pltpu.VMEM((rows, CMP_STRIDE * LANES), F32), pltpu.VMEM((seq, LANES), F32)],
        compiler_params=pltpu.CompilerParams(
            dimension_semantics=("arbitrary",),
            vmem_limit_bytes=VMEM_LIMIT),
        name="compress",
    )(proj, proj, proj, proj, pe, wp, wq, w2, feat)


def _nsa_kernel(q_ref, qall_ref, z_ref, gate_ref, slc_ref, win_ref, kc_ref, vct_ref, featk_ref, featw_ref,
                qfeat_ref, qrel_ref, seg_ref, mmap_ref, place_ref, norm_ref, o_ref,
                ksel_ref, vselt_ref, kwin_ref, vwint_ref, gt_ref, bounded_ref, *, n_cmp):
    g = pl.program_id(1)
    qi = pl.program_id(2)
    tq = NSA_TILE
    per_step = q_ref.shape[0] // tq
    tk = tq
    seq = slc_ref.shape[0]
    dh = NSA_HEAD_DIM
    rep = q_ref.shape[1] // dh
    n_half = kc_ref.shape[0]
    n_pad = WINDOW // tk

    @pl.when(qi == 0)
    def _():
        lane = lax.broadcasted_iota(jnp.int32, (seq, LANES), 1)

        def max_sq_norm(x, width):
            s = jnp.dot((x * x).astype(BF16), seg_ref[0:width, :], preferred_element_type=F32)
            return jnp.max(s) * SAFETY

        kv = slc_ref[...]
        k2_sel = max_sq_norm(jnp.where(lane < dh, kv.astype(F32), 0.0), LANES)
        ksel_ref[...] = jnp.where(lane < dh, kv, featk_ref[...])
        vt = kv.astype(F32).T[dh:, :].astype(BF16)
        for kt in range(seq // tk):
            vselt_ref[kt] = vt[:, kt * tk:(kt + 1) * tk]
        kv = win_ref[...]
        k2_win = max_sq_norm(jnp.where(lane < dh, kv.astype(F32), 0.0), LANES)
        kwin_ref[0:WINDOW, :] = featw_ref[0:WINDOW, :]
        kwin_ref[WINDOW:, :] = jnp.where(lane < dh, kv, featw_ref[WINDOW:, :])
        vt = kv.astype(F32).T[dh:, :].astype(BF16)
        for kt in range(n_pad):
            vwint_ref[kt] = jnp.zeros((dh, tk), BF16)
        for kt in range(seq // tk):
            vwint_ref[n_pad + kt] = vt[:, kt * tk:(kt + 1) * tk]
        q2 = max_sq_norm(qall_ref[...].astype(F32), rep * dh) * (LOG2E * LOG2E)
        limit = SCORE_BOUND * SCORE_BOUND
        bounded_ref[0] = ((q2 * k2_sel <= limit) & (q2 * k2_win <= limit)).astype(jnp.int32)

    heads = lambda a: jnp.concatenate([a] * rep, axis=1)
    sub8 = lambda a: a.reshape(a.shape[0] // 8, 8, a.shape[1])

    def softmax_pv(lgs, v_tiles, bounded, extra=None):
        if bounded:
            m = 0.0
        else:
            m8 = functools.reduce(jnp.maximum, [jnp.max(sub8(lg), axis=0) for lg in lgs])
            m = jnp.max(m8, axis=0, keepdims=True)
        l8 = jnp.zeros((8, lgs[0].shape[1]), F32)
        acc = jnp.zeros((dh, lgs[0].shape[1]), F32)
        for lg, vt in zip(lgs, v_tiles):
            pt = jnp.exp2(lg) if bounded else jnp.exp2(lg - m)
            l8 = l8 + jnp.sum(sub8(pt), axis=0)
            acc = acc + jnp.dot(vt, pt.astype(BF16), preferred_element_type=F32)
        if extra is not None:
            l8, acc = extra(l8, acc)
        return acc / jnp.sum(l8, axis=0, keepdims=True)

    def step(n, bounded, slot):
        static = isinstance(n, int)
        assert static == bounded
        own = (slice(slot * tq, (slot + 1) * tq) if static
               else pl.ds(pl.multiple_of(slot * tq, tq), tq))
        t0 = n * tq
        lane = lax.broadcasted_iota(jnp.int32, (tq, LANES), 1)
        key_i = lax.broadcasted_iota(jnp.int32, (tk, tq), 0)
        qry_i = lax.broadcasted_iota(jnp.int32, (tk, tq), 1)
        causal_1 = jnp.where(key_i <= qry_i, 0.0, NEG_INF)
        causal_bias = heads(causal_1)
        band_bias = heads(jnp.where(key_i > qry_i, 0.0, NEG_INF))

        qfeat = qfeat_ref[0, n]
        rel_lanes = (lane >= REL_LANE0) & (lane < REL_LANE0 + 3)
        qa = []
        for r in range(rep):
            qcol = q_ref[own, (r // 2) * LANES:(r // 2 + 1) * LANES].astype(F32)
            if r % 2:
                qcol = pltpu.roll(qcol, dh, 1)
            feat = jnp.where(rel_lanes, qrel_ref[0, r * tq:(r + 1) * tq, :].astype(F32), qfeat[r:r + 1, :])
            qa.append(jnp.where(lane < dh, qcol * LOG2E, feat))
        q0 = jnp.concatenate(qa, axis=0).astype(BF16)

        gt_ref[slot] = jax.nn.sigmoid(gate_ref[own, :]).T
        out_scale = []
        for pair in range(rep // 2):
            z = z_ref[own, pair * LANES:(pair + 1) * LANES]
            out_scale.append(norm_ref[:, pair * LANES:(pair + 1) * LANES] * (z * jax.nn.sigmoid(z)))

        nc = min(n_half, (t0 + tq) // CMP_STRIDE) if static else n_half
        n_idx = lax.broadcasted_iota(jnp.int32, (nc, tq), 0)
        t_idx = t0 + lax.broadcasted_iota(jnp.int32, (nc, tq), 1)
        valid_c = heads((CMP_STRIDE * n_idx + (CMP_BLOCK - 1) <= t_idx) & (n_idx < n_cmp))
        lg_c = jnp.where(valid_c, _nt_dot(kc_ref[0:nc, :], q0), NEG_INF)

        n_wt = n_pad + 1
        win_tiles = [w for w in range(n_wt) if not (static and n + w < n_pad)]
        lgs = []
        for w in win_tiles:
            rows = (slice(t0 + w * tk, t0 + (w + 1) * tk) if static
                    else pl.ds(pl.multiple_of(t0 + w * tk, tk), tk))
            lg = _nt_dot(kwin_ref[rows, :], q0)
            if w == 0:
                lg = lg + band_bias
            if w == n_wt - 1:
                lg = lg + causal_bias
            lgs.append(lg)
        o_win = softmax_pv(lgs, [vwint_ref[n + w] for w in win_tiles], bounded)

        e = jnp.exp2(lg_c - jnp.max(lg_c, axis=0, keepdims=True))
        p = jnp.where(valid_c, e / jnp.sum(e, axis=0, keepdims=True), 0.0)
        if nc < n_half:
            p = jnp.concatenate([p, jnp.zeros((n_half - nc, rep * tq), F32)], axis=0)
        o_cmp = jnp.dot(vct_ref[...], p.astype(BF16), preferred_element_type=F32)
        p_sum = p[:, 0:tq]
        for r in range(1, rep):
            p_sum = p_sum + p[:, r * tq:(r + 1) * tq]

        nb = min(MAX_SEL_BLOCKS, (t0 + tq) // SEL_BLOCK) if static else MAX_SEL_BLOCKS
        mm = mmap_ref[...]
        hi, mid, lo = _split3(p_sum)
        p_slc = (jnp.dot(mm, hi, preferred_element_type=F32) + jnp.dot(mm, mid, preferred_element_type=F32)
                 + jnp.dot(mm, lo, preferred_element_type=F32))[0:MAX_SEL_BLOCKS]
        if static:
            sel_scores = lambda kt: _nt_dot(ksel_ref[kt * tk:(kt + 1) * tk, :], q0)
            maybe_idle = [kt for kt in range(1, n + 1 - SEL_RECENT)]
            raw = {kt: sel_scores(kt) for kt in range(n + 1) if kt not in maybe_idle}
        jj = lax.broadcasted_iota(jnp.int32, (MAX_SEL_BLOCKS, tq), 0)
        tt = t0 + lax.broadcasted_iota(jnp.int32, (MAX_SEL_BLOCKS, tq), 1)
        cur = lax.shift_right_logical(tt, int(np.log2(SEL_BLOCK)))
        forced = (jj == 0) | (jj == cur) | (jj == cur - 1)
        future = jj > cur
        score = jnp.where(future, -1.0, p_slc + jnp.where(forced, SEL_BONUS, 0.0))
        rank = jnp.zeros((MAX_SEL_BLOCKS, tq), jnp.int32)
        for i in range(nb):
            other = jnp.broadcast_to(score[i:i + 1, :], score.shape)
            beats = (other > score) | ((other == score) & (jj > i))
            rank = rank + beats.astype(jnp.int32)
        keep = (rank < SEL_TOP) & jnp.logical_not(future)

        if static:
            blocks_per_tile = tk // SEL_BLOCK
            block_bias = jnp.where(keep, 0.0, NEG_INF)

            def tile_bias(kt):
                b = jnp.concatenate(
                    [jnp.broadcast_to(block_bias[blocks_per_tile * kt + j:blocks_per_tile * kt + j + 1, :],
                                      (SEL_BLOCK, tq)) for j in range(blocks_per_tile)], axis=0)
                return heads(b + causal_1 if kt == n else b)

            wanted = [jnp.max(block_bias[blocks_per_tile * kt:blocks_per_tile * (kt + 1)]) == 0.0
                      for kt in maybe_idle]

            def idle_tiles(l8, acc):
                for kt, some in zip(maybe_idle, wanted):
                    def live(l8, acc, kt=kt):
                        pt = jnp.exp2(sel_scores(kt) + tile_bias(kt))
                        return (l8 + jnp.sum(sub8(pt), axis=0),
                                acc + jnp.dot(vselt_ref[kt], pt.astype(BF16), preferred_element_type=F32))

                    l8, acc = lax.cond(some, live, lambda l8, acc: (l8, acc), l8, acc)
                return l8, acc

            o_slc = softmax_pv([raw[kt] + tile_bias(kt) for kt in raw], [vselt_ref[kt] for kt in raw],
                               bounded, extra=idle_tiles if maybe_idle else None)
        else:
            chosen = jnp.where(keep, 1.0, 0.0).astype(BF16)
            placed = _tn_dot(chosen, place_ref[...])
            sel_lanes = (lane >= SEL_LANE0) & (lane < SEL_LANE0 + MAX_SEL_BLOCKS)
            sel_bias = (placed - 1.0) * (-NEG_INF)
            qs = jnp.concatenate([jnp.where(sel_lanes, sel_bias, a) for a in qa], axis=0).astype(BF16)

            def sel_tile(kt, carry, bias=None):
                m, l, acc = carry
                lg = _nt_dot(ksel_ref[pl.ds(pl.multiple_of(kt * tk, tk), tk), :], qs)
                if bias is not None:
                    lg = lg + bias
                m_new = jnp.maximum(m, jnp.max(lg, axis=0, keepdims=True))
                alpha = jnp.exp2(m - m_new)
                pt = jnp.exp2(lg - m_new)
                return (m_new, alpha * l + jnp.sum(pt, axis=0, keepdims=True),
                        alpha * acc + jnp.dot(vselt_ref[kt], pt.astype(BF16), preferred_element_type=F32))

            init = (jnp.full((1, rep * tq), NEG_INF, F32), jnp.zeros((1, rep * tq), F32),
                    jnp.zeros((dh, rep * tq), F32))
            _, l, acc = sel_tile(n, lax.fori_loop(0, n, sel_tile, init), causal_bias)
            o_slc = acc / l

        ys = []
        for r in range(rep):
            cols = slice(r * tq, (r + 1) * tq)
            gate = lambda c: gt_ref[slot, pl.ds(3 * (g * rep + r) + c, 1), :]
            o = gate(0) * o_cmp[:, cols] + gate(1) * o_slc[:, cols] + gate(2) * o_win[:, cols]
            ys.append(o * lax.rsqrt(jnp.mean(o * o, axis=0, keepdims=True) + EPS))
        for pair in range(rep // 2):
            cols = slice(pair * LANES, (pair + 1) * LANES)
            y = jnp.concatenate([ys[2 * pair], ys[2 * pair + 1]], axis=0).T
            o_ref[own, cols] = (y * out_scale[pair]).astype(o_ref.dtype)

    def tiles(first):
        for slot in range(per_step):
            step(first + slot, True, slot)

    @pl.when(bounded_ref[0] == 1)
    def _():
        lax.switch(qi, [functools.partial(tiles, first) for first in range(0, seq // tq, per_step)])

    @pl.when(bounded_ref[0] == 0)
    def _():
        @pl.loop(0, per_step)
        def _(slot):
            step(qi * per_step + slot, False, slot)


def _nsa(streams, gate, kc, vc, featk, featw, qfeat, qrel, seg, mmap, place, norm, *, batch, seq, n_cmp):
    tq = NSA_TILE
    rows = NSA_TILES_PER_STEP * tq
    nq = seq // rows
    G = NSA_KV_HEADS
    gw = norm.shape[1] // G
    n_half = kc.shape[0] // (batch * G)
    const = lambda a: pl.BlockSpec(a.shape, lambda b, g, i: (0,) * a.ndim)
    stream = lambda name: pl.BlockSpec((seq, LANES),
                                       lambda b, g, i, o=streams[name][1] // LANES: (b, o + g))
    tile = lambda name: pl.BlockSpec((rows, gw), lambda b, g, i, o=streams[name][1] // gw: (b * nq + i, o + g))
    return pl.pallas_call(
        functools.partial(_nsa_kernel, n_cmp=n_cmp),
        grid=(batch, G, nq),
        in_specs=[
            tile("q_a"),
            pl.BlockSpec((seq, gw), lambda b, g, i, o=streams["q_a"][1] // gw: (b, o + g)),
            tile("z_a"),
            pl.BlockSpec((rows, gate.shape[1]), lambda b, g, i: (b * nq + i, 0)),
            stream("slc"), stream("win"),
            pl.BlockSpec((n_half, LANES), lambda b, g, i: (b * G + g, 0)),
            pl.BlockSpec((NSA_HEAD_DIM, n_half), lambda b, g, i: (b * G + g, 0)),
            const(featk), const(featw),
            pl.BlockSpec((1,) + qfeat.shape[1:], lambda b, g, i: (g, 0, 0, 0)),
            pl.BlockSpec((1,) + qrel.shape[1:], lambda b, g, i: (g, 0, 0)),
            const(seg), const(mmap), const(place),
            pl.BlockSpec((1, gw), lambda b, g, i: (0, g)),
        ],
        out_specs=pl.BlockSpec((rows, gw), lambda b, g, i: (b * nq + i, g)),
        out_shape=jax.ShapeDtypeStruct((batch * seq, G * gw), BF16),
        scratch_shapes=[
            pltpu.VMEM((seq, LANES), BF16),
            pltpu.VMEM((seq // tq, NSA_HEAD_DIM, tq), BF16),
            pltpu.VMEM((seq + WINDOW, LANES), BF16),
            pltpu.VMEM(((seq + WINDOW) // tq, NSA_HEAD_DIM, tq), BF16),
            pltpu.VMEM((NSA_TILES_PER_STEP, gate.shape[1], tq), F32),
            pltpu.SMEM((1,), jnp.int32),
        ],
        compiler_params=pltpu.CompilerParams(
            dimension_semantics=("arbitrary", "arbitrary", "arbitrary"),
            vmem_limit_bytes=VMEM_LIMIT),
        name="nsa",
    )(streams["q_a"][0], streams["q_a"][0], streams["z_a"][0], gate, streams["slc"][0], streams["win"][0],
      kc, vc, featk, featw, qfeat, qrel, seg, mmap, place, norm)


def _bf16_terms(x, n):
    terms, rest = [], np.asarray(x, np.float64)
    for _ in range(n):
        t = rest.astype(np.float32).astype(ml_dtypes.bfloat16).astype(np.float64)
        terms.append(t.astype(np.float32))
        rest = rest - t
    return terms


def _nsa_tables(seq, n_half, n_cmp, heads):
    assert seq // SEL_BLOCK <= MAX_SEL_BLOCKS and FLAG_LANE < LANES

    def key_features(pos, onehot_blocks):
        f = np.zeros((len(pos), LANES), np.float32)
        if onehot_blocks:
            f[np.arange(len(pos)), SEL_LANE0 + pos // SEL_BLOCK] = 1.0
        f[:, POS_LANE0:POS_LANE0 + N_SPLIT] = ((pos // 64) * 64)[:, None]
        f[:, POS_LANE0 + N_SPLIT:POS_LANE0 + 2 * N_SPLIT] = (pos % 64)[:, None]
        return f

    assert REL_LANE0 + 3 <= LANES
    featk = key_features(np.arange(seq), True)
    featk[:, SHIFT_LANE0:REL_LANE0 + 3] = 1.0
    featw = np.concatenate([np.zeros((WINDOW, LANES), np.float32), key_features(np.arange(seq), False)])
    featw[:WINDOW, FLAG_LANE] = NEG_INF
    featw[WINDOW:, SHIFT_LANE0:REL_LANE0 + 3] = 1.0
    featc = key_features(CMP_STRIDE * np.arange(n_half) + CMP_BLOCK - 1, False)
    featc[n_cmp:, FLAG_LANE] = NEG_INF

    slopes = (2.0 ** (-8.0 * np.arange(1, heads + 1) / heads)).astype(np.float32).astype(np.float64)
    slopes2 = slopes * LOG2E
    terms = _bf16_terms(slopes2, N_SPLIT)
    rep = heads // NSA_KV_HEADS
    nq = seq // NSA_TILE
    qfeat = np.zeros((NSA_KV_HEADS, nq, 8, LANES), np.float32)
    qrel = np.zeros((NSA_KV_HEADS, rep * NSA_TILE, LANES), np.float32)
    for h in range(heads):
        g, r = divmod(h, rep)
        for i, t in enumerate(terms):
            qfeat[g, :, r, POS_LANE0 + i] = t[h]
            qfeat[g, :, r, POS_LANE0 + N_SPLIT + i] = t[h]
        qfeat[g, :, r, FLAG_LANE] = 1.0
        for i, t in enumerate(_bf16_terms(-slopes2[h] * NSA_TILE * np.arange(nq), N_SPLIT)):
            qfeat[g, :, r, SHIFT_LANE0 + i] = t
        for i, t in enumerate(_bf16_terms(-slopes2[h] * np.arange(NSA_TILE), 3)):
            qrel[g, r * NSA_TILE:(r + 1) * NSA_TILE, REL_LANE0 + i] = t
    seg = np.zeros((2 * LANES, LANES), np.float32)
    seg[np.arange(2 * LANES), np.arange(2 * LANES) // NSA_HEAD_DIM] = 1.0

    cs = CMP_STRIDE * np.arange(n_half)[None, :]
    ss = SEL_BLOCK * np.arange(LANES)[:, None]
    overlap = np.clip(np.minimum(cs + CMP_BLOCK, ss + SEL_BLOCK) - np.maximum(cs, ss), 0, None)
    mmap = (overlap / CMP_BLOCK) * (np.arange(n_half)[None, :] < n_cmp) * (ss < seq)
    place = np.zeros((MAX_SEL_BLOCKS, LANES), np.float32)
    place[np.arange(MAX_SEL_BLOCKS), SEL_LANE0 + np.arange(MAX_SEL_BLOCKS)] = 1.0
    bf = lambda a: jnp.asarray(a, dtype=BF16)
    return bf(featk), bf(featw), bf(featc), jnp.asarray(qfeat), bf(qrel), bf(seg), bf(mmap), bf(place)


def _hgrn_stages(q_ref, f_ref, v_ref, z_ref, lb_ref, norm_ref, cum_ref, o_ref, rows):
    C, SUB = HGRN_CHUNK, HGRN_SUB
    n_sub = C // SUB
    chunks = range(len(rows))
    sub = lambda x, i: x[i * SUB:(i + 1) * SUB]
    w = {}

    def gates():
        lbr = lb_ref[...]
        e = jnp.exp(lbr - jnp.max(lbr, axis=0, keepdims=True))
        lb = e[0:1, :] / jnp.sum(e, axis=0, keepdims=True)
        w["v16"] = [v_ref[rows[j], :].astype(BF16) for j in chunks]
        f = [lb + (1.0 - lb) * jax.nn.sigmoid(f_ref[rows[j], :]) for j in chunks]
        w["k"] = [1.0 - f[j] for j in chunks]
        w["parts"] = [_split3(jnp.log(f[j])) for j in chunks]

    def cumsums():
        cum = cum_ref[...]
        parts = w["parts"]
        w["a"] = [(jnp.dot(cum, parts[j][0], preferred_element_type=F32)
                   + jnp.dot(cum, parts[j][1], preferred_element_type=F32)
                   + jnp.dot(cum, parts[j][2], preferred_element_type=F32)) for j in chunks]

    def operands():
        a, k = w["a"], w["k"]
        q1, k1, k2, qb, k3, start, dec = [], [], [], [], [], [], []
        for j in chunks:
            q1.append(q_ref[rows[j], :] * jnp.exp(a[j]))
            k1.append(k[j] * jnp.exp(-a[j]))
            tot = [a[j][(i + 1) * SUB - 1:(i + 1) * SUB] for i in range(n_sub)]
            s = [jnp.zeros_like(tot[0])]
            for i in range(n_sub):
                s.append(s[i] + tot[i])
            start.append(s)
            dec.append(jnp.exp(s[n_sub]))
            qb.append(jnp.concatenate([sub(q1[j], i) * jnp.exp(s[i]) for i in range(n_sub)], axis=0)
                      .astype(BF16))
            k2.append([sub(k1[j], i) * jnp.exp(tot[i]) for i in range(n_sub)])
            k3.append(jnp.concatenate([sub(k1[j], i) * jnp.exp(s[n_sub] - s[i]) for i in range(n_sub)],
                                      axis=0).astype(BF16))
        w.update(q1=q1, k1=k1, k2=k2, qb=qb, k3=k3, start=start, dec=dec)

    def scores():
        q1, k1, k2, start = w["q1"], w["k1"], w["k2"], w["start"]
        att = []
        for j in chunks:
            row_blocks = []
            for i in range(n_sub):
                rhs = [k2[j][jb] * jnp.exp(start[j][i] - start[j][jb + 1]) if jb < i - 1 else k2[j][jb]
                       for jb in range(i)]
                rhs.append(sub(k1[j], i))
                rhs = jnp.concatenate(rhs, axis=0) if len(rhs) > 1 else rhs[0]
                sc = _nt_dot(sub(q1[j], i).astype(BF16), rhs.astype(BF16))
                width = (i + 1) * SUB
                causal = (lax.broadcasted_iota(jnp.int32, (SUB, width), 1)
                          <= lax.broadcasted_iota(jnp.int32, (SUB, width), 0) + i * SUB)
                row_blocks.append(jnp.where(causal, sc, 0.0).astype(BF16))
            att.append(row_blocks)
        w["att"] = att

    def products():
        att, v16, k3 = w["att"], w["v16"], w["k3"]
        w["intra"] = [jnp.concatenate([jnp.dot(att[j][i], v16[j][0:(i + 1) * SUB],
                                               preferred_element_type=F32)
                                       for i in range(n_sub)], axis=0) for j in chunks]
        w["incr"] = [_tn_dot(v16[j], k3[j]) for j in chunks]

    def state_pass(st):
        inter = []
        for j in chunks:
            inter.append(_nt_dot(w["qb"][j], st.astype(BF16)))
            st = st * w["dec"][j] + w["incr"][j]
        w["inter"] = inter
        return st

    def finish():
        for j in chunks:
            _hgrn_head_out(w["inter"][j] + w["intra"][j], z_ref, norm_ref, o_ref, rows[j])

    return gates, cumsums, operands, scores, products, state_pass, finish


def _hgrn_head_out(o, z_ref, norm_ref, o_ref, rows):
    y = o * lax.rsqrt(jnp.mean(o * o, axis=-1, keepdims=True) + EPS) * norm_ref[...]
    z = z_ref[rows, :]
    o_ref[rows, :] = (y * (z * jax.nn.sigmoid(z))).astype(o_ref.dtype)


def _hgrn_pairwise_chunk(q_ref, f_ref, v_ref, z_ref, lb, norm_ref, cum_ref, o_ref, col_ref, c, st):
    C, SUB = HGRN_CHUNK, HGRN_SUB
    dk = q_ref.shape[1]
    rows = pl.ds(pl.multiple_of(c * C, C), C)
    f = lb + (1.0 - lb) * jax.nn.sigmoid(f_ref[rows, :])
    k = 1.0 - f
    cum = cum_ref[...]
    hi, mid, lo = _split3(jnp.log(f))
    a = (jnp.dot(cum, hi, preferred_element_type=F32) + jnp.dot(cum, mid, preferred_element_type=F32)
         + jnp.dot(cum, lo, preferred_element_type=F32))
    before, pieces = jnp.zeros((1, dk), F32), []
    for i in range(C // SUB):
        pieces.append(a[i * SUB:(i + 1) * SUB] + before)
        before = before + a[(i + 1) * SUB - 1:(i + 1) * SUB]
    b, b_end = jnp.concatenate(pieces, axis=0), before
    q = q_ref[rows, :].astype(F32)
    v16 = v_ref[rows, :].astype(BF16)
    col_ref[0] = b
    col_ref[1] = k
    key_i = lax.broadcasted_iota(jnp.int32, (C, C), 1)
    qry_i = lax.broadcasted_iota(jnp.int32, (C, C), 0)

    def column(s, sc):
        bs = col_ref[0, pl.ds(s, 1), :]
        ks = col_ref[1, pl.ds(s, 1), :]
        terms = jnp.exp(jnp.minimum(b - bs, 0.0)) * (q * ks)
        return jnp.where(key_i == s, jnp.sum(terms, axis=1, keepdims=True), sc)

    sc = lax.fori_loop(0, C, column, jnp.zeros((C, C), F32))
    att = jnp.where(key_i <= qry_i, sc, 0.0).astype(BF16)
    intra = jnp.dot(att, v16, preferred_element_type=F32)
    inter = _nt_dot((q * jnp.exp(b)).astype(BF16), st.astype(BF16))
    incr = _tn_dot(v16, (k * jnp.exp(b_end - b)).astype(BF16))
    _hgrn_head_out(inter + intra, z_ref, norm_ref, o_ref, rows)
    return st * jnp.exp(b_end) + incr


def _hgrn_kernel(q_ref, f_ref, v_ref, z_ref, lb_ref, norm_ref, cum_ref, o_ref, col_ref, factored_ref, *, group):
    seq, dk = q_ref.shape
    head = pl.program_id(1)

    def lower_bound():
        lbr = lb_ref[...]
        e = jnp.exp(lbr - jnp.max(lbr, axis=0, keepdims=True))
        return e[0:1, :] / jnp.sum(e, axis=0, keepdims=True)

    @pl.when(pl.program_id(0) == 0)
    def _():
        ok = jnp.min(jnp.log(lower_bound())) * HGRN_SUB >= -HGRN_MAX_LOG_DECAY
        factored_ref[head] = ok.astype(jnp.int32)

    def block(cb, st):
        rows = [pl.ds(pl.multiple_of((cb * group + j) * HGRN_CHUNK, HGRN_CHUNK), HGRN_CHUNK)
                for j in range(group)]
        gates, cumsums, operands, scores, products, state_pass, finish = _hgrn_stages(
            q_ref, f_ref, v_ref, z_ref, lb_ref, norm_ref, cum_ref, o_ref, rows)
        gates()
        cumsums()
        operands()
        scores()
        products()
        st = state_pass(st)
        finish()
        return st

    @pl.when(factored_ref[head] == 1)
    def _():
        st = jnp.zeros((dk, dk), F32)
        for cb in range(seq // (HGRN_CHUNK * group)):
            st = block(cb, st)

    @pl.when(factored_ref[head] == 0)
    def _():
        chunk = functools.partial(_hgrn_pairwise_chunk, q_ref, f_ref, v_ref, z_ref, lower_bound(), norm_ref,
                                  cum_ref, o_ref, col_ref)
        lax.fori_loop(0, seq // HGRN_CHUNK, chunk, jnp.zeros((dk, dk), F32))


def _hgrn_patterns():
    C, SUB = HGRN_CHUNK, HGRN_SUB
    t = np.arange(C)[:, None]
    s = np.arange(C)[None, :]
    return (((t // SUB) == (s // SUB)) & (s <= t)).astype(np.float32)


def _hgrn(streams, lower_bounds, norm, *, batch, seq):
    dk = HGRN_HEAD_DIM
    heads = norm.shape[1] // dk
    cum = jnp.asarray(_hgrn_patterns(), dtype=BF16)
    col = lambda name: (lambda b, h, o=streams[name][1] // dk: (b, o + h))
    return pl.pallas_call(
        functools.partial(_hgrn_kernel, group=16),
        grid=(batch, heads),
        in_specs=[
            pl.BlockSpec((seq, dk), col("q_h")),
            pl.BlockSpec((seq, dk), col("f_h")),
            pl.BlockSpec((seq, dk), col("i_h")),
            pl.BlockSpec((seq, dk), col("z_h")),
            pl.BlockSpec((lower_bounds.shape[0], dk), lambda b, h: (0, h)),
            pl.BlockSpec((1, dk), lambda b, h: (0, h)),
            pl.BlockSpec(cum.shape, lambda b, h: (0, 0)),
        ],
        out_specs=pl.BlockSpec((seq, dk), lambda b, h: (b, h)),
        out_shape=jax.ShapeDtypeStruct((batch * seq, heads * dk), BF16),
        scratch_shapes=[pltpu.VMEM((2, HGRN_CHUNK, dk), F32), pltpu.SMEM((heads,), jnp.int32)],
        compiler_params=pltpu.CompilerParams(
            dimension_semantics=("arbitrary", "arbitrary"),
            vmem_limit_bytes=VMEM_LIMIT),
        name="hgrn",
    )(streams["q_h"][0], streams["f_h"][0], streams["i_h"][0], streams["z_h"][0], lower_bounds, norm, cum)


def _outproj_kernel(x_ref, oa_ref, oh_ref, wa_ref, wh_ref, g_ref, o_ref):
    y = x_ref[...] + jnp.dot(oa_ref[...], wa_ref[...], preferred_element_type=F32)
    y = y + jnp.dot(oh_ref[...], wh_ref[...], preferred_element_type=F32)
    ms = jnp.mean(y * y, axis=-1, keepdims=True)
    o_ref[...] = y * lax.rsqrt(ms + EPS) * g_ref[...]


def _outproj(x2, o_a, o_h, wa, wh, g, *, tm):
    M, D = x2.shape
    const = lambda a: pl.BlockSpec(a.shape, lambda i: (0,) * a.ndim)
    return pl.pallas_call(
        _outproj_kernel,
        grid=(M // tm,),
        in_specs=[
            pl.BlockSpec((tm, D), lambda i: (i, 0)),
            pl.BlockSpec((tm, o_a.shape[1]), lambda i: (i, 0)),
            pl.BlockSpec((tm, o_h.shape[1]), lambda i: (i, 0)),
            const(wa), const(wh), const(g),
        ],
        out_specs=pl.BlockSpec((tm, D), lambda i: (i, 0)),
        out_shape=jax.ShapeDtypeStruct((M, D), F32),
        compiler_params=pltpu.CompilerParams(
            dimension_semantics=("arbitrary",),
            vmem_limit_bytes=VMEM_LIMIT),
        name="outproj",
    )(x2, o_a, o_h, wa, wh, g)


def kernel(x, norm_in, w_in, cmp_pe_k, cmp_w1_k, cmp_w2_k, cmp_pe_v, cmp_w1_v, cmp_w2_v,
           lower_bounds, nsa_out_norm, hgrn_out_norm, w_out, final_norm):
    B, S, D = x.shape
    assert norm_in.shape[0] == 1, "single-layer problem"
    nsa_w = nsa_out_norm.shape[1]
    hgrn_w = hgrn_out_norm.shape[1]
    dh = NSA_HEAD_DIM
    G = NSA_KV_HEADS
    heads = nsa_w // dh
    kvw = G * dh
    n_gate = 3 * heads
    n_cmp = (S - CMP_BLOCK) // CMP_STRIDE + 1
    n_half = S // CMP_STRIDE
    assert S % NSA_TILE == 0 and WINDOW % NSA_TILE == 0 and n_half <= LANES and 2 * dh == LANES

    names = ["q_a", "k_cmp", "v_cmp", "k_slc", "v_slc", "k_win", "v_win", "gate", "z_a",
             "q_h", "f_h", "i_h", "z_h"]
    widths = [nsa_w] + [kvw] * 6 + [n_gate, nsa_w] + [hgrn_w] * 4
    starts = dict(zip(names, np.cumsum([0] + widths[:-1]).tolist()))
    wd = dict(zip(names, widths))

    paired = lambda kname, vname: [(starts[n] + g * dh, dh) for g in range(G) for n in (kname, vname)]
    whole = lambda n: [(starts[n], wd[n])]
    groups = [[("q_a", whole("q_a")),
               ("cmp", paired("k_cmp", "v_cmp")), ("slc", paired("k_slc", "v_slc")),
               ("win", paired("k_win", "v_win")), ("q_h", whole("q_h")), ("i_h", whole("i_h"))],
              [("z_a", whole("z_a")), ("f_h", whole("f_h")), ("z_h", whole("z_h"))]]
    plan, col_of, off = [], {}, 0
    for group in groups:
        for n, pieces in group:
            col_of[n] = off
            for src, width in pieces:
                plan.append((off, src, width, dh ** -0.5 if n == "q_a" else 1.0))
                off += width
    n_main = off
    n16_cols = col_of["z_a"]
    plan.append((n_main, starts["gate"], n_gate, 1.0))
    w_all = _relayout_weights(w_in[0].T, tuple(plan), n_main + LANES)

    x2 = x.reshape(B * S, D)
    proj16, proj32, gate = _proj(x2, norm_in, w_all, tm=min(1024, B * S), tn=1536,
                                 n_main=n_main, n16_cols=n16_cols)
    streams = {}
    for arr, group, base in ((proj16, groups[0], 0), (proj32, groups[1], n16_cols)):
        for n, _ in group:
            streams[n] = (arr, col_of[n] - base)

    featk, featw, featc, qfeat, qrel, seg, mmap, place = _nsa_tables(S, n_half, n_cmp, heads)

    def w1_halves(w1k, w1v):
        hk = w1k.shape[1]
        k3 = w1k.reshape(2, CMP_STRIDE, dh, hk)
        v3 = w1v.reshape(2, CMP_STRIDE, dh, hk)
        zk = jnp.zeros_like(k3[0])
        top = lambda a: jnp.concatenate([a, zk], axis=-1)
        bot = lambda a: jnp.concatenate([zk, a], axis=-1)
        half = lambda i: jnp.concatenate([top(k3[i]), bot(v3[i])], axis=1).reshape(CMP_STRIDE * 2 * dh, 2 * hk)
        return half(0).astype(BF16), half(1).astype(BF16)

    wp, wq = w1_halves(cmp_w1_k[0], cmp_w1_v[0])
    zk = jnp.zeros_like(cmp_w2_k[0])
    w2 = jnp.concatenate([jnp.concatenate([cmp_w2_k[0], zk], axis=1),
                          jnp.concatenate([zk, cmp_w2_v[0]], axis=1)], axis=0).astype(BF16)
    pe = jnp.concatenate([cmp_pe_k[0].reshape(2, CMP_STRIDE, dh), cmp_pe_v[0].reshape(2, CMP_STRIDE, dh)],
                         axis=-1).reshape(2, CMP_STRIDE * 2 * dh)
    kc, vc = _compress(streams["cmp"][0], pe, wp, wq, w2, jnp.tile(featc, (G, 1)),
                       batch=B, seq=S, col0=streams["cmp"][1], n_half=n_half)

    o_a = _nsa(streams, gate, kc, vc, featk, featw, qfeat, qrel, seg, mmap, place, nsa_out_norm,
               batch=B, seq=S, n_cmp=n_cmp)
    o_h = _hgrn(streams, lower_bounds, hgrn_out_norm, batch=B, seq=S)

    wo = w_out[0].astype(BF16)
    out = _outproj(x2, o_a, o_h, wo[:nsa_w], wo[nsa_w:], final_norm.reshape(1, D), tm=512)
    return out.reshape(B, S, D)
```

```python
import functools

import ml_dtypes
import numpy as np
import jax
import jax.numpy as jnp
from jax import lax
from jax.experimental import pallas as pl
from jax.experimental.pallas import tpu as pltpu

F32 = jnp.float32
BF16 = jnp.bfloat16

EPS = 1e-6
NEG_INF = -1e30
LOG2E = 1.4426950408889634

NSA_HEAD_DIM = 64
NSA_KV_HEADS = 4
CMP_BLOCK = 32
CMP_STRIDE = 16
SEL_BLOCK = 64
SEL_TOP = 8
SEL_BONUS = 1.0e4
WINDOW = 512
HGRN_HEAD_DIM = 128
HGRN_CHUNK = 64
HGRN_SUB = 16
HGRN_HEADS_PER_STEP = 2
HGRN_MAX_LOG_DECAY = 60.0

LANES = 128
VMEM_LIMIT = 56 * 1024 * 1024
NSA_TILE = 256
NSA_TILES_PER_STEP = 8
SEL_RECENT = 3

SEL_LANE0 = NSA_HEAD_DIM
MAX_SEL_BLOCKS = 32
POS_LANE0 = SEL_LANE0 + MAX_SEL_BLOCKS
N_SPLIT = 4
FLAG_LANE = POS_LANE0 + 2 * N_SPLIT
SHIFT_LANE0 = FLAG_LANE + 1
REL_LANE0 = SHIFT_LANE0 + N_SPLIT
SCORE_BOUND = 100.0
SAFETY = 1.02


def _nt_dot(a, b):
    return lax.dot_general(a, b, (((1,), (1,)), ((), ())), preferred_element_type=F32)


def _tn_dot(a, b):
    return lax.dot_general(a, b, (((0,), (0,)), ((), ())), preferred_element_type=F32)


def _split3(x):
    hi = x.astype(BF16)
    r1 = x - hi.astype(F32)
    mid = r1.astype(BF16)
    lo = (r1 - mid.astype(F32)).astype(BF16)
    return hi, mid, lo


def _proj_kernel(x_ref, g_ref, w_ref, wg_ref, o16_ref, o32_ref, og_ref, h_ref, *, row_chunk, n16):
    j = pl.program_id(1)

    @pl.when(j == 0)
    def _():
        n_chunks = x_ref.shape[0] // row_chunk

        def body(c, carry):
            rows = pl.ds(pl.multiple_of(c * row_chunk, row_chunk), row_chunk)
            x = x_ref[rows, :]
            ms = jnp.mean(x * x, axis=-1, keepdims=True)
            h_ref[rows, :] = (x * lax.rsqrt(ms + EPS) * g_ref[...]).astype(BF16)
            return carry

        lax.fori_loop(0, n_chunks, body, 0)
        og_ref[...] = jnp.dot(h_ref[...], wg_ref[...], preferred_element_type=F32)

    @pl.when(j < n16)
    def _():
        o16_ref[...] = jnp.dot(h_ref[...], w_ref[...], preferred_element_type=F32).astype(BF16)

    @pl.when(j >= n16)
    def _():
        o32_ref[...] = jnp.dot(h_ref[...], w_ref[...], preferred_element_type=F32)


def _relayout_kernel(wt_ref, o_ref, *, plan):
    o_ref[:, o_ref.shape[1] - LANES:] = jnp.zeros((o_ref.shape[0], LANES), o_ref.dtype)
    for dst, src, width, scale in plan:
        o_ref[:, dst:dst + width] = (wt_ref[src:src + width, :].T * scale).astype(o_ref.dtype)


def _relayout_weights(wt, plan, n_out, *, rows=256):
    n_in, D = wt.shape
    return pl.pallas_call(
        functools.partial(_relayout_kernel, plan=plan),
        grid=(D // rows,),
        in_specs=[pl.BlockSpec((n_in, rows), lambda i: (0, i))],
        out_specs=pl.BlockSpec((rows, n_out), lambda i: (i, 0)),
        out_shape=jax.ShapeDtypeStruct((D, n_out), BF16),
        compiler_params=pltpu.CompilerParams(
            dimension_semantics=("arbitrary",),
            vmem_limit_bytes=VMEM_LIMIT),
        name="relayout",
    )(wt)


def _proj(x2, g, w, *, tm, tn, n_main, n16_cols):
    M, D = x2.shape
    N = n_main
    NG = w.shape[1] - n_main
    n16 = n16_cols // tn
    assert n16 * tn == n16_cols and N % tn == 0 and N % NG == 0
    return pl.pallas_call(
        functools.partial(_proj_kernel, row_chunk=128, n16=n16),
        grid=(M // tm, N // tn),
        in_specs=[
            pl.BlockSpec((tm, D), lambda i, j: (i, 0)),
            pl.BlockSpec((1, D), lambda i, j: (0, 0)),
            pl.BlockSpec((D, tn), lambda i, j: (0, j)),
            pl.BlockSpec((D, NG), lambda i, j: (0, N // NG)),
        ],
        out_specs=[
            pl.BlockSpec((tm, tn), lambda i, j: (i, jnp.minimum(j, n16 - 1))),
            pl.BlockSpec((tm, tn), lambda i, j: (i, jnp.maximum(j - n16, 0))),
            pl.BlockSpec((tm, NG), lambda i, j: (i, 0)),
        ],
        out_shape=[
            jax.ShapeDtypeStruct((M, n16_cols), BF16),
            jax.ShapeDtypeStruct((M, N - n16_cols), F32),
            jax.ShapeDtypeStruct((M, NG), F32),
        ],
        scratch_shapes=[pltpu.VMEM((tm, D), BF16)],
        compiler_params=pltpu.CompilerParams(
            dimension_semantics=("arbitrary", "arbitrary"),
            vmem_limit_bytes=VMEM_LIMIT),
        name="proj",
    )(x2, g, w, w)


def _compress_kernel(c0_ref, c1_ref, c2_ref, c3_ref, pe_ref, wp_ref, wq_ref, w2_ref, feat_ref,
                     kc_ref, vc_ref, x_ref, c32_ref, *, n_half):
    for g, c_ref in enumerate((c0_ref, c1_ref, c2_ref, c3_ref)):
        c32_ref[...] = c_ref[...].astype(F32)
        for l in range(CMP_STRIDE):
            x_ref[g * n_half:(g + 1) * n_half, l * LANES:(l + 1) * LANES] = (
                c32_ref[pl.ds(l, n_half, stride=CMP_STRIDE), :])
    x = x_ref[...]
    rows = x.shape[0]
    first = jnp.dot((x + pe_ref[0:1, :]).astype(BF16), wp_ref[...], preferred_element_type=F32)
    second = jnp.dot((x + pe_ref[1:2, :]).astype(BF16), wq_ref[...], preferred_element_type=F32)
    hidden = first + pltpu.roll(second, rows - 1, 0)
    out = jnp.dot(jax.nn.gelu(hidden).astype(BF16), w2_ref[...], preferred_element_type=F32)
    lane = lax.broadcasted_iota(jnp.int32, out.shape, 1)
    dh = NSA_HEAD_DIM
    kc_ref[...] = jnp.where(lane < dh, out.astype(BF16), feat_ref[...])
    for g in range(NSA_KV_HEADS):
        vc_ref[g * dh:(g + 1) * dh, :] = out[g * n_half:(g + 1) * n_half, :].T[dh:, :].astype(BF16)


def _compress(proj, pe, wp, wq, w2, feat, *, batch, seq, col0, n_half):
    rows = NSA_KV_HEADS * n_half
    rows_t = NSA_KV_HEADS * NSA_HEAD_DIM
    const = lambda a: pl.BlockSpec(a.shape, lambda b: (0,) * a.ndim)
    stream = lambda g: pl.BlockSpec((seq, LANES), lambda b, o=col0 // LANES + g: (b, o))
    return pl.pallas_call(
        functools.partial(_compress_kernel, n_half=n_half),
        grid=(batch,),
        in_specs=[stream(0), stream(1), stream(2), stream(3),
                  const(pe), const(wp), const(wq), const(w2), const(feat)],
        out_specs=[
            pl.BlockSpec((rows, LANES), lambda b: (b, 0)),
            pl.BlockSpec((rows_t, n_half), lambda b: (b, 0)),
        ],
        out_shape=[
            jax.ShapeDtypeStruct((batch * rows, LANES), BF16),
            jax.ShapeDtypeStruct((batch * rows_t, n_half), BF16),
        ],
        scratch_shapes=[pltpu.VMEM((rows, CMP_STRIDE * LANES), F32), pltpu.VMEM((seq, LANES), F32)],
        compiler_params=pltpu.CompilerParams(
            dimension_semantics=("arbitrary",),
            vmem_limit_bytes=VMEM_LIMIT),
        name="compress",
    )(proj, proj, proj, proj, pe, wp, wq, w2, feat)


def _nsa_kernel(q_ref, qall_ref, z_ref, gate_ref, slc_ref, win_ref, kc_ref, vct_ref, featk_ref, featw_ref,
                qfeat_ref, qrel_ref, seg_ref, mmap_ref, place_ref, norm_ref, o_ref,
                ksel_ref, vselt_ref, kwin_ref, vwint_ref, gt_ref, bounded_ref, *, n_cmp):
    g = pl.program_id(1)
    qi = pl.program_id(2)
    tq = NSA_TILE
    per_step = q_ref.shape[0] // tq
    tk = tq
    seq = slc_ref.shape[0]
    dh = NSA_HEAD_DIM
    rep = q_ref.shape[1] // dh
    n_half = kc_ref.shape[0]
    n_pad = WINDOW // tk

    @pl.when(qi == 0)
    def _():
        lane = lax.broadcasted_iota(jnp.int32, (seq, LANES), 1)

        def max_sq_norm(x, width):
            s = jnp.dot((x * x).astype(BF16), seg_ref[0:width, :], preferred_element_type=F32)
            return jnp.max(s) * SAFETY

        kv = slc_ref[...]
        k2_sel = max_sq_norm(jnp.where(lane < dh, kv.astype(F32), 0.0), LANES)
        ksel_ref[...] = jnp.where(lane < dh, kv, featk_ref[...])
        vt = kv.astype(F32).T[dh:, :].astype(BF16)
        for kt in range(seq // tk):
            vselt_ref[kt] = vt[:, kt * tk:(kt + 1) * tk]
        kv = win_ref[...]
        k2_win = max_sq_norm(jnp.where(lane < dh, kv.astype(F32), 0.0), LANES)
        kwin_ref[0:WINDOW, :] = featw_ref[0:WINDOW, :]
        kwin_ref[WINDOW:, :] = jnp.where(lane < dh, kv, featw_ref[WINDOW:, :])
        vt = kv.astype(F32).T[dh:, :].astype(BF16)
        for kt in range(n_pad):
            vwint_ref[kt] = jnp.zeros((dh, tk), BF16)
        for kt in range(seq // tk):
            vwint_ref[n_pad + kt] = vt[:, kt * tk:(kt + 1) * tk]
        q2 = max_sq_norm(qall_ref[...].astype(F32), rep * dh) * (LOG2E * LOG2E)
        limit = SCORE_BOUND * SCORE_BOUND
        bounded_ref[0] = ((q2 * k2_sel <= limit) & (q2 * k2_win <= limit)).astype(jnp.int32)

    heads = lambda a: jnp.concatenate([a] * rep, axis=1)
    sub8 = lambda a: a.reshape(a.shape[0] // 8, 8, a.shape[1])

    def softmax_pv(lgs, v_tiles, bounded, extra=None):
        if bounded:
            m = 0.0
        else:
            m8 = functools.reduce(jnp.maximum, [jnp.max(sub8(lg), axis=0) for lg in lgs])
            m = jnp.max(m8, axis=0, keepdims=True)
        l8 = jnp.zeros((8, lgs[0].shape[1]), F32)
        acc = jnp.zeros((dh, lgs[0].shape[1]), F32)
        for lg, vt in zip(lgs, v_tiles):
            pt = jnp.exp2(lg) if bounded else jnp.exp2(lg - m)
            l8 = l8 + jnp.sum(sub8(pt), axis=0)
            acc = acc + jnp.dot(vt, pt.astype(BF16), preferred_element_type=F32)
        if extra is not None:
            l8, acc = extra(l8, acc)
        return acc / jnp.sum(l8, axis=0, keepdims=True)

    def step(n, bounded, slot):
        static = isinstance(n, int)
        assert static == bounded
        own = (slice(slot * tq, (slot + 1) * tq) if static
               else pl.ds(pl.multiple_of(slot * tq, tq), tq))
        t0 = n * tq
        lane = lax.broadcasted_iota(jnp.int32, (tq, LANES), 1)
        key_i = lax.broadcasted_iota(jnp.int32, (tk, tq), 0)
        qry_i = lax.broadcasted_iota(jnp.int32, (tk, tq), 1)
        causal_1 = jnp.where(key_i <= qry_i, 0.0, NEG_INF)
        causal_bias = heads(causal_1)
        band_bias = heads(jnp.where(key_i > qry_i, 0.0, NEG_INF))

        qfeat = qfeat_ref[0, n]
        rel_lanes = (lane >= REL_LANE0) & (lane < REL_LANE0 + 3)
        qa = []
        for r in range(rep):
            qcol = q_ref[own, (r // 2) * LANES:(r // 2 + 1) * LANES].astype(F32)
            if r % 2:
                qcol = pltpu.roll(qcol, dh, 1)
            feat = jnp.where(rel_lanes, qrel_ref[0, r * tq:(r + 1) * tq, :].astype(F32), qfeat[r:r + 1, :])
            qa.append(jnp.where(lane < dh, qcol * LOG2E, feat))
        q0 = jnp.concatenate(qa, axis=0).astype(BF16)

        gt_ref[slot] = jax.nn.sigmoid(gate_ref[own, :]).T
        out_scale = []
        for pair in range(rep // 2):
            z = z_ref[own, pair * LANES:(pair + 1) * LANES]
            out_scale.append(norm_ref[:, pair * LANES:(pair + 1) * LANES] * (z * jax.nn.sigmoid(z)))

        nc = min(n_half, (t0 + tq) // CMP_STRIDE) if static else n_half
        n_idx = lax.broadcasted_iota(jnp.int32, (nc, tq), 0)
        t_idx = t0 + lax.broadcasted_iota(jnp.int32, (nc, tq), 1)
        valid_c = heads((CMP_STRIDE * n_idx + (CMP_BLOCK - 1) <= t_idx) & (n_idx < n_cmp))
        lg_c = jnp.where(valid_c, _nt_dot(kc_ref[0:nc, :], q0), NEG_INF)

        n_wt = n_pad + 1
        win_tiles = [w for w in range(n_wt) if not (static and n + w < n_pad)]
        lgs = []
        for w in win_tiles:
            rows = (slice(t0 + w * tk, t0 + (w + 1) * tk) if static
                    else pl.ds(pl.multiple_of(t0 + w * tk, tk), tk))
            lg = _nt_dot(kwin_ref[rows, :], q0)
            if w == 0:
                lg = lg + band_bias
            if w == n_wt - 1:
                lg = lg + causal_bias
            lgs.append(lg)
        o_win = softmax_pv(lgs, [vwint_ref[n + w] for w in win_tiles], bounded)

        e = jnp.exp2(lg_c - jnp.max(lg_c, axis=0, keepdims=True))
        p = jnp.where(valid_c, e / jnp.sum(e, axis=0, keepdims=True), 0.0)
        if nc < n_half:
            p = jnp.concatenate([p, jnp.zeros((n_half - nc, rep * tq), F32)], axis=0)
        o_cmp = jnp.dot(vct_ref[...], p.astype(BF16), preferred_element_type=F32)
        p_sum = p[:, 0:tq]
        for r in range(1, rep):
            p_sum = p_sum + p[:, r * tq:(r + 1) * tq]

        nb = min(MAX_SEL_BLOCKS, (t0 + tq) // SEL_BLOCK) if static else MAX_SEL_BLOCKS
        mm = mmap_ref[...]
        hi, mid, lo = _split3(p_sum)
        p_slc = (jnp.dot(mm, hi, preferred_element_type=F32) + jnp.dot(mm, mid, preferred_element_type=F32)
                 + jnp.dot(mm, lo, preferred_element_type=F32))[0:MAX_SEL_BLOCKS]
        if static:
            sel_scores = lambda kt: _nt_dot(ksel_ref[kt * tk:(kt + 1) * tk, :], q0)
            maybe_idle = [kt for kt in range(1, n + 1 - SEL_RECENT)]
            raw = {kt: sel_scores(kt) for kt in range(n + 1) if kt not in maybe_idle}
        jj = lax.broadcasted_iota(jnp.int32, (MAX_SEL_BLOCKS, tq), 0)
        tt = t0 + lax.broadcasted_iota(jnp.int32, (MAX_SEL_BLOCKS, tq), 1)
        cur = lax.shift_right_logical(tt, int(np.log2(SEL_BLOCK)))
        forced = (jj == 0) | (jj == cur) | (jj == cur - 1)
        future = jj > cur
        score = jnp.where(future, -1.0, p_slc + jnp.where(forced, SEL_BONUS, 0.0))
        rank = jnp.zeros((MAX_SEL_BLOCKS, tq), jnp.int32)
        for i in range(nb):
            other = jnp.broadcast_to(score[i:i + 1, :], score.shape)
            beats = (other > score) | ((other == score) & (jj > i))
            rank = rank + beats.astype(jnp.int32)
        keep = (rank < SEL_TOP) & jnp.logical_not(future)

        if static:
            blocks_per_tile = tk // SEL_BLOCK
            block_bias = jnp.where(keep, 0.0, NEG_INF)

            def tile_bias(kt):
                b = jnp.concatenate(
                    [jnp.broadcast_to(block_bias[blocks_per_tile * kt + j:blocks_per_tile * kt + j + 1, :],
                                      (SEL_BLOCK, tq)) for j in range(blocks_per_tile)], axis=0)
                return heads(b + causal_1 if kt == n else b)

            wanted = [jnp.max(block_bias[blocks_per_tile * kt:blocks_per_tile * (kt + 1)]) == 0.0
                      for kt in maybe_idle]

            def idle_tiles(l8, acc):
                for kt, some in zip(maybe_idle, wanted):
                    def live(l8, acc, kt=kt):
                        pt = jnp.exp2(sel_scores(kt) + tile_bias(kt))
                        return (l8 + jnp.sum(sub8(pt), axis=0),
                                acc + jnp.dot(vselt_ref[kt], pt.astype(BF16), preferred_element_type=F32))

                    l8, acc = lax.cond(some, live, lambda l8, acc: (l8, acc), l8, acc)
                return l8, acc

            o_slc = softmax_pv([raw[kt] + tile_bias(kt) for kt in raw], [vselt_ref[kt] for kt in raw],
                               bounded, extra=idle_tiles if maybe_idle else None)
        else:
            chosen = jnp.where(keep, 1.0, 0.0).astype(BF16)
            placed = _tn_dot(chosen, place_ref[...])
            sel_lanes = (lane >= SEL_LANE0) & (lane < SEL_LANE0 + MAX_SEL_BLOCKS)
            sel_bias = (placed - 1.0) * (-NEG_INF)
            qs = jnp.concatenate([jnp.where(sel_lanes, sel_bias, a) for a in qa], axis=0).astype(BF16)

            def sel_tile(kt, carry, bias=None):
                m, l, acc = carry
                lg = _nt_dot(ksel_ref[pl.ds(pl.multiple_of(kt * tk, tk), tk), :], qs)
                if bias is not None:
                    lg = lg + bias
                m_new = jnp.maximum(m, jnp.max(lg, axis=0, keepdims=True))
                alpha = jnp.exp2(m - m_new)
                pt = jnp.exp2(lg - m_new)
                return (m_new, alpha * l + jnp.sum(pt, axis=0, keepdims=True),
                        alpha * acc + jnp.dot(vselt_ref[kt], pt.astype(BF16), preferred_element_type=F32))

            init = (jnp.full((1, rep * tq), NEG_INF, F32), jnp.zeros((1, rep * tq), F32),
                    jnp.zeros((dh, rep * tq), F32))
            _, l, acc = sel_tile(n, lax.fori_loop(0, n, sel_tile, init), causal_bias)
            o_slc = acc / l

        ys = []
        for r in range(rep):
            cols = slice(r * tq, (r + 1) * tq)
            gate = lambda c: gt_ref[slot, pl.ds(3 * (g * rep + r) + c, 1), :]
            o = gate(0) * o_cmp[:, cols] + gate(1) * o_slc[:, cols] + gate(2) * o_win[:, cols]
            ys.append(o * lax.rsqrt(jnp.mean(o * o, axis=0, keepdims=True) + EPS))
        for pair in range(rep // 2):
            cols = slice(pair * LANES, (pair + 1) * LANES)
            y = jnp.concatenate([ys[2 * pair], ys[2 * pair + 1]], axis=0).T
            o_ref[own, cols] = (y * out_scale[pair]).astype(o_ref.dtype)

    def tiles(first):
        for slot in range(per_step):
            step(first + slot, True, slot)

    @pl.when(bounded_ref[0] == 1)
    def _():
        lax.switch(qi, [functools.partial(tiles, first) for first in range(0, seq // tq, per_step)])

    @pl.when(bounded_ref[0] == 0)
    def _():
        @pl.loop(0, per_step)
        def _(slot):
            step(qi * per_step + slot, False, slot)


def _nsa(streams, gate, kc, vc, featk, featw, qfeat, qrel, seg, mmap, place, norm, *, batch, seq, n_cmp):
    tq = NSA_TILE
    rows = NSA_TILES_PER_STEP * tq
    nq = seq // rows
    G = NSA_KV_HEADS
    gw = norm.shape[1] // G
    n_half = kc.shape[0] // (batch * G)
    const = lambda a: pl.BlockSpec(a.shape, lambda b, g, i: (0,) * a.ndim)
    stream = lambda name: pl.BlockSpec((seq, LANES),
                                       lambda b, g, i, o=streams[name][1] // LANES: (b, o + g))
    tile = lambda name: pl.BlockSpec((rows, gw), lambda b, g, i, o=streams[name][1] // gw: (b * nq + i, o + g))
    return pl.pallas_call(
        functools.partial(_nsa_kernel, n_cmp=n_cmp),
        grid=(batch, G, nq),
        in_specs=[
            tile("q_a"),
            pl.BlockSpec((seq, gw), lambda b, g, i, o=streams["q_a"][1] // gw: (b, o + g)),
            tile("z_a"),
            pl.BlockSpec((rows, gate.shape[1]), lambda b, g, i: (b * nq + i, 0)),
            stream("slc"), stream("win"),
            pl.BlockSpec((n_half, LANES), lambda b, g, i: (b * G + g, 0)),
            pl.BlockSpec((NSA_HEAD_DIM, n_half), lambda b, g, i: (b * G + g, 0)),
            const(featk), const(featw),
            pl.BlockSpec((1,) + qfeat.shape[1:], lambda b, g, i: (g, 0, 0, 0)),
            pl.BlockSpec((1,) + qrel.shape[1:], lambda b, g, i: (g, 0, 0)),
            const(seg), const(mmap), const(place),
            pl.BlockSpec((1, gw), lambda b, g, i: (0, g)),
        ],
        out_specs=pl.BlockSpec((rows, gw), lambda b, g, i: (b * nq + i, g)),
        out_shape=jax.ShapeDtypeStruct((batch * seq, G * gw), BF16),
        scratch_shapes=[
            pltpu.VMEM((seq, LANES), BF16),
            pltpu.VMEM((seq // tq, NSA_HEAD_DIM, tq), BF16),
            pltpu.VMEM((seq + WINDOW, LANES), BF16),
            pltpu.VMEM(((seq + WINDOW) // tq, NSA_HEAD_DIM, tq), BF16),
            pltpu.VMEM((NSA_TILES_PER_STEP, gate.shape[1], tq), F32),
            pltpu.SMEM((1,), jnp.int32),
        ],
        compiler_params=pltpu.CompilerParams(
            dimension_semantics=("arbitrary", "arbitrary", "arbitrary"),
            vmem_limit_bytes=VMEM_LIMIT),
        name="nsa",
    )(streams["q_a"][0], streams["q_a"][0], streams["z_a"][0], gate, streams["slc"][0], streams["win"][0],
      kc, vc, featk, featw, qfeat, qrel, seg, mmap, place, norm)


def _bf16_terms(x, n):
    terms, rest = [], np.asarray(x, np.float64)
    for _ in range(n):
        t = rest.astype(np.float32).astype(ml_dtypes.bfloat16).astype(np.float64)
        terms.append(t.astype(np.float32))
        rest = rest - t
    return terms


def _nsa_tables(seq, n_half, n_cmp, heads):
    assert seq // SEL_BLOCK <= MAX_SEL_BLOCKS and FLAG_LANE < LANES

    def key_features(pos, onehot_blocks):
        f = np.zeros((len(pos), LANES), np.float32)
        if onehot_blocks:
            f[np.arange(len(pos)), SEL_LANE0 + pos // SEL_BLOCK] = 1.0
        f[:, POS_LANE0:POS_LANE0 + N_SPLIT] = ((pos // 64) * 64)[:, None]
        f[:, POS_LANE0 + N_SPLIT:POS_LANE0 + 2 * N_SPLIT] = (pos % 64)[:, None]
        return f

    assert REL_LANE0 + 3 <= LANES
    featk = key_features(np.arange(seq), True)
    featk[:, SHIFT_LANE0:REL_LANE0 + 3] = 1.0
    featw = np.concatenate([np.zeros((WINDOW, LANES), np.float32), key_features(np.arange(seq), False)])
    featw[:WINDOW, FLAG_LANE] = NEG_INF
    featw[WINDOW:, SHIFT_LANE0:REL_LANE0 + 3] = 1.0
    featc = key_features(CMP_STRIDE * np.arange(n_half) + CMP_BLOCK - 1, False)
    featc[n_cmp:, FLAG_LANE] = NEG_INF

    slopes = (2.0 ** (-8.0 * np.arange(1, heads + 1) / heads)).astype(np.float32).astype(np.float64)
    slopes2 = slopes * LOG2E
    terms = _bf16_terms(slopes2, N_SPLIT)
    rep = heads // NSA_KV_HEADS
    nq = seq // NSA_TILE
    qfeat = np.zeros((NSA_KV_HEADS, nq, 8, LANES), np.float32)
    qrel = np.zeros((NSA_KV_HEADS, rep * NSA_TILE, LANES), np.float32)
    for h in range(heads):
        g, r = divmod(h, rep)
        for i, t in enumerate(terms):
            qfeat[g, :, r, POS_LANE0 + i] = t[h]
            qfeat[g, :, r, POS_LANE0 + N_SPLIT + i] = t[h]
        qfeat[g, :, r, FLAG_LANE] = 1.0
        for i, t in enumerate(_bf16_terms(-slopes2[h] * NSA_TILE * np.arange(nq), N_SPLIT)):
            qfeat[g, :, r, SHIFT_LANE0 + i] = t
        for i, t in enumerate(_bf16_terms(-slopes2[h] * np.arange(NSA_TILE), 3)):
            qrel[g, r * NSA_TILE:(r + 1) * NSA_TILE, REL_LANE0 + i] = t
    seg = np.zeros((2 * LANES, LANES), np.float32)
    seg[np.arange(2 * LANES), np.arange(2 * LANES) // NSA_HEAD_DIM] = 1.0

    cs = CMP_STRIDE * np.arange(n_half)[None, :]
    ss = SEL_BLOCK * np.arange(LANES)[:, None]
    overlap = np.clip(np.minimum(cs + CMP_BLOCK, ss + SEL_BLOCK) - np.maximum(cs, ss), 0, None)
    mmap = (overlap / CMP_BLOCK) * (np.arange(n_half)[None, :] < n_cmp) * (ss < seq)
    place = np.zeros((MAX_SEL_BLOCKS, LANES), np.float32)
    place[np.arange(MAX_SEL_BLOCKS), SEL_LANE0 + np.arange(MAX_SEL_BLOCKS)] = 1.0
    bf = lambda a: jnp.asarray(a, dtype=BF16)
    return bf(featk), bf(featw), bf(featc), jnp.asarray(qfeat), bf(qrel), bf(seg), bf(mmap), bf(place)


def _hgrn_stages(q_ref, f_ref, v_ref, z_ref, lb_ref, norm_ref, cum_ref, o_ref, rows):
    C, SUB = HGRN_CHUNK, HGRN_SUB
    n_sub = C // SUB
    chunks = range(len(rows))
    sub = lambda x, i: x[i * SUB:(i + 1) * SUB]
    w = {}

    def gates():
        lbr = lb_ref[...]
        e = jnp.exp(lbr - jnp.max(lbr, axis=0, keepdims=True))
        lb = e[0:1, :] / jnp.sum(e, axis=0, keepdims=True)
        w["v16"] = [v_ref[rows[j], :].astype(BF16) for j in chunks]
        f = [lb + (1.0 - lb) * jax.nn.sigmoid(f_ref[rows[j], :]) for j in chunks]
        w["k"] = [1.0 - f[j] for j in chunks]
        w["parts"] = [_split3(jnp.log(f[j])) for j in chunks]

    def cumsums():
        cum = cum_ref[...]
        parts = w["parts"]
        w["a"] = [(jnp.dot(cum, parts[j][0], preferred_element_type=F32)
                   + jnp.dot(cum, parts[j][1], preferred_element_type=F32)
                   + jnp.dot(cum, parts[j][2], preferred_element_type=F32)) for j in chunks]

    def operands():
        a, k = w["a"], w["k"]
        q1, k1, k2, qb, k3, start, dec = [], [], [], [], [], [], []
        for j in chunks:
            q1.append(q_ref[rows[j], :] * jnp.exp(a[j]))
            k1.append(k[j] * jnp.exp(-a[j]))
            tot = [a[j][(i + 1) * SUB - 1:(i + 1) * SUB] for i in range(n_sub)]
            s = [jnp.zeros_like(tot[0])]
            for i in range(n_sub):
                s.append(s[i] + tot[i])
            start.append(s)
            dec.append(jnp.exp(s[n_sub]))
            qb.append(jnp.concatenate([sub(q1[j], i) * jnp.exp(s[i]) for i in range(n_sub)], axis=0)
                      .astype(BF16))
            k2.append([sub(k1[j], i) * jnp.exp(tot[i]) for i in range(n_sub)])
            k3.append(jnp.concatenate([sub(k1[j], i) * jnp.exp(s[n_sub] - s[i]) for i in range(n_sub)],
                                      axis=0).astype(BF16))
        w.update(q1=q1, k1=k1, k2=k2, qb=qb, k3=k3, start=start, dec=dec)

    def scores():
        q1, k1, k2, start = w["q1"], w["k1"], w["k2"], w["start"]
        att = []
        for j in chunks:
            row_blocks = []
            for i in range(n_sub):
                rhs = [k2[j][jb] * jnp.exp(start[j][i] - start[j][jb + 1]) if jb < i - 1 else k2[j][jb]
                       for jb in range(i)]
                rhs.append(sub(k1[j], i))
                rhs = jnp.concatenate(rhs, axis=0) if len(rhs) > 1 else rhs[0]
                sc = _nt_dot(sub(q1[j], i).astype(BF16), rhs.astype(BF16))
                width = (i + 1) * SUB
                causal = (lax.broadcasted_iota(jnp.int32, (SUB, width), 1)
                          <= lax.broadcasted_iota(jnp.int32, (SUB, width), 0) + i * SUB)
                row_blocks.append(jnp.where(causal, sc, 0.0).astype(BF16))
            att.append(row_blocks)
        w["att"] = att

    def products():
        att, v16, k3 = w["att"], w["v16"], w["k3"]
        w["intra"] = [jnp.concatenate([jnp.dot(att[j][i], v16[j][0:(i + 1) * SUB],
                                               preferred_element_type=F32)
                                       for i in range(n_sub)], axis=0) for j in chunks]
        w["incr"] = [_tn_dot(v16[j], k3[j]) for j in chunks]

    def state_pass(st):
        inter = []
        for j in chunks:
            inter.append(_nt_dot(w["qb"][j], st.astype(BF16)))
            st = st * w["dec"][j] + w["incr"][j]
        w["inter"] = inter
        return st

    def finish():
        for j in chunks:
            _hgrn_head_out(w["inter"][j] + w["intra"][j], z_ref, norm_ref, o_ref, rows[j])

    return gates, cumsums, operands, scores, products, state_pass, finish


def _hgrn_head_out(o, z_ref, norm_ref, o_ref, rows):
    y = o * lax.rsqrt(jnp.mean(o * o, axis=-1, keepdims=True) + EPS) * norm_ref[...]
    z = z_ref[rows, :]
    o_ref[rows, :] = (y * (z * jax.nn.sigmoid(z))).astype(o_ref.dtype)


def _hgrn_pairwise_chunk(q_ref, f_ref, v_ref, z_ref, lb, norm_ref, cum_ref, o_ref, col_ref, c, st):
    C, SUB = HGRN_CHUNK, HGRN_SUB
    dk = q_ref.shape[1]
    rows = pl.ds(pl.multiple_of(c * C, C), C)
    f = lb + (1.0 - lb) * jax.nn.sigmoid(f_ref[rows, :])
    k = 1.0 - f
    cum = cum_ref[...]
    hi, mid, lo = _split3(jnp.log(f))
    a = (jnp.dot(cum, hi, preferred_element_type=F32) + jnp.dot(cum, mid, preferred_element_type=F32)
         + jnp.dot(cum, lo, preferred_element_type=F32))
    before, pieces = jnp.zeros((1, dk), F32), []
    for i in range(C // SUB):
        pieces.append(a[i * SUB:(i + 1) * SUB] + before)
        before = before + a[(i + 1) * SUB - 1:(i + 1) * SUB]
    b, b_end = jnp.concatenate(pieces, axis=0), before
    q = q_ref[rows, :].astype(F32)
    v16 = v_ref[rows, :].astype(BF16)
    col_ref[0] = b
    col_ref[1] = k
    key_i = lax.broadcasted_iota(jnp.int32, (C, C), 1)
    qry_i = lax.broadcasted_iota(jnp.int32, (C, C), 0)

    def column(s, sc):
        bs = col_ref[0, pl.ds(s, 1), :]
        ks = col_ref[1, pl.ds(s, 1), :]
        terms = jnp.exp(jnp.minimum(b - bs, 0.0)) * (q * ks)
        return jnp.where(key_i == s, jnp.sum(terms, axis=1, keepdims=True), sc)

    sc = lax.fori_loop(0, C, column, jnp.zeros((C, C), F32))
    att = jnp.where(key_i <= qry_i, sc, 0.0).astype(BF16)
    intra = jnp.dot(att, v16, preferred_element_type=F32)
    inter = _nt_dot((q * jnp.exp(b)).astype(BF16), st.astype(BF16))
    incr = _tn_dot(v16, (k * jnp.exp(b_end - b)).astype(BF16))
    _hgrn_head_out(inter + intra, z_ref, norm_ref, o_ref, rows)
    return st * jnp.exp(b_end) + incr


def _hgrn_kernel(q_ref, f_ref, v_ref, z_ref, lb_ref, norm_ref, cum_ref, o_ref, col_ref, factored_ref, *, group):
    seq, dk = q_ref.shape[0], HGRN_HEAD_DIM
    per_step = q_ref.shape[1] // dk
    first = pl.program_id(1) * per_step
    lanes = lambda ref, s: ref.at[:, pl.ds(s * dk, dk)]
    views = [[lanes(ref, s) for ref in (q_ref, f_ref, v_ref, z_ref, lb_ref, norm_ref, o_ref)]
             for s in range(per_step)]

    def lower_bound(lb_view):
        lbr = lb_view[...]
        e = jnp.exp(lbr - jnp.max(lbr, axis=0, keepdims=True))
        return e[0:1, :] / jnp.sum(e, axis=0, keepdims=True)

    @pl.when(pl.program_id(0) == 0)
    def _():
        for s in range(per_step):
            ok = jnp.min(jnp.log(lower_bound(views[s][4]))) * HGRN_SUB >= -HGRN_MAX_LOG_DECAY
            factored_ref[first + s] = ok.astype(jnp.int32)

    def factored(s):
        q, f, v, z, lb, norm, o = views[s]
        st = jnp.zeros((dk, dk), F32)
        for cb in range(seq // (HGRN_CHUNK * group)):
            rows = [pl.ds((cb * group + j) * HGRN_CHUNK, HGRN_CHUNK) for j in range(group)]
            gates, cumsums, operands, scores, products, state_pass, finish = _hgrn_stages(
                q, f, v, z, lb, norm, cum_ref, o, rows)
            gates()
            cumsums()
            operands()
            scores()
            products()
            st = state_pass(st)
            finish()

    def pairwise(s):
        q, f, v, z, lb, norm, o = views[s]
        chunk = functools.partial(_hgrn_pairwise_chunk, q, f, v, z, lower_bound(lb), norm, cum_ref, o, col_ref)
        lax.fori_loop(0, seq // HGRN_CHUNK, chunk, jnp.zeros((dk, dk), F32))

    flags = [factored_ref[first + s] for s in range(per_step)]
    all_factored = functools.reduce(jnp.logical_and, [flag == 1 for flag in flags])

    @pl.when(all_factored)
    def _():
        for s in range(per_step):
            factored(s)

    @pl.when(jnp.logical_not(all_factored))
    def _():
        for s in range(per_step):
            pl.when(flags[s] == 1)(functools.partial(factored, s))
            pl.when(flags[s] == 0)(functools.partial(pairwise, s))


def _hgrn_patterns():
    C, SUB = HGRN_CHUNK, HGRN_SUB
    t = np.arange(C)[:, None]
    s = np.arange(C)[None, :]
    return (((t // SUB) == (s // SUB)) & (s <= t)).astype(np.float32)


def _hgrn(streams, lower_bounds, norm, *, batch, seq):
    dk = HGRN_HEAD_DIM
    heads = norm.shape[1] // dk
    width = HGRN_HEADS_PER_STEP * dk
    cum = jnp.asarray(_hgrn_patterns(), dtype=BF16)
    col = lambda name: (lambda b, h, o=streams[name][1] // width: (b, o + h))
    return pl.pallas_call(
        functools.partial(_hgrn_kernel, group=16),
        grid=(batch, heads // HGRN_HEADS_PER_STEP),
        in_specs=[
            pl.BlockSpec((seq, width), col("q_h")),
            pl.BlockSpec((seq, width), col("f_h")),
            pl.BlockSpec((seq, width), col("i_h")),
            pl.BlockSpec((seq, width), col("z_h")),
            pl.BlockSpec((lower_bounds.shape[0], width), lambda b, h: (0, h)),
            pl.BlockSpec((1, width), lambda b, h: (0, h)),
            pl.BlockSpec(cum.shape, lambda b, h: (0, 0)),
        ],
        out_specs=pl.BlockSpec((seq, width), lambda b, h: (b, h)),
        out_shape=jax.ShapeDtypeStruct((batch * seq, heads * dk), BF16),
        scratch_shapes=[pltpu.VMEM((2, HGRN_CHUNK, dk), F32), pltpu.SMEM((heads,), jnp.int32)],
        compiler_params=pltpu.CompilerParams(
            dimension_semantics=("arbitrary", "arbitrary"),
            vmem_limit_bytes=VMEM_LIMIT),
        name="hgrn",
    )(streams["q_h"][0], streams["f_h"][0], streams["i_h"][0], streams["z_h"][0], lower_bounds, norm, cum)


def _outproj_kernel(x_ref, oa_ref, oh_ref, wa_ref, wh_ref, g_ref, o_ref):
    y = x_ref[...] + jnp.dot(oa_ref[...], wa_ref[...], preferred_element_type=F32)
    y = y + jnp.dot(oh_ref[...], wh_ref[...], preferred_element_type=F32)
    ms = jnp.mean(y * y, axis=-1, keepdims=True)
    o_ref[...] = y * lax.rsqrt(ms + EPS) * g_ref[...]


def _outproj(x2, o_a, o_h, wa, wh, g, *, tm):
    M, D = x2.shape
    const = lambda a: pl.BlockSpec(a.shape, lambda i: (0,) * a.ndim)
    return pl.pallas_call(
        _outproj_kernel,
        grid=(M // tm,),
        in_specs=[
            pl.BlockSpec((tm, D), lambda i: (i, 0)),
            pl.BlockSpec((tm, o_a.shape[1]), lambda i: (i, 0)),
            pl.BlockSpec((tm, o_h.shape[1]), lambda i: (i, 0)),
            const(wa), const(wh), const(g),
        ],
        out_specs=pl.BlockSpec((tm, D), lambda i: (i, 0)),
        out_shape=jax.ShapeDtypeStruct((M, D), F32),
        compiler_params=pltpu.CompilerParams(
            dimension_semantics=("arbitrary",),
            vmem_limit_bytes=VMEM_LIMIT),
        name="outproj",
    )(x2, o_a, o_h, wa, wh, g)


def kernel(x, norm_in, w_in, cmp_pe_k, cmp_w1_k, cmp_w2_k, cmp_pe_v, cmp_w1_v, cmp_w2_v,
           lower_bounds, nsa_out_norm, hgrn_out_norm, w_out, final_norm):
    B, S, D = x.shape
    assert norm_in.shape[0] == 1, "single-layer problem"
    nsa_w = nsa_out_norm.shape[1]
    hgrn_w = hgrn_out_norm.shape[1]
    dh = NSA_HEAD_DIM
    G = NSA_KV_HEADS
    heads = nsa_w // dh
    kvw = G * dh
    n_gate = 3 * heads
    n_cmp = (S - CMP_BLOCK) // CMP_STRIDE + 1
    n_half = S // CMP_STRIDE
    assert S % NSA_TILE == 0 and WINDOW % NSA_TILE == 0 and n_half <= LANES and 2 * dh == LANES

    names = ["q_a", "k_cmp", "v_cmp", "k_slc", "v_slc", "k_win", "v_win", "gate", "z_a",
             "q_h", "f_h", "i_h", "z_h"]
    widths = [nsa_w] + [kvw] * 6 + [n_gate, nsa_w] + [hgrn_w] * 4
    starts = dict(zip(names, np.cumsum([0] + widths[:-1]).tolist()))
    wd = dict(zip(names, widths))

    paired = lambda kname, vname: [(starts[n] + g * dh, dh) for g in range(G) for n in (kname, vname)]
    whole = lambda n: [(starts[n], wd[n])]
    groups = [[("q_a", whole("q_a")),
               ("cmp", paired("k_cmp", "v_cmp")), ("slc", paired("k_slc", "v_slc")),
               ("win", paired("k_win", "v_win")), ("q_h", whole("q_h")), ("i_h", whole("i_h"))],
              [("z_a", whole("z_a")), ("f_h", whole("f_h")), ("z_h", whole("z_h"))]]
    plan, col_of, off = [], {}, 0
    for group in groups:
        for n, pieces in group:
            col_of[n] = off
            for src, width in pieces:
                plan.append((off, src, width, dh ** -0.5 if n == "q_a" else 1.0))
                off += width
    n_main = off
    n16_cols = col_of["z_a"]
    plan.append((n_main, starts["gate"], n_gate, 1.0))
    w_all = _relayout_weights(w_in[0].T, tuple(plan), n_main + LANES)

    x2 = x.reshape(B * S, D)
    proj16, proj32, gate = _proj(x2, norm_in, w_all, tm=min(1024, B * S), tn=1536,
                                 n_main=n_main, n16_cols=n16_cols)
    streams = {}
    for arr, group, base in ((proj16, groups[0], 0), (proj32, groups[1], n16_cols)):
        for n, _ in group:
            streams[n] = (arr, col_of[n] - base)

    featk, featw, featc, qfeat, qrel, seg, mmap, place = _nsa_tables(S, n_half, n_cmp, heads)

    def w1_halves(w1k, w1v):
        hk = w1k.shape[1]
        k3 = w1k.reshape(2, CMP_STRIDE, dh, hk)
        v3 = w1v.reshape(2, CMP_STRIDE, dh, hk)
        zk = jnp.zeros_like(k3[0])
        top = lambda a: jnp.concatenate([a, zk], axis=-1)
        bot = lambda a: jnp.concatenate([zk, a], axis=-1)
        half = lambda i: jnp.concatenate([top(k3[i]), bot(v3[i])], axis=1).reshape(CMP_STRIDE * 2 * dh, 2 * hk)
        return half(0).astype(BF16), half(1).astype(BF16)

    wp, wq = w1_halves(cmp_w1_k[0], cmp_w1_v[0])
    zk = jnp.zeros_like(cmp_w2_k[0])
    w2 = jnp.concatenate([jnp.concatenate([cmp_w2_k[0], zk], axis=1),
                          jnp.concatenate([zk, cmp_w2_v[0]], axis=1)], axis=0).astype(BF16)
    pe = jnp.concatenate([cmp_pe_k[0].reshape(2, CMP_STRIDE, dh), cmp_pe_v[0].reshape(2, CMP_STRIDE, dh)],
                         axis=-1).reshape(2, CMP_STRIDE * 2 * dh)
    kc, vc = _compress(streams["cmp"][0], pe, wp, wq, w2, jnp.tile(featc, (G, 1)),
                       batch=B, seq=S, col0=streams["cmp"][1], n_half=n_half)

    o_a = _nsa(streams, gate, kc, vc, featk, featw, qfeat, qrel, seg, mmap, place, nsa_out_norm,
               batch=B, seq=S, n_cmp=n_cmp)
    o_h = _hgrn(streams, lower_bounds, hgrn_out_norm, batch=B, seq=S)

    wo = w_out[0].astype(BF16)
    out = _outproj(x2, o_a, o_h, wo[:nsa_w], wo[nsa_w:], final_norm.reshape(1, D), tm=512)
    return out.reshape(B, S, D)
```
